```python
import jax, jax.numpy as jnp
from jax import lax
import numpy as np

D_MODEL = 2048
BATCH = 2
SEQ = 4096
DEPTH = 1
DEC_BATCH = 32
DEC_SEQ = 32
PAST_LEN = 2048

CHUNK = 64
D_CONV = D_MODEL // 2
CONV_WIDTH = 3
H_ATTN = 8
HD_ATTN = 128
D_ATTN = H_ATTN * HD_ATTN
D_MIX = D_CONV + D_ATTN
D_IN = 3 * D_CONV + 3 * D_ATTN + H_ATTN
N_MEM = 256
H_X = 4
HD_X = 128
D_X = H_X * HD_X
D_FF = 4 * D_MODEL
Q_BLOCK = 128
RMS_EPS = 1e-6
FORGET_BIAS_INIT = 3.0

kernel_name = "hybrid_conv_fox_stream_step"


def rmsnorm(x, g):
    xf = x.astype(jnp.float32)
    y = xf * lax.rsqrt(jnp.mean(xf * xf, axis=-1, keepdims=True) + RMS_EPS)
    return (y * g.astype(jnp.float32)).astype(x.dtype)


def project(xn, w_in, b_f):
    N, T, _ = xn.shape
    z = xn @ w_in
    o1, o2, o3 = D_CONV, 2 * D_CONV, 3 * D_CONV
    o4, o5, o6 = o3 + D_ATTN, o3 + 2 * D_ATTN, o3 + 3 * D_ATTN
    gb, gc, hin = z[..., :o1], z[..., o1:o2], z[..., o2:o3]
    q = z[..., o3:o4].reshape(N, T, H_ATTN, HD_ATTN)
    k = z[..., o4:o5].reshape(N, T, H_ATTN, HD_ATTN)
    v = z[..., o5:o6].reshape(N, T, H_ATTN, HD_ATTN)
    logf = jax.nn.log_sigmoid(z[..., o6:].astype(jnp.float32) + b_f.astype(jnp.float32))
    return gb, gc, hin, q, k, v, logf


def short_conv(u, buf, w):
    T = u.shape[1]
    up = jnp.concatenate([buf, u], axis=1)
    y = w[0] * up[:, 0:T]
    for j in range(1, CONV_WIDTH):
        y = y + w[j] * up[:, j:j + T]
    return y, up[:, -(CONV_WIDTH - 1):]


def fox_attention(q, k, v, cq, ck, q_pos, k_pos):
    N, Tq, H, D = q.shape
    blk = Q_BLOCK if Tq % Q_BLOCK == 0 else Tq
    nb = Tq // blk
    scale = HD_ATTN ** -0.5
    qb = q.reshape(N, nb, blk, H, D).transpose(1, 0, 2, 3, 4)
    cqb = cq.reshape(N, nb, blk, H).transpose(1, 0, 3, 2)
    pb = q_pos.reshape(nb, blk)
    ckT = ck.transpose(0, 2, 1)[:, :, None, :]

    def one_block(args):
        qi, ci, pi = args
        s = jnp.einsum('nqhd,nkhd->nhqk', qi, k, preferred_element_type=jnp.float32) * scale
        s = s + (ci[..., None] - ckT)
        mask = k_pos[None, :] <= pi[:, None]
        s = jnp.where(mask, s, -jnp.inf)
        p = jax.nn.softmax(s, axis=-1)
        return jnp.einsum('nhqk,nkhd->nqhd', p.astype(v.dtype), v)

    out = lax.map(one_block, (qb, cqb, pb))
    return out.transpose(1, 0, 2, 3, 4).reshape(N, Tq, H * D)


def parallel_mixer(xn, conv_buf, k_past, v_past, logf_past, w_in, b_f, conv_w, g_conv_out, g_attn_out, w_out):
    N, T, _ = xn.shape
    P = k_past.shape[1]
    gb, gc, hin, q, k, v, logf = project(xn, w_in, b_f)
    yc, conv_new = short_conv(gc * hin, conv_buf, conv_w)
    yc = gb * yc
    k_all = jnp.concatenate([k_past, k], axis=1)
    v_all = jnp.concatenate([v_past, v], axis=1)
    c_all = jnp.cumsum(jnp.concatenate([logf_past.astype(jnp.float32), logf], axis=1), axis=1)
    k_pos = jnp.arange(P + T)
    q_pos = P + jnp.arange(T)
    ya = fox_attention(q, k_all, v_all, c_all[:, P:], c_all, q_pos, k_pos)
    y = jnp.concatenate([rmsnorm(yc, g_conv_out), rmsnorm(ya, g_attn_out)], axis=-1) @ w_out
    return y, conv_new, k, v, logf


def memory_kv(mem, g_mem, w_xkv):
    N = mem.shape[0]
    kv = rmsnorm(mem, g_mem) @ w_xkv
    mk = kv[..., :D_X].reshape(N, N_MEM, H_X, HD_X)
    mv = kv[..., D_X:].reshape(N, N_MEM, H_X, HD_X)
    return mk, mv


def cross_attention(xn, mk, mv, w_xq, w_xo):
    N, T, _ = xn.shape
    q = (xn @ w_xq).reshape(N, T, H_X, HD_X)
    s = jnp.einsum('nqhd,nkhd->nhqk', q, mk, preferred_element_type=jnp.float32) * (HD_X ** -0.5)
    p = jax.nn.softmax(s, axis=-1)
    o = jnp.einsum('nhqk,nkhd->nqhd', p.astype(mv.dtype), mv).reshape(N, T, D_X)
    return o @ w_xo


def sq_relu_mlp(xn, w_up, w_down):
    return jnp.square(jax.nn.relu(xn @ w_up)) @ w_down


def setup_inputs(seed: int = 0) -> dict:
    key = jax.random.key(seed)
    ks = jax.random.split(key, 32)
    f32 = jnp.float32

    def nrm(k, shape, scale):
        return jax.random.normal(k, shape, f32) * scale

    def gain(k, shape):
        return 1.0 + 0.01 * jax.random.normal(k, shape, f32)

    return {
        "x_prompt": nrm(ks[0], (BATCH, SEQ, D_MODEL), 1.0),
        "x_sample": nrm(ks[1], (DEC_BATCH, DEC_SEQ, D_MODEL), 1.0),
        "cache_k": nrm(ks[2], (DEPTH, DEC_BATCH, PAST_LEN, H_ATTN, HD_ATTN), 1.0),
        "cache_v": nrm(ks[3], (DEPTH, DEC_BATCH, PAST_LEN, H_ATTN, HD_ATTN), 1.0),
        "cache_logf": jax.nn.log_sigmoid(FORGET_BIAS_INIT + nrm(ks[4], (DEPTH, DEC_BATCH, PAST_LEN, H_ATTN), 1.0)),
        "cache_conv": nrm(ks[5], (DEPTH, DEC_BATCH, CONV_WIDTH - 1, D_CONV), 1.0),
        "cache_mem_k": nrm(ks[6], (DEPTH, DEC_BATCH, N_MEM, H_X, HD_X), 1.0),
        "cache_mem_v": nrm(ks[7], (DEPTH, DEC_BATCH, N_MEM, H_X, HD_X), 1.0),
        "mem_prompt": nrm(ks[8], (BATCH, N_MEM, D_MODEL), 1.0),
        "g_mix": gain(ks[9], (DEPTH, D_MODEL)),
        "w_in": nrm(ks[10], (DEPTH, D_MODEL, D_IN), D_MODEL ** -0.5),
        "b_f": FORGET_BIAS_INIT + nrm(ks[11], (DEPTH, H_ATTN), 0.1),
        "conv_w": nrm(ks[12], (DEPTH, CONV_WIDTH, D_CONV), CONV_WIDTH ** -0.5),
        "g_conv_out": gain(ks[13], (DEPTH, D_CONV)),
        "g_attn_out": gain(ks[14], (DEPTH, D_ATTN)),
        "w_out": nrm(ks[15], (DEPTH, D_MIX, D_MODEL), D_MIX ** -0.5),
        "g_xattn": gain(ks[16], (DEPTH, D_MODEL)),
        "g_mem": gain(ks[17], (DEPTH, D_MODEL)),
        "w_xq": nrm(ks[18], (DEPTH, D_MODEL, D_X), D_MODEL ** -0.5),
        "w_xkv": nrm(ks[19], (DEPTH, D_MODEL, 2 * D_X), D_MODEL ** -0.5),
        "w_xo": nrm(ks[20], (DEPTH, D_X, D_MODEL), D_X ** -0.5),
        "g_mlp": gain(ks[21], (DEPTH, D_MODEL)),
        "w_up": nrm(ks[22], (DEPTH, D_MODEL, D_FF), D_MODEL ** -0.5),
        "w_down": nrm(ks[23], (DEPTH, D_FF, D_MODEL), D_FF ** -0.5),
        "g_final": gain(ks[24], (D_MODEL,)),
    }


def reference(x_prompt, x_sample, cache_k, cache_v, cache_logf, cache_conv, cache_mem_k, cache_mem_v, mem_prompt,
              g_mix, w_in, b_f, conv_w, g_conv_out, g_attn_out, w_out, g_xattn, g_mem, w_xq, w_xkv, w_xo,
              g_mlp, w_up, w_down, g_final):
    xp, xs = x_prompt, x_sample
    Bp = xp.shape[0]
    kp_l, vp_l, fp_l, cp_l, mkp_l, mvp_l = [], [], [], [], [], []
    ks_l, vs_l, fs_l, cs_l = [], [], [], []
    for l in range(DEPTH):
        mix_w = (w_in[l], b_f[l], conv_w[l], g_conv_out[l], g_attn_out[l], w_out[l])
        conv0 = jnp.zeros((Bp, CONV_WIDTH - 1, D_CONV), xp.dtype)
        k0 = jnp.zeros((Bp, 0, H_ATTN, HD_ATTN), xp.dtype)
        f0 = jnp.zeros((Bp, 0, H_ATTN), jnp.float32)
        y, conv_p, k_p, v_p, f_p = parallel_mixer(rmsnorm(xp, g_mix[l]), conv0, k0, k0, f0, *mix_w)
        xp = xp + y
        mk_p, mv_p = memory_kv(mem_prompt, g_mem[l], w_xkv[l])
        xp = xp + cross_attention(rmsnorm(xp, g_xattn[l]), mk_p, mv_p, w_xq[l], w_xo[l])
        xp = xp + sq_relu_mlp(rmsnorm(xp, g_mlp[l]), w_up[l], w_down[l])
        y, conv_s, k_s, v_s, f_s = parallel_mixer(rmsnorm(xs, g_mix[l]), cache_conv[l], cache_k[l], cache_v[l],
                                                  cache_logf[l], *mix_w)
        xs = xs + y
        xs = xs + cross_attention(rmsnorm(xs, g_xattn[l]), cache_mem_k[l], cache_mem_v[l], w_xq[l], w_xo[l])
        xs = xs + sq_relu_mlp(rmsnorm(xs, g_mlp[l]), w_up[l], w_down[l])
        kp_l.append(k_p); vp_l.append(v_p); fp_l.append(f_p); cp_l.append(conv_p)
        mkp_l.append(mk_p); mvp_l.append(mv_p)
        ks_l.append(k_s); vs_l.append(v_s); fs_l.append(f_s); cs_l.append(conv_s)
    y_prompt = rmsnorm(xp, g_final)
    y_sample = rmsnorm(xs, g_final)
    new_k_prompt = jnp.stack(kp_l)
    new_v_prompt = jnp.stack(vp_l)
    new_logf_prompt = jnp.stack(fp_l)
    new_conv_prompt = jnp.stack(cp_l)
    new_mem_k_prompt = jnp.stack(mkp_l)
    new_mem_v_prompt = jnp.stack(mvp_l)
    new_k_sample = jnp.stack(ks_l)
    new_v_sample = jnp.stack(vs_l)
    new_logf_sample = jnp.stack(fs_l)
    new_conv_sample = jnp.stack(cs_l)
    return (y_prompt, y_sample, new_k_prompt, new_v_prompt, new_logf_prompt, new_conv_prompt,
            new_mem_k_prompt, new_mem_v_prompt, new_k_sample, new_v_sample, new_logf_sample, new_conv_sample)
```

```python
import functools

import jax
import jax.numpy as jnp
from jax import lax
from jax.experimental import pallas as pl
from jax.experimental.pallas import tpu as pltpu

F32 = jnp.float32
BF16 = jnp.bfloat16

RMS_EPS = 1e-6
HEAD_DIM = 128
N_FOX_HEADS = 8
N_X_HEADS = 4
CONV_WIDTH = 3
LANES = 128
VMEM_LIMIT = 52 * 1024 * 1024

_NT = (((1,), (1,)), ((), ()))


def _params(semantics):
    return pltpu.CompilerParams(dimension_semantics=semantics, vmem_limit_bytes=VMEM_LIMIT)


def _rms(x, g):
    ms = jnp.mean(x * x, axis=-1, keepdims=True)
    return x * lax.rsqrt(ms + RMS_EPS) * g


def _log_sigmoid(x):
    return jnp.minimum(x, 0.0) - jnp.log1p(jnp.exp(-jnp.abs(x)))


def _proj_kernel(x_ref, g_ref, w_ref, wf_ref, bf_ref,
                 zc_ref, q_ref, k_ref, v_ref, kb_ref, vb_ref, lf_ref, xn_ref):
    j = pl.program_id(1)

    @pl.when(j == 0)
    def _():
        xn = _rms(x_ref[...], g_ref[...]).astype(BF16)
        xn_ref[...] = xn
        fz = jnp.dot(xn, wf_ref[...], preferred_element_type=F32)
        lf_ref[...] = _log_sigmoid(fz[:, :N_FOX_HEADS] + bf_ref[...])

    res = jnp.dot(xn_ref[...], w_ref[...], preferred_element_type=F32)

    @pl.when(j < 3)
    def _():
        zc_ref[...] = res

    @pl.when(j == 3)
    def _():
        q_ref[...] = res.astype(BF16)

    @pl.when(j == 4)
    def _():
        k_ref[...] = res
        kb_ref[...] = res.astype(BF16)

    @pl.when(j == 5)
    def _():
        v_ref[...] = res
        vb_ref[...] = res.astype(BF16)


def _proj(x, g, w_main, w_f, b_f, *, tm):
    m, d = x.shape
    tn = w_main.shape[1] // 6
    row = lambda i, j: (i, 0)
    return pl.pallas_call(
        _proj_kernel,
        grid=(m // tm, 6),
        in_specs=[
            pl.BlockSpec((tm, d), row),
            pl.BlockSpec((1, d), lambda i, j: (0, 0)),
            pl.BlockSpec((d, tn), lambda i, j: (0, j)),
            pl.BlockSpec((d, LANES), lambda i, j: (0, 0)),
            pl.BlockSpec((1, N_FOX_HEADS), lambda i, j: (0, 0)),
        ],
        out_specs=[
            pl.BlockSpec((tm, tn), lambda i, j: (i, jnp.minimum(j, 2))),
            pl.BlockSpec((tm, tn), row),
            pl.BlockSpec((tm, tn), row),
            pl.BlockSpec((tm, tn), row),
            pl.BlockSpec((tm, tn), row),
            pl.BlockSpec((tm, tn), row),
            pl.BlockSpec((tm, N_FOX_HEADS), row),
        ],
        out_shape=[
            jax.ShapeDtypeStruct((m, 3 * tn), F32),
            jax.ShapeDtypeStruct((m, tn), BF16),
            jax.ShapeDtypeStruct((m, tn), F32),
            jax.ShapeDtypeStruct((m, tn), F32),
            jax.ShapeDtypeStruct((m, tn), BF16),
            jax.ShapeDtypeStruct((m, tn), BF16),
            jax.ShapeDtypeStruct((m, N_FOX_HEADS), F32),
        ],
        scratch_shapes=[pltpu.VMEM((tm, d), BF16)],
        compiler_params=_params(("arbitrary", "arbitrary")),
        name="proj",
    )(x, g, w_main, w_f, b_f)


def _cumsum_kernel(a_ref, o_ref):
    rows, length = a_ref.shape
    r = lax.broadcasted_iota(jnp.int32, (LANES, LANES), 0)
    c = lax.broadcasted_iota(jnp.int32, (LANES, LANES), 1)
    upper = (r <= c).astype(BF16)
    carry = jnp.zeros((rows, 1), F32)
    for b in range(length // LANES):
        a = a_ref[:, b * LANES:(b + 1) * LANES]
        hi = a.astype(BF16)
        r1 = a - hi.astype(F32)
        mid = r1.astype(BF16)
        lo = (r1 - mid.astype(F32)).astype(BF16)
        s = (jnp.dot(hi, upper, preferred_element_type=F32)
             + jnp.dot(mid, upper, preferred_element_type=F32)
             + jnp.dot(lo, upper, preferred_element_type=F32)) + carry
        o_ref[:, b * LANES:(b + 1) * LANES] = s
        carry = s[:, LANES - 1:LANES]


def _cumsum_lanes(a):
    return pl.pallas_call(
        _cumsum_kernel,
        out_shape=jax.ShapeDtypeStruct(a.shape, F32),
        compiler_params=pltpu.CompilerParams(vmem_limit_bytes=VMEM_LIMIT),
        name="cumsum",
    )(a)


def _fox_prompt_kernel(q_ref, k_ref, v_ref, cq_ref, ck_ref, o_ref, m_ref, l_ref, acc_ref, *, tile):
    h = pl.program_id(1)
    qi = pl.program_id(2)
    scale = HEAD_DIM ** -0.5
    q = q_ref[...]
    lane = lax.broadcasted_iota(jnp.int32, (tile, N_FOX_HEADS), 1)
    cq = jnp.sum(jnp.where(lane == h, cq_ref[0], 0.0), axis=1, keepdims=True)

    m_ref[...] = jnp.full(m_ref.shape, -jnp.inf, F32)
    l_ref[...] = jnp.zeros(l_ref.shape, F32)
    acc_ref[...] = jnp.zeros(acc_ref.shape, F32)

    def step(ki, diagonal):
        start = pl.multiple_of(ki * tile, tile)
        kb = k_ref[pl.ds(start, tile), :]
        s = lax.dot_general(q, kb, _NT, preferred_element_type=F32) * scale
        ck = ck_ref[0, ki, pl.ds(h, 1), :]
        s = s + (cq - ck)
        if diagonal:
            r = lax.broadcasted_iota(jnp.int32, (tile, tile), 0)
            c = lax.broadcasted_iota(jnp.int32, (tile, tile), 1)
            s = jnp.where(c <= r, s, -jnp.inf)
        m_prev = m_ref[...]
        m_new = jnp.maximum(m_prev, jnp.max(s, axis=1, keepdims=True))
        alpha = jnp.exp(m_prev - m_new)
        p = jnp.exp(s - m_new)
        l_ref[...] = alpha * l_ref[...] + jnp.sum(p, axis=1, keepdims=True)
        pv = jnp.dot(p.astype(BF16), v_ref[pl.ds(start, tile), :], preferred_element_type=F32)
        acc_ref[...] = alpha * acc_ref[...] + pv
        m_ref[...] = m_new

    def body(ki, carry):
        step(ki, False)
        return carry

    lax.fori_loop(0, qi, body, 0)
    step(qi, True)
    o_ref[...] = acc_ref[...] / l_ref[...]


def _fox_prompt(qb, kb, vb, cq, ck, *, n_batch, seq, tile):
    m = qb.shape[0]
    nq = seq // tile
    return pl.pallas_call(
        functools.partial(_fox_prompt_kernel, tile=tile),
        grid=(n_batch, N_FOX_HEADS, nq),
        in_specs=[
            pl.BlockSpec((tile, HEAD_DIM), lambda n, h, i: (n * nq + i, h)),
            pl.BlockSpec((seq, HEAD_DIM), lambda n, h, i: (n, h)),
            pl.BlockSpec((seq, HEAD_DIM), lambda n, h, i: (n, h)),
            pl.BlockSpec((1, tile, N_FOX_HEADS), lambda n, h, i: (n, i, 0)),
            pl.BlockSpec((1, nq, N_FOX_HEADS, tile), lambda n, h, i: (n, 0, 0, 0)),
        ],
        out_specs=pl.BlockSpec((tile, HEAD_DIM), lambda n, h, i: (n * nq + i, h)),
        out_shape=jax.ShapeDtypeStruct((m, N_FOX_HEADS * HEAD_DIM), F32),
        scratch_shapes=[
            pltpu.VMEM((tile, 1), F32),
            pltpu.VMEM((tile, 1), F32),
            pltpu.VMEM((tile, HEAD_DIM), F32),
        ],
        compiler_params=_params(("arbitrary", "arbitrary", "arbitrary")),
        name="fox_prompt",
    )(qb, kb, vb, cq, ck)


def _fox_sample_kernel(q_ref, kn_ref, vn_ref, kc_ref, vc_ref, cq_ref, ckp_ref, ckn_ref, o_ref, *, heads):
    scale = HEAD_DIM ** -0.5
    t = q_ref.shape[0]
    r = lax.broadcasted_iota(jnp.int32, (t, t), 0)
    c = lax.broadcasted_iota(jnp.int32, (t, t), 1)
    for hh in range(heads):
        sl = slice(hh * HEAD_DIM, (hh + 1) * HEAD_DIM)
        q = q_ref[:, sl]
        kp = kc_ref[0, :, sl].astype(BF16)
        vp = vc_ref[0, :, sl].astype(BF16)
        cq = cq_ref[0, 0, :, hh:hh + 1]
        s1 = lax.dot_general(q, kp, _NT, preferred_element_type=F32) * scale
        s1 = s1 + (cq - ckp_ref[0, 0, hh:hh + 1, :])
        s2 = lax.dot_general(q, kn_ref[:, sl], _NT, preferred_element_type=F32) * scale
        s2 = s2 + (cq - ckn_ref[0, 0, hh:hh + 1, :])
        s2 = jnp.where(c <= r, s2, -jnp.inf)
        m = jnp.maximum(jnp.max(s1, axis=1, keepdims=True), jnp.max(s2, axis=1, keepdims=True))
        p1 = jnp.exp(s1 - m)
        p2 = jnp.exp(s2 - m)
        l = jnp.sum(p1, axis=1, keepdims=True) + jnp.sum(p2, axis=1, keepdims=True)
        o = (jnp.dot(p1.astype(BF16), vp, preferred_element_type=F32)
             + jnp.dot(p2.astype(BF16), vn_ref[:, sl], preferred_element_type=F32))
        o_ref[:, sl] = o / l


def _fox_sample(qb, knb, vnb, cache_k, cache_v, cq, ck_past, ck_new, *, heads):
    n_batch, past, width = cache_k.shape
    t = qb.shape[0] // n_batch
    groups = N_FOX_HEADS // heads
    gw = heads * HEAD_DIM
    tok = lambda b, g: (b, g)
    return pl.pallas_call(
        functools.partial(_fox_sample_kernel, heads=heads),
        grid=(n_batch, groups),
        in_specs=[
            pl.BlockSpec((t, gw), tok),
            pl.BlockSpec((t, gw), tok),
            pl.BlockSpec((t, gw), tok),
            pl.BlockSpec((1, past, gw), lambda b, g: (b, 0, g)),
            pl.BlockSpec((1, past, gw), lambda b, g: (b, 0, g)),
            pl.BlockSpec((1, 1, t, heads), lambda b, g: (b, g, 0, 0)),
            pl.BlockSpec((1, 1, heads, past), lambda b, g: (b, g, 0, 0)),
            pl.BlockSpec((1, 1, heads, t), lambda b, g: (b, g, 0, 0)),
        ],
        out_specs=pl.BlockSpec((t, gw), tok),
        out_shape=jax.ShapeDtypeStruct((qb.shape[0], width), F32),
        compiler_params=_params(("arbitrary", "arbitrary")),
        name="fox_sample",
    )(qb, knb, vnb, cache_k, cache_v, cq, ck_past, ck_new)


def _mix_kernel(*refs, seg, nseg, tiles_per_seq, has_prev):
    if has_prev:
        (x_ref, gb_ref, gc_ref, hin_ref, pgc_ref, phin_ref, buf_ref, ya_ref, cw_ref, gco_ref, gao_ref,
         wout_ref, gx_ref, wxq_ref, x1_ref, qx_ref, cnew_ref) = refs
    else:
        (x_ref, gb_ref, gc_ref, hin_ref, buf_ref, ya_ref, cw_ref, gco_ref, gao_ref,
         wout_ref, gx_ref, wxq_ref, x1_ref, qx_ref, cnew_ref) = refs
    i = pl.program_id(0)
    u = gc_ref[...] * hin_ref[...]
    gb = gb_ref[...]
    w0 = cw_ref[0:1, :]
    w1 = cw_ref[1:2, :]
    w2 = cw_ref[2:3, :]
    rid = lax.broadcasted_iota(jnp.int32, (seg, u.shape[1]), 0)
    pieces = []
    for s in range(nseg):
        u_s = u[s * seg:(s + 1) * seg]
        b0 = buf_ref[s, 0:1, :]
        b1 = buf_ref[s, 1:2, :]
        if has_prev:
            pu = pgc_ref[...] * phin_ref[...]
            first = (i % tiles_per_seq) == 0
            b0 = jnp.where(first, b0, pu[6:7])
            b1 = jnp.where(first, b1, pu[7:8])
        r1 = jnp.where(rid == 0, b1, pltpu.roll(u_s, 1, 0))
        r2 = jnp.where(rid == 0, b0, jnp.where(rid == 1, b1, pltpu.roll(u_s, 2, 0)))
        y = w0 * r2 + w1 * r1 + w2 * u_s
        pieces.append(gb[s * seg:(s + 1) * seg] * y)
        cnew_ref[s] = u_s[seg - 2:seg]
    yc = pieces[0] if nseg == 1 else jnp.concatenate(pieces, axis=0)
    cat = jnp.concatenate([_rms(yc, gco_ref[...]).astype(BF16),
                           _rms(ya_ref[...], gao_ref[...]).astype(BF16)], axis=-1)
    x1 = x_ref[...] + jnp.dot(cat, wout_ref[...], preferred_element_type=F32)
    x1_ref[...] = x1
    xn = _rms(x1, gx_ref[...]).astype(BF16)
    qx_ref[...] = jnp.dot(xn, wxq_ref[...], preferred_element_type=F32).astype(BF16)


def _mix(x, zc, conv_buf, ya, conv_w, g_conv_out, g_attn_out, w_out, g_xattn, w_xq, *, tm, seq):
    m, d = x.shape
    dc = zc.shape[1] // 3
    dx = w_xq.shape[1]
    has_prev = seq > tm
    seg = tm if has_prev else seq
    nseg = tm // seg
    tiles_per_seq = max(seq // tm, 1)
    row = lambda i: (i, 0)
    const = lambda i: (0, 0)
    in_specs = [
        pl.BlockSpec((tm, d), row),
        pl.BlockSpec((tm, dc), lambda i: (i, 0)),
        pl.BlockSpec((tm, dc), lambda i: (i, 1)),
        pl.BlockSpec((tm, dc), lambda i: (i, 2)),
    ]
    args = [x, zc, zc, zc]
    if has_prev:
        prev = lambda col: (lambda i: (jnp.maximum(i * (tm // 8) - 1, 0), col))
        in_specs += [pl.BlockSpec((8, dc), prev(1)), pl.BlockSpec((8, dc), prev(2))]
        args += [zc, zc]
        buf_map = lambda i: (i // tiles_per_seq, 0, 0)
    else:
        buf_map = lambda i: (i, 0, 0)
    in_specs += [
        pl.BlockSpec((nseg, CONV_WIDTH - 1, dc), buf_map),
        pl.BlockSpec((tm, ya.shape[1]), row),
        pl.BlockSpec((CONV_WIDTH, dc), const),
        pl.BlockSpec((1, dc), const),
        pl.BlockSpec((1, ya.shape[1]), const),
        pl.BlockSpec(w_out.shape, const),
        pl.BlockSpec((1, d), const),
        pl.BlockSpec(w_xq.shape, const),
    ]
    args += [conv_buf, ya, conv_w, g_conv_out, g_attn_out, w_out, g_xattn, w_xq]
    return pl.pallas_call(
        functools.partial(_mix_kernel, seg=seg, nseg=nseg, tiles_per_seq=tiles_per_seq, has_prev=has_prev),
        grid=(m // tm,),
        in_specs=in_specs,
        out_specs=[
            pl.BlockSpec((tm, d), row),
            pl.BlockSpec((tm, dx), row),
            pl.BlockSpec((nseg, CONV_WIDTH - 1, dc), buf_map),
        ],
        out_shape=[
            jax.ShapeDtypeStruct((m, d), F32),
            jax.ShapeDtypeStruct((m, dx), BF16),
            jax.ShapeDtypeStruct(conv_buf.shape, F32),
        ],
        compiler_params=_params(("arbitrary",)),
        name="mix",
    )(*args)


def _memkv_kernel(x_ref, g_ref, w_ref, o_ref):
    xn = _rms(x_ref[...], g_ref[...]).astype(BF16)
    o_ref[0] = jnp.dot(xn, w_ref[...], preferred_element_type=F32)


def _memkv(mem, g, w):
    m, d = mem.shape
    half = w.shape[1] // 2
    return pl.pallas_call(
        _memkv_kernel,
        grid=(2,),
        in_specs=[
            pl.BlockSpec((m, d), lambda j: (0, 0)),
            pl.BlockSpec((1, d), lambda j: (0, 0)),
            pl.BlockSpec((d, half), lambda j: (0, j)),
        ],
        out_specs=pl.BlockSpec((1, m, half), lambda j: (j, 0, 0)),
        out_shape=jax.ShapeDtypeStruct((2, m, half), F32),
        compiler_params=_params(("arbitrary",)),
        name="memkv",
    )(mem, g, w)


def _xattn_kernel(x1_ref, qx_ref, mk_ref, mv_ref, wxo_ref, g_ref, x2_ref, h_ref, o_scr, *, seg, nseg):
    scale = HEAD_DIM ** -0.5
    for s in range(nseg):
        rows = slice(s * seg, (s + 1) * seg)
        for hh in range(N_X_HEADS):
            sl = slice(hh * HEAD_DIM, (hh + 1) * HEAD_DIM)
            q = qx_ref[rows, sl]
            kh = mk_ref[s, :, sl].astype(BF16)
            vh = mv_ref[s, :, sl].astype(BF16)
            sc = lax.dot_general(q, kh, _NT, preferred_element_type=F32) * scale
            m = jnp.max(sc, axis=1, keepdims=True)
            p = jnp.exp(sc - m)
            l = jnp.sum(p, axis=1, keepdims=True)
            o = jnp.dot(p.astype(BF16), vh, preferred_element_type=F32) / l
            o_scr[rows, sl] = o.astype(BF16)
    x2 = x1_ref[...] + jnp.dot(o_scr[...], wxo_ref[...], preferred_element_type=F32)
    x2_ref[...] = x2
    h_ref[...] = _rms(x2, g_ref[...]).astype(BF16)


def _xattn(x1, qx, mk, mv, w_xo, g_mlp, *, tm, seq):
    m, d = x1.shape
    dx = qx.shape[1]
    n_mem = mk.shape[1]
    seg = min(tm, seq)
    nseg = tm // seg
    tiles_per_seq = max(seq // tm, 1)
    row = lambda i: (i, 0)
    const = lambda i: (0, 0)
    mem_map = (lambda i: (i // tiles_per_seq, 0, 0)) if nseg == 1 else (lambda i: (i, 0, 0))
    return pl.pallas_call(
        functools.partial(_xattn_kernel, seg=seg, nseg=nseg),
        grid=(m // tm,),
        in_specs=[
            pl.BlockSpec((tm, d), row),
            pl.BlockSpec((tm, dx), row),
            pl.BlockSpec((nseg, n_mem, dx), mem_map),
            pl.BlockSpec((nseg, n_mem, dx), mem_map),
            pl.BlockSpec(w_xo.shape, const),
            pl.BlockSpec((1, d), const),
        ],
        out_specs=[pl.BlockSpec((tm, d), row), pl.BlockSpec((tm, d), row)],
        out_shape=[jax.ShapeDtypeStruct((m, d), F32), jax.ShapeDtypeStruct((m, d), BF16)],
        scratch_shapes=[pltpu.VMEM((tm, dx), BF16)],
        compiler_params=_params(("arbitrary",)),
        name="xattn",
    )(x1, qx, mk, mv, w_xo, g_mlp)


def _mlp_kernel(h_ref, x2_ref, wu_ref, wd_ref, g_ref, y_ref, acc_ref):
    j = pl.program_id(1)

    @pl.when(j == 0)
    def _():
        acc_ref[...] = jnp.zeros(acc_ref.shape, F32)

    a = jnp.dot(h_ref[...], wu_ref[...], preferred_element_type=F32)
    a = jnp.square(jnp.maximum(a, 0.0)).astype(BF16)
    acc_ref[...] += jnp.dot(a, wd_ref[...], preferred_element_type=F32)

    @pl.when(j == pl.num_programs(1) - 1)
    def _():
        y_ref[...] = _rms(x2_ref[...] + acc_ref[...], g_ref[...])


def _mlp(h, x2, w_up, w_down, g_final, *, tm, tf):
    m, d = h.shape
    ff = w_up.shape[1]
    row = lambda i, j: (i, 0)
    return pl.pallas_call(
        _mlp_kernel,
        grid=(m // tm, ff // tf),
        in_specs=[
            pl.BlockSpec((tm, d), row),
            pl.BlockSpec((tm, d), row),
            pl.BlockSpec((d, tf), lambda i, j: (0, j)),
            pl.BlockSpec((tf, d), lambda i, j: (j, 0)),
            pl.BlockSpec((1, d), lambda i, j: (0, 0)),
        ],
        out_specs=pl.BlockSpec((tm, d), row),
        out_shape=jax.ShapeDtypeStruct((m, d), F32),
        scratch_shapes=[pltpu.VMEM((tm, d), F32)],
        compiler_params=_params(("arbitrary", "arbitrary")),
        name="mlp",
    )(h, x2, w_up, w_down, g_final)


def kernel(x_prompt, x_sample, cache_k, cache_v, cache_logf, cache_conv, cache_mem_k, cache_mem_v, mem_prompt, g_mix, w_in, b_f, conv_w, g_conv_out, g_attn_out, w_out, g_xattn, g_mem, w_xq, w_xkv, w_xo, g_mlp, w_up, w_down, g_final):
    depth = w_in.shape[0]
    assert depth == 1, "single-layer trunk"
    nb, seq, d = x_prompt.shape
    db, dseq, _ = x_sample.shape
    past = cache_k.shape[2]
    heads, hd = cache_k.shape[3], cache_k.shape[4]
    assert (heads, hd) == (N_FOX_HEADS, HEAD_DIM)
    d_attn = heads * hd
    d_conv = cache_conv.shape[-1]
    n_mem, xh, xhd = cache_mem_k.shape[2:]
    assert (xh, xhd) == (N_X_HEADS, HEAD_DIM)
    d_x = xh * xhd
    n_main = 3 * d_conv + 3 * d_attn
    assert d_conv == d_attn and w_in.shape[2] == n_main + heads

    w_main = w_in[0, :, :n_main].astype(BF16)
    w_f = jnp.pad(w_in[0, :, n_main:], ((0, 0), (0, LANES - heads))).astype(BF16)
    bf = b_f[0].reshape(1, heads)
    w_out_b = w_out[0].astype(BF16)
    w_xq_b = w_xq[0].astype(BF16)
    w_xkv_b = w_xkv[0].astype(BF16)
    w_xo_b = w_xo[0].astype(BF16)
    w_up_b = w_up[0].astype(BF16)
    w_down_b = w_down[0].astype(BF16)
    row = lambda g: g.reshape(1, -1)

    def trunk(x, conv_buf, seq_len, attention, mk, mv, tm_mix, tm_x):
        zc, qb, k, v, kb, vb, lf = _proj(x, row(g_mix[0]), w_main, w_f, bf, tm=512)
        ya = attention(qb, kb, vb, lf)
        x1, qx, conv_new = _mix(x, zc, conv_buf, ya, conv_w[0], row(g_conv_out[0]), row(g_attn_out[0]),
                                w_out_b, row(g_xattn[0]), w_xq_b, tm=tm_mix, seq=seq_len)
        x2, h = _xattn(x1, qx, mk, mv, w_xo_b, row(g_mlp[0]), tm=tm_x, seq=seq_len)
        y = _mlp(h, x2, w_up_b, w_down_b, row(g_final), tm=512, tf=1024)
        return y, k, v, lf, conv_new

    tile = 512
    nq = seq // tile

    def prompt_attention(qb, kb, vb, lf):
        lft = lf.reshape(nb, seq, heads).transpose(0, 2, 1).reshape(nb * heads, seq)
        ct = _cumsum_lanes(lft).reshape(nb, heads, seq)
        cq = ct.transpose(0, 2, 1)
        ck = ct.reshape(nb, heads, nq, tile).transpose(0, 2, 1, 3)
        return _fox_prompt(qb, kb, vb, cq, ck, n_batch=nb, seq=seq, tile=tile)

    kv = _memkv(mem_prompt.reshape(nb * n_mem, d), row(g_mem[0]), w_xkv_b)
    mk_p = kv[0].reshape(nb, n_mem, d_x)
    mv_p = kv[1].reshape(nb, n_mem, d_x)
    conv0 = jnp.zeros((nb, CONV_WIDTH - 1, d_conv), F32)
    y_p, k_p, v_p, lf_p, conv_p = trunk(x_prompt.reshape(nb * seq, d), conv0, seq, prompt_attention,
                                        mk_p, mv_p, 256, 512)

    hg = 4
    pad = (-(past + dseq)) % LANES

    def sample_attention(qb, kb, vb, lf):
        lft_past = cache_logf[0].transpose(0, 2, 1).reshape(db * heads, past)
        lft_new = lf.reshape(db, dseq, heads).transpose(0, 2, 1).reshape(db * heads, dseq)
        ct = _cumsum_lanes(jnp.concatenate(
            [lft_past, lft_new, jnp.zeros((db * heads, pad), F32)], axis=1))
        ck_past = ct[:, :past].reshape(db, heads // hg, hg, past)
        ck_new = ct[:, past:past + dseq].reshape(db, heads // hg, hg, dseq)
        cq = ck_new.transpose(0, 1, 3, 2)
        return _fox_sample(qb, kb, vb, cache_k[0].reshape(db, past, d_attn),
                           cache_v[0].reshape(db, past, d_attn), cq, ck_past, ck_new, heads=hg)

    y_s, k_s, v_s, lf_s, conv_s = trunk(x_sample.reshape(db * dseq, d), cache_conv[0], dseq, sample_attention,
                                        cache_mem_k[0].reshape(db, n_mem, d_x),
                                        cache_mem_v[0].reshape(db, n_mem, d_x), 256, 256)

    return (y_p.reshape(nb, seq, d),
            y_s.reshape(db, dseq, d),
            k_p.reshape(1, nb, seq, heads, hd),
            v_p.reshape(1, nb, seq, heads, hd),
            lf_p.reshape(1, nb, seq, heads),
            conv_p[None],
            mk_p.reshape(1, nb, n_mem, xh, xhd),
            mv_p.reshape(1, nb, n_mem, xh, xhd),
            k_s.reshape(1, db, dseq, heads, hd),
            v_s.reshape(1, db, dseq, heads, hd),
            lf_s.reshape(1, db, dseq, heads),
            conv_s[None])
```

```python
import functools

import jax
import jax.numpy as jnp
from jax import lax
from jax.experimental import pallas as pl
from jax.experimental.pallas import tpu as pltpu

F32 = jnp.float32
BF16 = jnp.bfloat16

RMS_EPS = 1e-6
HEAD_DIM = 128
N_FOX_HEADS = 8
N_X_HEADS = 4
CONV_WIDTH = 3
LANES = 128
VMEM_LIMIT = 52 * 1024 * 1024

_NT = (((1,), (1,)), ((), ()))


def _params(semantics):
    return pltpu.CompilerParams(dimension_semantics=semantics, vmem_limit_bytes=VMEM_LIMIT)


def _rms(x, g):
    ms = jnp.mean(x * x, axis=-1, keepdims=True)
    return x * lax.rsqrt(ms + RMS_EPS) * g


def _log_sigmoid(x):
    return jnp.minimum(x, 0.0) - jnp.log1p(jnp.exp(-jnp.abs(x)))


def _store_head_rows(ref, val, heads):
    n = val.shape[0]
    for h in range(heads):
        ref[pl.ds(h, n, stride=heads), :] = val[:, h * HEAD_DIM:(h + 1) * HEAD_DIM]


def _load_head_rows(ref, lead, h, n, heads):
    return ref[lead, pl.ds(h, n, stride=heads), :]


def _proj_kernel(x_ref, g_ref, w_ref, wf_ref, bf_ref,
                 zc_ref, q_ref, k_ref, v_ref, kb_ref, vb_ref, lf_ref, xn_ref):
    j = pl.program_id(1)

    @pl.when(j == 0)
    def _():
        xn = _rms(x_ref[...], g_ref[...]).astype(BF16)
        xn_ref[...] = xn
        fz = jnp.dot(xn, wf_ref[...], preferred_element_type=F32)
        lf_ref[...] = _log_sigmoid(fz[:, :N_FOX_HEADS] + bf_ref[...])

    res = jnp.dot(xn_ref[...], w_ref[...], preferred_element_type=F32)

    @pl.when(j < 3)
    def _():
        zc_ref[...] = res

    @pl.when(j == 3)
    def _():
        q_ref[...] = res.astype(BF16)

    @pl.when(j == 4)
    def _():
        _store_head_rows(k_ref, res, N_FOX_HEADS)
        kb_ref[...] = res.astype(BF16)

    @pl.when(j == 5)
    def _():
        _store_head_rows(v_ref, res, N_FOX_HEADS)
        vb_ref[...] = res.astype(BF16)


def _proj(x, g, w_main, w_f, b_f, *, tm):
    m, d = x.shape
    tn = w_main.shape[1] // 6
    row = lambda i, j: (i, 0)
    return pl.pallas_call(
        _proj_kernel,
        grid=(m // tm, 6),
        in_specs=[
            pl.BlockSpec((tm, d), row),
            pl.BlockSpec((1, d), lambda i, j: (0, 0)),
            pl.BlockSpec((d, tn), lambda i, j: (0, j)),
            pl.BlockSpec((d, LANES), lambda i, j: (0, 0)),
            pl.BlockSpec((1, N_FOX_HEADS), lambda i, j: (0, 0)),
        ],
        out_specs=[
            pl.BlockSpec((tm, tn), lambda i, j: (i, jnp.minimum(j, 2))),
            pl.BlockSpec((tm, tn), row),
            pl.BlockSpec((tm * N_FOX_HEADS, HEAD_DIM), row),
            pl.BlockSpec((tm * N_FOX_HEADS, HEAD_DIM), row),
            pl.BlockSpec((tm, tn), row),
            pl.BlockSpec((tm, tn), row),
            pl.BlockSpec((tm, N_FOX_HEADS), row),
        ],
        out_shape=[
            jax.ShapeDtypeStruct((m, 3 * tn), F32),
            jax.ShapeDtypeStruct((m, tn), BF16),
            jax.ShapeDtypeStruct((m * N_FOX_HEADS, HEAD_DIM), F32),
            jax.ShapeDtypeStruct((m * N_FOX_HEADS, HEAD_DIM), F32),
            jax.ShapeDtypeStruct((m, tn), BF16),
            jax.ShapeDtypeStruct((m, tn), BF16),
            jax.ShapeDtypeStruct((m, N_FOX_HEADS), F32),
        ],
        scratch_shapes=[pltpu.VMEM((tm, d), BF16)],
        compiler_params=_params(("arbitrary", "arbitrary")),
        name="proj",
    )(x, g, w_main, w_f, b_f)


def _cumsum_kernel(a_ref, o_ref):
    rows, length = a_ref.shape
    r = lax.broadcasted_iota(jnp.int32, (LANES, LANES), 0)
    c = lax.broadcasted_iota(jnp.int32, (LANES, LANES), 1)
    upper = (r <= c).astype(BF16)
    carry = jnp.zeros((rows, 1), F32)
    for b in range(length // LANES):
        a = a_ref[:, b * LANES:(b + 1) * LANES]
        hi = a.astype(BF16)
        r1 = a - hi.astype(F32)
        mid = r1.astype(BF16)
        lo = (r1 - mid.astype(F32)).astype(BF16)
        s = (jnp.dot(hi, upper, preferred_element_type=F32)
             + jnp.dot(mid, upper, preferred_element_type=F32)
             + jnp.dot(lo, upper, preferred_element_type=F32)) + carry
        o_ref[:, b * LANES:(b + 1) * LANES] = s
        carry = s[:, LANES - 1:LANES]


def _cumsum_lanes(a):
    return pl.pallas_call(
        _cumsum_kernel,
        out_shape=jax.ShapeDtypeStruct(a.shape, F32),
        compiler_params=pltpu.CompilerParams(vmem_limit_bytes=VMEM_LIMIT),
        name="cumsum",
    )(a)


def _fox_prompt_kernel(q_ref, k_ref, v_ref, cq_ref, ck_ref, o_ref, m_ref, l_ref, acc_ref, *, tile):
    h = pl.program_id(1)
    qi = pl.program_id(2)
    scale = HEAD_DIM ** -0.5
    q = q_ref[...]
    lane = lax.broadcasted_iota(jnp.int32, (tile, N_FOX_HEADS), 1)
    cq = jnp.sum(jnp.where(lane == h, cq_ref[0], 0.0), axis=1, keepdims=True)

    m_ref[...] = jnp.full(m_ref.shape, -jnp.inf, F32)
    l_ref[...] = jnp.zeros(l_ref.shape, F32)
    acc_ref[...] = jnp.zeros(acc_ref.shape, F32)

    def step(ki, diagonal):
        start = pl.multiple_of(ki * tile, tile)
        kb = k_ref[pl.ds(start, tile), :]
        s = lax.dot_general(q, kb, _NT, preferred_element_type=F32) * scale
        ck = ck_ref[0, ki, pl.ds(h, 1), :]
        s = s + (cq - ck)
        if diagonal:
            r = lax.broadcasted_iota(jnp.int32, (tile, tile), 0)
            c = lax.broadcasted_iota(jnp.int32, (tile, tile), 1)
            s = jnp.where(c <= r, s, -jnp.inf)
        m_prev = m_ref[...]
        m_new = jnp.maximum(m_prev, jnp.max(s, axis=1, keepdims=True))
        alpha = jnp.exp(m_prev - m_new)
        p = jnp.exp(s - m_new)
        l_ref[...] = alpha * l_ref[...] + jnp.sum(p, axis=1, keepdims=True)
        pv = jnp.dot(p.astype(BF16), v_ref[pl.ds(start, tile), :], preferred_element_type=F32)
        acc_ref[...] = alpha * acc_ref[...] + pv
        m_ref[...] = m_new

    def body(ki, carry):
        step(ki, False)
        return carry

    lax.fori_loop(0, qi, body, 0)
    step(qi, True)
    o_ref[...] = acc_ref[...] / l_ref[...]


def _fox_prompt(qb, kb, vb, cq, ck, *, n_batch, seq, tile):
    m = qb.shape[0]
    nq = seq // tile
    return pl.pallas_call(
        functools.partial(_fox_prompt_kernel, tile=tile),
        grid=(n_batch, N_FOX_HEADS, nq),
        in_specs=[
            pl.BlockSpec((tile, HEAD_DIM), lambda n, h, i: (n * nq + i, h)),
            pl.BlockSpec((seq, HEAD_DIM), lambda n, h, i: (n, h)),
            pl.BlockSpec((seq, HEAD_DIM), lambda n, h, i: (n, h)),
            pl.BlockSpec((1, tile, N_FOX_HEADS), lambda n, h, i: (n, i, 0)),
            pl.BlockSpec((1, nq, N_FOX_HEADS, tile), lambda n, h, i: (n, 0, 0, 0)),
        ],
        out_specs=pl.BlockSpec((tile, HEAD_DIM), lambda n, h, i: (n * nq + i, h)),
        out_shape=jax.ShapeDtypeStruct((m, N_FOX_HEADS * HEAD_DIM), F32),
        scratch_shapes=[
            pltpu.VMEM((tile, 1), F32),
            pltpu.VMEM((tile, 1), F32),
            pltpu.VMEM((tile, HEAD_DIM), F32),
        ],
        compiler_params=_params(("arbitrary", "arbitrary", "arbitrary")),
        name="fox_prompt",
    )(qb, kb, vb, cq, ck)


def _fox_sample_kernel(q_ref, kn_ref, vn_ref, kc_ref, vc_ref, cq_ref, ckp_ref, ckn_ref, o_ref, *, past):
    scale = HEAD_DIM ** -0.5
    t = q_ref.shape[0]
    r = lax.broadcasted_iota(jnp.int32, (t, t), 0)
    c = lax.broadcasted_iota(jnp.int32, (t, t), 1)
    for hh in range(N_FOX_HEADS):
        sl = slice(hh * HEAD_DIM, (hh + 1) * HEAD_DIM)
        q = q_ref[:, sl]
        kp = _load_head_rows(kc_ref, 0, hh, past, N_FOX_HEADS).astype(BF16)
        vp = _load_head_rows(vc_ref, 0, hh, past, N_FOX_HEADS).astype(BF16)
        cq = cq_ref[0, :, hh:hh + 1]
        s1 = lax.dot_general(q, kp, _NT, preferred_element_type=F32) * scale
        s1 = s1 + (cq - ckp_ref[0, hh:hh + 1, :])
        s2 = lax.dot_general(q, kn_ref[:, sl], _NT, preferred_element_type=F32) * scale
        s2 = s2 + (cq - ckn_ref[0, hh:hh + 1, :])
        s2 = jnp.where(c <= r, s2, -jnp.inf)
        m = jnp.maximum(jnp.max(s1, axis=1, keepdims=True), jnp.max(s2, axis=1, keepdims=True))
        p1 = jnp.exp(s1 - m)
        p2 = jnp.exp(s2 - m)
        l = jnp.sum(p1, axis=1, keepdims=True) + jnp.sum(p2, axis=1, keepdims=True)
        o = (jnp.dot(p1.astype(BF16), vp, preferred_element_type=F32)
             + jnp.dot(p2.astype(BF16), vn_ref[:, sl], preferred_element_type=F32))
        o_ref[:, sl] = o / l


def _fox_sample(qb, knb, vnb, cache_k, cache_v, cq, ck_past, ck_new):
    n_batch, rows, _ = cache_k.shape
    past = rows // N_FOX_HEADS
    t = qb.shape[0] // n_batch
    width = qb.shape[1]
    tok = lambda b: (b, 0)
    lead = lambda b: (b, 0, 0)
    return pl.pallas_call(
        functools.partial(_fox_sample_kernel, past=past),
        grid=(n_batch,),
        in_specs=[
            pl.BlockSpec((t, width), tok),
            pl.BlockSpec((t, width), tok),
            pl.BlockSpec((t, width), tok),
            pl.BlockSpec((1, rows, HEAD_DIM), lead),
            pl.BlockSpec((1, rows, HEAD_DIM), lead),
            pl.BlockSpec((1, t, N_FOX_HEADS), lead),
            pl.BlockSpec((1, N_FOX_HEADS, past), lead),
            pl.BlockSpec((1, N_FOX_HEADS, t), lead),
        ],
        out_specs=pl.BlockSpec((t, width), tok),
        out_shape=jax.ShapeDtypeStruct(qb.shape, F32),
        compiler_params=_params(("arbitrary",)),
        name="fox_sample",
    )(qb, knb, vnb, cache_k, cache_v, cq, ck_past, ck_new)


def _mix_kernel(*refs, seg, nseg, tiles_per_seq, has_prev):
    if has_prev:
        (x_ref, gb_ref, gc_ref, hin_ref, pgc_ref, phin_ref, buf_ref, ya_ref, cw_ref, gco_ref, gao_ref,
         wout_ref, gx_ref, wxq_ref, x1_ref, qx_ref, cnew_ref) = refs
    else:
        (x_ref, gb_ref, gc_ref, hin_ref, buf_ref, ya_ref, cw_ref, gco_ref, gao_ref,
         wout_ref, gx_ref, wxq_ref, x1_ref, qx_ref, cnew_ref) = refs
    i = pl.program_id(0)
    u = gc_ref[...] * hin_ref[...]
    gb = gb_ref[...]
    w0 = cw_ref[0:1, :]
    w1 = cw_ref[1:2, :]
    w2 = cw_ref[2:3, :]
    rid = lax.broadcasted_iota(jnp.int32, (seg, u.shape[1]), 0)
    pieces = []
    for s in range(nseg):
        u_s = u[s * seg:(s + 1) * seg]
        b0 = buf_ref[s, 0:1, :]
        b1 = buf_ref[s, 1:2, :]
        if has_prev:
            pu = pgc_ref[...] * phin_ref[...]
            first = (i % tiles_per_seq) == 0
            b0 = jnp.where(first, b0, pu[6:7])
            b1 = jnp.where(first, b1, pu[7:8])
        r1 = jnp.where(rid == 0, b1, pltpu.roll(u_s, 1, 0))
        r2 = jnp.where(rid == 0, b0, jnp.where(rid == 1, b1, pltpu.roll(u_s, 2, 0)))
        y = w0 * r2 + w1 * r1 + w2 * u_s
        pieces.append(gb[s * seg:(s + 1) * seg] * y)
        cnew_ref[s] = u_s[seg - 2:seg]
    yc = pieces[0] if nseg == 1 else jnp.concatenate(pieces, axis=0)
    cat = jnp.concatenate([_rms(yc, gco_ref[...]).astype(BF16),
                           _rms(ya_ref[...], gao_ref[...]).astype(BF16)], axis=-1)
    x1 = x_ref[...] + jnp.dot(cat, wout_ref[...], preferred_element_type=F32)
    x1_ref[...] = x1
    xn = _rms(x1, gx_ref[...]).astype(BF16)
    qx_ref[...] = jnp.dot(xn, wxq_ref[...], preferred_element_type=F32).astype(BF16)


def _mix(x, zc, conv_buf, ya, conv_w, g_conv_out, g_attn_out, w_out, g_xattn, w_xq, *, tm, seq):
    m, d = x.shape
    dc = zc.shape[1] // 3
    dx = w_xq.shape[1]
    has_prev = seq > tm
    seg = tm if has_prev else seq
    nseg = tm // seg
    tiles_per_seq = max(seq // tm, 1)
    row = lambda i: (i, 0)
    const = lambda i: (0, 0)
    in_specs = [
        pl.BlockSpec((tm, d), row),
        pl.BlockSpec((tm, dc), lambda i: (i, 0)),
        pl.BlockSpec((tm, dc), lambda i: (i, 1)),
        pl.BlockSpec((tm, dc), lambda i: (i, 2)),
    ]
    args = [x, zc, zc, zc]
    if has_prev:
        prev = lambda col: (lambda i: (jnp.maximum(i * (tm // 8) - 1, 0), col))
        in_specs += [pl.BlockSpec((8, dc), prev(1)), pl.BlockSpec((8, dc), prev(2))]
        args += [zc, zc]
        buf_map = lambda i: (i // tiles_per_seq, 0, 0)
    else:
        buf_map = lambda i: (i, 0, 0)
    in_specs += [
        pl.BlockSpec((nseg, CONV_WIDTH - 1, dc), buf_map),
        pl.BlockSpec((tm, ya.shape[1]), row),
        pl.BlockSpec((CONV_WIDTH, dc), const),
        pl.BlockSpec((1, dc), const),
        pl.BlockSpec((1, ya.shape[1]), const),
        pl.BlockSpec(w_out.shape, const),
        pl.BlockSpec((1, d), const),
        pl.BlockSpec(w_xq.shape, const),
    ]
    args += [conv_buf, ya, conv_w, g_conv_out, g_attn_out, w_out, g_xattn, w_xq]
    return pl.pallas_call(
        functools.partial(_mix_kernel, seg=seg, nseg=nseg, tiles_per_seq=tiles_per_seq, has_prev=has_prev),
        grid=(m // tm,),
        in_specs=in_specs,
        out_specs=[
            pl.BlockSpec((tm, d), row),
            pl.BlockSpec((tm, dx), row),
            pl.BlockSpec((nseg, CONV_WIDTH - 1, dc), buf_map),
        ],
        out_shape=[
            jax.ShapeDtypeStruct((m, d), F32),
            jax.ShapeDtypeStruct((m, dx), BF16),
            jax.ShapeDtypeStruct(conv_buf.shape, F32),
        ],
        compiler_params=_params(("arbitrary",)),
        name="mix",
    )(*args)


def _memkv_kernel(x_ref, g_ref, w_ref, o_ref):
    xn = _rms(x_ref[...], g_ref[...]).astype(BF16)
    _store_head_rows(o_ref.at[0], jnp.dot(xn, w_ref[...], preferred_element_type=F32), N_X_HEADS)


def _memkv(mem, g, w):
    m, d = mem.shape
    half = w.shape[1] // 2
    return pl.pallas_call(
        _memkv_kernel,
        grid=(2,),
        in_specs=[
            pl.BlockSpec((m, d), lambda j: (0, 0)),
            pl.BlockSpec((1, d), lambda j: (0, 0)),
            pl.BlockSpec((d, half), lambda j: (0, j)),
        ],
        out_specs=pl.BlockSpec((1, m * N_X_HEADS, HEAD_DIM), lambda j: (j, 0, 0)),
        out_shape=jax.ShapeDtypeStruct((2, m * N_X_HEADS, HEAD_DIM), F32),
        compiler_params=_params(("arbitrary",)),
        name="memkv",
    )(mem, g, w)


def _xattn_kernel(x1_ref, qx_ref, mk_ref, mv_ref, wxo_ref, g_ref, x2_ref, h_ref, o_scr, *, seg, nseg):
    scale = HEAD_DIM ** -0.5
    n_mem = mk_ref.shape[1] // N_X_HEADS
    for s in range(nseg):
        rows = slice(s * seg, (s + 1) * seg)
        for hh in range(N_X_HEADS):
            sl = slice(hh * HEAD_DIM, (hh + 1) * HEAD_DIM)
            q = qx_ref[rows, sl]
            kh = _load_head_rows(mk_ref, s, hh, n_mem, N_X_HEADS).astype(BF16)
            vh = _load_head_rows(mv_ref, s, hh, n_mem, N_X_HEADS).astype(BF16)
            sc = lax.dot_general(q, kh, _NT, preferred_element_type=F32) * scale
            m = jnp.max(sc, axis=1, keepdims=True)
            p = jnp.exp(sc - m)
            l = jnp.sum(p, axis=1, keepdims=True)
            o = jnp.dot(p.astype(BF16), vh, preferred_element_type=F32) / l
            o_scr[rows, sl] = o.astype(BF16)
    x2 = x1_ref[...] + jnp.dot(o_scr[...], wxo_ref[...], preferred_element_type=F32)
    x2_ref[...] = x2
    h_ref[...] = _rms(x2, g_ref[...]).astype(BF16)


def _xattn(x1, qx, mk, mv, w_xo, g_mlp, *, tm, seq):
    m, d = x1.shape
    dx = qx.shape[1]
    mem_rows = mk.shape[1]
    seg = min(tm, seq)
    nseg = tm // seg
    tiles_per_seq = max(seq // tm, 1)
    row = lambda i: (i, 0)
    const = lambda i: (0, 0)
    mem_map = (lambda i: (i // tiles_per_seq, 0, 0)) if nseg == 1 else (lambda i: (i, 0, 0))
    return pl.pallas_call(
        functools.partial(_xattn_kernel, seg=seg, nseg=nseg),
        grid=(m // tm,),
        in_specs=[
            pl.BlockSpec((tm, d), row),
            pl.BlockSpec((tm, dx), row),
            pl.BlockSpec((nseg, mem_rows, HEAD_DIM), mem_map),
            pl.BlockSpec((nseg, mem_rows, HEAD_DIM), mem_map),
            pl.BlockSpec(w_xo.shape, const),
            pl.BlockSpec((1, d), const),
        ],
        out_specs=[pl.BlockSpec((tm, d), row), pl.BlockSpec((tm, d), row)],
        out_shape=[jax.ShapeDtypeStruct((m, d), F32), jax.ShapeDtypeStruct((m, d), BF16)],
        scratch_shapes=[pltpu.VMEM((tm, dx), BF16)],
        compiler_params=_params(("arbitrary",)),
        name="xattn",
    )(x1, qx, mk, mv, w_xo, g_mlp)


def _mlp_kernel(h_ref, x2_ref, wu_ref, wd_ref, g_ref, y_ref, acc_ref):
    j = pl.program_id(1)

    @pl.when(j == 0)
    def _():
        acc_ref[...] = jnp.zeros(acc_ref.shape, F32)

    a = jnp.dot(h_ref[...], wu_ref[...], preferred_element_type=F32)
    a = jnp.square(jnp.maximum(a, 0.0)).astype(BF16)
    acc_ref[...] += jnp.dot(a, wd_ref[...], preferred_element_type=F32)

    @pl.when(j == pl.num_programs(1) - 1)
    def _():
        y_ref[...] = _rms(x2_ref[...] + acc_ref[...], g_ref[...])


def _mlp(h, x2, w_up, w_down, g_final, *, tm, tf):
    m, d = h.shape
    ff = w_up.shape[1]
    row = lambda i, j: (i, 0)
    return pl.pallas_call(
        _mlp_kernel,
        grid=(m // tm, ff // tf),
        in_specs=[
            pl.BlockSpec((tm, d), row),
            pl.BlockSpec((tm, d), row),
            pl.BlockSpec((d, tf), lambda i, j: (0, j)),
            pl.BlockSpec((tf, d), lambda i, j: (j, 0)),
            pl.BlockSpec((1, d), lambda i, j: (0, 0)),
        ],
        out_specs=pl.BlockSpec((tm, d), row),
        out_shape=jax.ShapeDtypeStruct((m, d), F32),
        scratch_shapes=[pltpu.VMEM((tm, d), F32)],
        compiler_params=_params(("arbitrary", "arbitrary")),
        name="mlp",
    )(h, x2, w_up, w_down, g_final)


def kernel(x_prompt, x_sample, cache_k, cache_v, cache_logf, cache_conv, cache_mem_k, cache_mem_v, mem_prompt, g_mix, w_in, b_f, conv_w, g_conv_out, g_attn_out, w_out, g_xattn, g_mem, w_xq, w_xkv, w_xo, g_mlp, w_up, w_down, g_final):
    depth = w_in.shape[0]
    assert depth == 1, "single-layer trunk"
    nb, seq, d = x_prompt.shape
    db, dseq, _ = x_sample.shape
    past = cache_k.shape[2]
    heads, hd = cache_k.shape[3], cache_k.shape[4]
    assert (heads, hd) == (N_FOX_HEADS, HEAD_DIM)
    d_attn = heads * hd
    d_conv = cache_conv.shape[-1]
    n_mem, xh, xhd = cache_mem_k.shape[2:]
    assert (xh, xhd) == (N_X_HEADS, HEAD_DIM)
    d_x = xh * xhd
    n_main = 3 * d_conv + 3 * d_attn
    assert d_conv == d_attn and w_in.shape[2] == n_main + heads

    w_main = w_in[0, :, :n_main].astype(BF16)
    w_f = jnp.pad(w_in[0, :, n_main:], ((0, 0), (0, LANES - heads))).astype(BF16)
    bf = b_f[0].reshape(1, heads)
    w_out_b = w_out[0].astype(BF16)
    w_xq_b = w_xq[0].astype(BF16)
    w_xkv_b = w_xkv[0].astype(BF16)
    w_xo_b = w_xo[0].astype(BF16)
    w_up_b = w_up[0].astype(BF16)
    w_down_b = w_down[0].astype(BF16)
    row = lambda g: g.reshape(1, -1)

    def trunk(x, conv_buf, seq_len, attention, mk, mv, tm_mix, tm_x):
        zc, qb, k, v, kb, vb, lf = _proj(x, row(g_mix[0]), w_main, w_f, bf, tm=512)
        ya = attention(qb, kb, vb, lf)
        x1, qx, conv_new = _mix(x, zc, conv_buf, ya, conv_w[0], row(g_conv_out[0]), row(g_attn_out[0]),
                                w_out_b, row(g_xattn[0]), w_xq_b, tm=tm_mix, seq=seq_len)
        x2, h = _xattn(x1, qx, mk, mv, w_xo_b, row(g_mlp[0]), tm=tm_x, seq=seq_len)
        y = _mlp(h, x2, w_up_b, w_down_b, row(g_final), tm=512, tf=1024)
        return y, k, v, lf, conv_new

    tile = 512
    nq = seq // tile

    def prompt_attention(qb, kb, vb, lf):
        lft = lf.reshape(nb, seq, heads).transpose(0, 2, 1).reshape(nb * heads, seq)
        ct = _cumsum_lanes(lft).reshape(nb, heads, seq)
        cq = ct.transpose(0, 2, 1)
        ck = ct.reshape(nb, heads, nq, tile).transpose(0, 2, 1, 3)
        return _fox_prompt(qb, kb, vb, cq, ck, n_batch=nb, seq=seq, tile=tile)

    kv = _memkv(mem_prompt.reshape(nb * n_mem, d), row(g_mem[0]), w_xkv_b)
    mk_p = kv[0].reshape(nb, n_mem * xh, xhd)
    mv_p = kv[1].reshape(nb, n_mem * xh, xhd)
    conv0 = jnp.zeros((nb, CONV_WIDTH - 1, d_conv), F32)
    y_p, k_p, v_p, lf_p, conv_p = trunk(x_prompt.reshape(nb * seq, d), conv0, seq, prompt_attention,
                                        mk_p, mv_p, 256, 512)

    pad = (-(past + dseq)) % LANES

    def sample_attention(qb, kb, vb, lf):
        lft_past = cache_logf[0].transpose(0, 2, 1).reshape(db * heads, past)
        lft_new = lf.reshape(db, dseq, heads).transpose(0, 2, 1).reshape(db * heads, dseq)
        ct = _cumsum_lanes(jnp.concatenate(
            [lft_past, lft_new, jnp.zeros((db * heads, pad), F32)], axis=1))
        ck_past = ct[:, :past].reshape(db, heads, past)
        ck_new = ct[:, past:past + dseq].reshape(db, heads, dseq)
        cq = ck_new.transpose(0, 2, 1)
        return _fox_sample(qb, kb, vb, cache_k.reshape(db, past * heads, hd),
                           cache_v.reshape(db, past * heads, hd), cq, ck_past, ck_new)

    y_s, k_s, v_s, lf_s, conv_s = trunk(x_sample.reshape(db * dseq, d), cache_conv[0], dseq, sample_attention,
                                        cache_mem_k.reshape(db, n_mem * xh, xhd),
                                        cache_mem_v.reshape(db, n_mem * xh, xhd), 256, 256)

    return (y_p.reshape(nb, seq, d),
            y_s.reshape(db, dseq, d),
            k_p.reshape(1, nb, seq, heads, hd),
            v_p.reshape(1, nb, seq, heads, hd),
            lf_p.reshape(1, nb, seq, heads),
            conv_p[None],
            mk_p.reshape(1, nb, n_mem, xh, xhd),
            mv_p.reshape(1, nb, n_mem, xh, xhd),
            k_s.reshape(1, db, dseq, heads, hd),
            v_s.reshape(1, db, dseq, heads, hd),
            lf_s.reshape(1, db, dseq, heads),
            conv_s[None])
```

```python
import functools

import jax
import jax.numpy as jnp
from jax import lax
from jax.experimental import pallas as pl
from jax.experimental.pallas import tpu as pltpu

F32 = jnp.float32
BF16 = jnp.bfloat16

RMS_EPS = 1e-6
HEAD_DIM = 128
N_FOX_HEADS = 8
N_X_HEADS = 4
CONV_WIDTH = 3
LANES = 128
VMEM_LIMIT = 52 * 1024 * 1024

_NT = (((1,), (1,)), ((), ()))


def _params(semantics):
    return pltpu.CompilerParams(dimension_semantics=semantics, vmem_limit_bytes=VMEM_LIMIT)


def _rms(x, g):
    ms = jnp.mean(x * x, axis=-1, keepdims=True)
    return x * lax.rsqrt(ms + RMS_EPS) * g


def _log_sigmoid(x):
    return jnp.minimum(x, 0.0) - jnp.log1p(jnp.exp(-jnp.abs(x)))


def _store_head_rows(ref, val, heads):
    n = val.shape[0]
    for h in range(heads):
        ref[pl.ds(h, n, stride=heads), :] = val[:, h * HEAD_DIM:(h + 1) * HEAD_DIM]


def _load_head_rows(ref, lead, h, n, heads):
    return ref[lead, pl.ds(h, n, stride=heads), :]


def _proj_kernel(x_ref, g_ref, w_ref, wf_ref, bf_ref,
                 zc_ref, q_ref, k_ref, v_ref, kb_ref, vb_ref, lf_ref, xn_ref):
    j = pl.program_id(1)

    @pl.when(j == 0)
    def _():
        xn = _rms(x_ref[...], g_ref[...]).astype(BF16)
        xn_ref[...] = xn
        fz = jnp.dot(xn, wf_ref[...], preferred_element_type=F32)
        lf_ref[...] = _log_sigmoid(fz[:, :N_FOX_HEADS] + bf_ref[...])

    res = jnp.dot(xn_ref[...], w_ref[...], preferred_element_type=F32)

    @pl.when(j < 3)
    def _():
        zc_ref[...] = res

    @pl.when(j == 3)
    def _():
        q_ref[...] = res.astype(BF16)

    @pl.when(j == 4)
    def _():
        _store_head_rows(k_ref, res, N_FOX_HEADS)
        kb_ref[...] = res.astype(BF16)

    @pl.when(j == 5)
    def _():
        _store_head_rows(v_ref, res, N_FOX_HEADS)
        vb_ref[...] = res.astype(BF16)


def _proj(x, g, w_main, w_f, b_f, *, tm):
    m, d = x.shape
    tn = w_main.shape[1] // 6
    row = lambda i, j: (i, 0)
    return pl.pallas_call(
        _proj_kernel,
        grid=(m // tm, 6),
        in_specs=[
            pl.BlockSpec((tm, d), row),
            pl.BlockSpec((1, d), lambda i, j: (0, 0)),
            pl.BlockSpec((d, tn), lambda i, j: (0, j)),
            pl.BlockSpec((d, LANES), lambda i, j: (0, 0)),
            pl.BlockSpec((1, N_FOX_HEADS), lambda i, j: (0, 0)),
        ],
        out_specs=[
            pl.BlockSpec((tm, tn), lambda i, j: (i, jnp.minimum(j, 2))),
            pl.BlockSpec((tm, tn), row),
            pl.BlockSpec((tm * N_FOX_HEADS, HEAD_DIM), row),
            pl.BlockSpec((tm * N_FOX_HEADS, HEAD_DIM), row),
            pl.BlockSpec((tm, tn), row),
            pl.BlockSpec((tm, tn), row),
            pl.BlockSpec((tm, N_FOX_HEADS), row),
        ],
        out_shape=[
            jax.ShapeDtypeStruct((m, 3 * tn), F32),
            jax.ShapeDtypeStruct((m, tn), BF16),
            jax.ShapeDtypeStruct((m * N_FOX_HEADS, HEAD_DIM), F32),
            jax.ShapeDtypeStruct((m * N_FOX_HEADS, HEAD_DIM), F32),
            jax.ShapeDtypeStruct((m, tn), BF16),
            jax.ShapeDtypeStruct((m, tn), BF16),
            jax.ShapeDtypeStruct((m, N_FOX_HEADS), F32),
        ],
        scratch_shapes=[pltpu.VMEM((tm, d), BF16)],
        compiler_params=_params(("arbitrary", "arbitrary")),
        name="proj",
    )(x, g, w_main, w_f, b_f)


def _cumsum_kernel(a_ref, o_ref):
    rows, length = a_ref.shape
    r = lax.broadcasted_iota(jnp.int32, (LANES, LANES), 0)
    c = lax.broadcasted_iota(jnp.int32, (LANES, LANES), 1)
    upper = (r <= c).astype(BF16)
    carry = jnp.zeros((rows, 1), F32)
    for b in range(length // LANES):
        a = a_ref[:, b * LANES:(b + 1) * LANES]
        hi = a.astype(BF16)
        r1 = a - hi.astype(F32)
        mid = r1.astype(BF16)
        lo = (r1 - mid.astype(F32)).astype(BF16)
        s = (jnp.dot(hi, upper, preferred_element_type=F32)
             + jnp.dot(mid, upper, preferred_element_type=F32)
             + jnp.dot(lo, upper, preferred_element_type=F32)) + carry
        o_ref[:, b * LANES:(b + 1) * LANES] = s
        carry = s[:, LANES - 1:LANES]


def _cumsum_lanes(a):
    return pl.pallas_call(
        _cumsum_kernel,
        out_shape=jax.ShapeDtypeStruct(a.shape, F32),
        compiler_params=pltpu.CompilerParams(vmem_limit_bytes=VMEM_LIMIT),
        name="cumsum",
    )(a)


def _split3(x):
    hi = x.astype(BF16).astype(F32)
    r = x - hi
    mid = r.astype(BF16).astype(F32)
    lo = (r - mid).astype(BF16).astype(F32)
    return hi, mid, lo


def _fox_prompt_kernel(q_ref, k_ref, v_ref, cq_ref, ck_ref, o_ref, kaug_ref, vt_ref, qaug_ref, acc_ref, s0_ref, s1_ref,
                       *, tile):
    h = pl.program_id(1)
    qi = pl.program_id(2)
    nq = pl.num_programs(2)
    inv_scale = HEAD_DIM ** 0.5
    exp2_scale = (HEAD_DIM ** -0.5) * 1.4426950408889634

    @pl.when(qi == 0)
    def _():
        lane8 = lax.broadcasted_iota(jnp.int32, (tile, N_FOX_HEADS), 1)
        lane = lax.broadcasted_iota(jnp.int32, (tile, HEAD_DIM), 1)

        def prep(c, carry):
            rows = pl.ds(pl.multiple_of(c * tile, tile), tile)
            kaug_ref[rows, 0:HEAD_DIM] = k_ref[rows, :]
            col = jnp.sum(jnp.where(lane8 == h, ck_ref[0, rows, :], 0.0), axis=1, keepdims=True) * inv_scale
            hi, mid, lo = _split3(col)
            aug = jnp.where(lane == 0, hi, jnp.where(lane == 1, mid, jnp.where(lane == 2, lo,
                            jnp.where(lane < 6, 1.0, 0.0))))
            kaug_ref[rows, HEAD_DIM:2 * HEAD_DIM] = aug.astype(BF16)
            vt_ref[c] = v_ref[rows, :].astype(F32).T.astype(BF16)
            return carry

        lax.fori_loop(0, nq, prep, 0)

    qaug_ref[0:HEAD_DIM, :] = q_ref[...].astype(F32).T.astype(BF16)
    hi, mid, lo = _split3(cq_ref[0, pl.ds(h, 1), :] * inv_scale)
    sub = lax.broadcasted_iota(jnp.int32, (HEAD_DIM, tile), 0)
    qaug_ref[HEAD_DIM:2 * HEAD_DIM, :] = jnp.where(
        sub < 3, -1.0, jnp.where(sub == 3, hi, jnp.where(sub == 4, mid, jnp.where(sub == 5, lo, 0.0)))).astype(BF16)
    acc_ref[...] = jnp.zeros(acc_ref.shape, F32)

    def scores(ki, dst_ref):
        rows = pl.ds(pl.multiple_of(ki * tile, tile), tile)
        dst_ref[...] = jnp.dot(kaug_ref[rows, :], qaug_ref[...], preferred_element_type=F32)

    def step(ki, m_prev, l_prev, cur_ref, nxt_ref, diagonal):
        if not diagonal:
            scores(ki + 1, nxt_ref)
        s = cur_ref[...]
        if diagonal:
            r = lax.broadcasted_iota(jnp.int32, (tile, tile), 0)
            c = lax.broadcasted_iota(jnp.int32, (tile, tile), 1)
            s = jnp.where(r <= c, s, -jnp.inf)
        m_new = jnp.maximum(m_prev, jnp.max(s, axis=0, keepdims=True))
        alpha = jnp.exp2((m_prev - m_new) * exp2_scale)
        p = jnp.exp2((s - m_new) * exp2_scale)
        l_new = alpha * l_prev + jnp.sum(p, axis=0, keepdims=True)
        pv = jnp.dot(vt_ref[ki], p.astype(BF16), preferred_element_type=F32)
        acc_ref[...] = alpha * acc_ref[...] + pv
        return m_new, l_new

    m0 = jnp.full((1, tile), -jnp.inf, F32)
    l0 = jnp.zeros((1, tile), F32)
    scores(0, s0_ref)

    def pick(ki, ml, diagonal):
        return lax.cond(ki % 2 == 0,
                        lambda: step(ki, ml[0], ml[1], s0_ref, s1_ref, diagonal),
                        lambda: step(ki, ml[0], ml[1], s1_ref, s0_ref, diagonal))

    ml = lax.fori_loop(0, qi, lambda ki, ml: pick(ki, ml, False), (m0, l0))
    _, l = pick(qi, ml, True)
    o_ref[...] = (acc_ref[...] / l).T


def _fox_prompt(qb, kb, vb, cq, ck, *, n_batch, seq, tile):
    m = qb.shape[0]
    nq = seq // tile
    return pl.pallas_call(
        functools.partial(_fox_prompt_kernel, tile=tile),
        grid=(n_batch, N_FOX_HEADS, nq),
        in_specs=[
            pl.BlockSpec((tile, HEAD_DIM), lambda n, h, i: (n * nq + i, h)),
            pl.BlockSpec((seq, HEAD_DIM), lambda n, h, i: (n, h)),
            pl.BlockSpec((seq, HEAD_DIM), lambda n, h, i: (n, h)),
            pl.BlockSpec((1, N_FOX_HEADS, tile), lambda n, h, i: (n, 0, i)),
            pl.BlockSpec((1, seq, N_FOX_HEADS), lambda n, h, i: (n, 0, 0)),
        ],
        out_specs=pl.BlockSpec((tile, HEAD_DIM), lambda n, h, i: (n * nq + i, h)),
        out_shape=jax.ShapeDtypeStruct((m, N_FOX_HEADS * HEAD_DIM), F32),
        scratch_shapes=[
            pltpu.VMEM((seq, 2 * HEAD_DIM), BF16),
            pltpu.VMEM((nq, HEAD_DIM, tile), BF16),
            pltpu.VMEM((2 * HEAD_DIM, tile), BF16),
            pltpu.VMEM((HEAD_DIM, tile), F32),
            pltpu.VMEM((tile, tile), F32),
            pltpu.VMEM((tile, tile), F32),
        ],
        compiler_params=_params(("arbitrary", "arbitrary", "arbitrary")),
        name="fox_prompt",
    )(qb, kb, vb, cq, ck)


def _fox_sample_kernel(q_ref, kn_ref, vn_ref, kc_ref, vc_ref, cq_ref, ckp_ref, ckn_ref, o_ref, *, past):
    scale = HEAD_DIM ** -0.5
    t = q_ref.shape[0]
    r = lax.broadcasted_iota(jnp.int32, (t, t), 0)
    c = lax.broadcasted_iota(jnp.int32, (t, t), 1)
    for hh in range(N_FOX_HEADS):
        sl = slice(hh * HEAD_DIM, (hh + 1) * HEAD_DIM)
        q = q_ref[:, sl]
        kp = _load_head_rows(kc_ref, 0, hh, past, N_FOX_HEADS).astype(BF16)
        vp = _load_head_rows(vc_ref, 0, hh, past, N_FOX_HEADS).astype(BF16)
        cq = cq_ref[0, :, hh:hh + 1]
        s1 = lax.dot_general(q, kp, _NT, preferred_element_type=F32) * scale
        s1 = s1 + (cq - ckp_ref[0, hh:hh + 1, :])
        s2 = lax.dot_general(q, kn_ref[:, sl], _NT, preferred_element_type=F32) * scale
        s2 = s2 + (cq - ckn_ref[0, hh:hh + 1, :])
        s2 = jnp.where(c <= r, s2, -jnp.inf)
        m = jnp.maximum(jnp.max(s1, axis=1, keepdims=True), jnp.max(s2, axis=1, keepdims=True))
        p1 = jnp.exp(s1 - m)
        p2 = jnp.exp(s2 - m)
        l = jnp.sum(p1, axis=1, keepdims=True) + jnp.sum(p2, axis=1, keepdims=True)
        o = (jnp.dot(p1.astype(BF16), vp, preferred_element_type=F32)
             + jnp.dot(p2.astype(BF16), vn_ref[:, sl], preferred_element_type=F32))
        o_ref[:, sl] = o / l


def _fox_sample(qb, knb, vnb, cache_k, cache_v, cq, ck_past, ck_new):
    n_batch, rows, _ = cache_k.shape
    past = rows // N_FOX_HEADS
    t = qb.shape[0] // n_batch
    width = qb.shape[1]
    tok = lambda b: (b, 0)
    lead = lambda b: (b, 0, 0)
    return pl.pallas_call(
        functools.partial(_fox_sample_kernel, past=past),
        grid=(n_batch,),
        in_specs=[
            pl.BlockSpec((t, width), tok),
            pl.BlockSpec((t, width), tok),
            pl.BlockSpec((t, width), tok),
            pl.BlockSpec((1, rows, HEAD_DIM), lead),
            pl.BlockSpec((1, rows, HEAD_DIM), lead),
            pl.BlockSpec((1, t, N_FOX_HEADS), lead),
            pl.BlockSpec((1, N_FOX_HEADS, past), lead),
            pl.BlockSpec((1, N_FOX_HEADS, t), lead),
        ],
        out_specs=pl.BlockSpec((t, width), tok),
        out_shape=jax.ShapeDtypeStruct(qb.shape, F32),
        compiler_params=_params(("arbitrary",)),
        name="fox_sample",
    )(qb, knb, vnb, cache_k, cache_v, cq, ck_past, ck_new)


def _mix_kernel(*refs, seg, nseg, tiles_per_seq, has_prev):
    if has_prev:
        (x_ref, gb_ref, gc_ref, hin_ref, pgc_ref, phin_ref, buf_ref, ya_ref, cw_ref, gco_ref, gao_ref,
         wout_ref, gx_ref, wxq_ref, x1_ref, qx_ref, cnew_ref) = refs
    else:
        (x_ref, gb_ref, gc_ref, hin_ref, buf_ref, ya_ref, cw_ref, gco_ref, gao_ref,
         wout_ref, gx_ref, wxq_ref, x1_ref, qx_ref, cnew_ref) = refs
    i = pl.program_id(0)
    u = gc_ref[...] * hin_ref[...]
    gb = gb_ref[...]
    w0 = cw_ref[0:1, :]
    w1 = cw_ref[1:2, :]
    w2 = cw_ref[2:3, :]
    rid = lax.broadcasted_iota(jnp.int32, (seg, u.shape[1]), 0)
    pieces = []
    for s in range(nseg):
        u_s = u[s * seg:(s + 1) * seg]
        b0 = buf_ref[s, 0:1, :]
        b1 = buf_ref[s, 1:2, :]
        if has_prev:
            pu = pgc_ref[...] * phin_ref[...]
            first = (i % tiles_per_seq) == 0
            b0 = jnp.where(first, b0, pu[6:7])
            b1 = jnp.where(first, b1, pu[7:8])
        r1 = jnp.where(rid == 0, b1, pltpu.roll(u_s, 1, 0))
        r2 = jnp.where(rid == 0, b0, jnp.where(rid == 1, b1, pltpu.roll(u_s, 2, 0)))
        y = w0 * r2 + w1 * r1 + w2 * u_s
        pieces.append(gb[s * seg:(s + 1) * seg] * y)
        cnew_ref[s] = u_s[seg - 2:seg]
    yc = pieces[0] if nseg == 1 else jnp.concatenate(pieces, axis=0)
    cat = jnp.concatenate([_rms(yc, gco_ref[...]).astype(BF16),
                           _rms(ya_ref[...], gao_ref[...]).astype(BF16)], axis=-1)
    x1 = x_ref[...] + jnp.dot(cat, wout_ref[...], preferred_element_type=F32)
    x1_ref[...] = x1
    xn = _rms(x1, gx_ref[...]).astype(BF16)
    qx_ref[...] = jnp.dot(xn, wxq_ref[...], preferred_element_type=F32).astype(BF16)


def _mix(x, zc, conv_buf, ya, conv_w, g_conv_out, g_attn_out, w_out, g_xattn, w_xq, *, tm, seq):
    m, d = x.shape
    dc = zc.shape[1] // 3
    dx = w_xq.shape[1]
    has_prev = seq > tm
    seg = tm if has_prev else seq
    nseg = tm // seg
    tiles_per_seq = max(seq // tm, 1)
    row = lambda i: (i, 0)
    const = lambda i: (0, 0)
    in_specs = [
        pl.BlockSpec((tm, d), row),
        pl.BlockSpec((tm, dc), lambda i: (i, 0)),
        pl.BlockSpec((tm, dc), lambda i: (i, 1)),
        pl.BlockSpec((tm, dc), lambda i: (i, 2)),
    ]
    args = [x, zc, zc, zc]
    if has_prev:
        prev = lambda col: (lambda i: (jnp.maximum(i * (tm // 8) - 1, 0), col))
        in_specs += [pl.BlockSpec((8, dc), prev(1)), pl.BlockSpec((8, dc), prev(2))]
        args += [zc, zc]
        buf_map = lambda i: (i // tiles_per_seq, 0, 0)
    else:
        buf_map = lambda i: (i, 0, 0)
    in_specs += [
        pl.BlockSpec((nseg, CONV_WIDTH - 1, dc), buf_map),
        pl.BlockSpec((tm, ya.shape[1]), row),
        pl.BlockSpec((CONV_WIDTH, dc), const),
        pl.BlockSpec((1, dc), const),
        pl.BlockSpec((1, ya.shape[1]), const),
        pl.BlockSpec(w_out.shape, const),
        pl.BlockSpec((1, d), const),
        pl.BlockSpec(w_xq.shape, const),
    ]
    args += [conv_buf, ya, conv_w, g_conv_out, g_attn_out, w_out, g_xattn, w_xq]
    return pl.pallas_call(
        functools.partial(_mix_kernel, seg=seg, nseg=nseg, tiles_per_seq=tiles_per_seq, has_prev=has_prev),
        grid=(m // tm,),
        in_specs=in_specs,
        out_specs=[
            pl.BlockSpec((tm, d), row),
            pl.BlockSpec((tm, dx), row),
            pl.BlockSpec((nseg, CONV_WIDTH - 1, dc), buf_map),
        ],
        out_shape=[
            jax.ShapeDtypeStruct((m, d), F32),
            jax.ShapeDtypeStruct((m, dx), BF16),
            jax.ShapeDtypeStruct(conv_buf.shape, F32),
        ],
        compiler_params=_params(("arbitrary",)),
        name="mix",
    )(*args)


def _memkv_kernel(x_ref, g_ref, w_ref, o_ref):
    xn = _rms(x_ref[...], g_ref[...]).astype(BF16)
    _store_head_rows(o_ref.at[0], jnp.dot(xn, w_ref[...], preferred_element_type=F32), N_X_HEADS)


def _memkv(mem, g, w):
    m, d = mem.shape
    half = w.shape[1] // 2
    return pl.pallas_call(
        _memkv_kernel,
        grid=(2,),
        in_specs=[
            pl.BlockSpec((m, d), lambda j: (0, 0)),
            pl.BlockSpec((1, d), lambda j: (0, 0)),
            pl.BlockSpec((d, half), lambda j: (0, j)),
        ],
        out_specs=pl.BlockSpec((1, m * N_X_HEADS, HEAD_DIM), lambda j: (j, 0, 0)),
        out_shape=jax.ShapeDtypeStruct((2, m * N_X_HEADS, HEAD_DIM), F32),
        compiler_params=_params(("arbitrary",)),
        name="memkv",
    )(mem, g, w)


def _xattn_kernel(x1_ref, qx_ref, mk_ref, mv_ref, wxo_ref, g_ref, x2_ref, h_ref, o_scr, *, seg, nseg):
    scale = HEAD_DIM ** -0.5
    n_mem = mk_ref.shape[1] // N_X_HEADS
    for s in range(nseg):
        rows = slice(s * seg, (s + 1) * seg)
        for hh in range(N_X_HEADS):
            sl = slice(hh * HEAD_DIM, (hh + 1) * HEAD_DIM)
            q = qx_ref[rows, sl]
            kh = _load_head_rows(mk_ref, s, hh, n_mem, N_X_HEADS).astype(BF16)
            vh = _load_head_rows(mv_ref, s, hh, n_mem, N_X_HEADS).astype(BF16)
            sc = lax.dot_general(q, kh, _NT, preferred_element_type=F32) * scale
            m = jnp.max(sc, axis=1, keepdims=True)
            p = jnp.exp(sc - m)
            l = jnp.sum(p, axis=1, keepdims=True)
            o = jnp.dot(p.astype(BF16), vh, preferred_element_type=F32) / l
            o_scr[rows, sl] = o.astype(BF16)
    x2 = x1_ref[...] + jnp.dot(o_scr[...], wxo_ref[...], preferred_element_type=F32)
    x2_ref[...] = x2
    h_ref[...] = _rms(x2, g_ref[...]).astype(BF16)


def _xattn(x1, qx, mk, mv, w_xo, g_mlp, *, tm, seq):
    m, d = x1.shape
    dx = qx.shape[1]
    mem_rows = mk.shape[1]
    seg = min(tm, seq)
    nseg = tm // seg
    tiles_per_seq = max(seq // tm, 1)
    row = lambda i: (i, 0)
    const = lambda i: (0, 0)
    mem_map = (lambda i: (i // tiles_per_seq, 0, 0)) if nseg == 1 else (lambda i: (i, 0, 0))
    return pl.pallas_call(
        functools.partial(_xattn_kernel, seg=seg, nseg=nseg),
        grid=(m // tm,),
        in_specs=[
            pl.BlockSpec((tm, d), row),
            pl.BlockSpec((tm, dx), row),
            pl.BlockSpec((nseg, mem_rows, HEAD_DIM), mem_map),
            pl.BlockSpec((nseg, mem_rows, HEAD_DIM), mem_map),
            pl.BlockSpec(w_xo.shape, const),
            pl.BlockSpec((1, d), const),
        ],
        out_specs=[pl.BlockSpec((tm, d), row), pl.BlockSpec((tm, d), row)],
        out_shape=[jax.ShapeDtypeStruct((m, d), F32), jax.ShapeDtypeStruct((m, d), BF16)],
        scratch_shapes=[pltpu.VMEM((tm, dx), BF16)],
        compiler_params=_params(("arbitrary",)),
        name="xattn",
    )(x1, qx, mk, mv, w_xo, g_mlp)


def _mlp_kernel(h_ref, x2_ref, wu_ref, wd_ref, g_ref, y_ref, acc_ref):
    j = pl.program_id(1)

    @pl.when(j == 0)
    def _():
        acc_ref[...] = jnp.zeros(acc_ref.shape, F32)

    a = jnp.dot(h_ref[...], wu_ref[...], preferred_element_type=F32)
    a = jnp.square(jnp.maximum(a, 0.0)).astype(BF16)
    acc_ref[...] += jnp.dot(a, wd_ref[...], preferred_element_type=F32)

    @pl.when(j == pl.num_programs(1) - 1)
    def _():
        y_ref[...] = _rms(x2_ref[...] + acc_ref[...], g_ref[...])


def _mlp(h, x2, w_up, w_down, g_final, *, tm, tf):
    m, d = h.shape
    ff = w_up.shape[1]
    row = lambda i, j: (i, 0)
    return pl.pallas_call(
        _mlp_kernel,
        grid=(m // tm, ff // tf),
        in_specs=[
            pl.BlockSpec((tm, d), row),
            pl.BlockSpec((tm, d), row),
            pl.BlockSpec((d, tf), lambda i, j: (0, j)),
            pl.BlockSpec((tf, d), lambda i, j: (j, 0)),
            pl.BlockSpec((1, d), lambda i, j: (0, 0)),
        ],
        out_specs=pl.BlockSpec((tm, d), row),
        out_shape=jax.ShapeDtypeStruct((m, d), F32),
        scratch_shapes=[pltpu.VMEM((tm, d), F32)],
        compiler_params=_params(("arbitrary", "arbitrary")),
        name="mlp",
    )(h, x2, w_up, w_down, g_final)


def kernel(x_prompt, x_sample, cache_k, cache_v, cache_logf, cache_conv, cache_mem_k, cache_mem_v, mem_prompt, g_mix, w_in, b_f, conv_w, g_conv_out, g_attn_out, w_out, g_xattn, g_mem, w_xq, w_xkv, w_xo, g_mlp, w_up, w_down, g_final):
    depth = w_in.shape[0]
    assert depth == 1, "single-layer trunk"
    nb, seq, d = x_prompt.shape
    db, dseq, _ = x_sample.shape
    past = cache_k.shape[2]
    heads, hd = cache_k.shape[3], cache_k.shape[4]
    assert (heads, hd) == (N_FOX_HEADS, HEAD_DIM)
    d_attn = heads * hd
    d_conv = cache_conv.shape[-1]
    n_mem, xh, xhd = cache_mem_k.shape[2:]
    assert (xh, xhd) == (N_X_HEADS, HEAD_DIM)
    d_x = xh * xhd
    n_main = 3 * d_conv + 3 * d_attn
    assert d_conv == d_attn and w_in.shape[2] == n_main + heads

    w_main = w_in[0, :, :n_main].astype(BF16)
    w_f = jnp.pad(w_in[0, :, n_main:], ((0, 0), (0, LANES - heads))).astype(BF16)
    bf = b_f[0].reshape(1, heads)
    w_out_b = w_out[0].astype(BF16)
    w_xq_b = w_xq[0].astype(BF16)
    w_xkv_b = w_xkv[0].astype(BF16)
    w_xo_b = w_xo[0].astype(BF16)
    w_up_b = w_up[0].astype(BF16)
    w_down_b = w_down[0].astype(BF16)
    row = lambda g: g.reshape(1, -1)

    def trunk(x, conv_buf, seq_len, attention, mk, mv, tm_mix, tm_x):
        zc, qb, k, v, kb, vb, lf = _proj(x, row(g_mix[0]), w_main, w_f, bf, tm=512)
        ya = attention(qb, kb, vb, lf)
        x1, qx, conv_new = _mix(x, zc, conv_buf, ya, conv_w[0], row(g_conv_out[0]), row(g_attn_out[0]),
                                w_out_b, row(g_xattn[0]), w_xq_b, tm=tm_mix, seq=seq_len)
        x2, h = _xattn(x1, qx, mk, mv, w_xo_b, row(g_mlp[0]), tm=tm_x, seq=seq_len)
        y = _mlp(h, x2, w_up_b, w_down_b, row(g_final), tm=512, tf=1024)
        return y, k, v, lf, conv_new

    tile = 512
    nq = seq // tile

    def prompt_attention(qb, kb, vb, lf):
        lft = lf.reshape(nb, seq, heads).transpose(0, 2, 1).reshape(nb * heads, seq)
        ct = _cumsum_lanes(lft).reshape(nb, heads, seq)
        return _fox_prompt(qb, kb, vb, ct, ct.transpose(0, 2, 1), n_batch=nb, seq=seq, tile=tile)

    kv = _memkv(mem_prompt.reshape(nb * n_mem, d), row(g_mem[0]), w_xkv_b)
    mk_p = kv[0].reshape(nb, n_mem * xh, xhd)
    mv_p = kv[1].reshape(nb, n_mem * xh, xhd)
    conv0 = jnp.zeros((nb, CONV_WIDTH - 1, d_conv), F32)
    y_p, k_p, v_p, lf_p, conv_p = trunk(x_prompt.reshape(nb * seq, d), conv0, seq, prompt_attention,
                                        mk_p, mv_p, 256, 512)

    pad = (-(past + dseq)) % LANES

    def sample_attention(qb, kb, vb, lf):
        lft_past = cache_logf[0].transpose(0, 2, 1).reshape(db * heads, past)
        lft_new = lf.reshape(db, dseq, heads).transpose(0, 2, 1).reshape(db * heads, dseq)
        ct = _cumsum_lanes(jnp.concatenate(
            [lft_past, lft_new, jnp.zeros((db * heads, pad), F32)], axis=1))
        ck_past = ct[:, :past].reshape(db, heads, past)
        ck_new = ct[:, past:past + dseq].reshape(db, heads, dseq)
        cq = ck_new.transpose(0, 2, 1)
        return _fox_sample(qb, kb, vb, cache_k.reshape(db, past * heads, hd),
                           cache_v.reshape(db, past * heads, hd), cq, ck_past, ck_new)

    y_s, k_s, v_s, lf_s, conv_s = trunk(x_sample.reshape(db * dseq, d), cache_conv[0], dseq, sample_attention,
                                        cache_mem_k.reshape(db, n_mem * xh, xhd),
                                        cache_mem_v.reshape(db, n_mem * xh, xhd), 256, 256)

    return (y_p.reshape(nb, seq, d),
            y_s.reshape(db, dseq, d),
            k_p.reshape(1, nb, seq, heads, hd),
            v_p.reshape(1, nb, seq, heads, hd),
            lf_p.reshape(1, nb, seq, heads),
            conv_p[None],
            mk_p.reshape(1, nb, n_mem, xh, xhd),
            mv_p.reshape(1, nb, n_mem, xh, xhd),
            k_s.reshape(1, db, dseq, heads, hd),
            v_s.reshape(1, db, dseq, heads, hd),
            lf_s.reshape(1, db, dseq, heads),
            conv_s[None])
```

```python
import functools

import jax
import jax.numpy as jnp
from jax import lax
from jax.experimental import pallas as pl
from jax.experimental.pallas import tpu as pltpu

F32 = jnp.float32
BF16 = jnp.bfloat16

RMS_EPS = 1e-6
HEAD_DIM = 128
N_FOX_HEADS = 8
N_X_HEADS = 4
CONV_WIDTH = 3
LANES = 128
VMEM_LIMIT = 52 * 1024 * 1024

_NT = (((1,), (1,)), ((), ()))


def _params(semantics):
    return pltpu.CompilerParams(dimension_semantics=semantics, vmem_limit_bytes=VMEM_LIMIT)


def _rms(x, g):
    ms = jnp.mean(x * x, axis=-1, keepdims=True)
    return x * lax.rsqrt(ms + RMS_EPS) * g


def _log_sigmoid(x):
    return jnp.minimum(x, 0.0) - jnp.log1p(jnp.exp(-jnp.abs(x)))


def _store_head_rows(ref, val, heads):
    n = val.shape[0]
    for h in range(heads):
        ref[pl.ds(h, n, stride=heads), :] = val[:, h * HEAD_DIM:(h + 1) * HEAD_DIM]


def _load_head_rows(ref, lead, h, n, heads):
    return ref[lead, pl.ds(h, n, stride=heads), :]


def _proj_kernel(x_ref, g_ref, w_ref, wf_ref, bf_ref,
                 zc_ref, q_ref, k_ref, v_ref, kb_ref, vb_ref, lf_ref, xn_ref):
    j = pl.program_id(1)

    @pl.when(j == 0)
    def _():
        xn = _rms(x_ref[...], g_ref[...]).astype(BF16)
        xn_ref[...] = xn
        fz = jnp.dot(xn, wf_ref[...], preferred_element_type=F32)
        lf_ref[...] = _log_sigmoid(fz[:, :N_FOX_HEADS] + bf_ref[...])

    res = jnp.dot(xn_ref[...], w_ref[...], preferred_element_type=F32)

    @pl.when(j < 3)
    def _():
        zc_ref[...] = res

    @pl.when(j == 3)
    def _():
        q_ref[...] = res.astype(BF16)

    @pl.when(j == 4)
    def _():
        _store_head_rows(k_ref, res, N_FOX_HEADS)
        kb_ref[...] = res.astype(BF16)

    @pl.when(j == 5)
    def _():
        _store_head_rows(v_ref, res, N_FOX_HEADS)
        vb_ref[...] = res.astype(BF16)


def _proj(x, g, w_main, w_f, b_f, *, tm):
    m, d = x.shape
    tn = w_main.shape[1] // 6
    row = lambda i, j: (i, 0)
    return pl.pallas_call(
        _proj_kernel,
        grid=(m // tm, 6),
        in_specs=[
            pl.BlockSpec((tm, d), row),
            pl.BlockSpec((1, d), lambda i, j: (0, 0)),
            pl.BlockSpec((d, tn), lambda i, j: (0, j)),
            pl.BlockSpec((d, LANES), lambda i, j: (0, 0)),
            pl.BlockSpec((1, N_FOX_HEADS), lambda i, j: (0, 0)),
        ],
        out_specs=[
            pl.BlockSpec((tm, tn), lambda i, j: (i, jnp.minimum(j, 2))),
            pl.BlockSpec((tm, tn), row),
            pl.BlockSpec((tm * N_FOX_HEADS, HEAD_DIM), row),
            pl.BlockSpec((tm * N_FOX_HEADS, HEAD_DIM), row),
            pl.BlockSpec((tm, tn), row),
            pl.BlockSpec((tm, tn), row),
            pl.BlockSpec((tm, N_FOX_HEADS), row),
        ],
        out_shape=[
            jax.ShapeDtypeStruct((m, 3 * tn), F32),
            jax.ShapeDtypeStruct((m, tn), BF16),
            jax.ShapeDtypeStruct((m * N_FOX_HEADS, HEAD_DIM), F32),
            jax.ShapeDtypeStruct((m * N_FOX_HEADS, HEAD_DIM), F32),
            jax.ShapeDtypeStruct((m, tn), BF16),
            jax.ShapeDtypeStruct((m, tn), BF16),
            jax.ShapeDtypeStruct((m, N_FOX_HEADS), F32),
        ],
        scratch_shapes=[pltpu.VMEM((tm, d), BF16)],
        compiler_params=_params(("arbitrary", "arbitrary")),
        name="proj",
    )(x, g, w_main, w_f, b_f)


def _cumsum_kernel(a_ref, o_ref):
    rows, length = a_ref.shape
    r = lax.broadcasted_iota(jnp.int32, (LANES, LANES), 0)
    c = lax.broadcasted_iota(jnp.int32, (LANES, LANES), 1)
    upper = (r <= c).astype(BF16)
    carry = jnp.zeros((rows, 1), F32)
    for b in range(length // LANES):
        a = a_ref[:, b * LANES:(b + 1) * LANES]
        hi = a.astype(BF16)
        r1 = a - hi.astype(F32)
        mid = r1.astype(BF16)
        lo = (r1 - mid.astype(F32)).astype(BF16)
        s = (jnp.dot(hi, upper, preferred_element_type=F32)
             + jnp.dot(mid, upper, preferred_element_type=F32)
             + jnp.dot(lo, upper, preferred_element_type=F32)) + carry
        o_ref[:, b * LANES:(b + 1) * LANES] = s
        carry = s[:, LANES - 1:LANES]


def _cumsum_lanes(a):
    return pl.pallas_call(
        _cumsum_kernel,
        out_shape=jax.ShapeDtypeStruct(a.shape, F32),
        compiler_params=pltpu.CompilerParams(vmem_limit_bytes=VMEM_LIMIT),
        name="cumsum",
    )(a)


def _split3(x):
    hi = x.astype(BF16).astype(F32)
    r = x - hi
    mid = r.astype(BF16).astype(F32)
    lo = (r - mid).astype(BF16).astype(F32)
    return hi, mid, lo


def _fox_prompt_kernel(q_ref, k_ref, v_ref, cq_ref, ck_ref, o_ref, kaug_ref, vt_ref, qaug_ref, acc_ref, s0_ref, s1_ref,
                       *, tq, tk):
    assert tq == 2 * tk
    h = pl.program_id(1)
    qi = pl.program_id(2)
    nk = kaug_ref.shape[0] // tk
    inv_scale = HEAD_DIM ** 0.5
    exp2_scale = (HEAD_DIM ** -0.5) * 1.4426950408889634

    @pl.when(qi == 0)
    def _():
        lane8 = lax.broadcasted_iota(jnp.int32, (tk, N_FOX_HEADS), 1)
        lane = lax.broadcasted_iota(jnp.int32, (tk, HEAD_DIM), 1)

        def prep(c, carry):
            rows = pl.ds(pl.multiple_of(c * tk, tk), tk)
            kaug_ref[rows, 0:HEAD_DIM] = k_ref[rows, :]
            col = jnp.sum(jnp.where(lane8 == h, ck_ref[0, rows, :], 0.0), axis=1, keepdims=True) * inv_scale
            hi, mid, lo = _split3(col)
            aug = jnp.where(lane == 0, hi, jnp.where(lane == 1, mid, jnp.where(lane == 2, lo,
                            jnp.where(lane < 6, 1.0, 0.0))))
            kaug_ref[rows, HEAD_DIM:2 * HEAD_DIM] = aug.astype(BF16)
            vt_ref[c] = v_ref[rows, :].astype(F32).T.astype(BF16)
            return carry

        lax.fori_loop(0, nk, prep, 0)

    qaug_ref[0:HEAD_DIM, :] = q_ref[...].astype(F32).T.astype(BF16)
    hi, mid, lo = _split3(cq_ref[0, pl.ds(h, 1), :] * inv_scale)
    sub = lax.broadcasted_iota(jnp.int32, (HEAD_DIM, tq), 0)
    qaug_ref[HEAD_DIM:2 * HEAD_DIM, :] = jnp.where(
        sub < 3, -1.0, jnp.where(sub == 3, hi, jnp.where(sub == 4, mid, jnp.where(sub == 5, lo, 0.0)))).astype(BF16)
    acc_ref[...] = jnp.zeros(acc_ref.shape, F32)

    def scores(ki, dst_ref, lane0=0):
        rows = pl.ds(pl.multiple_of(ki * tk, tk), tk)
        dst_ref[:, lane0:] = jnp.dot(kaug_ref[rows, :], qaug_ref[:, lane0:], preferred_element_type=F32)

    def softmax_pv(ki, ml, s_ref, lane0=0, mask_offset=None):
        m_prev, l_prev = ml[0][:, lane0:], ml[1][:, lane0:]
        s = s_ref[:, lane0:]
        if mask_offset is not None:
            r = lax.broadcasted_iota(jnp.int32, s.shape, 0) + mask_offset
            c = lax.broadcasted_iota(jnp.int32, s.shape, 1) + lane0
            s = jnp.where(r <= c, s, -jnp.inf)
        m_new = jnp.maximum(m_prev, jnp.max(s, axis=0, keepdims=True))
        alpha = jnp.exp2((m_prev - m_new) * exp2_scale)
        p = jnp.exp2((s - m_new) * exp2_scale)
        l_new = alpha * l_prev + jnp.sum(p, axis=0, keepdims=True)
        pv = jnp.dot(vt_ref[ki], p.astype(BF16), preferred_element_type=F32)
        acc_ref[:, lane0:] = alpha * acc_ref[:, lane0:] + pv
        if lane0:
            m_new = jnp.concatenate([ml[0][:, :lane0], m_new], axis=1)
            l_new = jnp.concatenate([ml[1][:, :lane0], l_new], axis=1)
        return m_new, l_new

    def pair(j, ml):
        scores(2 * j + 1, s1_ref)
        ml = softmax_pv(2 * j, ml, s0_ref)
        scores(2 * j + 2, s0_ref)
        return softmax_pv(2 * j + 1, ml, s1_ref)

    ml = (jnp.full((1, tq), -jnp.inf, F32), jnp.zeros((1, tq), F32))
    scores(0, s0_ref)
    ml = lax.fori_loop(0, qi, pair, ml)
    scores(2 * qi + 1, s1_ref, lane0=tk)
    ml = softmax_pv(2 * qi, ml, s0_ref, mask_offset=0)
    _, l = softmax_pv(2 * qi + 1, ml, s1_ref, lane0=tk, mask_offset=tk)
    o_ref[...] = (acc_ref[...] / l).T


def _fox_prompt(qb, kb, vb, cq, ck, *, n_batch, seq, tq, tk):
    m = qb.shape[0]
    nq = seq // tq
    return pl.pallas_call(
        functools.partial(_fox_prompt_kernel, tq=tq, tk=tk),
        grid=(n_batch, N_FOX_HEADS, nq),
        in_specs=[
            pl.BlockSpec((tq, HEAD_DIM), lambda n, h, i: (n * nq + i, h)),
            pl.BlockSpec((seq, HEAD_DIM), lambda n, h, i: (n, h)),
            pl.BlockSpec((seq, HEAD_DIM), lambda n, h, i: (n, h)),
            pl.BlockSpec((1, N_FOX_HEADS, tq), lambda n, h, i: (n, 0, i)),
            pl.BlockSpec((1, seq, N_FOX_HEADS), lambda n, h, i: (n, 0, 0)),
        ],
        out_specs=pl.BlockSpec((tq, HEAD_DIM), lambda n, h, i: (n * nq + i, h)),
        out_shape=jax.ShapeDtypeStruct((m, N_FOX_HEADS * HEAD_DIM), F32),
        scratch_shapes=[
            pltpu.VMEM((seq, 2 * HEAD_DIM), BF16),
            pltpu.VMEM((seq // tk, HEAD_DIM, tk), BF16),
            pltpu.VMEM((2 * HEAD_DIM, tq), BF16),
            pltpu.VMEM((HEAD_DIM, tq), F32),
            pltpu.VMEM((tk, tq), F32),
            pltpu.VMEM((tk, tq), F32),
        ],
        compiler_params=_params(("arbitrary", "arbitrary", "arbitrary")),
        name="fox_prompt",
    )(qb, kb, vb, cq, ck)


def _fox_sample_kernel(q_ref, kn_ref, vn_ref, kc_ref, vc_ref, cq_ref, ckp_ref, ckn_ref, o_ref, *, past):
    scale = HEAD_DIM ** -0.5
    t = q_ref.shape[0]
    r = lax.broadcasted_iota(jnp.int32, (t, t), 0)
    c = lax.broadcasted_iota(jnp.int32, (t, t), 1)
    for hh in range(N_FOX_HEADS):
        sl = slice(hh * HEAD_DIM, (hh + 1) * HEAD_DIM)
        q = q_ref[:, sl]
        kp = _load_head_rows(kc_ref, 0, hh, past, N_FOX_HEADS).astype(BF16)
        vp = _load_head_rows(vc_ref, 0, hh, past, N_FOX_HEADS).astype(BF16)
        cq = cq_ref[0, :, hh:hh + 1]
        s1 = lax.dot_general(q, kp, _NT, preferred_element_type=F32) * scale
        s1 = s1 + (cq - ckp_ref[0, hh:hh + 1, :])
        s2 = lax.dot_general(q, kn_ref[:, sl], _NT, preferred_element_type=F32) * scale
        s2 = s2 + (cq - ckn_ref[0, hh:hh + 1, :])
        s2 = jnp.where(c <= r, s2, -jnp.inf)
        m = jnp.maximum(jnp.max(s1, axis=1, keepdims=True), jnp.max(s2, axis=1, keepdims=True))
        p1 = jnp.exp(s1 - m)
        p2 = jnp.exp(s2 - m)
        l = jnp.sum(p1, axis=1, keepdims=True) + jnp.sum(p2, axis=1, keepdims=True)
        o = (jnp.dot(p1.astype(BF16), vp, preferred_element_type=F32)
             + jnp.dot(p2.astype(BF16), vn_ref[:, sl], preferred_element_type=F32))
        o_ref[:, sl] = o / l


def _fox_sample(qb, knb, vnb, cache_k, cache_v, cq, ck_past, ck_new):
    n_batch, rows, _ = cache_k.shape
    past = rows // N_FOX_HEADS
    t = qb.shape[0] // n_batch
    width = qb.shape[1]
    tok = lambda b: (b, 0)
    lead = lambda b: (b, 0, 0)
    return pl.pallas_call(
        functools.partial(_fox_sample_kernel, past=past),
        grid=(n_batch,),
        in_specs=[
            pl.BlockSpec((t, width), tok),
            pl.BlockSpec((t, width), tok),
            pl.BlockSpec((t, width), tok),
            pl.BlockSpec((1, rows, HEAD_DIM), lead),
            pl.BlockSpec((1, rows, HEAD_DIM), lead),
            pl.BlockSpec((1, t, N_FOX_HEADS), lead),
            pl.BlockSpec((1, N_FOX_HEADS, past), lead),
            pl.BlockSpec((1, N_FOX_HEADS, t), lead),
        ],
        out_specs=pl.BlockSpec((t, width), tok),
        out_shape=jax.ShapeDtypeStruct(qb.shape, F32),
        compiler_params=_params(("arbitrary",)),
        name="fox_sample",
    )(qb, knb, vnb, cache_k, cache_v, cq, ck_past, ck_new)


def _mix_kernel(*refs, seg, nseg, tiles_per_seq, has_prev):
    if has_prev:
        (x_ref, gb_ref, gc_ref, hin_ref, pgc_ref, phin_ref, buf_ref, ya_ref, cw_ref, gco_ref, gao_ref,
         wout_ref, gx_ref, wxq_ref, x1_ref, qx_ref, cnew_ref) = refs
    else:
        (x_ref, gb_ref, gc_ref, hin_ref, buf_ref, ya_ref, cw_ref, gco_ref, gao_ref,
         wout_ref, gx_ref, wxq_ref, x1_ref, qx_ref, cnew_ref) = refs
    i = pl.program_id(0)
    u = gc_ref[...] * hin_ref[...]
    gb = gb_ref[...]
    w0 = cw_ref[0:1, :]
    w1 = cw_ref[1:2, :]
    w2 = cw_ref[2:3, :]
    rid = lax.broadcasted_iota(jnp.int32, (seg, u.shape[1]), 0)
    pieces = []
    for s in range(nseg):
        u_s = u[s * seg:(s + 1) * seg]
        b0 = buf_ref[s, 0:1, :]
        b1 = buf_ref[s, 1:2, :]
        if has_prev:
            pu = pgc_ref[...] * phin_ref[...]
            first = (i % tiles_per_seq) == 0
            b0 = jnp.where(first, b0, pu[6:7])
            b1 = jnp.where(first, b1, pu[7:8])
        r1 = jnp.where(rid == 0, b1, pltpu.roll(u_s, 1, 0))
        r2 = jnp.where(rid == 0, b0, jnp.where(rid == 1, b1, pltpu.roll(u_s, 2, 0)))
        y = w0 * r2 + w1 * r1 + w2 * u_s
        pieces.append(gb[s * seg:(s + 1) * seg] * y)
        cnew_ref[s] = u_s[seg - 2:seg]
    yc = pieces[0] if nseg == 1 else jnp.concatenate(pieces, axis=0)
    cat = jnp.concatenate([_rms(yc, gco_ref[...]).astype(BF16),
                           _rms(ya_ref[...], gao_ref[...]).astype(BF16)], axis=-1)
    x1 = x_ref[...] + jnp.dot(cat, wout_ref[...], preferred_element_type=F32)
    x1_ref[...] = x1
    xn = _rms(x1, gx_ref[...]).astype(BF16)
    qx_ref[...] = jnp.dot(xn, wxq_ref[...], preferred_element_type=F32).astype(BF16)


def _mix(x, zc, conv_buf, ya, conv_w, g_conv_out, g_attn_out, w_out, g_xattn, w_xq, *, tm, seq):
    m, d = x.shape
    dc = zc.shape[1] // 3
    dx = w_xq.shape[1]
    has_prev = seq > tm
    seg = tm if has_prev else seq
    nseg = tm // seg
    tiles_per_seq = max(seq // tm, 1)
    row = lambda i: (i, 0)
    const = lambda i: (0, 0)
    in_specs = [
        pl.BlockSpec((tm, d), row),
        pl.BlockSpec((tm, dc), lambda i: (i, 0)),
        pl.BlockSpec((tm, dc), lambda i: (i, 1)),
        pl.BlockSpec((tm, dc), lambda i: (i, 2)),
    ]
    args = [x, zc, zc, zc]
    if has_prev:
        prev = lambda col: (lambda i: (jnp.maximum(i * (tm // 8) - 1, 0), col))
        in_specs += [pl.BlockSpec((8, dc), prev(1)), pl.BlockSpec((8, dc), prev(2))]
        args += [zc, zc]
        buf_map = lambda i: (i // tiles_per_seq, 0, 0)
    else:
        buf_map = lambda i: (i, 0, 0)
    in_specs += [
        pl.BlockSpec((nseg, CONV_WIDTH - 1, dc), buf_map),
        pl.BlockSpec((tm, ya.shape[1]), row),
        pl.BlockSpec((CONV_WIDTH, dc), const),
        pl.BlockSpec((1, dc), const),
        pl.BlockSpec((1, ya.shape[1]), const),
        pl.BlockSpec(w_out.shape, const),
        pl.BlockSpec((1, d), const),
        pl.BlockSpec(w_xq.shape, const),
    ]
    args += [conv_buf, ya, conv_w, g_conv_out, g_attn_out, w_out, g_xattn, w_xq]
    return pl.pallas_call(
        functools.partial(_mix_kernel, seg=seg, nseg=nseg, tiles_per_seq=tiles_per_seq, has_prev=has_prev),
        grid=(m // tm,),
        in_specs=in_specs,
        out_specs=[
            pl.BlockSpec((tm, d), row),
            pl.BlockSpec((tm, dx), row),
            pl.BlockSpec((nseg, CONV_WIDTH - 1, dc), buf_map),
        ],
        out_shape=[
            jax.ShapeDtypeStruct((m, d), F32),
            jax.ShapeDtypeStruct((m, dx), BF16),
            jax.ShapeDtypeStruct(conv_buf.shape, F32),
        ],
        compiler_params=_params(("arbitrary",)),
        name="mix",
    )(*args)


def _memkv_kernel(x_ref, g_ref, w_ref, o_ref):
    xn = _rms(x_ref[...], g_ref[...]).astype(BF16)
    _store_head_rows(o_ref.at[0], jnp.dot(xn, w_ref[...], preferred_element_type=F32), N_X_HEADS)


def _memkv(mem, g, w):
    m, d = mem.shape
    half = w.shape[1] // 2
    return pl.pallas_call(
        _memkv_kernel,
        grid=(2,),
        in_specs=[
            pl.BlockSpec((m, d), lambda j: (0, 0)),
            pl.BlockSpec((1, d), lambda j: (0, 0)),
            pl.BlockSpec((d, half), lambda j: (0, j)),
        ],
        out_specs=pl.BlockSpec((1, m * N_X_HEADS, HEAD_DIM), lambda j: (j, 0, 0)),
        out_shape=jax.ShapeDtypeStruct((2, m * N_X_HEADS, HEAD_DIM), F32),
        compiler_params=_params(("arbitrary",)),
        name="memkv",
    )(mem, g, w)


def _xattn_kernel(x1_ref, qx_ref, mk_ref, mv_ref, wxo_ref, g_ref, x2_ref, h_ref, o_scr, *, seg, nseg):
    scale = HEAD_DIM ** -0.5
    n_mem = mk_ref.shape[1] // N_X_HEADS
    for s in range(nseg):
        rows = slice(s * seg, (s + 1) * seg)
        for hh in range(N_X_HEADS):
            sl = slice(hh * HEAD_DIM, (hh + 1) * HEAD_DIM)
            q = qx_ref[rows, sl]
            kh = _load_head_rows(mk_ref, s, hh, n_mem, N_X_HEADS).astype(BF16)
            vh = _load_head_rows(mv_ref, s, hh, n_mem, N_X_HEADS).astype(BF16)
            sc = lax.dot_general(q, kh, _NT, preferred_element_type=F32) * scale
            m = jnp.max(sc, axis=1, keepdims=True)
            p = jnp.exp(sc - m)
            l = jnp.sum(p, axis=1, keepdims=True)
            o = jnp.dot(p.astype(BF16), vh, preferred_element_type=F32) / l
            o_scr[rows, sl] = o.astype(BF16)
    x2 = x1_ref[...] + jnp.dot(o_scr[...], wxo_ref[...], preferred_element_type=F32)
    x2_ref[...] = x2
    h_ref[...] = _rms(x2, g_ref[...]).astype(BF16)


def _xattn(x1, qx, mk, mv, w_xo, g_mlp, *, tm, seq):
    m, d = x1.shape
    dx = qx.shape[1]
    mem_rows = mk.shape[1]
    seg = min(tm, seq)
    nseg = tm // seg
    tiles_per_seq = max(seq // tm, 1)
    row = lambda i: (i, 0)
    const = lambda i: (0, 0)
    mem_map = (lambda i: (i // tiles_per_seq, 0, 0)) if nseg == 1 else (lambda i: (i, 0, 0))
    return pl.pallas_call(
        functools.partial(_xattn_kernel, seg=seg, nseg=nseg),
        grid=(m // tm,),
        in_specs=[
            pl.BlockSpec((tm, d), row),
            pl.BlockSpec((tm, dx), row),
            pl.BlockSpec((nseg, mem_rows, HEAD_DIM), mem_map),
            pl.BlockSpec((nseg, mem_rows, HEAD_DIM), mem_map),
            pl.BlockSpec(w_xo.shape, const),
            pl.BlockSpec((1, d), const),
        ],
        out_specs=[pl.BlockSpec((tm, d), row), pl.BlockSpec((tm, d), row)],
        out_shape=[jax.ShapeDtypeStruct((m, d), F32), jax.ShapeDtypeStruct((m, d), BF16)],
        scratch_shapes=[pltpu.VMEM((tm, dx), BF16)],
        compiler_params=_params(("arbitrary",)),
        name="xattn",
    )(x1, qx, mk, mv, w_xo, g_mlp)


def _mlp_kernel(h_ref, x2_ref, wu_ref, wd_ref, g_ref, y_ref, acc_ref):
    j = pl.program_id(1)

    @pl.when(j == 0)
    def _():
        acc_ref[...] = jnp.zeros(acc_ref.shape, F32)

    a = jnp.dot(h_ref[...], wu_ref[...], preferred_element_type=F32)
    a = jnp.square(jnp.maximum(a, 0.0)).astype(BF16)
    acc_ref[...] += jnp.dot(a, wd_ref[...], preferred_element_type=F32)

    @pl.when(j == pl.num_programs(1) - 1)
    def _():
        y_ref[...] = _rms(x2_ref[...] + acc_ref[...], g_ref[...])


def _mlp(h, x2, w_up, w_down, g_final, *, tm, tf):
    m, d = h.shape
    ff = w_up.shape[1]
    row = lambda i, j: (i, 0)
    return pl.pallas_call(
        _mlp_kernel,
        grid=(m // tm, ff // tf),
        in_specs=[
            pl.BlockSpec((tm, d), row),
            pl.BlockSpec((tm, d), row),
            pl.BlockSpec((d, tf), lambda i, j: (0, j)),
            pl.BlockSpec((tf, d), lambda i, j: (j, 0)),
            pl.BlockSpec((1, d), lambda i, j: (0, 0)),
        ],
        out_specs=pl.BlockSpec((tm, d), row),
        out_shape=jax.ShapeDtypeStruct((m, d), F32),
        scratch_shapes=[pltpu.VMEM((tm, d), F32)],
        compiler_params=_params(("arbitrary", "arbitrary")),
        name="mlp",
    )(h, x2, w_up, w_down, g_final)


def kernel(x_prompt, x_sample, cache_k, cache_v, cache_logf, cache_conv, cache_mem_k, cache_mem_v, mem_prompt, g_mix, w_in, b_f, conv_w, g_conv_out, g_attn_out, w_out, g_xattn, g_mem, w_xq, w_xkv, w_xo, g_mlp, w_up, w_down, g_final):
    depth = w_in.shape[0]
    assert depth == 1, "single-layer trunk"
    nb, seq, d = x_prompt.shape
    db, dseq, _ = x_sample.shape
    past = cache_k.shape[2]
    heads, hd = cache_k.shape[3], cache_k.shape[4]
    assert (heads, hd) == (N_FOX_HEADS, HEAD_DIM)
    d_attn = heads * hd
    d_conv = cache_conv.shape[-1]
    n_mem, xh, xhd = cache_mem_k.shape[2:]
    assert (xh, xhd) == (N_X_HEADS, HEAD_DIM)
    d_x = xh * xhd
    n_main = 3 * d_conv + 3 * d_attn
    assert d_conv == d_attn and w_in.shape[2] == n_main + heads

    w_main = w_in[0, :, :n_main].astype(BF16)
    w_f = jnp.pad(w_in[0, :, n_main:], ((0, 0), (0, LANES - heads))).astype(BF16)
    bf = b_f[0].reshape(1, heads)
    w_out_b = w_out[0].astype(BF16)
    w_xq_b = w_xq[0].astype(BF16)
    w_xkv_b = w_xkv[0].astype(BF16)
    w_xo_b = w_xo[0].astype(BF16)
    w_up_b = w_up[0].astype(BF16)
    w_down_b = w_down[0].astype(BF16)
    row = lambda g: g.reshape(1, -1)

    def trunk(x, conv_buf, seq_len, attention, mk, mv, tm_mix, tm_x):
        zc, qb, k, v, kb, vb, lf = _proj(x, row(g_mix[0]), w_main, w_f, bf, tm=512)
        ya = attention(qb, kb, vb, lf)
        x1, qx, conv_new = _mix(x, zc, conv_buf, ya, conv_w[0], row(g_conv_out[0]), row(g_attn_out[0]),
                                w_out_b, row(g_xattn[0]), w_xq_b, tm=tm_mix, seq=seq_len)
        x2, h = _xattn(x1, qx, mk, mv, w_xo_b, row(g_mlp[0]), tm=tm_x, seq=seq_len)
        y = _mlp(h, x2, w_up_b, w_down_b, row(g_final), tm=512, tf=1024)
        return y, k, v, lf, conv_new


    def prompt_attention(qb, kb, vb, lf):
        lft = lf.reshape(nb, seq, heads).transpose(0, 2, 1).reshape(nb * heads, seq)
        ct = _cumsum_lanes(lft).reshape(nb, heads, seq)
        return _fox_prompt(qb, kb, vb, ct, ct.transpose(0, 2, 1), n_batch=nb, seq=seq, tq=1024, tk=512)

    kv = _memkv(mem_prompt.reshape(nb * n_mem, d), row(g_mem[0]), w_xkv_b)
    mk_p = kv[0].reshape(nb, n_mem * xh, xhd)
    mv_p = kv[1].reshape(nb, n_mem * xh, xhd)
    conv0 = jnp.zeros((nb, CONV_WIDTH - 1, d_conv), F32)
    y_p, k_p, v_p, lf_p, conv_p = trunk(x_prompt.reshape(nb * seq, d), conv0, seq, prompt_attention,
                                        mk_p, mv_p, 256, 512)

    pad = (-(past + dseq)) % LANES

    def sample_attention(qb, kb, vb, lf):
        lft_past = cache_logf[0].transpose(0, 2, 1).reshape(db * heads, past)
        lft_new = lf.reshape(db, dseq, heads).transpose(0, 2, 1).reshape(db * heads, dseq)
        ct = _cumsum_lanes(jnp.concatenate(
            [lft_past, lft_new, jnp.zeros((db * heads, pad), F32)], axis=1))
        ck_past = ct[:, :past].reshape(db, heads, past)
        ck_new = ct[:, past:past + dseq].reshape(db, heads, dseq)
        cq = ck_new.transpose(0, 2, 1)
        return _fox_sample(qb, kb, vb, cache_k.reshape(db, past * heads, hd),
                           cache_v.reshape(db, past * heads, hd), cq, ck_past, ck_new)

    y_s, k_s, v_s, lf_s, conv_s = trunk(x_sample.reshape(db * dseq, d), cache_conv[0], dseq, sample_attention,
                                        cache_mem_k.reshape(db, n_mem * xh, xhd),
                                        cache_mem_v.reshape(db, n_mem * xh, xhd), 256, 256)

    return (y_p.reshape(nb, seq, d),
            y_s.reshape(db, dseq, d),
            k_p.reshape(1, nb, seq, heads, hd),
            v_p.reshape(1, nb, seq, heads, hd),
            lf_p.reshape(1, nb, seq, heads),
            conv_p[None],
            mk_p.reshape(1, nb, n_mem, xh, xhd),
            mv_p.reshape(1, nb, n_mem, xh, xhd),
            k_s.reshape(1, db, dseq, heads, hd),
            v_s.reshape(1, db, dseq, heads, hd),
            lf_s.reshape(1, db, dseq, heads),
            conv_s[None])
```

```python
import functools

import jax
import jax.numpy as jnp
from jax import lax
from jax.experimental import pallas as pl
from jax.experimental.pallas import tpu as pltpu

F32 = jnp.float32
BF16 = jnp.bfloat16

RMS_EPS = 1e-6
HEAD_DIM = 128
N_FOX_HEADS = 8
N_X_HEADS = 4
CONV_WIDTH = 3
LANES = 128
VMEM_LIMIT = 52 * 1024 * 1024

_NT = (((1,), (1,)), ((), ()))


def _params(semantics):
    return pltpu.CompilerParams(dimension_semantics=semantics, vmem_limit_bytes=VMEM_LIMIT)


def _rms(x, g):
    ms = jnp.mean(x * x, axis=-1, keepdims=True)
    return x * lax.rsqrt(ms + RMS_EPS) * g


def _log_sigmoid(x):
    return jnp.minimum(x, 0.0) - jnp.log1p(jnp.exp(-jnp.abs(x)))


def _store_head_rows(ref, val, heads):
    n = val.shape[0]
    for h in range(heads):
        ref[pl.ds(h, n, stride=heads), :] = val[:, h * HEAD_DIM:(h + 1) * HEAD_DIM]


def _load_head_rows(ref, lead, h, n, heads):
    return ref[lead, pl.ds(h, n, stride=heads), :]


def _xnorm_kernel(x_ref, g_ref, wf_ref, bf_ref, xn_ref, lf_ref):
    xn = _rms(x_ref[...], g_ref[...]).astype(BF16)
    xn_ref[...] = xn
    wf = wf_ref[...].astype(BF16)
    wf = jnp.concatenate([wf, jnp.zeros((LANES - wf.shape[0], wf.shape[1]), BF16)], axis=0)
    fz = lax.dot_general(xn, wf, _NT, preferred_element_type=F32)
    lf_ref[...] = _log_sigmoid(fz[:, :N_FOX_HEADS] + bf_ref[...])


def _xnorm(x, g, w_t, b_f, *, tm, gate_row):
    m, d = x.shape
    row = lambda i: (i, 0)
    return pl.pallas_call(
        _xnorm_kernel,
        grid=(m // tm,),
        in_specs=[
            pl.BlockSpec((tm, d), row),
            pl.BlockSpec((1, d), lambda i: (0, 0)),
            pl.BlockSpec((N_FOX_HEADS, d), lambda i: (gate_row // N_FOX_HEADS, 0)),
            pl.BlockSpec((1, N_FOX_HEADS), lambda i: (0, 0)),
        ],
        out_specs=[pl.BlockSpec((tm, d), row), pl.BlockSpec((tm, N_FOX_HEADS), row)],
        out_shape=[jax.ShapeDtypeStruct((m, d), BF16), jax.ShapeDtypeStruct((m, N_FOX_HEADS), F32)],
        compiler_params=_params(("arbitrary",)),
        name="xnorm",
    )(x, g, w_t, b_f)


def _proj_kernel(xn_ref, wt_ref, *refs, mode):
    out_refs, wb_ref = refs[:-1], refs[-1]

    @pl.when(pl.program_id(1) == 0)
    def _():
        wb_ref[...] = wt_ref[...].astype(BF16)

    res = lax.dot_general(xn_ref[...], wb_ref[...], _NT, preferred_element_type=F32)
    if mode == "f32":
        out_refs[0][...] = res
    elif mode == "bf16":
        out_refs[0][...] = res.astype(BF16)
    else:
        _store_head_rows(out_refs[0], res, N_FOX_HEADS)
        out_refs[1][...] = res.astype(BF16)


def _proj(xn, w_t, *, j0, nj, tn, tm, mode):
    m, d = xn.shape
    if mode == "heads":
        assert nj == 1
        out_specs = [pl.BlockSpec((tm * N_FOX_HEADS, HEAD_DIM), lambda j, i: (i, 0)),
                     pl.BlockSpec((tm, tn), lambda j, i: (i, 0))]
        out_shape = [jax.ShapeDtypeStruct((m * N_FOX_HEADS, HEAD_DIM), F32), jax.ShapeDtypeStruct((m, tn), BF16)]
    else:
        out_specs = [pl.BlockSpec((tm, tn), lambda j, i: (i, j))]
        out_shape = [jax.ShapeDtypeStruct((m, nj * tn), F32 if mode == "f32" else BF16)]
    return pl.pallas_call(
        functools.partial(_proj_kernel, mode=mode),
        grid=(nj, m // tm),
        in_specs=[
            pl.BlockSpec((tm, d), lambda j, i: (i, 0)),
            pl.BlockSpec((tn, d), lambda j, i: (j0 + j, 0)),
        ],
        out_specs=out_specs,
        out_shape=out_shape,
        scratch_shapes=[pltpu.VMEM((tn, d), BF16)],
        compiler_params=_params(("arbitrary", "arbitrary")),
        name="proj_" + mode,
    )(xn, w_t)


def _cumsum_kernel(a_ref, o_ref):
    rows, length = a_ref.shape
    r = lax.broadcasted_iota(jnp.int32, (LANES, LANES), 0)
    c = lax.broadcasted_iota(jnp.int32, (LANES, LANES), 1)
    upper = (r <= c).astype(BF16)
    carry = jnp.zeros((rows, 1), F32)
    for b in range(length // LANES):
        a = a_ref[:, b * LANES:(b + 1) * LANES]
        hi = a.astype(BF16)
        r1 = a - hi.astype(F32)
        mid = r1.astype(BF16)
        lo = (r1 - mid.astype(F32)).astype(BF16)
        s = (jnp.dot(hi, upper, preferred_element_type=F32)
             + jnp.dot(mid, upper, preferred_element_type=F32)
             + jnp.dot(lo, upper, preferred_element_type=F32)) + carry
        o_ref[:, b * LANES:(b + 1) * LANES] = s
        carry = s[:, LANES - 1:LANES]


def _cumsum_lanes(a):
    return pl.pallas_call(
        _cumsum_kernel,
        out_shape=jax.ShapeDtypeStruct(a.shape, F32),
        compiler_params=pltpu.CompilerParams(vmem_limit_bytes=VMEM_LIMIT),
        name="cumsum",
    )(a)


def _split3(x):
    hi = x.astype(BF16).astype(F32)
    r = x - hi
    mid = r.astype(BF16).astype(F32)
    lo = (r - mid).astype(BF16).astype(F32)
    return hi, mid, lo


def _fox_prompt_kernel(q_ref, k_ref, v_ref, cq_ref, ck_ref, o_ref, kaug_ref, vt_ref, qaug_ref, acc_ref, s0_ref, s1_ref,
                       *, tq, tk):
    assert tq == 2 * tk
    h = pl.program_id(1)
    qi = pl.program_id(2)
    nk = kaug_ref.shape[0] // tk
    inv_scale = HEAD_DIM ** 0.5
    exp2_scale = (HEAD_DIM ** -0.5) * 1.4426950408889634

    @pl.when(qi == 0)
    def _():
        lane8 = lax.broadcasted_iota(jnp.int32, (tk, N_FOX_HEADS), 1)
        lane = lax.broadcasted_iota(jnp.int32, (tk, HEAD_DIM), 1)

        def prep(c, carry):
            rows = pl.ds(pl.multiple_of(c * tk, tk), tk)
            kaug_ref[rows, 0:HEAD_DIM] = k_ref[rows, :]
            col = jnp.sum(jnp.where(lane8 == h, ck_ref[0, rows, :], 0.0), axis=1, keepdims=True) * inv_scale
            hi, mid, lo = _split3(col)
            aug = jnp.where(lane == 0, hi, jnp.where(lane == 1, mid, jnp.where(lane == 2, lo,
                            jnp.where(lane < 6, 1.0, 0.0))))
            kaug_ref[rows, HEAD_DIM:2 * HEAD_DIM] = aug.astype(BF16)
            vt_ref[c] = v_ref[rows, :].astype(F32).T.astype(BF16)
            return carry

        lax.fori_loop(0, nk, prep, 0)

    qaug_ref[0:HEAD_DIM, :] = q_ref[...].astype(F32).T.astype(BF16)
    hi, mid, lo = _split3(cq_ref[0, pl.ds(h, 1), :] * inv_scale)
    sub = lax.broadcasted_iota(jnp.int32, (HEAD_DIM, tq), 0)
    qaug_ref[HEAD_DIM:2 * HEAD_DIM, :] = jnp.where(
        sub < 3, -1.0, jnp.where(sub == 3, hi, jnp.where(sub == 4, mid, jnp.where(sub == 5, lo, 0.0)))).astype(BF16)
    acc_ref[...] = jnp.zeros(acc_ref.shape, F32)

    def scores(ki, dst_ref, lane0=0):
        rows = pl.ds(pl.multiple_of(ki * tk, tk), tk)
        dst_ref[:, lane0:] = jnp.dot(kaug_ref[rows, :], qaug_ref[:, lane0:], preferred_element_type=F32)

    def softmax_pv(ki, ml, s_ref, lane0=0, mask_offset=None):
        m_prev, l_prev = ml[0][:, lane0:], ml[1][:, lane0:]
        s = s_ref[:, lane0:]
        if mask_offset is not None:
            r = lax.broadcasted_iota(jnp.int32, s.shape, 0) + mask_offset
            c = lax.broadcasted_iota(jnp.int32, s.shape, 1) + lane0
            s = jnp.where(r <= c, s, -jnp.inf)
        m_new = jnp.maximum(m_prev, jnp.max(s, axis=0, keepdims=True))
        alpha = jnp.exp2((m_prev - m_new) * exp2_scale)
        p = jnp.exp2((s - m_new) * exp2_scale)
        l_new = alpha * l_prev + jnp.sum(p, axis=0, keepdims=True)
        pv = jnp.dot(vt_ref[ki], p.astype(BF16), preferred_element_type=F32)
        acc_ref[:, lane0:] = alpha * acc_ref[:, lane0:] + pv
        if lane0:
            m_new = jnp.concatenate([ml[0][:, :lane0], m_new], axis=1)
            l_new = jnp.concatenate([ml[1][:, :lane0], l_new], axis=1)
        return m_new, l_new

    def pair(j, ml):
        scores(2 * j + 1, s1_ref)
        ml = softmax_pv(2 * j, ml, s0_ref)
        scores(2 * j + 2, s0_ref)
        return softmax_pv(2 * j + 1, ml, s1_ref)

    ml = (jnp.full((1, tq), -jnp.inf, F32), jnp.zeros((1, tq), F32))
    scores(0, s0_ref)
    ml = lax.fori_loop(0, qi, pair, ml)
    scores(2 * qi + 1, s1_ref, lane0=tk)
    ml = softmax_pv(2 * qi, ml, s0_ref, mask_offset=0)
    _, l = softmax_pv(2 * qi + 1, ml, s1_ref, lane0=tk, mask_offset=tk)
    o_ref[...] = (acc_ref[...] / l).T


def _fox_prompt(qb, kb, vb, cq, ck, *, n_batch, seq, tq, tk):
    m = qb.shape[0]
    nq = seq // tq
    return pl.pallas_call(
        functools.partial(_fox_prompt_kernel, tq=tq, tk=tk),
        grid=(n_batch, N_FOX_HEADS, nq),
        in_specs=[
            pl.BlockSpec((tq, HEAD_DIM), lambda n, h, i: (n * nq + i, h)),
            pl.BlockSpec((seq, HEAD_DIM), lambda n, h, i: (n, h)),
            pl.BlockSpec((seq, HEAD_DIM), lambda n, h, i: (n, h)),
            pl.BlockSpec((1, N_FOX_HEADS, tq), lambda n, h, i: (n, 0, i)),
            pl.BlockSpec((1, seq, N_FOX_HEADS), lambda n, h, i: (n, 0, 0)),
        ],
        out_specs=pl.BlockSpec((tq, HEAD_DIM), lambda n, h, i: (n * nq + i, h)),
        out_shape=jax.ShapeDtypeStruct((m, N_FOX_HEADS * HEAD_DIM), F32),
        scratch_shapes=[
            pltpu.VMEM((seq, 2 * HEAD_DIM), BF16),
            pltpu.VMEM((seq // tk, HEAD_DIM, tk), BF16),
            pltpu.VMEM((2 * HEAD_DIM, tq), BF16),
            pltpu.VMEM((HEAD_DIM, tq), F32),
            pltpu.VMEM((tk, tq), F32),
            pltpu.VMEM((tk, tq), F32),
        ],
        compiler_params=_params(("arbitrary", "arbitrary", "arbitrary")),
        name="fox_prompt",
    )(qb, kb, vb, cq, ck)


def _fox_sample_kernel(q_ref, kn_ref, vn_ref, kc_ref, vc_ref, cq_ref, ckp_ref, ckn_ref, o_ref, *, past):
    scale = HEAD_DIM ** -0.5
    t = q_ref.shape[0]
    r = lax.broadcasted_iota(jnp.int32, (t, t), 0)
    c = lax.broadcasted_iota(jnp.int32, (t, t), 1)
    for hh in range(N_FOX_HEADS):
        sl = slice(hh * HEAD_DIM, (hh + 1) * HEAD_DIM)
        q = q_ref[:, sl]
        kp = _load_head_rows(kc_ref, 0, hh, past, N_FOX_HEADS).astype(BF16)
        vp = _load_head_rows(vc_ref, 0, hh, past, N_FOX_HEADS).astype(BF16)
        cq = cq_ref[0, :, hh:hh + 1]
        s1 = lax.dot_general(q, kp, _NT, preferred_element_type=F32) * scale
        s1 = s1 + (cq - ckp_ref[0, hh:hh + 1, :])
        s2 = lax.dot_general(q, kn_ref[:, sl], _NT, preferred_element_type=F32) * scale
        s2 = s2 + (cq - ckn_ref[0, hh:hh + 1, :])
        s2 = jnp.where(c <= r, s2, -jnp.inf)
        m = jnp.maximum(jnp.max(s1, axis=1, keepdims=True), jnp.max(s2, axis=1, keepdims=True))
        p1 = jnp.exp(s1 - m)
        p2 = jnp.exp(s2 - m)
        l = jnp.sum(p1, axis=1, keepdims=True) + jnp.sum(p2, axis=1, keepdims=True)
        o = (jnp.dot(p1.astype(BF16), vp, preferred_element_type=F32)
             + jnp.dot(p2.astype(BF16), vn_ref[:, sl], preferred_element_type=F32))
        o_ref[:, sl] = o / l


def _fox_sample(qb, knb, vnb, cache_k, cache_v, cq, ck_past, ck_new):
    n_batch, rows, _ = cache_k.shape
    past = rows // N_FOX_HEADS
    t = qb.shape[0] // n_batch
    width = qb.shape[1]
    tok = lambda b: (b, 0)
    lead = lambda b: (b, 0, 0)
    return pl.pallas_call(
        functools.partial(_fox_sample_kernel, past=past),
        grid=(n_batch,),
        in_specs=[
            pl.BlockSpec((t, width), tok),
            pl.BlockSpec((t, width), tok),
            pl.BlockSpec((t, width), tok),
            pl.BlockSpec((1, rows, HEAD_DIM), lead),
            pl.BlockSpec((1, rows, HEAD_DIM), lead),
            pl.BlockSpec((1, t, N_FOX_HEADS), lead),
            pl.BlockSpec((1, N_FOX_HEADS, past), lead),
            pl.BlockSpec((1, N_FOX_HEADS, t), lead),
        ],
        out_specs=pl.BlockSpec((t, width), tok),
        out_shape=jax.ShapeDtypeStruct(qb.shape, F32),
        compiler_params=_params(("arbitrary",)),
        name="fox_sample",
    )(qb, knb, vnb, cache_k, cache_v, cq, ck_past, ck_new)


def _mix_kernel(*refs, seg, nseg, tiles_per_seq, has_prev):
    if has_prev:
        (x_ref, gb_ref, gc_ref, hin_ref, pgc_ref, phin_ref, buf_ref, ya_ref, cw_ref, gco_ref, gao_ref,
         wout_ref, gx_ref, wxq_ref, x1_ref, qx_ref, cnew_ref) = refs
    else:
        (x_ref, gb_ref, gc_ref, hin_ref, buf_ref, ya_ref, cw_ref, gco_ref, gao_ref,
         wout_ref, gx_ref, wxq_ref, x1_ref, qx_ref, cnew_ref) = refs
    i = pl.program_id(0)
    u = gc_ref[...] * hin_ref[...]
    gb = gb_ref[...]
    w0 = cw_ref[0:1, :]
    w1 = cw_ref[1:2, :]
    w2 = cw_ref[2:3, :]
    rid = lax.broadcasted_iota(jnp.int32, (seg, u.shape[1]), 0)
    pieces = []
    for s in range(nseg):
        u_s = u[s * seg:(s + 1) * seg]
        b0 = buf_ref[s, 0:1, :]
        b1 = buf_ref[s, 1:2, :]
        if has_prev:
            pu = pgc_ref[...] * phin_ref[...]
            first = (i % tiles_per_seq) == 0
            b0 = jnp.where(first, b0, pu[6:7])
            b1 = jnp.where(first, b1, pu[7:8])
        r1 = jnp.where(rid == 0, b1, pltpu.roll(u_s, 1, 0))
        r2 = jnp.where(rid == 0, b0, jnp.where(rid == 1, b1, pltpu.roll(u_s, 2, 0)))
        y = w0 * r2 + w1 * r1 + w2 * u_s
        pieces.append(gb[s * seg:(s + 1) * seg] * y)
        cnew_ref[s] = u_s[seg - 2:seg]
    yc = pieces[0] if nseg == 1 else jnp.concatenate(pieces, axis=0)
    cat = jnp.concatenate([_rms(yc, gco_ref[...]).astype(BF16),
                           _rms(ya_ref[...], gao_ref[...]).astype(BF16)], axis=-1)
    x1 = x_ref[...] + jnp.dot(cat, wout_ref[...], preferred_element_type=F32)
    x1_ref[...] = x1
    xn = _rms(x1, gx_ref[...]).astype(BF16)
    qx_ref[...] = jnp.dot(xn, wxq_ref[...], preferred_element_type=F32).astype(BF16)


def _mix(x, zc, conv_buf, ya, conv_w, g_conv_out, g_attn_out, w_out, g_xattn, w_xq, *, tm, seq):
    m, d = x.shape
    dc = zc.shape[1] // 3
    dx = w_xq.shape[1]
    has_prev = seq > tm
    seg = tm if has_prev else seq
    nseg = tm // seg
    tiles_per_seq = max(seq // tm, 1)
    row = lambda i: (i, 0)
    const = lambda i: (0, 0)
    in_specs = [
        pl.BlockSpec((tm, d), row),
        pl.BlockSpec((tm, dc), lambda i: (i, 0)),
        pl.BlockSpec((tm, dc), lambda i: (i, 1)),
        pl.BlockSpec((tm, dc), lambda i: (i, 2)),
    ]
    args = [x, zc, zc, zc]
    if has_prev:
        prev = lambda col: (lambda i: (jnp.maximum(i * (tm // 8) - 1, 0), col))
        in_specs += [pl.BlockSpec((8, dc), prev(1)), pl.BlockSpec((8, dc), prev(2))]
        args += [zc, zc]
        buf_map = lambda i: (i // tiles_per_seq, 0, 0)
    else:
        buf_map = lambda i: (i, 0, 0)
    in_specs += [
        pl.BlockSpec((nseg, CONV_WIDTH - 1, dc), buf_map),
        pl.BlockSpec((tm, ya.shape[1]), row),
        pl.BlockSpec((CONV_WIDTH, dc), const),
        pl.BlockSpec((1, dc), const),
        pl.BlockSpec((1, ya.shape[1]), const),
        pl.BlockSpec(w_out.shape, const),
        pl.BlockSpec((1, d), const),
        pl.BlockSpec(w_xq.shape, const),
    ]
    args += [conv_buf, ya, conv_w, g_conv_out, g_attn_out, w_out, g_xattn, w_xq]
    return pl.pallas_call(
        functools.partial(_mix_kernel, seg=seg, nseg=nseg, tiles_per_seq=tiles_per_seq, has_prev=has_prev),
        grid=(m // tm,),
        in_specs=in_specs,
        out_specs=[
            pl.BlockSpec((tm, d), row),
            pl.BlockSpec((tm, dx), row),
            pl.BlockSpec((nseg, CONV_WIDTH - 1, dc), buf_map),
        ],
        out_shape=[
            jax.ShapeDtypeStruct((m, d), F32),
            jax.ShapeDtypeStruct((m, dx), BF16),
            jax.ShapeDtypeStruct(conv_buf.shape, F32),
        ],
        compiler_params=_params(("arbitrary",)),
        name="mix",
    )(*args)


def _memkv_kernel(x_ref, g_ref, w_ref, o_ref):
    xn = _rms(x_ref[...], g_ref[...]).astype(BF16)
    _store_head_rows(o_ref.at[0], jnp.dot(xn, w_ref[...], preferred_element_type=F32), N_X_HEADS)


def _memkv(mem, g, w):
    m, d = mem.shape
    half = w.shape[1] // 2
    return pl.pallas_call(
        _memkv_kernel,
        grid=(2,),
        in_specs=[
            pl.BlockSpec((m, d), lambda j: (0, 0)),
            pl.BlockSpec((1, d), lambda j: (0, 0)),
            pl.BlockSpec((d, half), lambda j: (0, j)),
        ],
        out_specs=pl.BlockSpec((1, m * N_X_HEADS, HEAD_DIM), lambda j: (j, 0, 0)),
        out_shape=jax.ShapeDtypeStruct((2, m * N_X_HEADS, HEAD_DIM), F32),
        compiler_params=_params(("arbitrary",)),
        name="memkv",
    )(mem, g, w)


def _xattn_kernel(x1_ref, qx_ref, mk_ref, mv_ref, wxo_ref, g_ref, x2_ref, h_ref, o_scr, *, seg, nseg):
    scale = HEAD_DIM ** -0.5
    n_mem = mk_ref.shape[1] // N_X_HEADS
    for s in range(nseg):
        rows = slice(s * seg, (s + 1) * seg)
        for hh in range(N_X_HEADS):
            sl = slice(hh * HEAD_DIM, (hh + 1) * HEAD_DIM)
            q = qx_ref[rows, sl]
            kh = _load_head_rows(mk_ref, s, hh, n_mem, N_X_HEADS).astype(BF16)
            vh = _load_head_rows(mv_ref, s, hh, n_mem, N_X_HEADS).astype(BF16)
            sc = lax.dot_general(q, kh, _NT, preferred_element_type=F32) * scale
            m = jnp.max(sc, axis=1, keepdims=True)
            p = jnp.exp(sc - m)
            l = jnp.sum(p, axis=1, keepdims=True)
            o = jnp.dot(p.astype(BF16), vh, preferred_element_type=F32) / l
            o_scr[rows, sl] = o.astype(BF16)
    x2 = x1_ref[...] + jnp.dot(o_scr[...], wxo_ref[...], preferred_element_type=F32)
    x2_ref[...] = x2
    h_ref[...] = _rms(x2, g_ref[...]).astype(BF16)


def _xattn(x1, qx, mk, mv, w_xo, g_mlp, *, tm, seq):
    m, d = x1.shape
    dx = qx.shape[1]
    mem_rows = mk.shape[1]
    seg = min(tm, seq)
    nseg = tm // seg
    tiles_per_seq = max(seq // tm, 1)
    row = lambda i: (i, 0)
    const = lambda i: (0, 0)
    mem_map = (lambda i: (i // tiles_per_seq, 0, 0)) if nseg == 1 else (lambda i: (i, 0, 0))
    return pl.pallas_call(
        functools.partial(_xattn_kernel, seg=seg, nseg=nseg),
        grid=(m // tm,),
        in_specs=[
            pl.BlockSpec((tm, d), row),
            pl.BlockSpec((tm, dx), row),
            pl.BlockSpec((nseg, mem_rows, HEAD_DIM), mem_map),
            pl.BlockSpec((nseg, mem_rows, HEAD_DIM), mem_map),
            pl.BlockSpec(w_xo.shape, const),
            pl.BlockSpec((1, d), const),
        ],
        out_specs=[pl.BlockSpec((tm, d), row), pl.BlockSpec((tm, d), row)],
        out_shape=[jax.ShapeDtypeStruct((m, d), F32), jax.ShapeDtypeStruct((m, d), BF16)],
        scratch_shapes=[pltpu.VMEM((tm, dx), BF16)],
        compiler_params=_params(("arbitrary",)),
        name="xattn",
    )(x1, qx, mk, mv, w_xo, g_mlp)


def _mlp_kernel(h_ref, x2_ref, wu_ref, wd_ref, g_ref, y_ref, acc_ref):
    j = pl.program_id(1)

    @pl.when(j == 0)
    def _():
        acc_ref[...] = jnp.zeros(acc_ref.shape, F32)

    a = jnp.dot(h_ref[...], wu_ref[...], preferred_element_type=F32)
    a = jnp.square(jnp.maximum(a, 0.0)).astype(BF16)
    acc_ref[...] += jnp.dot(a, wd_ref[...], preferred_element_type=F32)

    @pl.when(j == pl.num_programs(1) - 1)
    def _():
        y_ref[...] = _rms(x2_ref[...] + acc_ref[...], g_ref[...])


def _mlp(h, x2, w_up, w_down, g_final, *, tm, tf):
    m, d = h.shape
    ff = w_up.shape[1]
    row = lambda i, j: (i, 0)
    return pl.pallas_call(
        _mlp_kernel,
        grid=(m // tm, ff // tf),
        in_specs=[
            pl.BlockSpec((tm, d), row),
            pl.BlockSpec((tm, d), row),
            pl.BlockSpec((d, tf), lambda i, j: (0, j)),
            pl.BlockSpec((tf, d), lambda i, j: (j, 0)),
            pl.BlockSpec((1, d), lambda i, j: (0, 0)),
        ],
        out_specs=pl.BlockSpec((tm, d), row),
        out_shape=jax.ShapeDtypeStruct((m, d), F32),
        scratch_shapes=[pltpu.VMEM((tm, d), F32)],
        compiler_params=_params(("arbitrary", "arbitrary")),
        name="mlp",
    )(h, x2, w_up, w_down, g_final)


def kernel(x_prompt, x_sample, cache_k, cache_v, cache_logf, cache_conv, cache_mem_k, cache_mem_v, mem_prompt, g_mix, w_in, b_f, conv_w, g_conv_out, g_attn_out, w_out, g_xattn, g_mem, w_xq, w_xkv, w_xo, g_mlp, w_up, w_down, g_final):
    depth = w_in.shape[0]
    assert depth == 1, "single-layer trunk"
    nb, seq, d = x_prompt.shape
    db, dseq, _ = x_sample.shape
    past = cache_k.shape[2]
    heads, hd = cache_k.shape[3], cache_k.shape[4]
    assert (heads, hd) == (N_FOX_HEADS, HEAD_DIM)
    d_attn = heads * hd
    d_conv = cache_conv.shape[-1]
    n_mem, xh, xhd = cache_mem_k.shape[2:]
    assert (xh, xhd) == (N_X_HEADS, HEAD_DIM)
    d_x = xh * xhd
    n_main = 3 * d_conv + 3 * d_attn
    assert d_conv == d_attn and w_in.shape[2] == n_main + heads

    w_in_t = jnp.swapaxes(w_in[0], 0, 1)
    bf = b_f[0].reshape(1, heads)
    w_out_b = w_out[0].astype(BF16)
    w_xq_b = w_xq[0].astype(BF16)
    w_xkv_b = w_xkv[0].astype(BF16)
    w_xo_b = w_xo[0].astype(BF16)
    w_up_b = w_up[0].astype(BF16)
    w_down_b = w_down[0].astype(BF16)
    row = lambda g: g.reshape(1, -1)

    def trunk(x, conv_buf, seq_len, attention, mk, mv, tm_mix, tm_x):
        xn, lf = _xnorm(x, row(g_mix[0]), w_in_t, bf, tm=512, gate_row=n_main)
        proj = functools.partial(_proj, xn, w_in_t, tn=d_attn, tm=1024)
        zc, = proj(j0=0, nj=3, mode="f32")
        qb, = proj(j0=3, nj=1, mode="bf16")
        k, kb = proj(j0=4, nj=1, mode="heads")
        v, vb = proj(j0=5, nj=1, mode="heads")
        ya = attention(qb, kb, vb, lf)
        x1, qx, conv_new = _mix(x, zc, conv_buf, ya, conv_w[0], row(g_conv_out[0]), row(g_attn_out[0]),
                                w_out_b, row(g_xattn[0]), w_xq_b, tm=tm_mix, seq=seq_len)
        x2, h = _xattn(x1, qx, mk, mv, w_xo_b, row(g_mlp[0]), tm=tm_x, seq=seq_len)
        y = _mlp(h, x2, w_up_b, w_down_b, row(g_final), tm=512, tf=1024)
        return y, k, v, lf, conv_new


    def prompt_attention(qb, kb, vb, lf):
        lft = lf.reshape(nb, seq, heads).transpose(0, 2, 1).reshape(nb * heads, seq)
        ct = _cumsum_lanes(lft).reshape(nb, heads, seq)
        return _fox_prompt(qb, kb, vb, ct, ct.transpose(0, 2, 1), n_batch=nb, seq=seq, tq=1024, tk=512)

    kv = _memkv(mem_prompt.reshape(nb * n_mem, d), row(g_mem[0]), w_xkv_b)
    mk_p = kv[0].reshape(nb, n_mem * xh, xhd)
    mv_p = kv[1].reshape(nb, n_mem * xh, xhd)
    conv0 = jnp.zeros((nb, CONV_WIDTH - 1, d_conv), F32)
    y_p, k_p, v_p, lf_p, conv_p = trunk(x_prompt.reshape(nb * seq, d), conv0, seq, prompt_attention,
                                        mk_p, mv_p, 256, 512)

    pad = (-(past + dseq)) % LANES

    def sample_attention(qb, kb, vb, lf):
        lft_past = cache_logf[0].transpose(0, 2, 1).reshape(db * heads, past)
        lft_new = lf.reshape(db, dseq, heads).transpose(0, 2, 1).reshape(db * heads, dseq)
        ct = _cumsum_lanes(jnp.concatenate(
            [lft_past, lft_new, jnp.zeros((db * heads, pad), F32)], axis=1))
        ck_past = ct[:, :past].reshape(db, heads, past)
        ck_new = ct[:, past:past + dseq].reshape(db, heads, dseq)
        cq = ck_new.transpose(0, 2, 1)
        return _fox_sample(qb, kb, vb, cache_k.reshape(db, past * heads, hd),
                           cache_v.reshape(db, past * heads, hd), cq, ck_past, ck_new)

    y_s, k_s, v_s, lf_s, conv_s = trunk(x_sample.reshape(db * dseq, d), cache_conv[0], dseq, sample_attention,
                                        cache_mem_k.reshape(db, n_mem * xh, xhd),
                                        cache_mem_v.reshape(db, n_mem * xh, xhd), 256, 256)

    return (y_p.reshape(nb, seq, d),
            y_s.reshape(db, dseq, d),
            k_p.reshape(1, nb, seq, heads, hd),
            v_p.reshape(1, nb, seq, heads, hd),
            lf_p.reshape(1, nb, seq, heads),
            conv_p[None],
            mk_p.reshape(1, nb, n_mem, xh, xhd),
            mv_p.reshape(1, nb, n_mem, xh, xhd),
            k_s.reshape(1, db, dseq, heads, hd),
            v_s.reshape(1, db, dseq, heads, hd),
            lf_s.reshape(1, db, dseq, heads),
            conv_s[None])
```

```python
import functools

import jax
import jax.numpy as jnp
from jax import lax
from jax.experimental import pallas as pl
from jax.experimental.pallas import tpu as pltpu

F32 = jnp.float32
BF16 = jnp.bfloat16

RMS_EPS = 1e-6
HEAD_DIM = 128
N_FOX_HEADS = 8
N_X_HEADS = 4
CONV_WIDTH = 3
LANES = 128
VMEM_LIMIT = 52 * 1024 * 1024

_NT = (((1,), (1,)), ((), ()))


def _params(semantics):
    return pltpu.CompilerParams(dimension_semantics=semantics, vmem_limit_bytes=VMEM_LIMIT)


def _rms(x, g):
    ms = jnp.mean(x * x, axis=-1, keepdims=True)
    return x * lax.rsqrt(ms + RMS_EPS) * g


def _log_sigmoid(x):
    return jnp.minimum(x, 0.0) - jnp.log1p(jnp.exp(-jnp.abs(x)))


def _store_head_rows(ref, val, heads):
    n = val.shape[0]
    for h in range(heads):
        ref[pl.ds(h, n, stride=heads), :] = val[:, h * HEAD_DIM:(h + 1) * HEAD_DIM]


def _load_head_rows(ref, lead, h, n, heads):
    return ref[lead, pl.ds(h, n, stride=heads), :]


def _xnorm_kernel(x_ref, g_ref, wf_ref, bf_ref, xn_ref, lf_ref):
    xn = _rms(x_ref[...], g_ref[...]).astype(BF16)
    xn_ref[...] = xn
    wf = wf_ref[...].astype(BF16)
    wf = jnp.concatenate([wf, jnp.zeros((LANES - wf.shape[0], wf.shape[1]), BF16)], axis=0)
    fz = lax.dot_general(xn, wf, _NT, preferred_element_type=F32)
    lf_ref[...] = _log_sigmoid(fz[:, :N_FOX_HEADS] + bf_ref[...])


def _xnorm(x, g, w_t, b_f, *, tm, gate_row):
    m, d = x.shape
    row = lambda i: (i, 0)
    return pl.pallas_call(
        _xnorm_kernel,
        grid=(m // tm,),
        in_specs=[
            pl.BlockSpec((tm, d), row),
            pl.BlockSpec((1, d), lambda i: (0, 0)),
            pl.BlockSpec((N_FOX_HEADS, d), lambda i: (gate_row // N_FOX_HEADS, 0)),
            pl.BlockSpec((1, N_FOX_HEADS), lambda i: (0, 0)),
        ],
        out_specs=[pl.BlockSpec((tm, d), row), pl.BlockSpec((tm, N_FOX_HEADS), row)],
        out_shape=[jax.ShapeDtypeStruct((m, d), BF16), jax.ShapeDtypeStruct((m, N_FOX_HEADS), F32)],
        compiler_params=_params(("arbitrary",)),
        name="xnorm",
    )(x, g, w_t, b_f)


def _proj_kernel(xn_ref, wt_ref, *refs, mode):
    out_refs, wb_ref = refs[:-1], refs[-1]

    @pl.when(pl.program_id(1) == 0)
    def _():
        wb_ref[...] = wt_ref[...].astype(BF16)

    res = lax.dot_general(xn_ref[...], wb_ref[...], _NT, preferred_element_type=F32)
    if mode == "f32":
        out_refs[0][...] = res
    elif mode == "bf16":
        out_refs[0][...] = res.astype(BF16)
    else:
        _store_head_rows(out_refs[0], res, N_FOX_HEADS)
        out_refs[1][...] = res.astype(BF16)


def _proj(xn, w_t, *, j0, nj, tn, tm, mode):
    m, d = xn.shape
    if mode == "heads":
        assert nj == 1
        out_specs = [pl.BlockSpec((tm * N_FOX_HEADS, HEAD_DIM), lambda j, i: (i, 0)),
                     pl.BlockSpec((tm, tn), lambda j, i: (i, 0))]
        out_shape = [jax.ShapeDtypeStruct((m * N_FOX_HEADS, HEAD_DIM), F32), jax.ShapeDtypeStruct((m, tn), BF16)]
    else:
        out_specs = [pl.BlockSpec((tm, tn), lambda j, i: (i, j))]
        out_shape = [jax.ShapeDtypeStruct((m, nj * tn), F32 if mode == "f32" else BF16)]
    return pl.pallas_call(
        functools.partial(_proj_kernel, mode=mode),
        grid=(nj, m // tm),
        in_specs=[
            pl.BlockSpec((tm, d), lambda j, i: (i, 0)),
            pl.BlockSpec((tn, d), lambda j, i: (j0 + j, 0)),
        ],
        out_specs=out_specs,
        out_shape=out_shape,
        scratch_shapes=[pltpu.VMEM((tn, d), BF16)],
        compiler_params=_params(("arbitrary", "arbitrary")),
        name="proj_" + mode,
    )(xn, w_t)


def _cumsum_kernel(a_ref, o_ref):
    rows, length = a_ref.shape
    r = lax.broadcasted_iota(jnp.int32, (LANES, LANES), 0)
    c = lax.broadcasted_iota(jnp.int32, (LANES, LANES), 1)
    upper = (r <= c).astype(BF16)
    carry = jnp.zeros((rows, 1), F32)
    for b in range(length // LANES):
        a = a_ref[:, b * LANES:(b + 1) * LANES]
        hi = a.astype(BF16)
        r1 = a - hi.astype(F32)
        mid = r1.astype(BF16)
        lo = (r1 - mid.astype(F32)).astype(BF16)
        s = (jnp.dot(hi, upper, preferred_element_type=F32)
             + jnp.dot(mid, upper, preferred_element_type=F32)
             + jnp.dot(lo, upper, preferred_element_type=F32)) + carry
        o_ref[:, b * LANES:(b + 1) * LANES] = s
        carry = s[:, LANES - 1:LANES]


def _cumsum_lanes(a):
    return pl.pallas_call(
        _cumsum_kernel,
        out_shape=jax.ShapeDtypeStruct(a.shape, F32),
        compiler_params=pltpu.CompilerParams(vmem_limit_bytes=VMEM_LIMIT),
        name="cumsum",
    )(a)


def _split3(x):
    hi = x.astype(BF16).astype(F32)
    r = x - hi
    mid = r.astype(BF16).astype(F32)
    lo = (r - mid).astype(BF16).astype(F32)
    return hi, mid, lo


def _fox_prompt_kernel(q_ref, k_ref, v_ref, cq_ref, ck_ref, o_ref, kaug_ref, vt_ref, qaug_ref, acc_ref, s0_ref, s1_ref,
                       *, tq, tk):
    assert tq == 2 * tk
    h = pl.program_id(1)
    qi = pl.program_id(2)
    nk = kaug_ref.shape[0] // tk
    inv_scale = HEAD_DIM ** 0.5
    exp2_scale = (HEAD_DIM ** -0.5) * 1.4426950408889634

    @pl.when(qi == 0)
    def _():
        lane8 = lax.broadcasted_iota(jnp.int32, (tk, N_FOX_HEADS), 1)
        lane = lax.broadcasted_iota(jnp.int32, (tk, HEAD_DIM), 1)

        def prep(c, carry):
            rows = pl.ds(pl.multiple_of(c * tk, tk), tk)
            kaug_ref[rows, 0:HEAD_DIM] = k_ref[rows, :]
            col = jnp.sum(jnp.where(lane8 == h, ck_ref[0, rows, :], 0.0), axis=1, keepdims=True) * inv_scale
            hi, mid, lo = _split3(col)
            aug = jnp.where(lane == 0, hi, jnp.where(lane == 1, mid, jnp.where(lane == 2, lo,
                            jnp.where(lane < 6, 1.0, 0.0))))
            kaug_ref[rows, HEAD_DIM:2 * HEAD_DIM] = aug.astype(BF16)
            vt_ref[c] = v_ref[rows, :].astype(F32).T.astype(BF16)
            return carry

        lax.fori_loop(0, nk, prep, 0)

    qaug_ref[0:HEAD_DIM, :] = q_ref[...].astype(F32).T.astype(BF16)
    hi, mid, lo = _split3(cq_ref[0, pl.ds(h, 1), :] * inv_scale)
    sub = lax.broadcasted_iota(jnp.int32, (HEAD_DIM, tq), 0)
    qaug_ref[HEAD_DIM:2 * HEAD_DIM, :] = jnp.where(
        sub < 3, -1.0, jnp.where(sub == 3, hi, jnp.where(sub == 4, mid, jnp.where(sub == 5, lo, 0.0)))).astype(BF16)
    acc_ref[...] = jnp.zeros(acc_ref.shape, F32)

    def scores(ki, dst_ref, lane0=0):
        rows = pl.ds(pl.multiple_of(ki * tk, tk), tk)
        dst_ref[:, lane0:] = jnp.dot(kaug_ref[rows, :], qaug_ref[:, lane0:], preferred_element_type=F32)

    def softmax_pv(ki, ml, s_ref, lane0=0, mask_offset=None):
        m_prev, l_prev = ml[0][:, lane0:], ml[1][:, lane0:]
        s = s_ref[:, lane0:]
        if mask_offset is not None:
            r = lax.broadcasted_iota(jnp.int32, s.shape, 0) + mask_offset
            c = lax.broadcasted_iota(jnp.int32, s.shape, 1) + lane0
            s = jnp.where(r <= c, s, -jnp.inf)
        m_new = jnp.maximum(m_prev, jnp.max(s, axis=0, keepdims=True))
        alpha = jnp.exp2((m_prev - m_new) * exp2_scale)
        p = jnp.exp2((s - m_new) * exp2_scale)
        l_new = alpha * l_prev + jnp.sum(p, axis=0, keepdims=True)
        pv = jnp.dot(vt_ref[ki], p.astype(BF16), preferred_element_type=F32)
        acc_ref[:, lane0:] = alpha * acc_ref[:, lane0:] + pv
        if lane0:
            m_new = jnp.concatenate([ml[0][:, :lane0], m_new], axis=1)
            l_new = jnp.concatenate([ml[1][:, :lane0], l_new], axis=1)
        return m_new, l_new

    def pair(j, ml):
        scores(2 * j + 1, s1_ref)
        ml = softmax_pv(2 * j, ml, s0_ref)
        scores(2 * j + 2, s0_ref)
        return softmax_pv(2 * j + 1, ml, s1_ref)

    ml = (jnp.full((1, tq), -jnp.inf, F32), jnp.zeros((1, tq), F32))
    scores(0, s0_ref)
    ml = lax.fori_loop(0, qi, pair, ml)
    scores(2 * qi + 1, s1_ref, lane0=tk)
    ml = softmax_pv(2 * qi, ml, s0_ref, mask_offset=0)
    _, l = softmax_pv(2 * qi + 1, ml, s1_ref, lane0=tk, mask_offset=tk)
    o_ref[...] = (acc_ref[...] / l).T


def _fox_prompt(qb, kb, vb, cq, ck, *, n_batch, seq, tq, tk):
    m = qb.shape[0]
    nq = seq // tq
    return pl.pallas_call(
        functools.partial(_fox_prompt_kernel, tq=tq, tk=tk),
        grid=(n_batch, N_FOX_HEADS, nq),
        in_specs=[
            pl.BlockSpec((tq, HEAD_DIM), lambda n, h, i: (n * nq + i, h)),
            pl.BlockSpec((seq, HEAD_DIM), lambda n, h, i: (n, h)),
            pl.BlockSpec((seq, HEAD_DIM), lambda n, h, i: (n, h)),
            pl.BlockSpec((1, N_FOX_HEADS, tq), lambda n, h, i: (n, 0, i)),
            pl.BlockSpec((1, seq, N_FOX_HEADS), lambda n, h, i: (n, 0, 0)),
        ],
        out_specs=pl.BlockSpec((tq, HEAD_DIM), lambda n, h, i: (n * nq + i, h)),
        out_shape=jax.ShapeDtypeStruct((m, N_FOX_HEADS * HEAD_DIM), F32),
        scratch_shapes=[
            pltpu.VMEM((seq, 2 * HEAD_DIM), BF16),
            pltpu.VMEM((seq // tk, HEAD_DIM, tk), BF16),
            pltpu.VMEM((2 * HEAD_DIM, tq), BF16),
            pltpu.VMEM((HEAD_DIM, tq), F32),
            pltpu.VMEM((tk, tq), F32),
            pltpu.VMEM((tk, tq), F32),
        ],
        compiler_params=_params(("arbitrary", "arbitrary", "arbitrary")),
        name="fox_prompt",
    )(qb, kb, vb, cq, ck)


def _fox_sample_kernel(q_ref, kn_ref, vn_ref, kc_ref, vc_ref, cq_ref, ckp_ref, ckn_ref, o_ref, *, past):
    scale = HEAD_DIM ** -0.5
    t = q_ref.shape[0]
    half = N_FOX_HEADS // 2
    r = lax.broadcasted_iota(jnp.int32, (2 * t, 2 * past), 0)
    c = lax.broadcasted_iota(jnp.int32, (2 * t, 2 * past), 1)
    own_past = (r >= t) == ((c & 1) == 1)
    rn = lax.broadcasted_iota(jnp.int32, (2 * t, 2 * t), 0)
    cn = lax.broadcasted_iota(jnp.int32, (2 * t, 2 * t), 1)
    own_new = ((rn >= t) == (cn >= t)) & ((cn & (t - 1)) <= (rn & (t - 1)))
    for j in range(half):
        sl0 = slice(j * HEAD_DIM, (j + 1) * HEAD_DIM)
        sl1 = slice((j + half) * HEAD_DIM, (j + half + 1) * HEAD_DIM)
        both = lambda ref: jnp.concatenate([ref[:, sl0], ref[:, sl1]], axis=0)
        q = both(q_ref)
        kp = kc_ref[0, pl.ds(j, 2 * past, stride=half), :].astype(BF16)
        vp = vc_ref[0, pl.ds(j, 2 * past, stride=half), :].astype(BF16)
        cq = jnp.concatenate([cq_ref[0, :, j:j + 1], cq_ref[0, :, j + half:j + half + 1]], axis=0)
        s1 = lax.dot_general(q, kp, _NT, preferred_element_type=F32) * scale
        s1 = jnp.where(own_past, s1 + (cq - ckp_ref[0, j:j + 1, :]), -jnp.inf)
        s2 = lax.dot_general(q, both(kn_ref), _NT, preferred_element_type=F32) * scale
        s2 = jnp.where(own_new, s2 + (cq - ckn_ref[0, j:j + 1, :]), -jnp.inf)
        m = jnp.maximum(jnp.max(s1, axis=1, keepdims=True), jnp.max(s2, axis=1, keepdims=True))
        p1 = jnp.exp(s1 - m)
        p2 = jnp.exp(s2 - m)
        l = jnp.sum(p1, axis=1, keepdims=True) + jnp.sum(p2, axis=1, keepdims=True)
        o = (jnp.dot(p1.astype(BF16), vp, preferred_element_type=F32)
             + jnp.dot(p2.astype(BF16), both(vn_ref), preferred_element_type=F32)) / l
        o_ref[:, sl0] = o[:t]
        o_ref[:, sl1] = o[t:]


def _fox_sample(qb, knb, vnb, cache_k, cache_v, cq, ck_past, ck_new):
    n_batch, rows, _ = cache_k.shape
    past = rows // N_FOX_HEADS
    t = qb.shape[0] // n_batch
    width = qb.shape[1]
    tok = lambda b: (b, 0)
    lead = lambda b: (b, 0, 0)
    return pl.pallas_call(
        functools.partial(_fox_sample_kernel, past=past),
        grid=(n_batch,),
        in_specs=[
            pl.BlockSpec((t, width), tok),
            pl.BlockSpec((t, width), tok),
            pl.BlockSpec((t, width), tok),
            pl.BlockSpec((1, rows, HEAD_DIM), lead),
            pl.BlockSpec((1, rows, HEAD_DIM), lead),
            pl.BlockSpec((1, t, N_FOX_HEADS), lead),
            pl.BlockSpec((1, N_FOX_HEADS // 2, 2 * past), lead),
            pl.BlockSpec((1, N_FOX_HEADS // 2, 2 * t), lead),
        ],
        out_specs=pl.BlockSpec((t, width), tok),
        out_shape=jax.ShapeDtypeStruct(qb.shape, F32),
        compiler_params=_params(("arbitrary",)),
        name="fox_sample",
    )(qb, knb, vnb, cache_k, cache_v, cq, ck_past, ck_new)


def _mix_kernel(*refs, seg, nseg, tiles_per_seq, has_prev):
    if has_prev:
        (x_ref, gb_ref, gc_ref, hin_ref, pgc_ref, phin_ref, buf_ref, ya_ref, cw_ref, gco_ref, gao_ref,
         wout_ref, gx_ref, wxq_ref, x1_ref, qx_ref, cnew_ref) = refs
    else:
        (x_ref, gb_ref, gc_ref, hin_ref, buf_ref, ya_ref, cw_ref, gco_ref, gao_ref,
         wout_ref, gx_ref, wxq_ref, x1_ref, qx_ref, cnew_ref) = refs
    i = pl.program_id(0)
    u = gc_ref[...] * hin_ref[...]
    gb = gb_ref[...]
    w0 = cw_ref[0:1, :]
    w1 = cw_ref[1:2, :]
    w2 = cw_ref[2:3, :]
    rid = lax.broadcasted_iota(jnp.int32, (seg, u.shape[1]), 0)
    pieces = []
    for s in range(nseg):
        u_s = u[s * seg:(s + 1) * seg]
        b0 = buf_ref[s, 0:1, :]
        b1 = buf_ref[s, 1:2, :]
        if has_prev:
            pu = pgc_ref[...] * phin_ref[...]
            first = (i % tiles_per_seq) == 0
            b0 = jnp.where(first, b0, pu[6:7])
            b1 = jnp.where(first, b1, pu[7:8])
        r1 = jnp.where(rid == 0, b1, pltpu.roll(u_s, 1, 0))
        r2 = jnp.where(rid == 0, b0, jnp.where(rid == 1, b1, pltpu.roll(u_s, 2, 0)))
        y = w0 * r2 + w1 * r1 + w2 * u_s
        pieces.append(gb[s * seg:(s + 1) * seg] * y)
        cnew_ref[s] = u_s[seg - 2:seg]
    yc = pieces[0] if nseg == 1 else jnp.concatenate(pieces, axis=0)
    cat = jnp.concatenate([_rms(yc, gco_ref[...]).astype(BF16),
                           _rms(ya_ref[...], gao_ref[...]).astype(BF16)], axis=-1)
    x1 = x_ref[...] + jnp.dot(cat, wout_ref[...], preferred_element_type=F32)
    x1_ref[...] = x1
    xn = _rms(x1, gx_ref[...]).astype(BF16)
    qx_ref[...] = jnp.dot(xn, wxq_ref[...], preferred_element_type=F32).astype(BF16)


def _mix(x, zc, conv_buf, ya, conv_w, g_conv_out, g_attn_out, w_out, g_xattn, w_xq, *, tm, seq):
    m, d = x.shape
    dc = zc.shape[1] // 3
    dx = w_xq.shape[1]
    has_prev = seq > tm
    seg = tm if has_prev else seq
    nseg = tm // seg
    tiles_per_seq = max(seq // tm, 1)
    row = lambda i: (i, 0)
    const = lambda i: (0, 0)
    in_specs = [
        pl.BlockSpec((tm, d), row),
        pl.BlockSpec((tm, dc), lambda i: (i, 0)),
        pl.BlockSpec((tm, dc), lambda i: (i, 1)),
        pl.BlockSpec((tm, dc), lambda i: (i, 2)),
    ]
    args = [x, zc, zc, zc]
    if has_prev:
        prev = lambda col: (lambda i: (jnp.maximum(i * (tm // 8) - 1, 0), col))
        in_specs += [pl.BlockSpec((8, dc), prev(1)), pl.BlockSpec((8, dc), prev(2))]
        args += [zc, zc]
        buf_map = lambda i: (i // tiles_per_seq, 0, 0)
    else:
        buf_map = lambda i: (i, 0, 0)
    in_specs += [
        pl.BlockSpec((nseg, CONV_WIDTH - 1, dc), buf_map),
        pl.BlockSpec((tm, ya.shape[1]), row),
        pl.BlockSpec((CONV_WIDTH, dc), const),
        pl.BlockSpec((1, dc), const),
        pl.BlockSpec((1, ya.shape[1]), const),
        pl.BlockSpec(w_out.shape, const),
        pl.BlockSpec((1, d), const),
        pl.BlockSpec(w_xq.shape, const),
    ]
    args += [conv_buf, ya, conv_w, g_conv_out, g_attn_out, w_out, g_xattn, w_xq]
    return pl.pallas_call(
        functools.partial(_mix_kernel, seg=seg, nseg=nseg, tiles_per_seq=tiles_per_seq, has_prev=has_prev),
        grid=(m // tm,),
        in_specs=in_specs,
        out_specs=[
            pl.BlockSpec((tm, d), row),
            pl.BlockSpec((tm, dx), row),
            pl.BlockSpec((nseg, CONV_WIDTH - 1, dc), buf_map),
        ],
        out_shape=[
            jax.ShapeDtypeStruct((m, d), F32),
            jax.ShapeDtypeStruct((m, dx), BF16),
            jax.ShapeDtypeStruct(conv_buf.shape, F32),
        ],
        compiler_params=_params(("arbitrary",)),
        name="mix",
    )(*args)


def _memkv_kernel(x_ref, g_ref, w_ref, o_ref):
    xn = _rms(x_ref[...], g_ref[...]).astype(BF16)
    _store_head_rows(o_ref.at[0], jnp.dot(xn, w_ref[...], preferred_element_type=F32), N_X_HEADS)


def _memkv(mem, g, w):
    m, d = mem.shape
    half = w.shape[1] // 2
    return pl.pallas_call(
        _memkv_kernel,
        grid=(2,),
        in_specs=[
            pl.BlockSpec((m, d), lambda j: (0, 0)),
            pl.BlockSpec((1, d), lambda j: (0, 0)),
            pl.BlockSpec((d, half), lambda j: (0, j)),
        ],
        out_specs=pl.BlockSpec((1, m * N_X_HEADS, HEAD_DIM), lambda j: (j, 0, 0)),
        out_shape=jax.ShapeDtypeStruct((2, m * N_X_HEADS, HEAD_DIM), F32),
        compiler_params=_params(("arbitrary",)),
        name="memkv",
    )(mem, g, w)


def _xattn_kernel(x1_ref, qx_ref, mk_ref, mv_ref, wxo_ref, g_ref, x2_ref, h_ref, o_scr, *, seg, nseg):
    scale = HEAD_DIM ** -0.5
    n_mem = mk_ref.shape[1] // N_X_HEADS
    for s in range(nseg):
        rows = slice(s * seg, (s + 1) * seg)
        for hh in range(N_X_HEADS):
            sl = slice(hh * HEAD_DIM, (hh + 1) * HEAD_DIM)
            q = qx_ref[rows, sl]
            kh = _load_head_rows(mk_ref, s, hh, n_mem, N_X_HEADS).astype(BF16)
            vh = _load_head_rows(mv_ref, s, hh, n_mem, N_X_HEADS).astype(BF16)
            sc = lax.dot_general(q, kh, _NT, preferred_element_type=F32) * scale
            m = jnp.max(sc, axis=1, keepdims=True)
            p = jnp.exp(sc - m)
            l = jnp.sum(p, axis=1, keepdims=True)
            o = jnp.dot(p.astype(BF16), vh, preferred_element_type=F32) / l
            o_scr[rows, sl] = o.astype(BF16)
    x2 = x1_ref[...] + jnp.dot(o_scr[...], wxo_ref[...], preferred_element_type=F32)
    x2_ref[...] = x2
    h_ref[...] = _rms(x2, g_ref[...]).astype(BF16)


def _xattn(x1, qx, mk, mv, w_xo, g_mlp, *, tm, seq):
    m, d = x1.shape
    dx = qx.shape[1]
    mem_rows = mk.shape[1]
    seg = min(tm, seq)
    nseg = tm // seg
    tiles_per_seq = max(seq // tm, 1)
    row = lambda i: (i, 0)
    const = lambda i: (0, 0)
    mem_map = (lambda i: (i // tiles_per_seq, 0, 0)) if nseg == 1 else (lambda i: (i, 0, 0))
    return pl.pallas_call(
        functools.partial(_xattn_kernel, seg=seg, nseg=nseg),
        grid=(m // tm,),
        in_specs=[
            pl.BlockSpec((tm, d), row),
            pl.BlockSpec((tm, dx), row),
            pl.BlockSpec((nseg, mem_rows, HEAD_DIM), mem_map),
            pl.BlockSpec((nseg, mem_rows, HEAD_DIM), mem_map),
            pl.BlockSpec(w_xo.shape, const),
            pl.BlockSpec((1, d), const),
        ],
        out_specs=[pl.BlockSpec((tm, d), row), pl.BlockSpec((tm, d), row)],
        out_shape=[jax.ShapeDtypeStruct((m, d), F32), jax.ShapeDtypeStruct((m, d), BF16)],
        scratch_shapes=[pltpu.VMEM((tm, dx), BF16)],
        compiler_params=_params(("arbitrary",)),
        name="xattn",
    )(x1, qx, mk, mv, w_xo, g_mlp)


def _mlp_kernel(h_ref, x2_ref, wu_ref, wd_ref, g_ref, y_ref, acc_ref):
    j = pl.program_id(1)

    @pl.when(j == 0)
    def _():
        acc_ref[...] = jnp.zeros(acc_ref.shape, F32)

    a = jnp.dot(h_ref[...], wu_ref[...], preferred_element_type=F32)
    a = jnp.square(jnp.maximum(a, 0.0)).astype(BF16)
    acc_ref[...] += jnp.dot(a, wd_ref[...], preferred_element_type=F32)

    @pl.when(j == pl.num_programs(1) - 1)
    def _():
        y_ref[...] = _rms(x2_ref[...] + acc_ref[...], g_ref[...])


def _mlp(h, x2, w_up, w_down, g_final, *, tm, tf):
    m, d = h.shape
    ff = w_up.shape[1]
    row = lambda i, j: (i, 0)
    return pl.pallas_call(
        _mlp_kernel,
        grid=(m // tm, ff // tf),
        in_specs=[
            pl.BlockSpec((tm, d), row),
            pl.BlockSpec((tm, d), row),
            pl.BlockSpec((d, tf), lambda i, j: (0, j)),
            pl.BlockSpec((tf, d), lambda i, j: (j, 0)),
            pl.BlockSpec((1, d), lambda i, j: (0, 0)),
        ],
        out_specs=pl.BlockSpec((tm, d), row),
        out_shape=jax.ShapeDtypeStruct((m, d), F32),
        scratch_shapes=[pltpu.VMEM((tm, d), F32)],
        compiler_params=_params(("arbitrary", "arbitrary")),
        name="mlp",
    )(h, x2, w_up, w_down, g_final)


def kernel(x_prompt, x_sample, cache_k, cache_v, cache_logf, cache_conv, cache_mem_k, cache_mem_v, mem_prompt, g_mix, w_in, b_f, conv_w, g_conv_out, g_attn_out, w_out, g_xattn, g_mem, w_xq, w_xkv, w_xo, g_mlp, w_up, w_down, g_final):
    depth = w_in.shape[0]
    assert depth == 1, "single-layer trunk"
    nb, seq, d = x_prompt.shape
    db, dseq, _ = x_sample.shape
    past = cache_k.shape[2]
    heads, hd = cache_k.shape[3], cache_k.shape[4]
    assert (heads, hd) == (N_FOX_HEADS, HEAD_DIM)
    d_attn = heads * hd
    d_conv = cache_conv.shape[-1]
    n_mem, xh, xhd = cache_mem_k.shape[2:]
    assert (xh, xhd) == (N_X_HEADS, HEAD_DIM)
    d_x = xh * xhd
    n_main = 3 * d_conv + 3 * d_attn
    assert d_conv == d_attn and w_in.shape[2] == n_main + heads

    w_in_t = jnp.swapaxes(w_in[0], 0, 1)
    bf = b_f[0].reshape(1, heads)
    w_out_b = w_out[0].astype(BF16)
    w_xq_b = w_xq[0].astype(BF16)
    w_xkv_b = w_xkv[0].astype(BF16)
    w_xo_b = w_xo[0].astype(BF16)
    w_up_b = w_up[0].astype(BF16)
    w_down_b = w_down[0].astype(BF16)
    row = lambda g: g.reshape(1, -1)

    def trunk(x, conv_buf, seq_len, attention, mk, mv, tm_mix, tm_x):
        xn, lf = _xnorm(x, row(g_mix[0]), w_in_t, bf, tm=512, gate_row=n_main)
        proj = functools.partial(_proj, xn, w_in_t, tn=d_attn, tm=1024)
        zc, = proj(j0=0, nj=3, mode="f32")
        qb, = proj(j0=3, nj=1, mode="bf16")
        k, kb = proj(j0=4, nj=1, mode="heads")
        v, vb = proj(j0=5, nj=1, mode="heads")
        ya = attention(qb, kb, vb, lf)
        x1, qx, conv_new = _mix(x, zc, conv_buf, ya, conv_w[0], row(g_conv_out[0]), row(g_attn_out[0]),
                                w_out_b, row(g_xattn[0]), w_xq_b, tm=tm_mix, seq=seq_len)
        x2, h = _xattn(x1, qx, mk, mv, w_xo_b, row(g_mlp[0]), tm=tm_x, seq=seq_len)
        y = _mlp(h, x2, w_up_b, w_down_b, row(g_final), tm=512, tf=1024)
        return y, k, v, lf, conv_new


    def prompt_attention(qb, kb, vb, lf):
        lft = lf.reshape(nb, seq, heads).transpose(0, 2, 1).reshape(nb * heads, seq)
        ct = _cumsum_lanes(lft).reshape(nb, heads, seq)
        return _fox_prompt(qb, kb, vb, ct, ct.transpose(0, 2, 1), n_batch=nb, seq=seq, tq=1024, tk=512)

    kv = _memkv(mem_prompt.reshape(nb * n_mem, d), row(g_mem[0]), w_xkv_b)
    mk_p = kv[0].reshape(nb, n_mem * xh, xhd)
    mv_p = kv[1].reshape(nb, n_mem * xh, xhd)
    conv0 = jnp.zeros((nb, CONV_WIDTH - 1, d_conv), F32)
    y_p, k_p, v_p, lf_p, conv_p = trunk(x_prompt.reshape(nb * seq, d), conv0, seq, prompt_attention,
                                        mk_p, mv_p, 256, 512)

    pad = (-(past + dseq)) % LANES

    def sample_attention(qb, kb, vb, lf):
        lft_past = cache_logf[0].transpose(0, 2, 1).reshape(db * heads, past)
        lft_new = lf.reshape(db, dseq, heads).transpose(0, 2, 1).reshape(db * heads, dseq)
        ct = _cumsum_lanes(jnp.concatenate(
            [lft_past, lft_new, jnp.zeros((db * heads, pad), F32)], axis=1))
        ck_past = ct[:, :past].reshape(db, heads, past)
        ck_new = ct[:, past:past + dseq].reshape(db, heads, dseq)
        cq = ck_new.transpose(0, 2, 1)
        hp = heads // 2
        ck_past_pairs = jnp.stack([ck_past[:, :hp], ck_past[:, hp:]], axis=-1).reshape(db, hp, 2 * past)
        ck_new_pairs = jnp.concatenate([ck_new[:, :hp], ck_new[:, hp:]], axis=-1)
        return _fox_sample(qb, kb, vb, cache_k.reshape(db, past * heads, hd),
                           cache_v.reshape(db, past * heads, hd), cq, ck_past_pairs, ck_new_pairs)

    y_s, k_s, v_s, lf_s, conv_s = trunk(x_sample.reshape(db * dseq, d), cache_conv[0], dseq, sample_attention,
                                        cache_mem_k.reshape(db, n_mem * xh, xhd),
                                        cache_mem_v.reshape(db, n_mem * xh, xhd), 256, 256)

    return (y_p.reshape(nb, seq, d),
            y_s.reshape(db, dseq, d),
            k_p.reshape(1, nb, seq, heads, hd),
            v_p.reshape(1, nb, seq, heads, hd),
            lf_p.reshape(1, nb, seq, heads),
            conv_p[None],
            mk_p.reshape(1, nb, n_mem, xh, xhd),
            mv_p.reshape(1, nb, n_mem, xh, xhd),
            k_s.reshape(1, db, dseq, heads, hd),
            v_s.reshape(1, db, dseq, heads, hd),
            lf_s.reshape(1, db, dseq, heads),
            conv_s[None])
```

```python
import functools

import jax
import jax.numpy as jnp
from jax import lax
from jax.experimental import pallas as pl
from jax.experimental.pallas import tpu as pltpu

F32 = jnp.float32
BF16 = jnp.bfloat16

RMS_EPS = 1e-6
HEAD_DIM = 128
N_FOX_HEADS = 8
N_X_HEADS = 4
CONV_WIDTH = 3
LANES = 128
VMEM_LIMIT = 52 * 1024 * 1024

_NT = (((1,), (1,)), ((), ()))


def _params(semantics):
    return pltpu.CompilerParams(dimension_semantics=semantics, vmem_limit_bytes=VMEM_LIMIT)


def _rms(x, g):
    ms = jnp.mean(x * x, axis=-1, keepdims=True)
    return x * lax.rsqrt(ms + RMS_EPS) * g


def _log_sigmoid(x):
    return jnp.minimum(x, 0.0) - jnp.log1p(jnp.exp(-jnp.abs(x)))


def _store_head_rows(ref, val, heads):
    n = val.shape[0]
    for h in range(heads):
        ref[pl.ds(h, n, stride=heads), :] = val[:, h * HEAD_DIM:(h + 1) * HEAD_DIM]


def _load_head_rows(ref, lead, h, n, heads):
    return ref[lead, pl.ds(h, n, stride=heads), :]


def _xnorm_kernel(x_ref, g_ref, wf_ref, bf_ref, xn_ref, lf_ref):
    xn = _rms(x_ref[...], g_ref[...]).astype(BF16)
    xn_ref[...] = xn
    wf = wf_ref[...].astype(BF16)
    wf = jnp.concatenate([wf, jnp.zeros((LANES - wf.shape[0], wf.shape[1]), BF16)], axis=0)
    fz = lax.dot_general(xn, wf, _NT, preferred_element_type=F32)
    lf_ref[...] = _log_sigmoid(fz[:, :N_FOX_HEADS] + bf_ref[...])


def _xnorm(x, g, w_t, b_f, *, tm, gate_row):
    m, d = x.shape
    row = lambda i: (i, 0)
    return pl.pallas_call(
        _xnorm_kernel,
        grid=(m // tm,),
        in_specs=[
            pl.BlockSpec((tm, d), row),
            pl.BlockSpec((1, d), lambda i: (0, 0)),
            pl.BlockSpec((N_FOX_HEADS, d), lambda i: (gate_row // N_FOX_HEADS, 0)),
            pl.BlockSpec((1, N_FOX_HEADS), lambda i: (0, 0)),
        ],
        out_specs=[pl.BlockSpec((tm, d), row), pl.BlockSpec((tm, N_FOX_HEADS), row)],
        out_shape=[jax.ShapeDtypeStruct((m, d), BF16), jax.ShapeDtypeStruct((m, N_FOX_HEADS), F32)],
        compiler_params=_params(("arbitrary",)),
        name="xnorm",
    )(x, g, w_t, b_f)


def _proj_kernel(xn_ref, wt_ref, *refs, mode):
    out_refs, wb_ref = refs[:-1], refs[-1]

    @pl.when(pl.program_id(1) == 0)
    def _():
        wb_ref[...] = wt_ref[...].astype(BF16)

    res = lax.dot_general(xn_ref[...], wb_ref[...], _NT, preferred_element_type=F32)
    if mode == "f32":
        out_refs[0][...] = res
    elif mode == "bf16":
        out_refs[0][...] = res.astype(BF16)
    else:
        _store_head_rows(out_refs[0], res, N_FOX_HEADS)
        out_refs[1][...] = res.astype(BF16)


def _proj(xn, w_t, *, j0, nj, tn, tm, mode):
    m, d = xn.shape
    if mode == "heads":
        assert nj == 1
        out_specs = [pl.BlockSpec((tm * N_FOX_HEADS, HEAD_DIM), lambda j, i: (i, 0)),
                     pl.BlockSpec((tm, tn), lambda j, i: (i, 0))]
        out_shape = [jax.ShapeDtypeStruct((m * N_FOX_HEADS, HEAD_DIM), F32), jax.ShapeDtypeStruct((m, tn), BF16)]
    else:
        out_specs = [pl.BlockSpec((tm, tn), lambda j, i: (i, j))]
        out_shape = [jax.ShapeDtypeStruct((m, nj * tn), F32 if mode == "f32" else BF16)]
    return pl.pallas_call(
        functools.partial(_proj_kernel, mode=mode),
        grid=(nj, m // tm),
        in_specs=[
            pl.BlockSpec((tm, d), lambda j, i: (i, 0)),
            pl.BlockSpec((tn, d), lambda j, i: (j0 + j, 0)),
        ],
        out_specs=out_specs,
        out_shape=out_shape,
        scratch_shapes=[pltpu.VMEM((tn, d), BF16)],
        compiler_params=_params(("arbitrary", "arbitrary")),
        name="proj_" + mode,
    )(xn, w_t)


def _cumsum_kernel(a_ref, o_ref):
    rows, length = a_ref.shape
    r = lax.broadcasted_iota(jnp.int32, (LANES, LANES), 0)
    c = lax.broadcasted_iota(jnp.int32, (LANES, LANES), 1)
    upper = (r <= c).astype(BF16)
    carry = jnp.zeros((rows, 1), F32)
    for b in range(length // LANES):
        a = a_ref[:, b * LANES:(b + 1) * LANES]
        hi = a.astype(BF16)
        r1 = a - hi.astype(F32)
        mid = r1.astype(BF16)
        lo = (r1 - mid.astype(F32)).astype(BF16)
        s = (jnp.dot(hi, upper, preferred_element_type=F32)
             + jnp.dot(mid, upper, preferred_element_type=F32)
             + jnp.dot(lo, upper, preferred_element_type=F32)) + carry
        o_ref[:, b * LANES:(b + 1) * LANES] = s
        carry = s[:, LANES - 1:LANES]


def _cumsum_lanes(a):
    return pl.pallas_call(
        _cumsum_kernel,
        out_shape=jax.ShapeDtypeStruct(a.shape, F32),
        compiler_params=pltpu.CompilerParams(vmem_limit_bytes=VMEM_LIMIT),
        name="cumsum",
    )(a)


def _split3(x):
    hi = x.astype(BF16).astype(F32)
    r = x - hi
    mid = r.astype(BF16).astype(F32)
    lo = (r - mid).astype(BF16).astype(F32)
    return hi, mid, lo


def _fox_prompt_kernel(q_ref, k_ref, v_ref, cq_ref, ck_ref, o_ref, kaug_ref, vt_ref, qaug_ref, acc_ref, s0_ref, s1_ref,
                       *, tq, tk):
    assert tq == 2 * tk
    h = pl.program_id(1)
    qi = pl.program_id(2)
    nk = kaug_ref.shape[0] // tk
    inv_scale = HEAD_DIM ** 0.5
    exp2_scale = (HEAD_DIM ** -0.5) * 1.4426950408889634

    @pl.when(qi == 0)
    def _():
        lane8 = lax.broadcasted_iota(jnp.int32, (tk, N_FOX_HEADS), 1)
        lane = lax.broadcasted_iota(jnp.int32, (tk, HEAD_DIM), 1)

        def prep(c, carry):
            rows = pl.ds(pl.multiple_of(c * tk, tk), tk)
            kaug_ref[rows, 0:HEAD_DIM] = k_ref[rows, :]
            col = jnp.sum(jnp.where(lane8 == h, ck_ref[0, rows, :], 0.0), axis=1, keepdims=True) * inv_scale
            hi, mid, lo = _split3(col)
            aug = jnp.where(lane == 0, hi, jnp.where(lane == 1, mid, jnp.where(lane == 2, lo,
                            jnp.where(lane < 6, 1.0, 0.0))))
            kaug_ref[rows, HEAD_DIM:2 * HEAD_DIM] = aug.astype(BF16)
            vt_ref[c] = v_ref[rows, :].astype(F32).T.astype(BF16)
            return carry

        lax.fori_loop(0, nk, prep, 0)

    qaug_ref[0:HEAD_DIM, :] = q_ref[...].astype(F32).T.astype(BF16)
    hi, mid, lo = _split3(cq_ref[0, pl.ds(h, 1), :] * inv_scale)
    sub = lax.broadcasted_iota(jnp.int32, (HEAD_DIM, tq), 0)
    qaug_ref[HEAD_DIM:2 * HEAD_DIM, :] = jnp.where(
        sub < 3, -1.0, jnp.where(sub == 3, hi, jnp.where(sub == 4, mid, jnp.where(sub == 5, lo, 0.0)))).astype(BF16)
    acc_ref[...] = jnp.zeros(acc_ref.shape, F32)

    def scores(ki, dst_ref, lane0=0):
        rows = pl.ds(pl.multiple_of(ki * tk, tk), tk)
        dst_ref[:, lane0:] = jnp.dot(kaug_ref[rows, :], qaug_ref[:, lane0:], preferred_element_type=F32)

    def softmax_pv(ki, ml, s_ref, lane0=0, mask_offset=None):
        m_prev, l_prev = ml[0][:, lane0:], ml[1][:, lane0:]
        s = s_ref[:, lane0:]
        if mask_offset is not None:
            r = lax.broadcasted_iota(jnp.int32, s.shape, 0) + mask_offset
            c = lax.broadcasted_iota(jnp.int32, s.shape, 1) + lane0
            s = jnp.where(r <= c, s, -jnp.inf)
        m_new = jnp.maximum(m_prev, jnp.max(s, axis=0, keepdims=True))
        alpha = jnp.exp2((m_prev - m_new) * exp2_scale)
        p = jnp.exp2((s - m_new) * exp2_scale)
        l_new = alpha * l_prev + jnp.sum(p, axis=0, keepdims=True)
        pv = jnp.dot(vt_ref[ki], p.astype(BF16), preferred_element_type=F32)
        acc_ref[:, lane0:] = alpha * acc_ref[:, lane0:] + pv
        if lane0:
            m_new = jnp.concatenate([ml[0][:, :lane0], m_new], axis=1)
            l_new = jnp.concatenate([ml[1][:, :lane0], l_new], axis=1)
        return m_new, l_new

    def pair(j, ml):
        scores(2 * j + 1, s1_ref)
        ml = softmax_pv(2 * j, ml, s0_ref)
        scores(2 * j + 2, s0_ref)
        return softmax_pv(2 * j + 1, ml, s1_ref)

    ml = (jnp.full((1, tq), -jnp.inf, F32), jnp.zeros((1, tq), F32))
    scores(0, s0_ref)
    ml = lax.fori_loop(0, qi, pair, ml)
    scores(2 * qi + 1, s1_ref, lane0=tk)
    ml = softmax_pv(2 * qi, ml, s0_ref, mask_offset=0)
    _, l = softmax_pv(2 * qi + 1, ml, s1_ref, lane0=tk, mask_offset=tk)
    o_ref[...] = (acc_ref[...] / l).T


def _fox_prompt(qb, kb, vb, cq, ck, *, n_batch, seq, tq, tk):
    m = qb.shape[0]
    nq = seq // tq
    return pl.pallas_call(
        functools.partial(_fox_prompt_kernel, tq=tq, tk=tk),
        grid=(n_batch, N_FOX_HEADS, nq),
        in_specs=[
            pl.BlockSpec((tq, HEAD_DIM), lambda n, h, i: (n * nq + i, h)),
            pl.BlockSpec((seq, HEAD_DIM), lambda n, h, i: (n, h)),
            pl.BlockSpec((seq, HEAD_DIM), lambda n, h, i: (n, h)),
            pl.BlockSpec((1, N_FOX_HEADS, tq), lambda n, h, i: (n, 0, i)),
            pl.BlockSpec((1, seq, N_FOX_HEADS), lambda n, h, i: (n, 0, 0)),
        ],
        out_specs=pl.BlockSpec((tq, HEAD_DIM), lambda n, h, i: (n * nq + i, h)),
        out_shape=jax.ShapeDtypeStruct((m, N_FOX_HEADS * HEAD_DIM), F32),
        scratch_shapes=[
            pltpu.VMEM((seq, 2 * HEAD_DIM), BF16),
            pltpu.VMEM((seq // tk, HEAD_DIM, tk), BF16),
            pltpu.VMEM((2 * HEAD_DIM, tq), BF16),
            pltpu.VMEM((HEAD_DIM, tq), F32),
            pltpu.VMEM((tk, tq), F32),
            pltpu.VMEM((tk, tq), F32),
        ],
        compiler_params=_params(("arbitrary", "arbitrary", "arbitrary")),
        name="fox_prompt",
    )(qb, kb, vb, cq, ck)


def _fox_sample_kernel(q_ref, kn_ref, vn_ref, kc_ref, vc_ref, cq_ref, ckp_ref, ckn_ref, o_ref, *, past):
    scale = HEAD_DIM ** -0.5
    t = q_ref.shape[0]
    half = N_FOX_HEADS // 2
    r = lax.broadcasted_iota(jnp.int32, (2 * t, 2 * past), 0)
    c = lax.broadcasted_iota(jnp.int32, (2 * t, 2 * past), 1)
    own_past = (r >= t) == ((c & 1) == 1)
    rn = lax.broadcasted_iota(jnp.int32, (2 * t, 2 * t), 0)
    cn = lax.broadcasted_iota(jnp.int32, (2 * t, 2 * t), 1)
    own_new = ((rn >= t) == (cn >= t)) & ((cn & (t - 1)) <= (rn & (t - 1)))
    for j in range(half):
        sl0 = slice(j * HEAD_DIM, (j + 1) * HEAD_DIM)
        sl1 = slice((j + half) * HEAD_DIM, (j + half + 1) * HEAD_DIM)
        both = lambda ref: jnp.concatenate([ref[:, sl0], ref[:, sl1]], axis=0)
        q = both(q_ref)
        kp = kc_ref[0, pl.ds(j, 2 * past, stride=half), :].astype(BF16)
        vp = vc_ref[0, pl.ds(j, 2 * past, stride=half), :].astype(BF16)
        cq = jnp.concatenate([cq_ref[0, :, j:j + 1], cq_ref[0, :, j + half:j + half + 1]], axis=0)
        s1 = lax.dot_general(q, kp, _NT, preferred_element_type=F32) * scale
        s1 = jnp.where(own_past, s1 + (cq - ckp_ref[0, j:j + 1, :]), -jnp.inf)
        s2 = lax.dot_general(q, both(kn_ref), _NT, preferred_element_type=F32) * scale
        s2 = jnp.where(own_new, s2 + (cq - ckn_ref[0, j:j + 1, :]), -jnp.inf)
        m = jnp.maximum(jnp.max(s1, axis=1, keepdims=True), jnp.max(s2, axis=1, keepdims=True))
        p1 = jnp.exp(s1 - m)
        p2 = jnp.exp(s2 - m)
        l = jnp.sum(p1, axis=1, keepdims=True) + jnp.sum(p2, axis=1, keepdims=True)
        o = (jnp.dot(p1.astype(BF16), vp, preferred_element_type=F32)
             + jnp.dot(p2.astype(BF16), both(vn_ref), preferred_element_type=F32)) / l
        o_ref[:, sl0] = o[:t]
        o_ref[:, sl1] = o[t:]


def _fox_sample(qb, knb, vnb, cache_k, cache_v, cq, ck_past, ck_new):
    n_batch, rows, _ = cache_k.shape
    past = rows // N_FOX_HEADS
    t = qb.shape[0] // n_batch
    width = qb.shape[1]
    tok = lambda b: (b, 0)
    lead = lambda b: (b, 0, 0)
    return pl.pallas_call(
        functools.partial(_fox_sample_kernel, past=past),
        grid=(n_batch,),
        in_specs=[
            pl.BlockSpec((t, width), tok),
            pl.BlockSpec((t, width), tok),
            pl.BlockSpec((t, width), tok),
            pl.BlockSpec((1, rows, HEAD_DIM), lead),
            pl.BlockSpec((1, rows, HEAD_DIM), lead),
            pl.BlockSpec((1, t, N_FOX_HEADS), lead),
            pl.BlockSpec((1, N_FOX_HEADS // 2, 2 * past), lead),
            pl.BlockSpec((1, N_FOX_HEADS // 2, 2 * t), lead),
        ],
        out_specs=pl.BlockSpec((t, width), tok),
        out_shape=jax.ShapeDtypeStruct(qb.shape, F32),
        compiler_params=_params(("arbitrary",)),
        name="fox_sample",
    )(qb, knb, vnb, cache_k, cache_v, cq, ck_past, ck_new)


def _mix_kernel(*refs, seg, nseg, tiles_per_seq, has_prev):
    if has_prev:
        (x_ref, gb_ref, gc_ref, hin_ref, pgc_ref, phin_ref, *refs) = refs
    else:
        (x_ref, gb_ref, gc_ref, hin_ref, *refs) = refs
    (buf_ref, ya_ref, cw_ref, gco_ref, gao_ref, wout_ref, gx_ref, wxq_ref, mk_ref, mv_ref, wxo_ref, gm_ref,
     x2_ref, h_ref, cnew_ref, o_scr) = refs
    i = pl.program_id(0)
    u = gc_ref[...] * hin_ref[...]
    gb = gb_ref[...]
    w0 = cw_ref[0:1, :]
    w1 = cw_ref[1:2, :]
    w2 = cw_ref[2:3, :]
    rid = lax.broadcasted_iota(jnp.int32, (seg, u.shape[1]), 0)
    pieces = []
    for s in range(nseg):
        u_s = u[s * seg:(s + 1) * seg]
        b0 = buf_ref[s, 0:1, :]
        b1 = buf_ref[s, 1:2, :]
        if has_prev:
            pu = pgc_ref[...] * phin_ref[...]
            first = (i % tiles_per_seq) == 0
            b0 = jnp.where(first, b0, pu[6:7])
            b1 = jnp.where(first, b1, pu[7:8])
        r1 = jnp.where(rid == 0, b1, pltpu.roll(u_s, 1, 0))
        r2 = jnp.where(rid == 0, b0, jnp.where(rid == 1, b1, pltpu.roll(u_s, 2, 0)))
        y = w0 * r2 + w1 * r1 + w2 * u_s
        pieces.append(gb[s * seg:(s + 1) * seg] * y)
        cnew_ref[s] = u_s[seg - 2:seg]
    yc = pieces[0] if nseg == 1 else jnp.concatenate(pieces, axis=0)
    cat = jnp.concatenate([_rms(yc, gco_ref[...]).astype(BF16),
                           _rms(ya_ref[...], gao_ref[...]).astype(BF16)], axis=-1)
    x1 = x_ref[...] + jnp.dot(cat, wout_ref[...], preferred_element_type=F32)
    xn = _rms(x1, gx_ref[...]).astype(BF16)
    qx = jnp.dot(xn, wxq_ref[...], preferred_element_type=F32).astype(BF16)

    scale = HEAD_DIM ** -0.5
    n_mem = mk_ref.shape[1] // N_X_HEADS
    for s in range(nseg):
        rows = slice(s * seg, (s + 1) * seg)
        for hh in range(N_X_HEADS):
            sl = slice(hh * HEAD_DIM, (hh + 1) * HEAD_DIM)
            kh = _load_head_rows(mk_ref, s, hh, n_mem, N_X_HEADS).astype(BF16)
            vh = _load_head_rows(mv_ref, s, hh, n_mem, N_X_HEADS).astype(BF16)
            sc = lax.dot_general(qx[rows, sl], kh, _NT, preferred_element_type=F32) * scale
            m = jnp.max(sc, axis=1, keepdims=True)
            p = jnp.exp(sc - m)
            l = jnp.sum(p, axis=1, keepdims=True)
            o = jnp.dot(p.astype(BF16), vh, preferred_element_type=F32) / l
            o_scr[rows, sl] = o.astype(BF16)
    x2 = x1 + jnp.dot(o_scr[...], wxo_ref[...], preferred_element_type=F32)
    x2_ref[...] = x2
    h_ref[...] = _rms(x2, gm_ref[...]).astype(BF16)


def _mix(x, zc, conv_buf, ya, conv_w, g_conv_out, g_attn_out, w_out, g_xattn, w_xq, mk, mv, w_xo, g_mlp, *, tm, seq):
    m, d = x.shape
    dc = zc.shape[1] // 3
    dx = w_xq.shape[1]
    mem_rows = mk.shape[1]
    has_prev = seq > tm
    seg = tm if has_prev else seq
    nseg = tm // seg
    tiles_per_seq = max(seq // tm, 1)
    row = lambda i: (i, 0)
    const = lambda i: (0, 0)
    resident = functools.partial(pl.BlockSpec, index_map=const, pipeline_mode=pl.Buffered(1))
    in_specs = [
        pl.BlockSpec((tm, d), row),
        pl.BlockSpec((tm, dc), lambda i: (i, 0)),
        pl.BlockSpec((tm, dc), lambda i: (i, 1)),
        pl.BlockSpec((tm, dc), lambda i: (i, 2)),
    ]
    args = [x, zc, zc, zc]
    if has_prev:
        prev = lambda col: (lambda i: (jnp.maximum(i * (tm // 8) - 1, 0), col))
        in_specs += [pl.BlockSpec((8, dc), prev(1)), pl.BlockSpec((8, dc), prev(2))]
        args += [zc, zc]
        buf_map = lambda i: (i // tiles_per_seq, 0, 0)
    else:
        buf_map = lambda i: (i, 0, 0)
    in_specs += [
        pl.BlockSpec((nseg, CONV_WIDTH - 1, dc), buf_map),
        pl.BlockSpec((tm, ya.shape[1]), row),
        pl.BlockSpec((CONV_WIDTH, dc), const),
        pl.BlockSpec((1, dc), const),
        pl.BlockSpec((1, ya.shape[1]), const),
        resident(w_out.shape),
        pl.BlockSpec((1, d), const),
        resident(w_xq.shape),
        pl.BlockSpec((nseg, mem_rows, HEAD_DIM), buf_map),
        pl.BlockSpec((nseg, mem_rows, HEAD_DIM), buf_map),
        resident(w_xo.shape),
        pl.BlockSpec((1, d), const),
    ]
    args += [conv_buf, ya, conv_w, g_conv_out, g_attn_out, w_out, g_xattn, w_xq, mk, mv, w_xo, g_mlp]
    return pl.pallas_call(
        functools.partial(_mix_kernel, seg=seg, nseg=nseg, tiles_per_seq=tiles_per_seq, has_prev=has_prev),
        grid=(m // tm,),
        in_specs=in_specs,
        out_specs=[
            pl.BlockSpec((tm, d), row),
            pl.BlockSpec((tm, d), row),
            pl.BlockSpec((nseg, CONV_WIDTH - 1, dc), buf_map),
        ],
        out_shape=[
            jax.ShapeDtypeStruct((m, d), F32),
            jax.ShapeDtypeStruct((m, d), BF16),
            jax.ShapeDtypeStruct(conv_buf.shape, F32),
        ],
        scratch_shapes=[pltpu.VMEM((tm, dx), BF16)],
        compiler_params=_params(("arbitrary",)),
        name="mix",
    )(*args)


def _memkv_kernel(x_ref, g_ref, w_ref, o_ref):
    xn = _rms(x_ref[...], g_ref[...]).astype(BF16)
    _store_head_rows(o_ref.at[0], jnp.dot(xn, w_ref[...], preferred_element_type=F32), N_X_HEADS)


def _memkv(mem, g, w):
    m, d = mem.shape
    half = w.shape[1] // 2
    return pl.pallas_call(
        _memkv_kernel,
        grid=(2,),
        in_specs=[
            pl.BlockSpec((m, d), lambda j: (0, 0)),
            pl.BlockSpec((1, d), lambda j: (0, 0)),
            pl.BlockSpec((d, half), lambda j: (0, j)),
        ],
        out_specs=pl.BlockSpec((1, m * N_X_HEADS, HEAD_DIM), lambda j: (j, 0, 0)),
        out_shape=jax.ShapeDtypeStruct((2, m * N_X_HEADS, HEAD_DIM), F32),
        compiler_params=_params(("arbitrary",)),
        name="memkv",
    )(mem, g, w)


def _mlp_kernel(h_ref, x2_ref, wu_ref, wd_ref, g_ref, y_ref, acc_ref):
    j = pl.program_id(1)

    @pl.when(j == 0)
    def _():
        acc_ref[...] = jnp.zeros(acc_ref.shape, F32)

    a = jnp.dot(h_ref[...], wu_ref[...], preferred_element_type=F32)
    a = jnp.square(jnp.maximum(a, 0.0)).astype(BF16)
    acc_ref[...] += jnp.dot(a, wd_ref[...], preferred_element_type=F32)

    @pl.when(j == pl.num_programs(1) - 1)
    def _():
        y_ref[...] = _rms(x2_ref[...] + acc_ref[...], g_ref[...])


def _mlp(h, x2, w_up, w_down, g_final, *, tm, tf):
    m, d = h.shape
    ff = w_up.shape[1]
    row = lambda i, j: (i, 0)
    return pl.pallas_call(
        _mlp_kernel,
        grid=(m // tm, ff // tf),
        in_specs=[
            pl.BlockSpec((tm, d), row),
            pl.BlockSpec((tm, d), row),
            pl.BlockSpec((d, tf), lambda i, j: (0, j)),
            pl.BlockSpec((tf, d), lambda i, j: (j, 0)),
            pl.BlockSpec((1, d), lambda i, j: (0, 0)),
        ],
        out_specs=pl.BlockSpec((tm, d), row),
        out_shape=jax.ShapeDtypeStruct((m, d), F32),
        scratch_shapes=[pltpu.VMEM((tm, d), F32)],
        compiler_params=_params(("arbitrary", "arbitrary")),
        name="mlp",
    )(h, x2, w_up, w_down, g_final)


def kernel(x_prompt, x_sample, cache_k, cache_v, cache_logf, cache_conv, cache_mem_k, cache_mem_v, mem_prompt, g_mix, w_in, b_f, conv_w, g_conv_out, g_attn_out, w_out, g_xattn, g_mem, w_xq, w_xkv, w_xo, g_mlp, w_up, w_down, g_final):
    depth = w_in.shape[0]
    assert depth == 1, "single-layer trunk"
    nb, seq, d = x_prompt.shape
    db, dseq, _ = x_sample.shape
    past = cache_k.shape[2]
    heads, hd = cache_k.shape[3], cache_k.shape[4]
    assert (heads, hd) == (N_FOX_HEADS, HEAD_DIM)
    d_attn = heads * hd
    d_conv = cache_conv.shape[-1]
    n_mem, xh, xhd = cache_mem_k.shape[2:]
    assert (xh, xhd) == (N_X_HEADS, HEAD_DIM)
    d_x = xh * xhd
    n_main = 3 * d_conv + 3 * d_attn
    assert d_conv == d_attn and w_in.shape[2] == n_main + heads

    w_in_t = jnp.swapaxes(w_in[0], 0, 1)
    bf = b_f[0].reshape(1, heads)
    w_out_b = w_out[0].astype(BF16)
    w_xq_b = w_xq[0].astype(BF16)
    w_xkv_b = w_xkv[0].astype(BF16)
    w_xo_b = w_xo[0].astype(BF16)
    w_up_b = w_up[0].astype(BF16)
    w_down_b = w_down[0].astype(BF16)
    row = lambda g: g.reshape(1, -1)

    def trunk(x, conv_buf, seq_len, attention, mk, mv):
        xn, lf = _xnorm(x, row(g_mix[0]), w_in_t, bf, tm=512, gate_row=n_main)
        proj = functools.partial(_proj, xn, w_in_t, tn=d_attn, tm=1024)
        zc, = proj(j0=0, nj=3, mode="f32")
        qb, = proj(j0=3, nj=1, mode="bf16")
        k, kb = proj(j0=4, nj=1, mode="heads")
        v, vb = proj(j0=5, nj=1, mode="heads")
        ya = attention(qb, kb, vb, lf)
        x2, h, conv_new = _mix(x, zc, conv_buf, ya, conv_w[0], row(g_conv_out[0]), row(g_attn_out[0]),
                               w_out_b, row(g_xattn[0]), w_xq_b, mk, mv, w_xo_b, row(g_mlp[0]), tm=256, seq=seq_len)
        y = _mlp(h, x2, w_up_b, w_down_b, row(g_final), tm=512, tf=1024)
        return y, k, v, lf, conv_new


    def prompt_attention(qb, kb, vb, lf):
        lft = lf.reshape(nb, seq, heads).transpose(0, 2, 1).reshape(nb * heads, seq)
        ct = _cumsum_lanes(lft).reshape(nb, heads, seq)
        return _fox_prompt(qb, kb, vb, ct, ct.transpose(0, 2, 1), n_batch=nb, seq=seq, tq=1024, tk=512)

    kv = _memkv(mem_prompt.reshape(nb * n_mem, d), row(g_mem[0]), w_xkv_b)
    mk_p = kv[0].reshape(nb, n_mem * xh, xhd)
    mv_p = kv[1].reshape(nb, n_mem * xh, xhd)
    conv0 = jnp.zeros((nb, CONV_WIDTH - 1, d_conv), F32)
    y_p, k_p, v_p, lf_p, conv_p = trunk(x_prompt.reshape(nb * seq, d), conv0, seq, prompt_attention,
                                        mk_p, mv_p)

    pad = (-(past + dseq)) % LANES

    def sample_attention(qb, kb, vb, lf):
        lft_past = cache_logf[0].transpose(0, 2, 1).reshape(db * heads, past)
        lft_new = lf.reshape(db, dseq, heads).transpose(0, 2, 1).reshape(db * heads, dseq)
        ct = _cumsum_lanes(jnp.concatenate(
            [lft_past, lft_new, jnp.zeros((db * heads, pad), F32)], axis=1))
        ck_past = ct[:, :past].reshape(db, heads, past)
        ck_new = ct[:, past:past + dseq].reshape(db, heads, dseq)
        cq = ck_new.transpose(0, 2, 1)
        hp = heads // 2
        ck_past_pairs = jnp.stack([ck_past[:, :hp], ck_past[:, hp:]], axis=-1).reshape(db, hp, 2 * past)
        ck_new_pairs = jnp.concatenate([ck_new[:, :hp], ck_new[:, hp:]], axis=-1)
        return _fox_sample(qb, kb, vb, cache_k.reshape(db, past * heads, hd),
                           cache_v.reshape(db, past * heads, hd), cq, ck_past_pairs, ck_new_pairs)

    y_s, k_s, v_s, lf_s, conv_s = trunk(x_sample.reshape(db * dseq, d), cache_conv[0], dseq, sample_attention,
                                        cache_mem_k.reshape(db, n_mem * xh, xhd),
                                        cache_mem_v.reshape(db, n_mem * xh, xhd))

    return (y_p.reshape(nb, seq, d),
            y_s.reshape(db, dseq, d),
            k_p.reshape(1, nb, seq, heads, hd),
            v_p.reshape(1, nb, seq, heads, hd),
            lf_p.reshape(1, nb, seq, heads),
            conv_p[None],
            mk_p.reshape(1, nb, n_mem, xh, xhd),
            mv_p.reshape(1, nb, n_mem, xh, xhd),
            k_s.reshape(1, db, dseq, heads, hd),
            v_s.reshape(1, db, dseq, heads, hd),
            lf_s.reshape(1, db, dseq, heads),
            conv_s[None])
```

```python
import functools

import jax
import jax.numpy as jnp
from jax import lax
from jax.experimental import pallas as pl
from jax.experimental.pallas import tpu as pltpu

F32 = jnp.float32
BF16 = jnp.bfloat16

RMS_EPS = 1e-6
HEAD_DIM = 128
N_FOX_HEADS = 8
N_X_HEADS = 4
CONV_WIDTH = 3
LANES = 128
VMEM_LIMIT = 52 * 1024 * 1024
MLP_VMEM_LIMIT = 56 * 1024 * 1024

_NT = (((1,), (1,)), ((), ()))


def _params(semantics):
    return pltpu.CompilerParams(dimension_semantics=semantics, vmem_limit_bytes=VMEM_LIMIT)


def _rms(x, g):
    ms = jnp.mean(x * x, axis=-1, keepdims=True)
    return x * lax.rsqrt(ms + RMS_EPS) * g


def _log_sigmoid(x):
    return jnp.minimum(x, 0.0) - jnp.log1p(jnp.exp(-jnp.abs(x)))


def _store_head_rows(ref, val, heads):
    n = val.shape[0]
    for h in range(heads):
        ref[pl.ds(h, n, stride=heads), :] = val[:, h * HEAD_DIM:(h + 1) * HEAD_DIM]


def _load_head_rows(ref, lead, h, n, heads):
    return ref[lead, pl.ds(h, n, stride=heads), :]


def _xnorm_kernel(x_ref, g_ref, wf_ref, bf_ref, xn_ref, lf_ref):
    xn = _rms(x_ref[...], g_ref[...]).astype(BF16)
    xn_ref[...] = xn
    wf = wf_ref[...].astype(BF16)
    wf = jnp.concatenate([wf, jnp.zeros((LANES - wf.shape[0], wf.shape[1]), BF16)], axis=0)
    fz = lax.dot_general(xn, wf, _NT, preferred_element_type=F32)
    lf_ref[...] = _log_sigmoid(fz[:, :N_FOX_HEADS] + bf_ref[...])


def _xnorm(x, g, w_t, b_f, *, tm, gate_row):
    m, d = x.shape
    row = lambda i: (i, 0)
    return pl.pallas_call(
        _xnorm_kernel,
        grid=(m // tm,),
        in_specs=[
            pl.BlockSpec((tm, d), row),
            pl.BlockSpec((1, d), lambda i: (0, 0)),
            pl.BlockSpec((N_FOX_HEADS, d), lambda i: (gate_row // N_FOX_HEADS, 0)),
            pl.BlockSpec((1, N_FOX_HEADS), lambda i: (0, 0)),
        ],
        out_specs=[pl.BlockSpec((tm, d), row), pl.BlockSpec((tm, N_FOX_HEADS), row)],
        out_shape=[jax.ShapeDtypeStruct((m, d), BF16), jax.ShapeDtypeStruct((m, N_FOX_HEADS), F32)],
        compiler_params=_params(("arbitrary",)),
        name="xnorm",
    )(x, g, w_t, b_f)


def _proj_kernel(xn_ref, wt_ref, *refs, mode):
    out_refs, wb_ref = refs[:-1], refs[-1]

    @pl.when(pl.program_id(1) == 0)
    def _():
        wb_ref[...] = wt_ref[...].astype(BF16)

    res = lax.dot_general(xn_ref[...], wb_ref[...], _NT, preferred_element_type=F32)
    if mode == "f32":
        out_refs[0][...] = res
    elif mode == "bf16":
        out_refs[0][...] = res.astype(BF16)
    else:
        _store_head_rows(out_refs[0], res, N_FOX_HEADS)
        out_refs[1][...] = res.astype(BF16)


def _proj(xn, w_t, *, j0, nj, tn, tm, mode):
    m, d = xn.shape
    if mode == "heads":
        assert nj == 1
        out_specs = [pl.BlockSpec((tm * N_FOX_HEADS, HEAD_DIM), lambda j, i: (i, 0)),
                     pl.BlockSpec((tm, tn), lambda j, i: (i, 0))]
        out_shape = [jax.ShapeDtypeStruct((m * N_FOX_HEADS, HEAD_DIM), F32), jax.ShapeDtypeStruct((m, tn), BF16)]
    else:
        out_specs = [pl.BlockSpec((tm, tn), lambda j, i: (i, j))]
        out_shape = [jax.ShapeDtypeStruct((m, nj * tn), F32 if mode == "f32" else BF16)]
    return pl.pallas_call(
        functools.partial(_proj_kernel, mode=mode),
        grid=(nj, m // tm),
        in_specs=[
            pl.BlockSpec((tm, d), lambda j, i: (i, 0)),
            pl.BlockSpec((tn, d), lambda j, i: (j0 + j, 0)),
        ],
        out_specs=out_specs,
        out_shape=out_shape,
        scratch_shapes=[pltpu.VMEM((tn, d), BF16)],
        compiler_params=_params(("arbitrary", "arbitrary")),
        name="proj_" + mode,
    )(xn, w_t)


def _cumsum_kernel(a_ref, o_ref):
    rows, length = a_ref.shape
    r = lax.broadcasted_iota(jnp.int32, (LANES, LANES), 0)
    c = lax.broadcasted_iota(jnp.int32, (LANES, LANES), 1)
    upper = (r <= c).astype(BF16)
    carry = jnp.zeros((rows, 1), F32)
    for b in range(length // LANES):
        a = a_ref[:, b * LANES:(b + 1) * LANES]
        hi = a.astype(BF16)
        r1 = a - hi.astype(F32)
        mid = r1.astype(BF16)
        lo = (r1 - mid.astype(F32)).astype(BF16)
        s = (jnp.dot(hi, upper, preferred_element_type=F32)
             + jnp.dot(mid, upper, preferred_element_type=F32)
             + jnp.dot(lo, upper, preferred_element_type=F32)) + carry
        o_ref[:, b * LANES:(b + 1) * LANES] = s
        carry = s[:, LANES - 1:LANES]


def _cumsum_lanes(a):
    return pl.pallas_call(
        _cumsum_kernel,
        out_shape=jax.ShapeDtypeStruct(a.shape, F32),
        compiler_params=pltpu.CompilerParams(vmem_limit_bytes=VMEM_LIMIT),
        name="cumsum",
    )(a)


def _split3(x):
    hi = x.astype(BF16).astype(F32)
    r = x - hi
    mid = r.astype(BF16).astype(F32)
    lo = (r - mid).astype(BF16).astype(F32)
    return hi, mid, lo


def _fox_prompt_kernel(q_ref, k_ref, v_ref, cq_ref, ck_ref, o_ref, kaug_ref, vt_ref, qaug_ref, acc_ref, s0_ref, s1_ref,
                       *, tq, tk):
    assert tq == 2 * tk
    h = pl.program_id(1)
    qi = pl.program_id(2)
    nk = kaug_ref.shape[0] // tk
    inv_scale = HEAD_DIM ** 0.5
    exp2_scale = (HEAD_DIM ** -0.5) * 1.4426950408889634

    @pl.when(qi == 0)
    def _():
        lane8 = lax.broadcasted_iota(jnp.int32, (tk, N_FOX_HEADS), 1)
        lane = lax.broadcasted_iota(jnp.int32, (tk, HEAD_DIM), 1)

        def prep(c, carry):
            rows = pl.ds(pl.multiple_of(c * tk, tk), tk)
            kaug_ref[rows, 0:HEAD_DIM] = k_ref[rows, :]
            col = jnp.sum(jnp.where(lane8 == h, ck_ref[0, rows, :], 0.0), axis=1, keepdims=True) * inv_scale
            hi, mid, lo = _split3(col)
            aug = jnp.where(lane == 0, hi, jnp.where(lane == 1, mid, jnp.where(lane == 2, lo,
                            jnp.where(lane < 6, 1.0, 0.0))))
            kaug_ref[rows, HEAD_DIM:2 * HEAD_DIM] = aug.astype(BF16)
            vt_ref[c] = v_ref[rows, :].astype(F32).T.astype(BF16)
            return carry

        lax.fori_loop(0, nk, prep, 0)

    qaug_ref[0:HEAD_DIM, :] = q_ref[...].astype(F32).T.astype(BF16)
    hi, mid, lo = _split3(cq_ref[0, pl.ds(h, 1), :] * inv_scale)
    sub = lax.broadcasted_iota(jnp.int32, (HEAD_DIM, tq), 0)
    qaug_ref[HEAD_DIM:2 * HEAD_DIM, :] = jnp.where(
        sub < 3, -1.0, jnp.where(sub == 3, hi, jnp.where(sub == 4, mid, jnp.where(sub == 5, lo, 0.0)))).astype(BF16)
    acc_ref[...] = jnp.zeros(acc_ref.shape, F32)

    def scores(ki, dst_ref, lane0=0):
        rows = pl.ds(pl.multiple_of(ki * tk, tk), tk)
        dst_ref[:, lane0:] = jnp.dot(kaug_ref[rows, :], qaug_ref[:, lane0:], preferred_element_type=F32)

    def softmax_pv(ki, ml, s_ref, lane0=0, mask_offset=None):
        m_prev, l_prev = ml[0][:, lane0:], ml[1][:, lane0:]
        s = s_ref[:, lane0:]
        if mask_offset is not None:
            r = lax.broadcasted_iota(jnp.int32, s.shape, 0) + mask_offset
            c = lax.broadcasted_iota(jnp.int32, s.shape, 1) + lane0
            s = jnp.where(r <= c, s, -jnp.inf)
        m_new = jnp.maximum(m_prev, jnp.max(s, axis=0, keepdims=True))
        alpha = jnp.exp2((m_prev - m_new) * exp2_scale)
        p = jnp.exp2((s - m_new) * exp2_scale)
        l_new = alpha * l_prev + jnp.sum(p, axis=0, keepdims=True)
        pv = jnp.dot(vt_ref[ki], p.astype(BF16), preferred_element_type=F32)
        acc_ref[:, lane0:] = alpha * acc_ref[:, lane0:] + pv
        if lane0:
            m_new = jnp.concatenate([ml[0][:, :lane0], m_new], axis=1)
            l_new = jnp.concatenate([ml[1][:, :lane0], l_new], axis=1)
        return m_new, l_new

    def pair(j, ml):
        scores(2 * j + 1, s1_ref)
        ml = softmax_pv(2 * j, ml, s0_ref)
        scores(2 * j + 2, s0_ref)
        return softmax_pv(2 * j + 1, ml, s1_ref)

    ml = (jnp.full((1, tq), -jnp.inf, F32), jnp.zeros((1, tq), F32))
    scores(0, s0_ref)
    ml = lax.fori_loop(0, qi, pair, ml)
    scores(2 * qi + 1, s1_ref, lane0=tk)
    ml = softmax_pv(2 * qi, ml, s0_ref, mask_offset=0)
    _, l = softmax_pv(2 * qi + 1, ml, s1_ref, lane0=tk, mask_offset=tk)
    o_ref[...] = (acc_ref[...] / l).T


def _fox_prompt(qb, kb, vb, cq, ck, *, n_batch, seq, tq, tk):
    m = qb.shape[0]
    nq = seq // tq
    return pl.pallas_call(
        functools.partial(_fox_prompt_kernel, tq=tq, tk=tk),
        grid=(n_batch, N_FOX_HEADS, nq),
        in_specs=[
            pl.BlockSpec((tq, HEAD_DIM), lambda n, h, i: (n * nq + i, h)),
            pl.BlockSpec((seq, HEAD_DIM), lambda n, h, i: (n, h)),
            pl.BlockSpec((seq, HEAD_DIM), lambda n, h, i: (n, h)),
            pl.BlockSpec((1, N_FOX_HEADS, tq), lambda n, h, i: (n, 0, i)),
            pl.BlockSpec((1, seq, N_FOX_HEADS), lambda n, h, i: (n, 0, 0)),
        ],
        out_specs=pl.BlockSpec((tq, HEAD_DIM), lambda n, h, i: (n * nq + i, h)),
        out_shape=jax.ShapeDtypeStruct((m, N_FOX_HEADS * HEAD_DIM), F32),
        scratch_shapes=[
            pltpu.VMEM((seq, 2 * HEAD_DIM), BF16),
            pltpu.VMEM((seq // tk, HEAD_DIM, tk), BF16),
            pltpu.VMEM((2 * HEAD_DIM, tq), BF16),
            pltpu.VMEM((HEAD_DIM, tq), F32),
            pltpu.VMEM((tk, tq), F32),
            pltpu.VMEM((tk, tq), F32),
        ],
        compiler_params=_params(("arbitrary", "arbitrary", "arbitrary")),
        name="fox_prompt",
    )(qb, kb, vb, cq, ck)


def _fox_sample_kernel(q_ref, kn_ref, vn_ref, kc_ref, vc_ref, cq_ref, ckp_ref, ckn_ref, o_ref, *, past):
    scale = HEAD_DIM ** -0.5
    t = q_ref.shape[0]
    half = N_FOX_HEADS // 2
    r = lax.broadcasted_iota(jnp.int32, (2 * t, 2 * past), 0)
    c = lax.broadcasted_iota(jnp.int32, (2 * t, 2 * past), 1)
    own_past = (r >= t) == ((c & 1) == 1)
    rn = lax.broadcasted_iota(jnp.int32, (2 * t, 2 * t), 0)
    cn = lax.broadcasted_iota(jnp.int32, (2 * t, 2 * t), 1)
    own_new = ((rn >= t) == (cn >= t)) & ((cn & (t - 1)) <= (rn & (t - 1)))
    for j in range(half):
        sl0 = slice(j * HEAD_DIM, (j + 1) * HEAD_DIM)
        sl1 = slice((j + half) * HEAD_DIM, (j + half + 1) * HEAD_DIM)
        both = lambda ref: jnp.concatenate([ref[:, sl0], ref[:, sl1]], axis=0)
        q = both(q_ref)
        kp = kc_ref[0, pl.ds(j, 2 * past, stride=half), :].astype(BF16)
        vp = vc_ref[0, pl.ds(j, 2 * past, stride=half), :].astype(BF16)
        cq = jnp.concatenate([cq_ref[0, :, j:j + 1], cq_ref[0, :, j + half:j + half + 1]], axis=0)
        s1 = lax.dot_general(q, kp, _NT, preferred_element_type=F32) * scale
        s1 = jnp.where(own_past, s1 + (cq - ckp_ref[0, j:j + 1, :]), -jnp.inf)
        s2 = lax.dot_general(q, both(kn_ref), _NT, preferred_element_type=F32) * scale
        s2 = jnp.where(own_new, s2 + (cq - ckn_ref[0, j:j + 1, :]), -jnp.inf)
        m = jnp.maximum(jnp.max(s1, axis=1, keepdims=True), jnp.max(s2, axis=1, keepdims=True))
        p1 = jnp.exp(s1 - m)
        p2 = jnp.exp(s2 - m)
        l = jnp.sum(p1, axis=1, keepdims=True) + jnp.sum(p2, axis=1, keepdims=True)
        o = (jnp.dot(p1.astype(BF16), vp, preferred_element_type=F32)
             + jnp.dot(p2.astype(BF16), both(vn_ref), preferred_element_type=F32)) / l
        o_ref[:, sl0] = o[:t]
        o_ref[:, sl1] = o[t:]


def _fox_sample(qb, knb, vnb, cache_k, cache_v, cq, ck_past, ck_new):
    n_batch, rows, _ = cache_k.shape
    past = rows // N_FOX_HEADS
    t = qb.shape[0] // n_batch
    width = qb.shape[1]
    tok = lambda b: (b, 0)
    lead = lambda b: (b, 0, 0)
    return pl.pallas_call(
        functools.partial(_fox_sample_kernel, past=past),
        grid=(n_batch,),
        in_specs=[
            pl.BlockSpec((t, width), tok),
            pl.BlockSpec((t, width), tok),
            pl.BlockSpec((t, width), tok),
            pl.BlockSpec((1, rows, HEAD_DIM), lead),
            pl.BlockSpec((1, rows, HEAD_DIM), lead),
            pl.BlockSpec((1, t, N_FOX_HEADS), lead),
            pl.BlockSpec((1, N_FOX_HEADS // 2, 2 * past), lead),
            pl.BlockSpec((1, N_FOX_HEADS // 2, 2 * t), lead),
        ],
        out_specs=pl.BlockSpec((t, width), tok),
        out_shape=jax.ShapeDtypeStruct(qb.shape, F32),
        compiler_params=_params(("arbitrary",)),
        name="fox_sample",
    )(qb, knb, vnb, cache_k, cache_v, cq, ck_past, ck_new)


def _mix_kernel(*refs, seg, nseg, tiles_per_seq, has_prev, n_cast):
    if has_prev:
        (x_ref, gb_ref, gc_ref, hin_ref, pgc_ref, phin_ref, *refs) = refs
    else:
        (x_ref, gb_ref, gc_ref, hin_ref, *refs) = refs
    (buf_ref, ya_ref, cw_ref, gco_ref, gao_ref, wout_ref, gx_ref, wxq_ref, mk_ref, mv_ref, wxo_ref, gm_ref,
     *refs) = refs
    cast_in, (x2_ref, h_ref, cnew_ref, *cast_out, o_scr) = refs[:n_cast], refs[n_cast:]
    for src_ref, dst_ref in zip(cast_in, cast_out):
        dst_ref[...] = src_ref[...].astype(BF16)
    i = pl.program_id(0)
    u = gc_ref[...] * hin_ref[...]
    gb = gb_ref[...]
    w0 = cw_ref[0:1, :]
    w1 = cw_ref[1:2, :]
    w2 = cw_ref[2:3, :]
    rid = lax.broadcasted_iota(jnp.int32, (seg, u.shape[1]), 0)
    pieces = []
    for s in range(nseg):
        u_s = u[s * seg:(s + 1) * seg]
        b0 = buf_ref[s, 0:1, :]
        b1 = buf_ref[s, 1:2, :]
        if has_prev:
            pu = pgc_ref[...] * phin_ref[...]
            first = (i % tiles_per_seq) == 0
            b0 = jnp.where(first, b0, pu[6:7])
            b1 = jnp.where(first, b1, pu[7:8])
        r1 = jnp.where(rid == 0, b1, pltpu.roll(u_s, 1, 0))
        r2 = jnp.where(rid == 0, b0, jnp.where(rid == 1, b1, pltpu.roll(u_s, 2, 0)))
        y = w0 * r2 + w1 * r1 + w2 * u_s
        pieces.append(gb[s * seg:(s + 1) * seg] * y)
        cnew_ref[s] = u_s[seg - 2:seg]
    yc = pieces[0] if nseg == 1 else jnp.concatenate(pieces, axis=0)
    cat = jnp.concatenate([_rms(yc, gco_ref[...]).astype(BF16),
                           _rms(ya_ref[...], gao_ref[...]).astype(BF16)], axis=-1)
    x1 = x_ref[...] + jnp.dot(cat, wout_ref[...], preferred_element_type=F32)
    xn = _rms(x1, gx_ref[...]).astype(BF16)
    qx = jnp.dot(xn, wxq_ref[...], preferred_element_type=F32).astype(BF16)

    scale = HEAD_DIM ** -0.5
    n_mem = mk_ref.shape[1] // N_X_HEADS
    for s in range(nseg):
        rows = slice(s * seg, (s + 1) * seg)
        for hh in range(N_X_HEADS):
            sl = slice(hh * HEAD_DIM, (hh + 1) * HEAD_DIM)
            kh = _load_head_rows(mk_ref, s, hh, n_mem, N_X_HEADS).astype(BF16)
            vh = _load_head_rows(mv_ref, s, hh, n_mem, N_X_HEADS).astype(BF16)
            sc = lax.dot_general(qx[rows, sl], kh, _NT, preferred_element_type=F32) * scale
            m = jnp.max(sc, axis=1, keepdims=True)
            p = jnp.exp(sc - m)
            l = jnp.sum(p, axis=1, keepdims=True)
            o = jnp.dot(p.astype(BF16), vh, preferred_element_type=F32) / l
            o_scr[rows, sl] = o.astype(BF16)
    x2 = x1 + jnp.dot(o_scr[...], wxo_ref[...], preferred_element_type=F32)
    x2_ref[...] = x2
    h_ref[...] = _rms(x2, gm_ref[...]).astype(BF16)


def _mix(x, zc, conv_buf, ya, conv_w, g_conv_out, g_attn_out, w_out, g_xattn, w_xq, mk, mv, w_xo, g_mlp, *, tm, seq,
         cast=()):
    m, d = x.shape
    dc = zc.shape[1] // 3
    dx = w_xq.shape[1]
    mem_rows = mk.shape[1]
    has_prev = seq > tm
    seg = tm if has_prev else seq
    nseg = tm // seg
    tiles_per_seq = max(seq // tm, 1)
    row = lambda i: (i, 0)
    const = lambda i: (0, 0)
    resident = functools.partial(pl.BlockSpec, index_map=const, pipeline_mode=pl.Buffered(1))
    in_specs = [
        pl.BlockSpec((tm, d), row),
        pl.BlockSpec((tm, dc), lambda i: (i, 0)),
        pl.BlockSpec((tm, dc), lambda i: (i, 1)),
        pl.BlockSpec((tm, dc), lambda i: (i, 2)),
    ]
    args = [x, zc, zc, zc]
    if has_prev:
        prev = lambda col: (lambda i: (jnp.maximum(i * (tm // 8) - 1, 0), col))
        in_specs += [pl.BlockSpec((8, dc), prev(1)), pl.BlockSpec((8, dc), prev(2))]
        args += [zc, zc]
        buf_map = lambda i: (i // tiles_per_seq, 0, 0)
    else:
        buf_map = lambda i: (i, 0, 0)
    in_specs += [
        pl.BlockSpec((nseg, CONV_WIDTH - 1, dc), buf_map),
        pl.BlockSpec((tm, ya.shape[1]), row),
        pl.BlockSpec((CONV_WIDTH, dc), const),
        pl.BlockSpec((1, dc), const),
        pl.BlockSpec((1, ya.shape[1]), const),
        resident(w_out.shape),
        pl.BlockSpec((1, d), const),
        resident(w_xq.shape),
        pl.BlockSpec((nseg, mem_rows, HEAD_DIM), buf_map),
        pl.BlockSpec((nseg, mem_rows, HEAD_DIM), buf_map),
        resident(w_xo.shape),
        pl.BlockSpec((1, d), const),
    ]
    args += [conv_buf, ya, conv_w, g_conv_out, g_attn_out, w_out, g_xattn, w_xq, mk, mv, w_xo, g_mlp]
    steps = m // tm
    cast_specs = []
    for w, axis in cast:
        block = tuple(n // steps if a == axis else n for a, n in enumerate(w.shape))
        cast_specs.append(pl.BlockSpec(block, (lambda i: (i, 0)) if axis == 0 else (lambda i: (0, i))))
        args.append(w)
    return pl.pallas_call(
        functools.partial(_mix_kernel, seg=seg, nseg=nseg, tiles_per_seq=tiles_per_seq, has_prev=has_prev,
                          n_cast=len(cast)),
        grid=(steps,),
        in_specs=in_specs + cast_specs,
        out_specs=[
            pl.BlockSpec((tm, d), row),
            pl.BlockSpec((tm, d), row),
            pl.BlockSpec((nseg, CONV_WIDTH - 1, dc), buf_map),
        ] + cast_specs,
        out_shape=[
            jax.ShapeDtypeStruct((m, d), F32),
            jax.ShapeDtypeStruct((m, d), BF16),
            jax.ShapeDtypeStruct(conv_buf.shape, F32),
        ] + [jax.ShapeDtypeStruct(w.shape, BF16) for w, _ in cast],
        scratch_shapes=[pltpu.VMEM((tm, dx), BF16)],
        compiler_params=_params(("arbitrary",)),
        name="mix",
    )(*args)


def _memkv_kernel(x_ref, g_ref, w_ref, o_ref):
    xn = _rms(x_ref[...], g_ref[...]).astype(BF16)
    _store_head_rows(o_ref.at[0], jnp.dot(xn, w_ref[...], preferred_element_type=F32), N_X_HEADS)


def _memkv(mem, g, w):
    m, d = mem.shape
    half = w.shape[1] // 2
    return pl.pallas_call(
        _memkv_kernel,
        grid=(2,),
        in_specs=[
            pl.BlockSpec((m, d), lambda j: (0, 0)),
            pl.BlockSpec((1, d), lambda j: (0, 0)),
            pl.BlockSpec((d, half), lambda j: (0, j)),
        ],
        out_specs=pl.BlockSpec((1, m * N_X_HEADS, HEAD_DIM), lambda j: (j, 0, 0)),
        out_shape=jax.ShapeDtypeStruct((2, m * N_X_HEADS, HEAD_DIM), F32),
        compiler_params=_params(("arbitrary",)),
        name="memkv",
    )(mem, g, w)


def _mlp_kernel(h_ref, x2_hbm, wu_ref, wd_ref, g_ref, y_ref, x2_scr, sem):
    i = pl.program_id(0)
    j = pl.program_id(1)
    tm = y_ref.shape[0]
    x2_copy = pltpu.make_async_copy(x2_hbm.at[pl.ds(i * tm, tm), :], x2_scr, sem)

    @pl.when(j == 0)
    def _():
        x2_copy.start()
        y_ref[...] = jnp.zeros(y_ref.shape, F32)

    a = jnp.dot(h_ref[...], wu_ref[...], preferred_element_type=F32)
    a = jnp.square(jnp.maximum(a, 0.0)).astype(BF16)
    y_ref[...] += jnp.dot(a, wd_ref[...], preferred_element_type=F32)

    @pl.when(j == pl.num_programs(1) - 1)
    def _():
        x2_copy.wait()
        y_ref[...] = _rms(x2_scr[...] + y_ref[...], g_ref[...])


def _mlp(h, x2, w_up, w_down, g_final, *, tm, tf):
    m, d = h.shape
    ff = w_up.shape[1]
    row = lambda i, j: (i, 0)
    return pl.pallas_call(
        _mlp_kernel,
        grid=(m // tm, ff // tf),
        in_specs=[
            pl.BlockSpec((tm, d), row),
            pl.BlockSpec(memory_space=pl.ANY),
            pl.BlockSpec((d, tf), lambda i, j: (0, j)),
            pl.BlockSpec((tf, d), lambda i, j: (j, 0)),
            pl.BlockSpec((1, d), lambda i, j: (0, 0)),
        ],
        out_specs=pl.BlockSpec((tm, d), row),
        out_shape=jax.ShapeDtypeStruct((m, d), F32),
        scratch_shapes=[pltpu.VMEM((tm, d), F32), pltpu.SemaphoreType.DMA(())],
        compiler_params=pltpu.CompilerParams(dimension_semantics=("arbitrary", "arbitrary"),
                                             vmem_limit_bytes=MLP_VMEM_LIMIT),
        name="mlp",
    )(h, x2, w_up, w_down, g_final)


def kernel(x_prompt, x_sample, cache_k, cache_v, cache_logf, cache_conv, cache_mem_k, cache_mem_v, mem_prompt, g_mix, w_in, b_f, conv_w, g_conv_out, g_attn_out, w_out, g_xattn, g_mem, w_xq, w_xkv, w_xo, g_mlp, w_up, w_down, g_final):
    depth = w_in.shape[0]
    assert depth == 1, "single-layer trunk"
    nb, seq, d = x_prompt.shape
    db, dseq, _ = x_sample.shape
    past = cache_k.shape[2]
    heads, hd = cache_k.shape[3], cache_k.shape[4]
    assert (heads, hd) == (N_FOX_HEADS, HEAD_DIM)
    d_attn = heads * hd
    d_conv = cache_conv.shape[-1]
    n_mem, xh, xhd = cache_mem_k.shape[2:]
    assert (xh, xhd) == (N_X_HEADS, HEAD_DIM)
    d_x = xh * xhd
    n_main = 3 * d_conv + 3 * d_attn
    assert d_conv == d_attn and w_in.shape[2] == n_main + heads

    w_in_t = jnp.swapaxes(w_in[0], 0, 1)
    bf = b_f[0].reshape(1, heads)
    w_out_b = w_out[0].astype(BF16)
    w_xq_b = w_xq[0].astype(BF16)
    w_xkv_b = w_xkv[0].astype(BF16)
    w_xo_b = w_xo[0].astype(BF16)
    row = lambda g: g.reshape(1, -1)

    def trunk(x, conv_buf, seq_len, attention, mk, mv, mlp_w):
        xn, lf = _xnorm(x, row(g_mix[0]), w_in_t, bf, tm=512, gate_row=n_main)
        proj = functools.partial(_proj, xn, w_in_t, tn=d_attn, tm=1024)
        zc, = proj(j0=0, nj=3, mode="f32")
        qb, = proj(j0=3, nj=1, mode="bf16")
        k, kb = proj(j0=4, nj=1, mode="heads")
        v, vb = proj(j0=5, nj=1, mode="heads")
        ya = attention(qb, kb, vb, lf)
        cast = ((w_up[0], 1), (w_down[0], 0)) if mlp_w is None else ()
        x2, h, conv_new, *cast_w = _mix(x, zc, conv_buf, ya, conv_w[0], row(g_conv_out[0]), row(g_attn_out[0]),
                                        w_out_b, row(g_xattn[0]), w_xq_b, mk, mv, w_xo_b, row(g_mlp[0]),
                                        tm=256, seq=seq_len, cast=cast)
        mlp_w = mlp_w or tuple(cast_w)
        y = _mlp(h, x2, *mlp_w, row(g_final), tm=1024, tf=512)
        return y, k, v, lf, conv_new, mlp_w


    def prompt_attention(qb, kb, vb, lf):
        lft = lf.reshape(nb, seq, heads).transpose(0, 2, 1).reshape(nb * heads, seq)
        ct = _cumsum_lanes(lft).reshape(nb, heads, seq)
        return _fox_prompt(qb, kb, vb, ct, ct.transpose(0, 2, 1), n_batch=nb, seq=seq, tq=1024, tk=512)

    kv = _memkv(mem_prompt.reshape(nb * n_mem, d), row(g_mem[0]), w_xkv_b)
    mk_p = kv[0].reshape(nb, n_mem * xh, xhd)
    mv_p = kv[1].reshape(nb, n_mem * xh, xhd)
    conv0 = jnp.zeros((nb, CONV_WIDTH - 1, d_conv), F32)
    y_p, k_p, v_p, lf_p, conv_p, mlp_w = trunk(x_prompt.reshape(nb * seq, d), conv0, seq, prompt_attention,
                                               mk_p, mv_p, None)

    pad = (-(past + dseq)) % LANES

    def sample_attention(qb, kb, vb, lf):
        lft_past = cache_logf[0].transpose(0, 2, 1).reshape(db * heads, past)
        lft_new = lf.reshape(db, dseq, heads).transpose(0, 2, 1).reshape(db * heads, dseq)
        ct = _cumsum_lanes(jnp.concatenate(
            [lft_past, lft_new, jnp.zeros((db * heads, pad), F32)], axis=1))
        ck_past = ct[:, :past].reshape(db, heads, past)
        ck_new = ct[:, past:past + dseq].reshape(db, heads, dseq)
        cq = ck_new.transpose(0, 2, 1)
        hp = heads // 2
        ck_past_pairs = jnp.stack([ck_past[:, :hp], ck_past[:, hp:]], axis=-1).reshape(db, hp, 2 * past)
        ck_new_pairs = jnp.concatenate([ck_new[:, :hp], ck_new[:, hp:]], axis=-1)
        return _fox_sample(qb, kb, vb, cache_k.reshape(db, past * heads, hd),
                           cache_v.reshape(db, past * heads, hd), cq, ck_past_pairs, ck_new_pairs)

    y_s, k_s, v_s, lf_s, conv_s, _ = trunk(x_sample.reshape(db * dseq, d), cache_conv[0], dseq, sample_attention,
                                           cache_mem_k.reshape(db, n_mem * xh, xhd),
                                           cache_mem_v.reshape(db, n_mem * xh, xhd), mlp_w)

    return (y_p.reshape(nb, seq, d),
            y_s.reshape(db, dseq, d),
            k_p.reshape(1, nb, seq, heads, hd),
            v_p.reshape(1, nb, seq, heads, hd),
            lf_p.reshape(1, nb, seq, heads),
            conv_p[None],
            mk_p.reshape(1, nb, n_mem, xh, xhd),
            mv_p.reshape(1, nb, n_mem, xh, xhd),
            k_s.reshape(1, db, dseq, heads, hd),
            v_s.reshape(1, db, dseq, heads, hd),
            lf_s.reshape(1, db, dseq, heads),
            conv_s[None])
```

```python
import functools

import jax
import jax.numpy as jnp
from jax import lax
from jax.experimental import pallas as pl
from jax.experimental.pallas import tpu as pltpu

F32 = jnp.float32
BF16 = jnp.bfloat16

RMS_EPS = 1e-6
HEAD_DIM = 128
N_FOX_HEADS = 8
N_X_HEADS = 4
CONV_WIDTH = 3
LANES = 128
V_PAD_ROWS = 16
LOG2_E = 1.4426950408889634
VMEM_LIMIT = 52 * 1024 * 1024
MLP_VMEM_LIMIT = 56 * 1024 * 1024

_NT = (((1,), (1,)), ((), ()))


def _params(semantics):
    return pltpu.CompilerParams(dimension_semantics=semantics, vmem_limit_bytes=VMEM_LIMIT)


def _rms(x, g):
    ms = jnp.mean(x * x, axis=-1, keepdims=True)
    return x * lax.rsqrt(ms + RMS_EPS) * g


def _log_sigmoid(x):
    return jnp.minimum(x, 0.0) - jnp.log1p(jnp.exp(-jnp.abs(x)))


def _store_head_rows(ref, val, heads):
    n = val.shape[0]
    for h in range(heads):
        ref[pl.ds(h, n, stride=heads), :] = val[:, h * HEAD_DIM:(h + 1) * HEAD_DIM]


def _load_head_rows(ref, lead, h, n, heads):
    return ref[lead, pl.ds(h, n, stride=heads), :]


def _xnorm_kernel(x_ref, g_ref, wf_ref, bf_ref, xn_ref, lf_ref):
    xn = _rms(x_ref[...], g_ref[...]).astype(BF16)
    xn_ref[...] = xn
    wf = wf_ref[...].astype(BF16)
    wf = jnp.concatenate([wf, jnp.zeros((LANES - wf.shape[0], wf.shape[1]), BF16)], axis=0)
    fz = lax.dot_general(xn, wf, _NT, preferred_element_type=F32)
    lf_ref[...] = _log_sigmoid(fz[:, :N_FOX_HEADS] + bf_ref[...])


def _xnorm(x, g, w_t, b_f, *, tm, gate_row):
    m, d = x.shape
    row = lambda i: (i, 0)
    return pl.pallas_call(
        _xnorm_kernel,
        grid=(m // tm,),
        in_specs=[
            pl.BlockSpec((tm, d), row),
            pl.BlockSpec((1, d), lambda i: (0, 0)),
            pl.BlockSpec((N_FOX_HEADS, d), lambda i: (gate_row // N_FOX_HEADS, 0)),
            pl.BlockSpec((1, N_FOX_HEADS), lambda i: (0, 0)),
        ],
        out_specs=[pl.BlockSpec((tm, d), row), pl.BlockSpec((tm, N_FOX_HEADS), row)],
        out_shape=[jax.ShapeDtypeStruct((m, d), BF16), jax.ShapeDtypeStruct((m, N_FOX_HEADS), F32)],
        compiler_params=_params(("arbitrary",)),
        name="xnorm",
    )(x, g, w_t, b_f)


def _proj_kernel(xn_ref, wt_ref, *refs, mode, out_scale):
    out_refs, wb_ref = refs[:-1], refs[-1]

    @pl.when(pl.program_id(1) == 0)
    def _():
        wb_ref[...] = wt_ref[...].astype(BF16)

    res = lax.dot_general(xn_ref[...], wb_ref[...], _NT, preferred_element_type=F32)
    if mode == "f32":
        out_refs[0][...] = res
    elif mode == "bf16":
        out_refs[0][...] = (res if out_scale is None else res * out_scale).astype(BF16)
    else:
        _store_head_rows(out_refs[0], res, N_FOX_HEADS)
        out_refs[1][...] = res.astype(BF16)


def _proj(xn, w_t, *, j0, nj, tn, tm, mode, out_scale=None):
    m, d = xn.shape
    if mode == "heads":
        assert nj == 1
        out_specs = [pl.BlockSpec((tm * N_FOX_HEADS, HEAD_DIM), lambda j, i: (i, 0)),
                     pl.BlockSpec((tm, tn), lambda j, i: (i, 0))]
        out_shape = [jax.ShapeDtypeStruct((m * N_FOX_HEADS, HEAD_DIM), F32), jax.ShapeDtypeStruct((m, tn), BF16)]
    else:
        out_specs = [pl.BlockSpec((tm, tn), lambda j, i: (i, j))]
        out_shape = [jax.ShapeDtypeStruct((m, nj * tn), F32 if mode == "f32" else BF16)]
    return pl.pallas_call(
        functools.partial(_proj_kernel, mode=mode, out_scale=out_scale),
        grid=(nj, m // tm),
        in_specs=[
            pl.BlockSpec((tm, d), lambda j, i: (i, 0)),
            pl.BlockSpec((tn, d), lambda j, i: (j0 + j, 0)),
        ],
        out_specs=out_specs,
        out_shape=out_shape,
        scratch_shapes=[pltpu.VMEM((tn, d), BF16)],
        compiler_params=_params(("arbitrary", "arbitrary")),
        name="proj_" + mode,
    )(xn, w_t)


def _cumsum_kernel(a_ref, o_ref):
    rows, length = a_ref.shape
    r = lax.broadcasted_iota(jnp.int32, (LANES, LANES), 0)
    c = lax.broadcasted_iota(jnp.int32, (LANES, LANES), 1)
    upper = (r <= c).astype(BF16)
    carry = jnp.zeros((rows, 1), F32)
    for b in range(length // LANES):
        a = a_ref[:, b * LANES:(b + 1) * LANES]
        hi = a.astype(BF16)
        r1 = a - hi.astype(F32)
        mid = r1.astype(BF16)
        lo = (r1 - mid.astype(F32)).astype(BF16)
        s = (jnp.dot(hi, upper, preferred_element_type=F32)
             + jnp.dot(mid, upper, preferred_element_type=F32)
             + jnp.dot(lo, upper, preferred_element_type=F32)) + carry
        o_ref[:, b * LANES:(b + 1) * LANES] = s
        carry = s[:, LANES - 1:LANES]


def _cumsum_lanes(a):
    return pl.pallas_call(
        _cumsum_kernel,
        out_shape=jax.ShapeDtypeStruct(a.shape, F32),
        compiler_params=pltpu.CompilerParams(vmem_limit_bytes=VMEM_LIMIT),
        name="cumsum",
    )(a)


def _split3(x):
    hi = x.astype(BF16).astype(F32)
    r = x - hi
    mid = r.astype(BF16).astype(F32)
    lo = (r - mid).astype(BF16).astype(F32)
    return hi, mid, lo


def _fox_prompt_kernel(q_ref, k_ref, v_ref, cq_ref, ck_ref, o_ref, kaug_ref, vt_ref, qaug_ref, acc_ref, s0_ref, s1_ref,
                       *, tq, tk):
    assert tq == 2 * tk
    h = pl.program_id(1)
    qi = pl.program_id(2)
    nk = kaug_ref.shape[0] // tk

    @pl.when(qi == 0)
    def _():
        lane8 = lax.broadcasted_iota(jnp.int32, (tk, N_FOX_HEADS), 1)
        lane = lax.broadcasted_iota(jnp.int32, (tk, HEAD_DIM), 1)
        ones_row = (lax.broadcasted_iota(jnp.int32, (V_PAD_ROWS, tk), 0) == 0).astype(BF16)

        def prep(c, carry):
            rows = pl.ds(pl.multiple_of(c * tk, tk), tk)
            kaug_ref[rows, 0:HEAD_DIM] = k_ref[rows, :]
            col = jnp.sum(jnp.where(lane8 == h, ck_ref[0, rows, :], 0.0), axis=1, keepdims=True) * LOG2_E
            hi, mid, lo = _split3(col)
            aug = jnp.where(lane == 0, hi, jnp.where(lane == 1, mid, jnp.where(lane == 2, lo,
                            jnp.where(lane < 6, 1.0, 0.0))))
            kaug_ref[rows, HEAD_DIM:2 * HEAD_DIM] = aug.astype(BF16)
            vt_ref[c, 0:HEAD_DIM, :] = v_ref[rows, :].astype(F32).T.astype(BF16)
            vt_ref[c, HEAD_DIM:, :] = ones_row
            return carry

        lax.fori_loop(0, nk, prep, 0)

    qaug_ref[0:HEAD_DIM, :] = q_ref[...].astype(F32).T.astype(BF16)
    hi, mid, lo = _split3(cq_ref[0, pl.ds(h, 1), :] * LOG2_E)
    sub = lax.broadcasted_iota(jnp.int32, (HEAD_DIM, tq), 0)
    qaug_ref[HEAD_DIM:2 * HEAD_DIM, :] = jnp.where(
        sub < 3, -1.0, jnp.where(sub == 3, hi, jnp.where(sub == 4, mid, jnp.where(sub == 5, lo, 0.0)))).astype(BF16)
    acc_ref[...] = jnp.zeros(acc_ref.shape, F32)

    def scores(ki, dst_ref, lane0=0):
        rows = pl.ds(pl.multiple_of(ki * tk, tk), tk)
        dst_ref[:, lane0:] = jnp.dot(kaug_ref[rows, :], qaug_ref[:, lane0:], preferred_element_type=F32)

    def softmax_pv(ki, m_all, s_ref, lane0=0, mask_offset=None):
        m_prev = m_all[:, lane0:]
        s = s_ref[:, lane0:]
        if mask_offset is not None:
            r = lax.broadcasted_iota(jnp.int32, s.shape, 0) + mask_offset
            c = lax.broadcasted_iota(jnp.int32, s.shape, 1) + lane0
            s = jnp.where(r <= c, s, -jnp.inf)
        m_new = jnp.maximum(m_prev, jnp.max(s, axis=0, keepdims=True))
        alpha = jnp.exp2(m_prev - m_new)
        p = jnp.exp2(s - m_new).astype(BF16)
        pv = jnp.dot(vt_ref[ki], p, preferred_element_type=F32)
        acc_ref[:, lane0:] = alpha * acc_ref[:, lane0:] + pv
        return jnp.concatenate([m_all[:, :lane0], m_new], axis=1) if lane0 else m_new

    def pair(j, m):
        scores(2 * j + 1, s1_ref)
        m = softmax_pv(2 * j, m, s0_ref)
        scores(2 * j + 2, s0_ref)
        return softmax_pv(2 * j + 1, m, s1_ref)

    scores(0, s0_ref)
    m = lax.fori_loop(0, qi, pair, jnp.full((1, tq), -jnp.inf, F32))
    scores(2 * qi + 1, s1_ref, lane0=tk)
    m = softmax_pv(2 * qi, m, s0_ref, mask_offset=0)
    softmax_pv(2 * qi + 1, m, s1_ref, lane0=tk, mask_offset=tk)
    o_ref[...] = (acc_ref[0:HEAD_DIM, :] / acc_ref[HEAD_DIM:HEAD_DIM + 1, :]).T


def _fox_prompt(qb, kb, vb, cq, ck, *, n_batch, seq, tq, tk):
    m = qb.shape[0]
    nq = seq // tq
    return pl.pallas_call(
        functools.partial(_fox_prompt_kernel, tq=tq, tk=tk),
        grid=(n_batch, N_FOX_HEADS, nq),
        in_specs=[
            pl.BlockSpec((tq, HEAD_DIM), lambda n, h, i: (n * nq + i, h)),
            pl.BlockSpec((seq, HEAD_DIM), lambda n, h, i: (n, h)),
            pl.BlockSpec((seq, HEAD_DIM), lambda n, h, i: (n, h)),
            pl.BlockSpec((1, N_FOX_HEADS, tq), lambda n, h, i: (n, 0, i)),
            pl.BlockSpec((1, seq, N_FOX_HEADS), lambda n, h, i: (n, 0, 0)),
        ],
        out_specs=pl.BlockSpec((tq, HEAD_DIM), lambda n, h, i: (n * nq + i, h)),
        out_shape=jax.ShapeDtypeStruct((m, N_FOX_HEADS * HEAD_DIM), F32),
        scratch_shapes=[
            pltpu.VMEM((seq, 2 * HEAD_DIM), BF16),
            pltpu.VMEM((seq // tk, HEAD_DIM + V_PAD_ROWS, tk), BF16),
            pltpu.VMEM((2 * HEAD_DIM, tq), BF16),
            pltpu.VMEM((HEAD_DIM + V_PAD_ROWS, tq), F32),
            pltpu.VMEM((tk, tq), F32),
            pltpu.VMEM((tk, tq), F32),
        ],
        compiler_params=_params(("arbitrary", "arbitrary", "arbitrary")),
        name="fox_prompt",
    )(qb, kb, vb, cq, ck)


def _fox_sample_kernel(q_ref, kn_ref, vn_ref, kc_ref, vc_ref, cq_ref, ckp_ref, ckn_ref, o_ref, *, past):
    scale = HEAD_DIM ** -0.5
    t = q_ref.shape[0]
    half = N_FOX_HEADS // 2
    r = lax.broadcasted_iota(jnp.int32, (2 * t, 2 * past), 0)
    c = lax.broadcasted_iota(jnp.int32, (2 * t, 2 * past), 1)
    own_past = (r >= t) == ((c & 1) == 1)
    rn = lax.broadcasted_iota(jnp.int32, (2 * t, 2 * t), 0)
    cn = lax.broadcasted_iota(jnp.int32, (2 * t, 2 * t), 1)
    own_new = ((rn >= t) == (cn >= t)) & ((cn & (t - 1)) <= (rn & (t - 1)))
    for j in range(half):
        sl0 = slice(j * HEAD_DIM, (j + 1) * HEAD_DIM)
        sl1 = slice((j + half) * HEAD_DIM, (j + half + 1) * HEAD_DIM)
        both = lambda ref: jnp.concatenate([ref[:, sl0], ref[:, sl1]], axis=0)
        q = both(q_ref)
        kp = kc_ref[0, pl.ds(j, 2 * past, stride=half), :].astype(BF16)
        vp = vc_ref[0, pl.ds(j, 2 * past, stride=half), :].astype(BF16)
        cq = jnp.concatenate([cq_ref[0, :, j:j + 1], cq_ref[0, :, j + half:j + half + 1]], axis=0)
        s1 = lax.dot_general(q, kp, _NT, preferred_element_type=F32) * scale
        s1 = jnp.where(own_past, s1 + (cq - ckp_ref[0, j:j + 1, :]), -jnp.inf)
        s2 = lax.dot_general(q, both(kn_ref), _NT, preferred_element_type=F32) * scale
        s2 = jnp.where(own_new, s2 + (cq - ckn_ref[0, j:j + 1, :]), -jnp.inf)
        m = jnp.maximum(jnp.max(s1, axis=1, keepdims=True), jnp.max(s2, axis=1, keepdims=True))
        p1 = jnp.exp(s1 - m)
        p2 = jnp.exp(s2 - m)
        l = jnp.sum(p1, axis=1, keepdims=True) + jnp.sum(p2, axis=1, keepdims=True)
        o = (jnp.dot(p1.astype(BF16), vp, preferred_element_type=F32)
             + jnp.dot(p2.astype(BF16), both(vn_ref), preferred_element_type=F32)) / l
        o_ref[:, sl0] = o[:t]
        o_ref[:, sl1] = o[t:]


def _fox_sample(qb, knb, vnb, cache_k, cache_v, cq, ck_past, ck_new):
    n_batch, rows, _ = cache_k.shape
    past = rows // N_FOX_HEADS
    t = qb.shape[0] // n_batch
    width = qb.shape[1]
    tok = lambda b: (b, 0)
    lead = lambda b: (b, 0, 0)
    return pl.pallas_call(
        functools.partial(_fox_sample_kernel, past=past),
        grid=(n_batch,),
        in_specs=[
            pl.BlockSpec((t, width), tok),
            pl.BlockSpec((t, width), tok),
            pl.BlockSpec((t, width), tok),
            pl.BlockSpec((1, rows, HEAD_DIM), lead),
            pl.BlockSpec((1, rows, HEAD_DIM), lead),
            pl.BlockSpec((1, t, N_FOX_HEADS), lead),
            pl.BlockSpec((1, N_FOX_HEADS // 2, 2 * past), lead),
            pl.BlockSpec((1, N_FOX_HEADS // 2, 2 * t), lead),
        ],
        out_specs=pl.BlockSpec((t, width), tok),
        out_shape=jax.ShapeDtypeStruct(qb.shape, F32),
        compiler_params=_params(("arbitrary",)),
        name="fox_sample",
    )(qb, knb, vnb, cache_k, cache_v, cq, ck_past, ck_new)


def _mix_kernel(*refs, seg, nseg, tiles_per_seq, has_prev, n_cast):
    if has_prev:
        (x_ref, gb_ref, gc_ref, hin_ref, pgc_ref, phin_ref, *refs) = refs
    else:
        (x_ref, gb_ref, gc_ref, hin_ref, *refs) = refs
    (buf_ref, ya_ref, cw_ref, gco_ref, gao_ref, wout_ref, gx_ref, wxq_ref, mk_ref, mv_ref, wxo_ref, gm_ref,
     *refs) = refs
    cast_in, (x2_ref, h_ref, cnew_ref, *cast_out, o_scr) = refs[:n_cast], refs[n_cast:]
    for src_ref, dst_ref in zip(cast_in, cast_out):
        dst_ref[...] = src_ref[...].astype(BF16)
    i = pl.program_id(0)
    u = gc_ref[...] * hin_ref[...]
    gb = gb_ref[...]
    w0 = cw_ref[0:1, :]
    w1 = cw_ref[1:2, :]
    w2 = cw_ref[2:3, :]
    rid = lax.broadcasted_iota(jnp.int32, (seg, u.shape[1]), 0)
    pieces = []
    for s in range(nseg):
        u_s = u[s * seg:(s + 1) * seg]
        b0 = buf_ref[s, 0:1, :]
        b1 = buf_ref[s, 1:2, :]
        if has_prev:
            pu = pgc_ref[...] * phin_ref[...]
            first = (i % tiles_per_seq) == 0
            b0 = jnp.where(first, b0, pu[6:7])
            b1 = jnp.where(first, b1, pu[7:8])
        r1 = jnp.where(rid == 0, b1, pltpu.roll(u_s, 1, 0))
        r2 = jnp.where(rid == 0, b0, jnp.where(rid == 1, b1, pltpu.roll(u_s, 2, 0)))
        y = w0 * r2 + w1 * r1 + w2 * u_s
        pieces.append(gb[s * seg:(s + 1) * seg] * y)
        cnew_ref[s] = u_s[seg - 2:seg]
    yc = pieces[0] if nseg == 1 else jnp.concatenate(pieces, axis=0)
    cat = jnp.concatenate([_rms(yc, gco_ref[...]).astype(BF16),
                           _rms(ya_ref[...], gao_ref[...]).astype(BF16)], axis=-1)
    x1 = x_ref[...] + jnp.dot(cat, wout_ref[...], preferred_element_type=F32)
    xn = _rms(x1, gx_ref[...]).astype(BF16)
    qx = jnp.dot(xn, wxq_ref[...], preferred_element_type=F32).astype(BF16)

    scale = HEAD_DIM ** -0.5
    n_mem = mk_ref.shape[1] // N_X_HEADS
    for s in range(nseg):
        rows = slice(s * seg, (s + 1) * seg)
        for hh in range(N_X_HEADS):
            sl = slice(hh * HEAD_DIM, (hh + 1) * HEAD_DIM)
            kh = _load_head_rows(mk_ref, s, hh, n_mem, N_X_HEADS).astype(BF16)
            vh = _load_head_rows(mv_ref, s, hh, n_mem, N_X_HEADS).astype(BF16)
            sc = lax.dot_general(qx[rows, sl], kh, _NT, preferred_element_type=F32) * scale
            m = jnp.max(sc, axis=1, keepdims=True)
            p = jnp.exp(sc - m)
            l = jnp.sum(p, axis=1, keepdims=True)
            o = jnp.dot(p.astype(BF16), vh, preferred_element_type=F32) / l
            o_scr[rows, sl] = o.astype(BF16)
    x2 = x1 + jnp.dot(o_scr[...], wxo_ref[...], preferred_element_type=F32)
    x2_ref[...] = x2
    h_ref[...] = _rms(x2, gm_ref[...]).astype(BF16)


def _mix(x, zc, conv_buf, ya, conv_w, g_conv_out, g_attn_out, w_out, g_xattn, w_xq, mk, mv, w_xo, g_mlp, *, tm, seq,
         cast=()):
    m, d = x.shape
    dc = zc.shape[1] // 3
    dx = w_xq.shape[1]
    mem_rows = mk.shape[1]
    has_prev = seq > tm
    seg = tm if has_prev else seq
    nseg = tm // seg
    tiles_per_seq = max(seq // tm, 1)
    row = lambda i: (i, 0)
    const = lambda i: (0, 0)
    resident = functools.partial(pl.BlockSpec, index_map=const, pipeline_mode=pl.Buffered(1))
    in_specs = [
        pl.BlockSpec((tm, d), row),
        pl.BlockSpec((tm, dc), lambda i: (i, 0)),
        pl.BlockSpec((tm, dc), lambda i: (i, 1)),
        pl.BlockSpec((tm, dc), lambda i: (i, 2)),
    ]
    args = [x, zc, zc, zc]
    if has_prev:
        prev = lambda col: (lambda i: (jnp.maximum(i * (tm // 8) - 1, 0), col))
        in_specs += [pl.BlockSpec((8, dc), prev(1)), pl.BlockSpec((8, dc), prev(2))]
        args += [zc, zc]
        buf_map = lambda i: (i // tiles_per_seq, 0, 0)
    else:
        buf_map = lambda i: (i, 0, 0)
    in_specs += [
        pl.BlockSpec((nseg, CONV_WIDTH - 1, dc), buf_map),
        pl.BlockSpec((tm, ya.shape[1]), row),
        pl.BlockSpec((CONV_WIDTH, dc), const),
        pl.BlockSpec((1, dc), const),
        pl.BlockSpec((1, ya.shape[1]), const),
        resident(w_out.shape),
        pl.BlockSpec((1, d), const),
        resident(w_xq.shape),
        pl.BlockSpec((nseg, mem_rows, HEAD_DIM), buf_map),
        pl.BlockSpec((nseg, mem_rows, HEAD_DIM), buf_map),
        resident(w_xo.shape),
        pl.BlockSpec((1, d), const),
    ]
    args += [conv_buf, ya, conv_w, g_conv_out, g_attn_out, w_out, g_xattn, w_xq, mk, mv, w_xo, g_mlp]
    steps = m // tm
    cast_specs = []
    for w, axis in cast:
        block = tuple(n // steps if a == axis else n for a, n in enumerate(w.shape))
        cast_specs.append(pl.BlockSpec(block, (lambda i: (i, 0)) if axis == 0 else (lambda i: (0, i))))
        args.append(w)
    return pl.pallas_call(
        functools.partial(_mix_kernel, seg=seg, nseg=nseg, tiles_per_seq=tiles_per_seq, has_prev=has_prev,
                          n_cast=len(cast)),
        grid=(steps,),
        in_specs=in_specs + cast_specs,
        out_specs=[
            pl.BlockSpec((tm, d), row),
            pl.BlockSpec((tm, d), row),
            pl.BlockSpec((nseg, CONV_WIDTH - 1, dc), buf_map),
        ] + cast_specs,
        out_shape=[
            jax.ShapeDtypeStruct((m, d), F32),
            jax.ShapeDtypeStruct((m, d), BF16),
            jax.ShapeDtypeStruct(conv_buf.shape, F32),
        ] + [jax.ShapeDtypeStruct(w.shape, BF16) for w, _ in cast],
        scratch_shapes=[pltpu.VMEM((tm, dx), BF16)],
        compiler_params=_params(("arbitrary",)),
        name="mix",
    )(*args)


def _memkv_kernel(x_ref, g_ref, w_ref, o_ref):
    xn = _rms(x_ref[...], g_ref[...]).astype(BF16)
    _store_head_rows(o_ref.at[0], jnp.dot(xn, w_ref[...], preferred_element_type=F32), N_X_HEADS)


def _memkv(mem, g, w):
    m, d = mem.shape
    half = w.shape[1] // 2
    return pl.pallas_call(
        _memkv_kernel,
        grid=(2,),
        in_specs=[
            pl.BlockSpec((m, d), lambda j: (0, 0)),
            pl.BlockSpec((1, d), lambda j: (0, 0)),
            pl.BlockSpec((d, half), lambda j: (0, j)),
        ],
        out_specs=pl.BlockSpec((1, m * N_X_HEADS, HEAD_DIM), lambda j: (j, 0, 0)),
        out_shape=jax.ShapeDtypeStruct((2, m * N_X_HEADS, HEAD_DIM), F32),
        compiler_params=_params(("arbitrary",)),
        name="memkv",
    )(mem, g, w)


def _mlp_kernel(h_ref, x2_hbm, wu_ref, wd_ref, g_ref, y_ref, sem):
    i = pl.program_id(0)
    j = pl.program_id(1)
    tm = y_ref.shape[0]
    x2_copy = pltpu.make_async_copy(x2_hbm.at[pl.ds(i * tm, tm), :], y_ref, sem)

    def hidden():
        a = jnp.dot(h_ref[...], wu_ref[...], preferred_element_type=F32)
        return jnp.square(jnp.maximum(a, 0.0)).astype(BF16)

    @pl.when(j == 0)
    def _():
        x2_copy.start()
        a = hidden()
        x2_copy.wait()
        y_ref[...] += jnp.dot(a, wd_ref[...], preferred_element_type=F32)

    @pl.when(j > 0)
    def _():
        y_ref[...] += jnp.dot(hidden(), wd_ref[...], preferred_element_type=F32)

    @pl.when(j == pl.num_programs(1) - 1)
    def _():
        y_ref[...] = _rms(y_ref[...], g_ref[...])


def _mlp(h, x2, w_up, w_down, g_final, *, tm, tf):
    m, d = h.shape
    ff = w_up.shape[1]
    row = lambda i, j: (i, 0)
    return pl.pallas_call(
        _mlp_kernel,
        grid=(m // tm, ff // tf),
        in_specs=[
            pl.BlockSpec((tm, d), row),
            pl.BlockSpec(memory_space=pl.ANY),
            pl.BlockSpec((d, tf), lambda i, j: (0, j)),
            pl.BlockSpec((tf, d), lambda i, j: (j, 0)),
            pl.BlockSpec((1, d), lambda i, j: (0, 0)),
        ],
        out_specs=pl.BlockSpec((tm, d), row),
        out_shape=jax.ShapeDtypeStruct((m, d), F32),
        scratch_shapes=[pltpu.SemaphoreType.DMA(())],
        compiler_params=pltpu.CompilerParams(dimension_semantics=("arbitrary", "arbitrary"),
                                             vmem_limit_bytes=MLP_VMEM_LIMIT),
        name="mlp",
    )(h, x2, w_up, w_down, g_final)


def kernel(x_prompt, x_sample, cache_k, cache_v, cache_logf, cache_conv, cache_mem_k, cache_mem_v, mem_prompt, g_mix, w_in, b_f, conv_w, g_conv_out, g_attn_out, w_out, g_xattn, g_mem, w_xq, w_xkv, w_xo, g_mlp, w_up, w_down, g_final):
    depth = w_in.shape[0]
    assert depth == 1, "single-layer trunk"
    nb, seq, d = x_prompt.shape
    db, dseq, _ = x_sample.shape
    past = cache_k.shape[2]
    heads, hd = cache_k.shape[3], cache_k.shape[4]
    assert (heads, hd) == (N_FOX_HEADS, HEAD_DIM)
    d_attn = heads * hd
    d_conv = cache_conv.shape[-1]
    n_mem, xh, xhd = cache_mem_k.shape[2:]
    assert (xh, xhd) == (N_X_HEADS, HEAD_DIM)
    d_x = xh * xhd
    n_main = 3 * d_conv + 3 * d_attn
    assert d_conv == d_attn and w_in.shape[2] == n_main + heads

    w_in_t = jnp.swapaxes(w_in[0], 0, 1)
    bf = b_f[0].reshape(1, heads)
    w_out_b = w_out[0].astype(BF16)
    w_xq_b = w_xq[0].astype(BF16)
    w_xkv_b = w_xkv[0].astype(BF16)
    w_xo_b = w_xo[0].astype(BF16)
    row = lambda g: g.reshape(1, -1)

    def trunk(x, conv_buf, seq_len, attention, mk, mv, mlp_w, q_scale=None):
        xn, lf = _xnorm(x, row(g_mix[0]), w_in_t, bf, tm=512, gate_row=n_main)
        proj = functools.partial(_proj, xn, w_in_t, tn=d_attn, tm=1024)
        zc, = proj(j0=0, nj=3, mode="f32")
        qb, = proj(j0=3, nj=1, mode="bf16", out_scale=q_scale)
        k, kb = proj(j0=4, nj=1, mode="heads")
        v, vb = proj(j0=5, nj=1, mode="heads")
        ya = attention(qb, kb, vb, lf)
        cast = ((w_up[0], 1), (w_down[0], 0)) if mlp_w is None else ()
        x2, h, conv_new, *cast_w = _mix(x, zc, conv_buf, ya, conv_w[0], row(g_conv_out[0]), row(g_attn_out[0]),
                                        w_out_b, row(g_xattn[0]), w_xq_b, mk, mv, w_xo_b, row(g_mlp[0]),
                                        tm=256, seq=seq_len, cast=cast)
        mlp_w = mlp_w or tuple(cast_w)
        y = _mlp(h, x2, *mlp_w, row(g_final), tm=1024, tf=1024)
        return y, k, v, lf, conv_new, mlp_w


    def prompt_attention(qb, kb, vb, lf):
        lft = lf.reshape(nb, seq, heads).transpose(0, 2, 1).reshape(nb * heads, seq)
        ct = _cumsum_lanes(lft).reshape(nb, heads, seq)
        return _fox_prompt(qb, kb, vb, ct, ct.transpose(0, 2, 1), n_batch=nb, seq=seq, tq=1024, tk=512)

    kv = _memkv(mem_prompt.reshape(nb * n_mem, d), row(g_mem[0]), w_xkv_b)
    mk_p = kv[0].reshape(nb, n_mem * xh, xhd)
    mv_p = kv[1].reshape(nb, n_mem * xh, xhd)
    conv0 = jnp.zeros((nb, CONV_WIDTH - 1, d_conv), F32)
    y_p, k_p, v_p, lf_p, conv_p, mlp_w = trunk(x_prompt.reshape(nb * seq, d), conv0, seq, prompt_attention,
                                               mk_p, mv_p, None, q_scale=hd ** -0.5 * LOG2_E)

    pad = (-(past + dseq)) % LANES

    def sample_attention(qb, kb, vb, lf):
        lft_past = cache_logf[0].transpose(0, 2, 1).reshape(db * heads, past)
        lft_new = lf.reshape(db, dseq, heads).transpose(0, 2, 1).reshape(db * heads, dseq)
        ct = _cumsum_lanes(jnp.concatenate(
            [lft_past, lft_new, jnp.zeros((db * heads, pad), F32)], axis=1))
        ck_past = ct[:, :past].reshape(db, heads, past)
        ck_new = ct[:, past:past + dseq].reshape(db, heads, dseq)
        cq = ck_new.transpose(0, 2, 1)
        hp = heads // 2
        ck_past_pairs = jnp.stack([ck_past[:, :hp], ck_past[:, hp:]], axis=-1).reshape(db, hp, 2 * past)
        ck_new_pairs = jnp.concatenate([ck_new[:, :hp], ck_new[:, hp:]], axis=-1)
        return _fox_sample(qb, kb, vb, cache_k.reshape(db, past * heads, hd),
                           cache_v.reshape(db, past * heads, hd), cq, ck_past_pairs, ck_new_pairs)

    y_s, k_s, v_s, lf_s, conv_s, _ = trunk(x_sample.reshape(db * dseq, d), cache_conv[0], dseq, sample_attention,
                                           cache_mem_k.reshape(db, n_mem * xh, xhd),
                                           cache_mem_v.reshape(db, n_mem * xh, xhd), mlp_w)

    return (y_p.reshape(nb, seq, d),
            y_s.reshape(db, dseq, d),
            k_p.reshape(1, nb, seq, heads, hd),
            v_p.reshape(1, nb, seq, heads, hd),
            lf_p.reshape(1, nb, seq, heads),
            conv_p[None],
            mk_p.reshape(1, nb, n_mem, xh, xhd),
            mv_p.reshape(1, nb, n_mem, xh, xhd),
            k_s.reshape(1, db, dseq, heads, hd),
            v_s.reshape(1, db, dseq, heads, hd),
            lf_s.reshape(1, db, dseq, heads),
            conv_s[None])
```

```python
import functools

import jax
import jax.numpy as jnp
from jax import lax
from jax.experimental import pallas as pl
from jax.experimental.pallas import tpu as pltpu

F32 = jnp.float32
BF16 = jnp.bfloat16

RMS_EPS = 1e-6
HEAD_DIM = 128
N_FOX_HEADS = 8
N_X_HEADS = 4
CONV_WIDTH = 3
LANES = 128
V_PAD_ROWS = 16
LOG2_E = 1.4426950408889634
VMEM_LIMIT = 52 * 1024 * 1024
MLP_VMEM_LIMIT = 56 * 1024 * 1024

_NT = (((1,), (1,)), ((), ()))


def _params(semantics):
    return pltpu.CompilerParams(dimension_semantics=semantics, vmem_limit_bytes=VMEM_LIMIT)


def _rms(x, g):
    ms = jnp.mean(x * x, axis=-1, keepdims=True)
    return x * lax.rsqrt(ms + RMS_EPS) * g


def _log_sigmoid(x):
    return jnp.minimum(x, 0.0) - jnp.log1p(jnp.exp(-jnp.abs(x)))


def _store_head_rows(ref, val, heads):
    n = val.shape[0]
    for h in range(heads):
        ref[pl.ds(h, n, stride=heads), :] = val[:, h * HEAD_DIM:(h + 1) * HEAD_DIM]


def _load_head_rows(ref, lead, h, n, heads):
    return ref[lead, pl.ds(h, n, stride=heads), :]


def _xnorm_kernel(x_ref, g_ref, wf_ref, bf_ref, xn_ref, lf_ref):
    xn = _rms(x_ref[...], g_ref[...]).astype(BF16)
    xn_ref[...] = xn
    wf = wf_ref[...].astype(BF16)
    wf = jnp.concatenate([wf, jnp.zeros((LANES - wf.shape[0], wf.shape[1]), BF16)], axis=0)
    fz = lax.dot_general(xn, wf, _NT, preferred_element_type=F32)
    lf_ref[...] = _log_sigmoid(fz[:, :N_FOX_HEADS] + bf_ref[...])


def _xnorm(x, g, w_t, b_f, *, tm, gate_row):
    m, d = x.shape
    row = lambda i: (i, 0)
    return pl.pallas_call(
        _xnorm_kernel,
        grid=(m // tm,),
        in_specs=[
            pl.BlockSpec((tm, d), row),
            pl.BlockSpec((1, d), lambda i: (0, 0)),
            pl.BlockSpec((N_FOX_HEADS, d), lambda i: (gate_row // N_FOX_HEADS, 0)),
            pl.BlockSpec((1, N_FOX_HEADS), lambda i: (0, 0)),
        ],
        out_specs=[pl.BlockSpec((tm, d), row), pl.BlockSpec((tm, N_FOX_HEADS), row)],
        out_shape=[jax.ShapeDtypeStruct((m, d), BF16), jax.ShapeDtypeStruct((m, N_FOX_HEADS), F32)],
        compiler_params=_params(("arbitrary",)),
        name="xnorm",
    )(x, g, w_t, b_f)


def _proj_kernel(xn_ref, wt_ref, *refs, mode, out_scale):
    out_refs, wb_ref = refs[:-1], refs[-1]

    @pl.when(pl.program_id(1) == 0)
    def _():
        wb_ref[...] = wt_ref[...].astype(BF16)

    res = lax.dot_general(xn_ref[...], wb_ref[...], _NT, preferred_element_type=F32)
    if mode == "f32":
        out_refs[0][...] = res
    elif mode == "bf16":
        out_refs[0][...] = (res if out_scale is None else res * out_scale).astype(BF16)
    else:
        _store_head_rows(out_refs[0], res, N_FOX_HEADS)
        out_refs[1][...] = res.astype(BF16)


def _proj(xn, w_t, *, j0, nj, tn, tm, mode, out_scale=None):
    m, d = xn.shape
    if mode == "heads":
        assert nj == 1
        out_specs = [pl.BlockSpec((tm * N_FOX_HEADS, HEAD_DIM), lambda j, i: (i, 0)),
                     pl.BlockSpec((tm, tn), lambda j, i: (i, 0))]
        out_shape = [jax.ShapeDtypeStruct((m * N_FOX_HEADS, HEAD_DIM), F32), jax.ShapeDtypeStruct((m, tn), BF16)]
    else:
        out_specs = [pl.BlockSpec((tm, tn), lambda j, i: (i, j))]
        out_shape = [jax.ShapeDtypeStruct((m, nj * tn), F32 if mode == "f32" else BF16)]
    return pl.pallas_call(
        functools.partial(_proj_kernel, mode=mode, out_scale=out_scale),
        grid=(nj, m // tm),
        in_specs=[
            pl.BlockSpec((tm, d), lambda j, i: (i, 0)),
            pl.BlockSpec((tn, d), lambda j, i: (j0 + j, 0)),
        ],
        out_specs=out_specs,
        out_shape=out_shape,
        scratch_shapes=[pltpu.VMEM((tn, d), BF16)],
        compiler_params=_params(("arbitrary", "arbitrary")),
        name="proj_" + mode,
    )(xn, w_t)


def _cumsum_kernel(a_ref, o_ref):
    rows, length = a_ref.shape
    r = lax.broadcasted_iota(jnp.int32, (LANES, LANES), 0)
    c = lax.broadcasted_iota(jnp.int32, (LANES, LANES), 1)
    upper = (r <= c).astype(BF16)
    carry = jnp.zeros((rows, 1), F32)
    for b in range(length // LANES):
        a = a_ref[:, b * LANES:(b + 1) * LANES]
        hi = a.astype(BF16)
        r1 = a - hi.astype(F32)
        mid = r1.astype(BF16)
        lo = (r1 - mid.astype(F32)).astype(BF16)
        s = (jnp.dot(hi, upper, preferred_element_type=F32)
             + jnp.dot(mid, upper, preferred_element_type=F32)
             + jnp.dot(lo, upper, preferred_element_type=F32)) + carry
        o_ref[:, b * LANES:(b + 1) * LANES] = s
        carry = s[:, LANES - 1:LANES]


def _cumsum_lanes(a):
    return pl.pallas_call(
        _cumsum_kernel,
        out_shape=jax.ShapeDtypeStruct(a.shape, F32),
        compiler_params=pltpu.CompilerParams(vmem_limit_bytes=VMEM_LIMIT),
        name="cumsum",
    )(a)


def _split3(x):
    hi = x.astype(BF16).astype(F32)
    r = x - hi
    mid = r.astype(BF16).astype(F32)
    lo = (r - mid).astype(BF16).astype(F32)
    return hi, mid, lo


def _fox_prompt_kernel(q_ref, k_ref, v_ref, cq_ref, ck_ref, o_ref, kaug_ref, vt_ref, qaug0_ref, qaug1_ref,
                       acc0_ref, acc1_ref, s0_ref, s1_ref, *, tq, tk):
    assert tq == 2 * tk
    qaug_refs, acc_refs = (qaug0_ref, qaug1_ref), (acc0_ref, acc1_ref)
    h = pl.program_id(1)
    seq = kaug_ref.shape[0]
    nk = seq // tk

    lane8 = lax.broadcasted_iota(jnp.int32, (tk, N_FOX_HEADS), 1)
    lane = lax.broadcasted_iota(jnp.int32, (tk, HEAD_DIM), 1)
    ones_row = (lax.broadcasted_iota(jnp.int32, (V_PAD_ROWS, tk), 0) == 0).astype(BF16)

    def prep(c, carry):
        rows = pl.ds(pl.multiple_of(c * tk, tk), tk)
        kaug_ref[rows, 0:HEAD_DIM] = k_ref[rows, :]
        col = jnp.sum(jnp.where(lane8 == h, ck_ref[0, rows, :], 0.0), axis=1, keepdims=True) * LOG2_E
        hi, mid, lo = _split3(col)
        aug = jnp.where(lane == 0, hi, jnp.where(lane == 1, mid, jnp.where(lane == 2, lo,
                        jnp.where(lane < 6, 1.0, 0.0))))
        kaug_ref[rows, HEAD_DIM:2 * HEAD_DIM] = aug.astype(BF16)
        vt_ref[c, 0:HEAD_DIM, :] = v_ref[rows, :].astype(F32).T.astype(BF16)
        vt_ref[c, HEAD_DIM:, :] = ones_row
        return carry

    lax.fori_loop(0, nk, prep, 0)

    sub = lax.broadcasted_iota(jnp.int32, (HEAD_DIM, tq), 0)

    def scores(qaug_ref, ki, dst_ref, lane0=0):
        dst_ref[:, lane0:] = jnp.dot(kaug_ref[ki * tk:(ki + 1) * tk, :], qaug_ref[:, lane0:],
                                     preferred_element_type=F32)

    def softmax_pv(acc_ref, ki, m_all, s_ref, lane0=0, mask_offset=None):
        m_prev = m_all[:, lane0:]
        s = s_ref[:, lane0:]
        if mask_offset is not None:
            r = lax.broadcasted_iota(jnp.int32, s.shape, 0) + mask_offset
            c = lax.broadcasted_iota(jnp.int32, s.shape, 1) + lane0
            s = jnp.where(r <= c, s, -jnp.inf)
        m_new = jnp.maximum(m_prev, jnp.max(s, axis=0, keepdims=True))
        alpha = jnp.exp2(m_prev - m_new)
        p = jnp.exp2(s - m_new).astype(BF16)
        pv = jnp.dot(vt_ref[ki], p, preferred_element_type=F32)
        acc_ref[:, lane0:] = alpha * acc_ref[:, lane0:] + pv
        return jnp.concatenate([m_all[:, :lane0], m_new], axis=1) if lane0 else m_new

    s_refs = (s0_ref, s1_ref)
    for qi in range(seq // tq):
        qaug_ref, acc_ref = qaug_refs[qi % 2], acc_refs[qi % 2]
        qrows = slice(qi * tq, (qi + 1) * tq)
        qaug_ref[0:HEAD_DIM, :] = q_ref[qrows, :].astype(F32).T.astype(BF16)
        hi, mid, lo = _split3(cq_ref[0, pl.ds(h, 1), qrows] * LOG2_E)
        qaug_ref[HEAD_DIM:2 * HEAD_DIM, :] = jnp.where(
            sub < 3, -1.0, jnp.where(sub == 3, hi, jnp.where(sub == 4, mid, jnp.where(sub == 5, lo, 0.0)))
        ).astype(BF16)
        acc_ref[...] = jnp.zeros(acc_ref.shape, F32)
        n_full = 2 * qi
        m = jnp.full((1, tq), -jnp.inf, F32)
        scores(qaug_ref, 0, s_refs[0])
        for ki in range(n_full):
            scores(qaug_ref, ki + 1, s_refs[(ki + 1) % 2])
            m = softmax_pv(acc_ref, ki, m, s_refs[ki % 2])
        scores(qaug_ref, n_full + 1, s_refs[1], lane0=tk)
        m = softmax_pv(acc_ref, n_full, m, s_refs[0], mask_offset=0)
        softmax_pv(acc_ref, n_full + 1, m, s_refs[1], lane0=tk, mask_offset=tk)
        o_ref[qrows, :] = (acc_ref[0:HEAD_DIM, :] / acc_ref[HEAD_DIM:HEAD_DIM + 1, :]).T


def _fox_prompt(qb, kb, vb, cq, ck, *, n_batch, seq, tq, tk):
    m = qb.shape[0]
    head = lambda n, h: (n, h)
    return pl.pallas_call(
        functools.partial(_fox_prompt_kernel, tq=tq, tk=tk),
        grid=(n_batch, N_FOX_HEADS),
        in_specs=[
            pl.BlockSpec((seq, HEAD_DIM), head),
            pl.BlockSpec((seq, HEAD_DIM), head),
            pl.BlockSpec((seq, HEAD_DIM), head),
            pl.BlockSpec((1, N_FOX_HEADS, seq), lambda n, h: (n, 0, 0)),
            pl.BlockSpec((1, seq, N_FOX_HEADS), lambda n, h: (n, 0, 0)),
        ],
        out_specs=pl.BlockSpec((seq, HEAD_DIM), head),
        out_shape=jax.ShapeDtypeStruct((m, N_FOX_HEADS * HEAD_DIM), F32),
        scratch_shapes=[
            pltpu.VMEM((seq, 2 * HEAD_DIM), BF16),
            pltpu.VMEM((seq // tk, HEAD_DIM + V_PAD_ROWS, tk), BF16),
            pltpu.VMEM((2 * HEAD_DIM, tq), BF16),
            pltpu.VMEM((2 * HEAD_DIM, tq), BF16),
            pltpu.VMEM((HEAD_DIM + V_PAD_ROWS, tq), F32),
            pltpu.VMEM((HEAD_DIM + V_PAD_ROWS, tq), F32),
            pltpu.VMEM((tk, tq), F32),
            pltpu.VMEM((tk, tq), F32),
        ],
        compiler_params=_params(("arbitrary", "arbitrary")),
        name="fox_prompt",
    )(qb, kb, vb, cq, ck)


def _fox_sample_kernel(q_ref, kn_ref, vn_ref, kc_ref, vc_ref, cq_ref, ckp_ref, ckn_ref, o_ref, *, past):
    scale = HEAD_DIM ** -0.5
    t = q_ref.shape[0]
    half = N_FOX_HEADS // 2
    r = lax.broadcasted_iota(jnp.int32, (2 * t, 2 * past), 0)
    c = lax.broadcasted_iota(jnp.int32, (2 * t, 2 * past), 1)
    own_past = (r >= t) == ((c & 1) == 1)
    rn = lax.broadcasted_iota(jnp.int32, (2 * t, 2 * t), 0)
    cn = lax.broadcasted_iota(jnp.int32, (2 * t, 2 * t), 1)
    own_new = ((rn >= t) == (cn >= t)) & ((cn & (t - 1)) <= (rn & (t - 1)))
    for j in range(half):
        sl0 = slice(j * HEAD_DIM, (j + 1) * HEAD_DIM)
        sl1 = slice((j + half) * HEAD_DIM, (j + half + 1) * HEAD_DIM)
        both = lambda ref: jnp.concatenate([ref[:, sl0], ref[:, sl1]], axis=0)
        q = both(q_ref)
        kp = kc_ref[0, pl.ds(j, 2 * past, stride=half), :].astype(BF16)
        vp = vc_ref[0, pl.ds(j, 2 * past, stride=half), :].astype(BF16)
        cq = jnp.concatenate([cq_ref[0, :, j:j + 1], cq_ref[0, :, j + half:j + half + 1]], axis=0)
        s1 = lax.dot_general(q, kp, _NT, preferred_element_type=F32) * scale
        s1 = jnp.where(own_past, s1 + (cq - ckp_ref[0, j:j + 1, :]), -jnp.inf)
        s2 = lax.dot_general(q, both(kn_ref), _NT, preferred_element_type=F32) * scale
        s2 = jnp.where(own_new, s2 + (cq - ckn_ref[0, j:j + 1, :]), -jnp.inf)
        m = jnp.maximum(jnp.max(s1, axis=1, keepdims=True), jnp.max(s2, axis=1, keepdims=True))
        p1 = jnp.exp(s1 - m)
        p2 = jnp.exp(s2 - m)
        l = jnp.sum(p1, axis=1, keepdims=True) + jnp.sum(p2, axis=1, keepdims=True)
        o = (jnp.dot(p1.astype(BF16), vp, preferred_element_type=F32)
             + jnp.dot(p2.astype(BF16), both(vn_ref), preferred_element_type=F32)) / l
        o_ref[:, sl0] = o[:t]
        o_ref[:, sl1] = o[t:]


def _fox_sample(qb, knb, vnb, cache_k, cache_v, cq, ck_past, ck_new):
    n_batch, rows, _ = cache_k.shape
    past = rows // N_FOX_HEADS
    t = qb.shape[0] // n_batch
    width = qb.shape[1]
    tok = lambda b: (b, 0)
    lead = lambda b: (b, 0, 0)
    return pl.pallas_call(
        functools.partial(_fox_sample_kernel, past=past),
        grid=(n_batch,),
        in_specs=[
            pl.BlockSpec((t, width), tok),
            pl.BlockSpec((t, width), tok),
            pl.BlockSpec((t, width), tok),
            pl.BlockSpec((1, rows, HEAD_DIM), lead),
            pl.BlockSpec((1, rows, HEAD_DIM), lead),
            pl.BlockSpec((1, t, N_FOX_HEADS), lead),
            pl.BlockSpec((1, N_FOX_HEADS // 2, 2 * past), lead),
            pl.BlockSpec((1, N_FOX_HEADS // 2, 2 * t), lead),
        ],
        out_specs=pl.BlockSpec((t, width), tok),
        out_shape=jax.ShapeDtypeStruct(qb.shape, F32),
        compiler_params=_params(("arbitrary",)),
        name="fox_sample",
    )(qb, knb, vnb, cache_k, cache_v, cq, ck_past, ck_new)


def _mix_kernel(*refs, seg, nseg, tiles_per_seq, has_prev, n_cast):
    if has_prev:
        (x_ref, gb_ref, gc_ref, hin_ref, pgc_ref, phin_ref, *refs) = refs
    else:
        (x_ref, gb_ref, gc_ref, hin_ref, *refs) = refs
    (buf_ref, ya_ref, cw_ref, gco_ref, gao_ref, wout_ref, gx_ref, wxq_ref, mk_ref, mv_ref, wxo_ref, gm_ref,
     *refs) = refs
    cast_in, (x2_ref, h_ref, cnew_ref, *cast_out, o_scr) = refs[:n_cast], refs[n_cast:]
    for src_ref, dst_ref in zip(cast_in, cast_out):
        dst_ref[...] = src_ref[...].astype(BF16)
    i = pl.program_id(0)
    u = gc_ref[...] * hin_ref[...]
    gb = gb_ref[...]
    w0 = cw_ref[0:1, :]
    w1 = cw_ref[1:2, :]
    w2 = cw_ref[2:3, :]
    rid = lax.broadcasted_iota(jnp.int32, (seg, u.shape[1]), 0)
    pieces = []
    for s in range(nseg):
        u_s = u[s * seg:(s + 1) * seg]
        b0 = buf_ref[s, 0:1, :]
        b1 = buf_ref[s, 1:2, :]
        if has_prev:
            pu = pgc_ref[...] * phin_ref[...]
            first = (i % tiles_per_seq) == 0
            b0 = jnp.where(first, b0, pu[6:7])
            b1 = jnp.where(first, b1, pu[7:8])
        r1 = jnp.where(rid == 0, b1, pltpu.roll(u_s, 1, 0))
        r2 = jnp.where(rid == 0, b0, jnp.where(rid == 1, b1, pltpu.roll(u_s, 2, 0)))
        y = w0 * r2 + w1 * r1 + w2 * u_s
        pieces.append(gb[s * seg:(s + 1) * seg] * y)
        cnew_ref[s] = u_s[seg - 2:seg]
    yc = pieces[0] if nseg == 1 else jnp.concatenate(pieces, axis=0)
    cat = jnp.concatenate([_rms(yc, gco_ref[...]).astype(BF16),
                           _rms(ya_ref[...], gao_ref[...]).astype(BF16)], axis=-1)
    x1 = x_ref[...] + jnp.dot(cat, wout_ref[...], preferred_element_type=F32)
    xn = _rms(x1, gx_ref[...]).astype(BF16)
    qx = jnp.dot(xn, wxq_ref[...], preferred_element_type=F32).astype(BF16)

    scale = HEAD_DIM ** -0.5
    n_mem = mk_ref.shape[1] // N_X_HEADS
    for s in range(nseg):
        rows = slice(s * seg, (s + 1) * seg)
        for hh in range(N_X_HEADS):
            sl = slice(hh * HEAD_DIM, (hh + 1) * HEAD_DIM)
            kh = _load_head_rows(mk_ref, s, hh, n_mem, N_X_HEADS).astype(BF16)
            vh = _load_head_rows(mv_ref, s, hh, n_mem, N_X_HEADS).astype(BF16)
            sc = lax.dot_general(qx[rows, sl], kh, _NT, preferred_element_type=F32) * scale
            m = jnp.max(sc, axis=1, keepdims=True)
            p = jnp.exp(sc - m)
            l = jnp.sum(p, axis=1, keepdims=True)
            o = jnp.dot(p.astype(BF16), vh, preferred_element_type=F32) / l
            o_scr[rows, sl] = o.astype(BF16)
    x2 = x1 + jnp.dot(o_scr[...], wxo_ref[...], preferred_element_type=F32)
    x2_ref[...] = x2
    h_ref[...] = _rms(x2, gm_ref[...]).astype(BF16)


def _mix(x, zc, conv_buf, ya, conv_w, g_conv_out, g_attn_out, w_out, g_xattn, w_xq, mk, mv, w_xo, g_mlp, *, tm, seq,
         cast=()):
    m, d = x.shape
    dc = zc.shape[1] // 3
    dx = w_xq.shape[1]
    mem_rows = mk.shape[1]
    has_prev = seq > tm
    seg = tm if has_prev else seq
    nseg = tm // seg
    tiles_per_seq = max(seq // tm, 1)
    row = lambda i: (i, 0)
    const = lambda i: (0, 0)
    resident = functools.partial(pl.BlockSpec, index_map=const, pipeline_mode=pl.Buffered(1))
    in_specs = [
        pl.BlockSpec((tm, d), row),
        pl.BlockSpec((tm, dc), lambda i: (i, 0)),
        pl.BlockSpec((tm, dc), lambda i: (i, 1)),
        pl.BlockSpec((tm, dc), lambda i: (i, 2)),
    ]
    args = [x, zc, zc, zc]
    if has_prev:
        prev = lambda col: (lambda i: (jnp.maximum(i * (tm // 8) - 1, 0), col))
        in_specs += [pl.BlockSpec((8, dc), prev(1)), pl.BlockSpec((8, dc), prev(2))]
        args += [zc, zc]
        buf_map = lambda i: (i // tiles_per_seq, 0, 0)
    else:
        buf_map = lambda i: (i, 0, 0)
    in_specs += [
        pl.BlockSpec((nseg, CONV_WIDTH - 1, dc), buf_map),
        pl.BlockSpec((tm, ya.shape[1]), row),
        pl.BlockSpec((CONV_WIDTH, dc), const),
        pl.BlockSpec((1, dc), const),
        pl.BlockSpec((1, ya.shape[1]), const),
        resident(w_out.shape),
        pl.BlockSpec((1, d), const),
        resident(w_xq.shape),
        pl.BlockSpec((nseg, mem_rows, HEAD_DIM), buf_map),
        pl.BlockSpec((nseg, mem_rows, HEAD_DIM), buf_map),
        resident(w_xo.shape),
        pl.BlockSpec((1, d), const),
    ]
    args += [conv_buf, ya, conv_w, g_conv_out, g_attn_out, w_out, g_xattn, w_xq, mk, mv, w_xo, g_mlp]
    steps = m // tm
    cast_specs = []
    for w, axis in cast:
        block = tuple(n // steps if a == axis else n for a, n in enumerate(w.shape))
        cast_specs.append(pl.BlockSpec(block, (lambda i: (i, 0)) if axis == 0 else (lambda i: (0, i))))
        args.append(w)
    return pl.pallas_call(
        functools.partial(_mix_kernel, seg=seg, nseg=nseg, tiles_per_seq=tiles_per_seq, has_prev=has_prev,
                          n_cast=len(cast)),
        grid=(steps,),
        in_specs=in_specs + cast_specs,
        out_specs=[
            pl.BlockSpec((tm, d), row),
            pl.BlockSpec((tm, d), row),
            pl.BlockSpec((nseg, CONV_WIDTH - 1, dc), buf_map),
        ] + cast_specs,
        out_shape=[
            jax.ShapeDtypeStruct((m, d), F32),
            jax.ShapeDtypeStruct((m, d), BF16),
            jax.ShapeDtypeStruct(conv_buf.shape, F32),
        ] + [jax.ShapeDtypeStruct(w.shape, BF16) for w, _ in cast],
        scratch_shapes=[pltpu.VMEM((tm, dx), BF16)],
        compiler_params=_params(("arbitrary",)),
        name="mix",
    )(*args)


def _memkv_kernel(x_ref, g_ref, w_ref, o_ref):
    xn = _rms(x_ref[...], g_ref[...]).astype(BF16)
    _store_head_rows(o_ref.at[0], jnp.dot(xn, w_ref[...], preferred_element_type=F32), N_X_HEADS)


def _memkv(mem, g, w):
    m, d = mem.shape
    half = w.shape[1] // 2
    return pl.pallas_call(
        _memkv_kernel,
        grid=(2,),
        in_specs=[
            pl.BlockSpec((m, d), lambda j: (0, 0)),
            pl.BlockSpec((1, d), lambda j: (0, 0)),
            pl.BlockSpec((d, half), lambda j: (0, j)),
        ],
        out_specs=pl.BlockSpec((1, m * N_X_HEADS, HEAD_DIM), lambda j: (j, 0, 0)),
        out_shape=jax.ShapeDtypeStruct((2, m * N_X_HEADS, HEAD_DIM), F32),
        compiler_params=_params(("arbitrary",)),
        name="memkv",
    )(mem, g, w)


def _mlp_kernel(h_ref, x2_hbm, wu_ref, wd_ref, g_ref, y_ref, sem):
    i = pl.program_id(0)
    j = pl.program_id(1)
    tm = y_ref.shape[0]
    x2_copy = pltpu.make_async_copy(x2_hbm.at[pl.ds(i * tm, tm), :], y_ref, sem)

    def hidden():
        a = jnp.dot(h_ref[...], wu_ref[...], preferred_element_type=F32)
        return jnp.square(jnp.maximum(a, 0.0)).astype(BF16)

    @pl.when(j == 0)
    def _():
        x2_copy.start()
        a = hidden()
        x2_copy.wait()
        y_ref[...] += jnp.dot(a, wd_ref[...], preferred_element_type=F32)

    @pl.when(j > 0)
    def _():
        y_ref[...] += jnp.dot(hidden(), wd_ref[...], preferred_element_type=F32)

    @pl.when(j == pl.num_programs(1) - 1)
    def _():
        y_ref[...] = _rms(y_ref[...], g_ref[...])


def _mlp(h, x2, w_up, w_down, g_final, *, tm, tf):
    m, d = h.shape
    ff = w_up.shape[1]
    row = lambda i, j: (i, 0)
    return pl.pallas_call(
        _mlp_kernel,
        grid=(m // tm, ff // tf),
        in_specs=[
            pl.BlockSpec((tm, d), row),
            pl.BlockSpec(memory_space=pl.ANY),
            pl.BlockSpec((d, tf), lambda i, j: (0, j)),
            pl.BlockSpec((tf, d), lambda i, j: (j, 0)),
            pl.BlockSpec((1, d), lambda i, j: (0, 0)),
        ],
        out_specs=pl.BlockSpec((tm, d), row),
        out_shape=jax.ShapeDtypeStruct((m, d), F32),
        scratch_shapes=[pltpu.SemaphoreType.DMA(())],
        compiler_params=pltpu.CompilerParams(dimension_semantics=("arbitrary", "arbitrary"),
                                             vmem_limit_bytes=MLP_VMEM_LIMIT),
        name="mlp",
    )(h, x2, w_up, w_down, g_final)


def kernel(x_prompt, x_sample, cache_k, cache_v, cache_logf, cache_conv, cache_mem_k, cache_mem_v, mem_prompt, g_mix, w_in, b_f, conv_w, g_conv_out, g_attn_out, w_out, g_xattn, g_mem, w_xq, w_xkv, w_xo, g_mlp, w_up, w_down, g_final):
    depth = w_in.shape[0]
    assert depth == 1, "single-layer trunk"
    nb, seq, d = x_prompt.shape
    db, dseq, _ = x_sample.shape
    past = cache_k.shape[2]
    heads, hd = cache_k.shape[3], cache_k.shape[4]
    assert (heads, hd) == (N_FOX_HEADS, HEAD_DIM)
    d_attn = heads * hd
    d_conv = cache_conv.shape[-1]
    n_mem, xh, xhd = cache_mem_k.shape[2:]
    assert (xh, xhd) == (N_X_HEADS, HEAD_DIM)
    d_x = xh * xhd
    n_main = 3 * d_conv + 3 * d_attn
    assert d_conv == d_attn and w_in.shape[2] == n_main + heads

    w_in_t = jnp.swapaxes(w_in[0], 0, 1)
    bf = b_f[0].reshape(1, heads)
    w_out_b = w_out[0].astype(BF16)
    w_xq_b = w_xq[0].astype(BF16)
    w_xkv_b = w_xkv[0].astype(BF16)
    w_xo_b = w_xo[0].astype(BF16)
    row = lambda g: g.reshape(1, -1)

    def trunk(x, conv_buf, seq_len, attention, mk, mv, mlp_w, q_scale=None):
        xn, lf = _xnorm(x, row(g_mix[0]), w_in_t, bf, tm=512, gate_row=n_main)
        proj = functools.partial(_proj, xn, w_in_t, tn=d_attn, tm=1024)
        zc, = proj(j0=0, nj=3, mode="f32")
        qb, = proj(j0=3, nj=1, mode="bf16", out_scale=q_scale)
        k, kb = proj(j0=4, nj=1, mode="heads")
        v, vb = proj(j0=5, nj=1, mode="heads")
        ya = attention(qb, kb, vb, lf)
        cast = ((w_up[0], 1), (w_down[0], 0)) if mlp_w is None else ()
        x2, h, conv_new, *cast_w = _mix(x, zc, conv_buf, ya, conv_w[0], row(g_conv_out[0]), row(g_attn_out[0]),
                                        w_out_b, row(g_xattn[0]), w_xq_b, mk, mv, w_xo_b, row(g_mlp[0]),
                                        tm=256, seq=seq_len, cast=cast)
        mlp_w = mlp_w or tuple(cast_w)
        y = _mlp(h, x2, *mlp_w, row(g_final), tm=1024, tf=1024)
        return y, k, v, lf, conv_new, mlp_w


    def prompt_attention(qb, kb, vb, lf):
        lft = lf.reshape(nb, seq, heads).transpose(0, 2, 1).reshape(nb * heads, seq)
        ct = _cumsum_lanes(lft).reshape(nb, heads, seq)
        return _fox_prompt(qb, kb, vb, ct, ct.transpose(0, 2, 1), n_batch=nb, seq=seq, tq=1024, tk=512)

    kv = _memkv(mem_prompt.reshape(nb * n_mem, d), row(g_mem[0]), w_xkv_b)
    mk_p = kv[0].reshape(nb, n_mem * xh, xhd)
    mv_p = kv[1].reshape(nb, n_mem * xh, xhd)
    conv0 = jnp.zeros((nb, CONV_WIDTH - 1, d_conv), F32)
    y_p, k_p, v_p, lf_p, conv_p, mlp_w = trunk(x_prompt.reshape(nb * seq, d), conv0, seq, prompt_attention,
                                               mk_p, mv_p, None, q_scale=hd ** -0.5 * LOG2_E)

    pad = (-(past + dseq)) % LANES

    def sample_attention(qb, kb, vb, lf):
        lft_past = cache_logf[0].transpose(0, 2, 1).reshape(db * heads, past)
        lft_new = lf.reshape(db, dseq, heads).transpose(0, 2, 1).reshape(db * heads, dseq)
        ct = _cumsum_lanes(jnp.concatenate(
            [lft_past, lft_new, jnp.zeros((db * heads, pad), F32)], axis=1))
        ck_past = ct[:, :past].reshape(db, heads, past)
        ck_new = ct[:, past:past + dseq].reshape(db, heads, dseq)
        cq = ck_new.transpose(0, 2, 1)
        hp = heads // 2
        ck_past_pairs = jnp.stack([ck_past[:, :hp], ck_past[:, hp:]], axis=-1).reshape(db, hp, 2 * past)
        ck_new_pairs = jnp.concatenate([ck_new[:, :hp], ck_new[:, hp:]], axis=-1)
        return _fox_sample(qb, kb, vb, cache_k.reshape(db, past * heads, hd),
                           cache_v.reshape(db, past * heads, hd), cq, ck_past_pairs, ck_new_pairs)

    y_s, k_s, v_s, lf_s, conv_s, _ = trunk(x_sample.reshape(db * dseq, d), cache_conv[0], dseq, sample_attention,
                                           cache_mem_k.reshape(db, n_mem * xh, xhd),
                                           cache_mem_v.reshape(db, n_mem * xh, xhd), mlp_w)

    return (y_p.reshape(nb, seq, d),
            y_s.reshape(db, dseq, d),
            k_p.reshape(1, nb, seq, heads, hd),
            v_p.reshape(1, nb, seq, heads, hd),
            lf_p.reshape(1, nb, seq, heads),
            conv_p[None],
            mk_p.reshape(1, nb, n_mem, xh, xhd),
            mv_p.reshape(1, nb, n_mem, xh, xhd),
            k_s.reshape(1, db, dseq, heads, hd),
            v_s.reshape(1, db, dseq, heads, hd),
            lf_s.reshape(1, db, dseq, heads),
            conv_s[None])
```

```python
import functools

import jax
import jax.numpy as jnp
from jax import lax
from jax.experimental import pallas as pl
from jax.experimental.pallas import tpu as pltpu

F32 = jnp.float32
BF16 = jnp.bfloat16

RMS_EPS = 1e-6
HEAD_DIM = 128
N_FOX_HEADS = 8
N_X_HEADS = 4
CONV_WIDTH = 3
LANES = 128
V_PAD_ROWS = 16
LOG2_E = 1.4426950408889634
VMEM_LIMIT = 52 * 1024 * 1024
MLP_VMEM_LIMIT = 56 * 1024 * 1024

_NT = (((1,), (1,)), ((), ()))


def _params(semantics):
    return pltpu.CompilerParams(dimension_semantics=semantics, vmem_limit_bytes=VMEM_LIMIT)


def _rms(x, g):
    ms = jnp.mean(x * x, axis=-1, keepdims=True)
    return x * lax.rsqrt(ms + RMS_EPS) * g


def _log_sigmoid(x):
    return jnp.minimum(x, 0.0) - jnp.log1p(jnp.exp(-jnp.abs(x)))


def _store_head_rows(ref, val, heads):
    n = val.shape[0]
    for h in range(heads):
        ref[pl.ds(h, n, stride=heads), :] = val[:, h * HEAD_DIM:(h + 1) * HEAD_DIM]


def _load_head_rows(ref, lead, h, n, heads):
    return ref[lead, pl.ds(h, n, stride=heads), :]


def _xnorm_kernel(x_ref, g_ref, wf_ref, bf_ref, xn_ref, lf_ref):
    xn = _rms(x_ref[...], g_ref[...]).astype(BF16)
    xn_ref[...] = xn
    wf = wf_ref[...].astype(BF16)
    wf = jnp.concatenate([wf, jnp.zeros((LANES - wf.shape[0], wf.shape[1]), BF16)], axis=0)
    fz = lax.dot_general(xn, wf, _NT, preferred_element_type=F32)
    lf_ref[...] = _log_sigmoid(fz[:, :N_FOX_HEADS] + bf_ref[...])


def _xnorm(x, g, w_t, b_f, *, tm, gate_row):
    m, d = x.shape
    row = lambda i: (i, 0)
    return pl.pallas_call(
        _xnorm_kernel,
        grid=(m // tm,),
        in_specs=[
            pl.BlockSpec((tm, d), row),
            pl.BlockSpec((1, d), lambda i: (0, 0)),
            pl.BlockSpec((N_FOX_HEADS, d), lambda i: (gate_row // N_FOX_HEADS, 0)),
            pl.BlockSpec((1, N_FOX_HEADS), lambda i: (0, 0)),
        ],
        out_specs=[pl.BlockSpec((tm, d), row), pl.BlockSpec((tm, N_FOX_HEADS), row)],
        out_shape=[jax.ShapeDtypeStruct((m, d), BF16), jax.ShapeDtypeStruct((m, N_FOX_HEADS), F32)],
        compiler_params=_params(("arbitrary",)),
        name="xnorm",
    )(x, g, w_t, b_f)


def _proj_kernel(xn_ref, wt_ref, *refs, mode, out_scale):
    out_refs, wb_ref = refs[:-1], refs[-1]

    @pl.when(pl.program_id(1) == 0)
    def _():
        wb_ref[...] = wt_ref[...].astype(BF16)

    res = lax.dot_general(xn_ref[...], wb_ref[...], _NT, preferred_element_type=F32)
    if mode == "f32":
        out_refs[0][...] = res
    elif mode == "bf16":
        out_refs[0][...] = (res if out_scale is None else res * out_scale).astype(BF16)
    else:
        _store_head_rows(out_refs[0], res, N_FOX_HEADS)
        out_refs[1][...] = res.astype(BF16)


def _proj(xn, w_t, *, j0, nj, tn, tm, mode, out_scale=None):
    m, d = xn.shape
    if mode == "heads":
        assert nj == 1
        out_specs = [pl.BlockSpec((tm * N_FOX_HEADS, HEAD_DIM), lambda j, i: (i, 0)),
                     pl.BlockSpec((tm, tn), lambda j, i: (i, 0))]
        out_shape = [jax.ShapeDtypeStruct((m * N_FOX_HEADS, HEAD_DIM), F32), jax.ShapeDtypeStruct((m, tn), BF16)]
    else:
        out_specs = [pl.BlockSpec((tm, tn), lambda j, i: (i, j))]
        out_shape = [jax.ShapeDtypeStruct((m, nj * tn), F32 if mode == "f32" else BF16)]
    return pl.pallas_call(
        functools.partial(_proj_kernel, mode=mode, out_scale=out_scale),
        grid=(nj, m // tm),
        in_specs=[
            pl.BlockSpec((tm, d), lambda j, i: (i, 0)),
            pl.BlockSpec((tn, d), lambda j, i: (j0 + j, 0)),
        ],
        out_specs=out_specs,
        out_shape=out_shape,
        scratch_shapes=[pltpu.VMEM((tn, d), BF16)],
        compiler_params=_params(("arbitrary", "arbitrary")),
        name="proj_" + mode,
    )(xn, w_t)


def _cumsum_kernel(a_ref, o_ref):
    rows, length = a_ref.shape
    r = lax.broadcasted_iota(jnp.int32, (LANES, LANES), 0)
    c = lax.broadcasted_iota(jnp.int32, (LANES, LANES), 1)
    upper = (r <= c).astype(BF16)
    carry = jnp.zeros((rows, 1), F32)
    for b in range(length // LANES):
        a = a_ref[:, b * LANES:(b + 1) * LANES]
        hi = a.astype(BF16)
        r1 = a - hi.astype(F32)
        mid = r1.astype(BF16)
        lo = (r1 - mid.astype(F32)).astype(BF16)
        s = (jnp.dot(hi, upper, preferred_element_type=F32)
             + jnp.dot(mid, upper, preferred_element_type=F32)
             + jnp.dot(lo, upper, preferred_element_type=F32)) + carry
        o_ref[:, b * LANES:(b + 1) * LANES] = s
        carry = s[:, LANES - 1:LANES]


def _cumsum_lanes(a):
    return pl.pallas_call(
        _cumsum_kernel,
        out_shape=jax.ShapeDtypeStruct(a.shape, F32),
        compiler_params=pltpu.CompilerParams(vmem_limit_bytes=VMEM_LIMIT),
        name="cumsum",
    )(a)


def _cumsum_pairs_kernel(p_ref, n_ref, op_ref, on_ref, *, t_new):
    rows, length = p_ref.shape
    r = lax.broadcasted_iota(jnp.int32, (LANES, LANES), 0)
    c = lax.broadcasted_iota(jnp.int32, (LANES, LANES), 1)
    upper = ((r <= c) & (((r ^ c) & 1) == 0)).astype(BF16)
    unzip = ((r < 2 * t_new) & (c == (r & 1) * t_new + (r >> 1))).astype(BF16)
    lane_odd = (lax.broadcasted_iota(jnp.int32, (rows, LANES), 1) & 1) == 1

    def times(a, mat):
        hi, mid, lo = _split3(a)
        return (jnp.dot(hi.astype(BF16), mat, preferred_element_type=F32)
                + jnp.dot(mid.astype(BF16), mat, preferred_element_type=F32)
                + jnp.dot(lo.astype(BF16), mat, preferred_element_type=F32))

    carry = jnp.zeros((rows, LANES), F32)
    for b in range(length // LANES):
        s = times(p_ref[:, b * LANES:(b + 1) * LANES], upper) + carry
        op_ref[:, b * LANES:(b + 1) * LANES] = s
        carry = jnp.where(lane_odd, s[:, LANES - 1:LANES], s[:, LANES - 2:LANES - 1])
    on_ref[...] = times(times(n_ref[...], upper) + carry, unzip)


def _cumsum_pairs(p, n, *, t_new):
    return pl.pallas_call(
        functools.partial(_cumsum_pairs_kernel, t_new=t_new),
        out_shape=[jax.ShapeDtypeStruct(p.shape, F32), jax.ShapeDtypeStruct(n.shape, F32)],
        compiler_params=pltpu.CompilerParams(vmem_limit_bytes=VMEM_LIMIT),
        name="cumsum_pairs",
    )(p, n)


def _split3(x):
    hi = x.astype(BF16).astype(F32)
    r = x - hi
    mid = r.astype(BF16).astype(F32)
    lo = (r - mid).astype(BF16).astype(F32)
    return hi, mid, lo


def _fox_prompt_kernel(q_ref, k_ref, v_ref, cq_ref, ck_ref, *refs, tq, tk, n_cast):
    assert tq == 2 * tk
    cast_in, (o_ref, *cast_out), scratch = refs[:n_cast], refs[n_cast:2 * n_cast + 1], refs[2 * n_cast + 1:]
    kaug_ref, vt_ref, qaug0_ref, qaug1_ref, acc0_ref, acc1_ref, s0_ref, s1_ref = scratch
    for src_ref, dst_ref in zip(cast_in, cast_out):
        dst_ref[...] = src_ref[...].astype(BF16)
    qaug_refs, acc_refs = (qaug0_ref, qaug1_ref), (acc0_ref, acc1_ref)
    h = pl.program_id(1)
    seq = kaug_ref.shape[0]
    nk = seq // tk

    lane8 = lax.broadcasted_iota(jnp.int32, (tk, N_FOX_HEADS), 1)
    lane = lax.broadcasted_iota(jnp.int32, (tk, HEAD_DIM), 1)
    ones_row = (lax.broadcasted_iota(jnp.int32, (V_PAD_ROWS, tk), 0) == 0).astype(BF16)

    def prep(c, carry):
        rows = pl.ds(pl.multiple_of(c * tk, tk), tk)
        kaug_ref[rows, 0:HEAD_DIM] = k_ref[rows, :]
        col = jnp.sum(jnp.where(lane8 == h, ck_ref[0, rows, :], 0.0), axis=1, keepdims=True) * LOG2_E
        hi, mid, lo = _split3(col)
        aug = jnp.where(lane == 0, hi, jnp.where(lane == 1, mid, jnp.where(lane == 2, lo,
                        jnp.where(lane < 6, 1.0, 0.0))))
        kaug_ref[rows, HEAD_DIM:2 * HEAD_DIM] = aug.astype(BF16)
        vt_ref[c, 0:HEAD_DIM, :] = v_ref[rows, :].astype(F32).T.astype(BF16)
        vt_ref[c, HEAD_DIM:, :] = ones_row
        return carry

    lax.fori_loop(0, nk, prep, 0)

    sub = lax.broadcasted_iota(jnp.int32, (HEAD_DIM, tq), 0)

    def scores(qaug_ref, ki, dst_ref, lane0=0):
        dst_ref[:, lane0:] = jnp.dot(kaug_ref[ki * tk:(ki + 1) * tk, :], qaug_ref[:, lane0:],
                                     preferred_element_type=F32)

    def softmax_pv(acc_ref, ki, m_all, s_ref, lane0=0, mask_offset=None):
        m_prev = m_all[:, lane0:]
        s = s_ref[:, lane0:]
        if mask_offset is not None:
            r = lax.broadcasted_iota(jnp.int32, s.shape, 0) + mask_offset
            c = lax.broadcasted_iota(jnp.int32, s.shape, 1) + lane0
            s = jnp.where(r <= c, s, -jnp.inf)
        m_new = jnp.maximum(m_prev, jnp.max(s, axis=0, keepdims=True))
        alpha = jnp.exp2(m_prev - m_new)
        p = jnp.exp2(s - m_new).astype(BF16)
        pv = jnp.dot(vt_ref[ki], p, preferred_element_type=F32)
        acc_ref[:, lane0:] = alpha * acc_ref[:, lane0:] + pv
        return jnp.concatenate([m_all[:, :lane0], m_new], axis=1) if lane0 else m_new

    s_refs = (s0_ref, s1_ref)
    for qi in range(seq // tq):
        qaug_ref, acc_ref = qaug_refs[qi % 2], acc_refs[qi % 2]
        qrows = slice(qi * tq, (qi + 1) * tq)
        qaug_ref[0:HEAD_DIM, :] = q_ref[qrows, :].astype(F32).T.astype(BF16)
        hi, mid, lo = _split3(cq_ref[0, pl.ds(h, 1), qrows] * LOG2_E)
        qaug_ref[HEAD_DIM:2 * HEAD_DIM, :] = jnp.where(
            sub < 3, -1.0, jnp.where(sub == 3, hi, jnp.where(sub == 4, mid, jnp.where(sub == 5, lo, 0.0)))
        ).astype(BF16)
        acc_ref[...] = jnp.zeros(acc_ref.shape, F32)
        n_full = 2 * qi
        m = jnp.full((1, tq), -jnp.inf, F32)
        scores(qaug_ref, 0, s_refs[0])
        for ki in range(n_full):
            scores(qaug_ref, ki + 1, s_refs[(ki + 1) % 2])
            m = softmax_pv(acc_ref, ki, m, s_refs[ki % 2])
        scores(qaug_ref, n_full + 1, s_refs[1], lane0=tk)
        m = softmax_pv(acc_ref, n_full, m, s_refs[0], mask_offset=0)
        softmax_pv(acc_ref, n_full + 1, m, s_refs[1], lane0=tk, mask_offset=tk)
        o_ref[qrows, :] = (acc_ref[0:HEAD_DIM, :] / acc_ref[HEAD_DIM:HEAD_DIM + 1, :]).T


def _cast_specs(cast, steps, step_of):
    specs = []
    for w, axis in cast:
        block = tuple(n // steps if a == axis else n for a, n in enumerate(w.shape))
        specs.append(pl.BlockSpec(block, (lambda *g: (step_of(*g), 0)) if axis == 0
                                  else (lambda *g: (0, step_of(*g)))))
    return specs


def _fox_prompt(qb, kb, vb, cq, ck, *, n_batch, seq, tq, tk, cast=()):
    m = qb.shape[0]
    head = lambda n, h: (n, h)
    cast_specs = _cast_specs(cast, n_batch * N_FOX_HEADS, lambda n, h: n * N_FOX_HEADS + h)
    return pl.pallas_call(
        functools.partial(_fox_prompt_kernel, tq=tq, tk=tk, n_cast=len(cast)),
        grid=(n_batch, N_FOX_HEADS),
        in_specs=[
            pl.BlockSpec((seq, HEAD_DIM), head),
            pl.BlockSpec((seq, HEAD_DIM), head),
            pl.BlockSpec((seq, HEAD_DIM), head),
            pl.BlockSpec((1, N_FOX_HEADS, seq), lambda n, h: (n, 0, 0)),
            pl.BlockSpec((1, seq, N_FOX_HEADS), lambda n, h: (n, 0, 0)),
        ] + cast_specs,
        out_specs=[pl.BlockSpec((seq, HEAD_DIM), head)] + cast_specs,
        out_shape=[jax.ShapeDtypeStruct((m, N_FOX_HEADS * HEAD_DIM), F32)]
        + [jax.ShapeDtypeStruct(w.shape, BF16) for w, _ in cast],
        scratch_shapes=[
            pltpu.VMEM((seq, 2 * HEAD_DIM), BF16),
            pltpu.VMEM((seq // tk, HEAD_DIM + V_PAD_ROWS, tk), BF16),
            pltpu.VMEM((2 * HEAD_DIM, tq), BF16),
            pltpu.VMEM((2 * HEAD_DIM, tq), BF16),
            pltpu.VMEM((HEAD_DIM + V_PAD_ROWS, tq), F32),
            pltpu.VMEM((HEAD_DIM + V_PAD_ROWS, tq), F32),
            pltpu.VMEM((tk, tq), F32),
            pltpu.VMEM((tk, tq), F32),
        ],
        compiler_params=_params(("arbitrary", "arbitrary")),
        name="fox_prompt",
    )(qb, kb, vb, cq, ck, *[w for w, _ in cast])


def _fox_sample_kernel(q_ref, kn_ref, vn_ref, kc_ref, vc_ref, cq_ref, ckp_ref, ckn_ref, o_ref, *, past):
    scale = HEAD_DIM ** -0.5
    t = q_ref.shape[0]
    half = N_FOX_HEADS // 2
    r = lax.broadcasted_iota(jnp.int32, (2 * t, 2 * past), 0)
    c = lax.broadcasted_iota(jnp.int32, (2 * t, 2 * past), 1)
    own_past = (r >= t) == ((c & 1) == 1)
    rn = lax.broadcasted_iota(jnp.int32, (2 * t, 2 * t), 0)
    cn = lax.broadcasted_iota(jnp.int32, (2 * t, 2 * t), 1)
    own_new = ((rn >= t) == (cn >= t)) & ((cn & (t - 1)) <= (rn & (t - 1)))
    for j in range(half):
        sl0 = slice(j * HEAD_DIM, (j + 1) * HEAD_DIM)
        sl1 = slice((j + half) * HEAD_DIM, (j + half + 1) * HEAD_DIM)
        both = lambda ref: jnp.concatenate([ref[:, sl0], ref[:, sl1]], axis=0)
        q = both(q_ref)
        kp = kc_ref[0, pl.ds(j, 2 * past, stride=half), :].astype(BF16)
        vp = vc_ref[0, pl.ds(j, 2 * past, stride=half), :].astype(BF16)
        cq = jnp.concatenate([cq_ref[0, :, j:j + 1], cq_ref[0, :, j + half:j + half + 1]], axis=0)
        s1 = lax.dot_general(q, kp, _NT, preferred_element_type=F32) * scale
        s1 = jnp.where(own_past, s1 + (cq - ckp_ref[0, j:j + 1, :]), -jnp.inf)
        s2 = lax.dot_general(q, both(kn_ref), _NT, preferred_element_type=F32) * scale
        s2 = jnp.where(own_new, s2 + (cq - ckn_ref[0, j:j + 1, :]), -jnp.inf)
        m = jnp.maximum(jnp.max(s1, axis=1, keepdims=True), jnp.max(s2, axis=1, keepdims=True))
        p1 = jnp.exp(s1 - m)
        p2 = jnp.exp(s2 - m)
        l = jnp.sum(p1, axis=1, keepdims=True) + jnp.sum(p2, axis=1, keepdims=True)
        o = (jnp.dot(p1.astype(BF16), vp, preferred_element_type=F32)
             + jnp.dot(p2.astype(BF16), both(vn_ref), preferred_element_type=F32)) / l
        o_ref[:, sl0] = o[:t]
        o_ref[:, sl1] = o[t:]


def _fox_sample(qb, knb, vnb, cache_k, cache_v, cq, ck_past, ck_new):
    n_batch, rows, _ = cache_k.shape
    past = rows // N_FOX_HEADS
    t = qb.shape[0] // n_batch
    width = qb.shape[1]
    tok = lambda b: (b, 0)
    lead = lambda b: (b, 0, 0)
    return pl.pallas_call(
        functools.partial(_fox_sample_kernel, past=past),
        grid=(n_batch,),
        in_specs=[
            pl.BlockSpec((t, width), tok),
            pl.BlockSpec((t, width), tok),
            pl.BlockSpec((t, width), tok),
            pl.BlockSpec((1, rows, HEAD_DIM), lead),
            pl.BlockSpec((1, rows, HEAD_DIM), lead),
            pl.BlockSpec((1, t, N_FOX_HEADS), lead),
            pl.BlockSpec((1, N_FOX_HEADS // 2, 2 * past), lead),
            pl.BlockSpec((1, N_FOX_HEADS // 2, 2 * t), lead),
        ],
        out_specs=pl.BlockSpec((t, width), tok),
        out_shape=jax.ShapeDtypeStruct(qb.shape, F32),
        compiler_params=_params(("arbitrary",)),
        name="fox_sample",
    )(qb, knb, vnb, cache_k, cache_v, cq, ck_past, ck_new)


def _mix_kernel(*refs, seg, nseg, tiles_per_seq, has_prev, n_cast):
    if has_prev:
        (x_ref, gb_ref, gc_ref, hin_ref, pgc_ref, phin_ref, *refs) = refs
    else:
        (x_ref, gb_ref, gc_ref, hin_ref, *refs) = refs
    (buf_ref, ya_ref, cw_ref, gco_ref, gao_ref, wout_ref, gx_ref, wxq_ref, mk_ref, mv_ref, wxo_ref, gm_ref,
     *refs) = refs
    cast_in, (x2_ref, h_ref, cnew_ref, *cast_out, o_scr) = refs[:n_cast], refs[n_cast:]
    for src_ref, dst_ref in zip(cast_in, cast_out):
        dst_ref[...] = src_ref[...].astype(BF16)
    i = pl.program_id(0)
    u = gc_ref[...] * hin_ref[...]
    gb = gb_ref[...]
    w0 = cw_ref[0:1, :]
    w1 = cw_ref[1:2, :]
    w2 = cw_ref[2:3, :]
    rid = lax.broadcasted_iota(jnp.int32, (seg, u.shape[1]), 0)
    pieces = []
    for s in range(nseg):
        u_s = u[s * seg:(s + 1) * seg]
        b0 = buf_ref[s, 0:1, :]
        b1 = buf_ref[s, 1:2, :]
        if has_prev:
            pu = pgc_ref[...] * phin_ref[...]
            first = (i % tiles_per_seq) == 0
            b0 = jnp.where(first, b0, pu[6:7])
            b1 = jnp.where(first, b1, pu[7:8])
        r1 = jnp.where(rid == 0, b1, pltpu.roll(u_s, 1, 0))
        r2 = jnp.where(rid == 0, b0, jnp.where(rid == 1, b1, pltpu.roll(u_s, 2, 0)))
        y = w0 * r2 + w1 * r1 + w2 * u_s
        pieces.append(gb[s * seg:(s + 1) * seg] * y)
        cnew_ref[s] = u_s[seg - 2:seg]
    yc = pieces[0] if nseg == 1 else jnp.concatenate(pieces, axis=0)
    cat = jnp.concatenate([_rms(yc, gco_ref[...]).astype(BF16),
                           _rms(ya_ref[...], gao_ref[...]).astype(BF16)], axis=-1)
    x1 = x_ref[...] + jnp.dot(cat, wout_ref[...], preferred_element_type=F32)
    xn = _rms(x1, gx_ref[...]).astype(BF16)
    qx = jnp.dot(xn, wxq_ref[...], preferred_element_type=F32).astype(BF16)

    scale = HEAD_DIM ** -0.5
    n_mem = mk_ref.shape[1] // N_X_HEADS
    for s in range(nseg):
        rows = slice(s * seg, (s + 1) * seg)
        for hh in range(N_X_HEADS):
            sl = slice(hh * HEAD_DIM, (hh + 1) * HEAD_DIM)
            kh = _load_head_rows(mk_ref, s, hh, n_mem, N_X_HEADS).astype(BF16)
            vh = _load_head_rows(mv_ref, s, hh, n_mem, N_X_HEADS).astype(BF16)
            sc = lax.dot_general(qx[rows, sl], kh, _NT, preferred_element_type=F32) * scale
            m = jnp.max(sc, axis=1, keepdims=True)
            p = jnp.exp(sc - m)
            l = jnp.sum(p, axis=1, keepdims=True)
            o = jnp.dot(p.astype(BF16), vh, preferred_element_type=F32) / l
            o_scr[rows, sl] = o.astype(BF16)
    x2 = x1 + jnp.dot(o_scr[...], wxo_ref[...], preferred_element_type=F32)
    x2_ref[...] = x2
    h_ref[...] = _rms(x2, gm_ref[...]).astype(BF16)


def _mix(x, zc, conv_buf, ya, conv_w, g_conv_out, g_attn_out, w_out, g_xattn, w_xq, mk, mv, w_xo, g_mlp, *, tm, seq,
         cast=()):
    m, d = x.shape
    dc = zc.shape[1] // 3
    dx = w_xq.shape[1]
    mem_rows = mk.shape[1]
    has_prev = seq > tm
    seg = tm if has_prev else seq
    nseg = tm // seg
    tiles_per_seq = max(seq // tm, 1)
    row = lambda i: (i, 0)
    const = lambda i: (0, 0)
    resident = functools.partial(pl.BlockSpec, index_map=const, pipeline_mode=pl.Buffered(1))
    in_specs = [
        pl.BlockSpec((tm, d), row),
        pl.BlockSpec((tm, dc), lambda i: (i, 0)),
        pl.BlockSpec((tm, dc), lambda i: (i, 1)),
        pl.BlockSpec((tm, dc), lambda i: (i, 2)),
    ]
    args = [x, zc, zc, zc]
    if has_prev:
        prev = lambda col: (lambda i: (jnp.maximum(i * (tm // 8) - 1, 0), col))
        in_specs += [pl.BlockSpec((8, dc), prev(1)), pl.BlockSpec((8, dc), prev(2))]
        args += [zc, zc]
        buf_map = lambda i: (i // tiles_per_seq, 0, 0)
    else:
        buf_map = lambda i: (i, 0, 0)
    in_specs += [
        pl.BlockSpec((nseg, CONV_WIDTH - 1, dc), buf_map),
        pl.BlockSpec((tm, ya.shape[1]), row),
        pl.BlockSpec((CONV_WIDTH, dc), const),
        pl.BlockSpec((1, dc), const),
        pl.BlockSpec((1, ya.shape[1]), const),
        resident(w_out.shape),
        pl.BlockSpec((1, d), const),
        resident(w_xq.shape),
        pl.BlockSpec((nseg, mem_rows, HEAD_DIM), buf_map),
        pl.BlockSpec((nseg, mem_rows, HEAD_DIM), buf_map),
        resident(w_xo.shape),
        pl.BlockSpec((1, d), const),
    ]
    args += [conv_buf, ya, conv_w, g_conv_out, g_attn_out, w_out, g_xattn, w_xq, mk, mv, w_xo, g_mlp]
    steps = m // tm
    cast_specs = _cast_specs(cast, steps, lambda i: i)
    args += [w for w, _ in cast]
    return pl.pallas_call(
        functools.partial(_mix_kernel, seg=seg, nseg=nseg, tiles_per_seq=tiles_per_seq, has_prev=has_prev,
                          n_cast=len(cast)),
        grid=(steps,),
        in_specs=in_specs + cast_specs,
        out_specs=[
            pl.BlockSpec((tm, d), row),
            pl.BlockSpec((tm, d), row),
            pl.BlockSpec((nseg, CONV_WIDTH - 1, dc), buf_map),
        ] + cast_specs,
        out_shape=[
            jax.ShapeDtypeStruct((m, d), F32),
            jax.ShapeDtypeStruct((m, d), BF16),
            jax.ShapeDtypeStruct(conv_buf.shape, F32),
        ] + [jax.ShapeDtypeStruct(w.shape, BF16) for w, _ in cast],
        scratch_shapes=[pltpu.VMEM((tm, dx), BF16)],
        compiler_params=_params(("arbitrary",)),
        name="mix",
    )(*args)


def _memkv_kernel(x_ref, g_ref, w_ref, o_ref):
    xn = _rms(x_ref[...], g_ref[...]).astype(BF16)
    _store_head_rows(o_ref.at[0], jnp.dot(xn, w_ref[...], preferred_element_type=F32), N_X_HEADS)


def _memkv(mem, g, w):
    m, d = mem.shape
    half = w.shape[1] // 2
    return pl.pallas_call(
        _memkv_kernel,
        grid=(2,),
        in_specs=[
            pl.BlockSpec((m, d), lambda j: (0, 0)),
            pl.BlockSpec((1, d), lambda j: (0, 0)),
            pl.BlockSpec((d, half), lambda j: (0, j)),
        ],
        out_specs=pl.BlockSpec((1, m * N_X_HEADS, HEAD_DIM), lambda j: (j, 0, 0)),
        out_shape=jax.ShapeDtypeStruct((2, m * N_X_HEADS, HEAD_DIM), F32),
        compiler_params=_params(("arbitrary",)),
        name="memkv",
    )(mem, g, w)


def _mlp_kernel(h_ref, x2_hbm, wu_ref, wd_ref, g_ref, y_ref, sem):
    i = pl.program_id(0)
    j = pl.program_id(1)
    tm = y_ref.shape[0]
    x2_copy = pltpu.make_async_copy(x2_hbm.at[pl.ds(i * tm, tm), :], y_ref, sem)

    def hidden():
        a = jnp.dot(h_ref[...], wu_ref[...], preferred_element_type=F32)
        return jnp.square(jnp.maximum(a, 0.0)).astype(BF16)

    @pl.when(j == 0)
    def _():
        x2_copy.start()
        a = hidden()
        x2_copy.wait()
        y_ref[...] += jnp.dot(a, wd_ref[...], preferred_element_type=F32)

    @pl.when(j > 0)
    def _():
        y_ref[...] += jnp.dot(hidden(), wd_ref[...], preferred_element_type=F32)

    @pl.when(j == pl.num_programs(1) - 1)
    def _():
        y_ref[...] = _rms(y_ref[...], g_ref[...])


def _mlp(h, x2, w_up, w_down, g_final, *, tm, tf):
    m, d = h.shape
    ff = w_up.shape[1]
    row = lambda i, j: (i, 0)
    return pl.pallas_call(
        _mlp_kernel,
        grid=(m // tm, ff // tf),
        in_specs=[
            pl.BlockSpec((tm, d), row),
            pl.BlockSpec(memory_space=pl.ANY),
            pl.BlockSpec((d, tf), lambda i, j: (0, j)),
            pl.BlockSpec((tf, d), lambda i, j: (j, 0)),
            pl.BlockSpec((1, d), lambda i, j: (0, 0)),
        ],
        out_specs=pl.BlockSpec((tm, d), row),
        out_shape=jax.ShapeDtypeStruct((m, d), F32),
        scratch_shapes=[pltpu.SemaphoreType.DMA(())],
        compiler_params=pltpu.CompilerParams(dimension_semantics=("arbitrary", "arbitrary"),
                                             vmem_limit_bytes=MLP_VMEM_LIMIT),
        name="mlp",
    )(h, x2, w_up, w_down, g_final)


def kernel(x_prompt, x_sample, cache_k, cache_v, cache_logf, cache_conv, cache_mem_k, cache_mem_v, mem_prompt, g_mix, w_in, b_f, conv_w, g_conv_out, g_attn_out, w_out, g_xattn, g_mem, w_xq, w_xkv, w_xo, g_mlp, w_up, w_down, g_final):
    depth = w_in.shape[0]
    assert depth == 1, "single-layer trunk"
    nb, seq, d = x_prompt.shape
    db, dseq, _ = x_sample.shape
    past = cache_k.shape[2]
    heads, hd = cache_k.shape[3], cache_k.shape[4]
    assert (heads, hd) == (N_FOX_HEADS, HEAD_DIM)
    d_attn = heads * hd
    d_conv = cache_conv.shape[-1]
    n_mem, xh, xhd = cache_mem_k.shape[2:]
    assert (xh, xhd) == (N_X_HEADS, HEAD_DIM)
    d_x = xh * xhd
    n_main = 3 * d_conv + 3 * d_attn
    assert d_conv == d_attn and w_in.shape[2] == n_main + heads

    w_in_t = jnp.swapaxes(w_in[0], 0, 1)
    bf = b_f[0].reshape(1, heads)
    row = lambda g: g.reshape(1, -1)

    def project(x, q_scale=None):
        xn, lf = _xnorm(x, row(g_mix[0]), w_in_t, bf, tm=512, gate_row=n_main)
        proj = functools.partial(_proj, xn, w_in_t, tn=d_attn, tm=1024)
        zc, = proj(j0=0, nj=3, mode="f32")
        qb, = proj(j0=3, nj=1, mode="bf16", out_scale=q_scale)
        k, kb = proj(j0=4, nj=1, mode="heads")
        v, vb = proj(j0=5, nj=1, mode="heads")
        return zc, qb, k, kb, v, vb, lf

    def mix_mlp(x, zc, conv_buf, ya, seq_len, mk, mv, mix_w, mlp_w):
        cast = ((w_up[0], 1), (w_down[0], 0)) if mlp_w is None else ()
        w_out_b, w_xq_b, w_xo_b = mix_w
        x2, h, conv_new, *cast_w = _mix(x, zc, conv_buf, ya, conv_w[0], row(g_conv_out[0]), row(g_attn_out[0]),
                                        w_out_b, row(g_xattn[0]), w_xq_b, mk, mv, w_xo_b, row(g_mlp[0]),
                                        tm=256, seq=seq_len, cast=cast)
        mlp_w = mlp_w or tuple(cast_w)
        return _mlp(h, x2, *mlp_w, row(g_final), tm=1024, tf=1024), conv_new, mlp_w

    xp = x_prompt.reshape(nb * seq, d)
    zc, qb, k_p, kb, v_p, vb, lf_p = project(xp, q_scale=hd ** -0.5 * LOG2_E)
    lft = lf_p.reshape(nb, seq, heads).transpose(0, 2, 1).reshape(nb * heads, seq)
    ct = _cumsum_lanes(lft).reshape(nb, heads, seq)
    ya, *mix_w, w_xkv_b = _fox_prompt(qb, kb, vb, ct, ct.transpose(0, 2, 1), n_batch=nb, seq=seq, tq=1024, tk=512,
                                      cast=((w_out[0], 0), (w_xq[0], 0), (w_xo[0], 0), (w_xkv[0], 0)))
    kv = _memkv(mem_prompt.reshape(nb * n_mem, d), row(g_mem[0]), w_xkv_b)
    mk_p = kv[0].reshape(nb, n_mem * xh, xhd)
    mv_p = kv[1].reshape(nb, n_mem * xh, xhd)
    conv0 = jnp.zeros((nb, CONV_WIDTH - 1, d_conv), F32)
    y_p, conv_p, mlp_w = mix_mlp(xp, zc, conv0, ya, seq, mk_p, mv_p, mix_w, None)

    hp = heads // 2

    def sample_attention(qb, kb, vb, lf):
        lf_past = cache_logf[0].reshape(db, 2 * past, hp).transpose(0, 2, 1).reshape(db * hp, 2 * past)
        lf_new = lf.reshape(db, 2 * dseq, hp).transpose(0, 2, 1).reshape(db * hp, 2 * dseq)
        lf_new = jnp.pad(lf_new, ((0, 0), (0, LANES - 2 * dseq)))
        ck_past, ck_new = _cumsum_pairs(lf_past, lf_new, t_new=dseq)
        ck_new = ck_new[:, :2 * dseq].reshape(db, hp, 2 * dseq)
        cq = ck_new.reshape(db, hp, 2, dseq).transpose(0, 3, 2, 1).reshape(db, dseq, heads)
        return _fox_sample(qb, kb, vb, cache_k.reshape(db, past * heads, hd),
                           cache_v.reshape(db, past * heads, hd), cq, ck_past.reshape(db, hp, 2 * past), ck_new)

    xs = x_sample.reshape(db * dseq, d)
    zc, qb, k_s, kb, v_s, vb, lf_s = project(xs)
    ya = sample_attention(qb, kb, vb, lf_s)
    y_s, conv_s, _ = mix_mlp(xs, zc, cache_conv[0], ya, dseq, cache_mem_k.reshape(db, n_mem * xh, xhd),
                             cache_mem_v.reshape(db, n_mem * xh, xhd), mix_w, mlp_w)

    return (y_p.reshape(nb, seq, d),
            y_s.reshape(db, dseq, d),
            k_p.reshape(1, nb, seq, heads, hd),
            v_p.reshape(1, nb, seq, heads, hd),
            lf_p.reshape(1, nb, seq, heads),
            conv_p[None],
            mk_p.reshape(1, nb, n_mem, xh, xhd),
            mv_p.reshape(1, nb, n_mem, xh, xhd),
            k_s.reshape(1, db, dseq, heads, hd),
            v_s.reshape(1, db, dseq, heads, hd),
            lf_s.reshape(1, db, dseq, heads),
            conv_s[None])
```

```python
import functools

import jax
import jax.numpy as jnp
from jax import lax
from jax.experimental import pallas as pl
from jax.experimental.pallas import tpu as pltpu

F32 = jnp.float32
BF16 = jnp.bfloat16

RMS_EPS = 1e-6
HEAD_DIM = 128
N_FOX_HEADS = 8
N_X_HEADS = 4
CONV_WIDTH = 3
LANES = 128
V_PAD_ROWS = 16
LOG2_E = 1.4426950408889634
VMEM_LIMIT = 52 * 1024 * 1024
BIG_VMEM_LIMIT = 58 * 1024 * 1024

_NT = (((1,), (1,)), ((), ()))


def _params(semantics):
    return pltpu.CompilerParams(dimension_semantics=semantics, vmem_limit_bytes=VMEM_LIMIT)


def _rms(x, g):
    ms = jnp.mean(x * x, axis=-1, keepdims=True)
    return x * lax.rsqrt(ms + RMS_EPS) * g


def _log_sigmoid(x):
    return jnp.minimum(x, 0.0) - jnp.log1p(jnp.exp(-jnp.abs(x)))


def _store_head_rows(ref, val, heads):
    n = val.shape[0]
    for h in range(heads):
        ref[pl.ds(h, n, stride=heads), :] = val[:, h * HEAD_DIM:(h + 1) * HEAD_DIM]


def _load_head_rows(ref, lead, h, n, heads):
    return ref[lead, pl.ds(h, n, stride=heads), :]


def _xnorm_kernel(x_ref, g_ref, wf_ref, bf_ref, xn_ref, lf_ref):
    xn = _rms(x_ref[...], g_ref[...]).astype(BF16)
    xn_ref[...] = xn
    wf = wf_ref[...].astype(BF16)
    wf = jnp.concatenate([wf, jnp.zeros((LANES - wf.shape[0], wf.shape[1]), BF16)], axis=0)
    fz = lax.dot_general(xn, wf, _NT, preferred_element_type=F32)
    lf_ref[...] = _log_sigmoid(fz[:, :N_FOX_HEADS] + bf_ref[...])


def _xnorm(x, g, w_t, b_f, *, tm, gate_row):
    m, d = x.shape
    row = lambda i: (i, 0)
    return pl.pallas_call(
        _xnorm_kernel,
        grid=(m // tm,),
        in_specs=[
            pl.BlockSpec((tm, d), row),
            pl.BlockSpec((1, d), lambda i: (0, 0)),
            pl.BlockSpec((N_FOX_HEADS, d), lambda i: (gate_row // N_FOX_HEADS, 0)),
            pl.BlockSpec((1, N_FOX_HEADS), lambda i: (0, 0)),
        ],
        out_specs=[pl.BlockSpec((tm, d), row), pl.BlockSpec((tm, N_FOX_HEADS), row)],
        out_shape=[jax.ShapeDtypeStruct((m, d), BF16), jax.ShapeDtypeStruct((m, N_FOX_HEADS), F32)],
        compiler_params=_params(("arbitrary",)),
        name="xnorm",
    )(x, g, w_t, b_f)


def _proj_kernel(xn_ref, wt_ref, *refs, mode, out_scale):
    out_refs, wb_ref = refs[:-1], refs[-1]

    @pl.when(pl.program_id(1) == 0)
    def _():
        wb_ref[...] = wt_ref[...].astype(BF16)

    res = lax.dot_general(xn_ref[...], wb_ref[...], _NT, preferred_element_type=F32)
    if mode == "f32":
        out_refs[0][...] = res
    elif mode == "bf16":
        out_refs[0][...] = (res if out_scale is None else res * out_scale).astype(BF16)
    else:
        _store_head_rows(out_refs[0], res, N_FOX_HEADS)
        out_refs[1][...] = res.astype(BF16)


def _proj(xn, w_t, *, j0, nj, tn, tm, mode, out_scale=None):
    m, d = xn.shape
    if mode == "heads":
        assert nj == 1
        out_specs = [pl.BlockSpec((tm * N_FOX_HEADS, HEAD_DIM), lambda j, i: (i, 0)),
                     pl.BlockSpec((tm, tn), lambda j, i: (i, 0))]
        out_shape = [jax.ShapeDtypeStruct((m * N_FOX_HEADS, HEAD_DIM), F32), jax.ShapeDtypeStruct((m, tn), BF16)]
    else:
        out_specs = [pl.BlockSpec((tm, tn), lambda j, i: (i, j))]
        out_shape = [jax.ShapeDtypeStruct((m, nj * tn), F32 if mode == "f32" else BF16)]
    return pl.pallas_call(
        functools.partial(_proj_kernel, mode=mode, out_scale=out_scale),
        grid=(nj, m // tm),
        in_specs=[
            pl.BlockSpec((tm, d), lambda j, i: (i, 0)),
            pl.BlockSpec((tn, d), lambda j, i: (j0 + j, 0)),
        ],
        out_specs=out_specs,
        out_shape=out_shape,
        scratch_shapes=[pltpu.VMEM((tn, d), BF16)],
        compiler_params=_params(("arbitrary", "arbitrary")),
        name="proj_" + mode,
    )(xn, w_t)


def _cumsum_kernel(a_ref, o_ref):
    rows, length = a_ref.shape
    r = lax.broadcasted_iota(jnp.int32, (LANES, LANES), 0)
    c = lax.broadcasted_iota(jnp.int32, (LANES, LANES), 1)
    upper = (r <= c).astype(BF16)
    carry = jnp.zeros((rows, 1), F32)
    for b in range(length // LANES):
        a = a_ref[:, b * LANES:(b + 1) * LANES]
        hi = a.astype(BF16)
        r1 = a - hi.astype(F32)
        mid = r1.astype(BF16)
        lo = (r1 - mid.astype(F32)).astype(BF16)
        s = (jnp.dot(hi, upper, preferred_element_type=F32)
             + jnp.dot(mid, upper, preferred_element_type=F32)
             + jnp.dot(lo, upper, preferred_element_type=F32)) + carry
        o_ref[:, b * LANES:(b + 1) * LANES] = s
        carry = s[:, LANES - 1:LANES]


def _cumsum_lanes(a):
    return pl.pallas_call(
        _cumsum_kernel,
        out_shape=jax.ShapeDtypeStruct(a.shape, F32),
        compiler_params=pltpu.CompilerParams(vmem_limit_bytes=VMEM_LIMIT),
        name="cumsum",
    )(a)


def _cumsum_pairs_kernel(p_ref, n_ref, op_ref, on_ref, *, t_new):
    rows, length = p_ref.shape
    r = lax.broadcasted_iota(jnp.int32, (LANES, LANES), 0)
    c = lax.broadcasted_iota(jnp.int32, (LANES, LANES), 1)
    upper = ((r <= c) & (((r ^ c) & 1) == 0)).astype(BF16)
    unzip = ((r < 2 * t_new) & (c == (r & 1) * t_new + (r >> 1))).astype(BF16)
    lane_odd = (lax.broadcasted_iota(jnp.int32, (rows, LANES), 1) & 1) == 1

    def times(a, mat):
        hi, mid, lo = _split3(a)
        return (jnp.dot(hi.astype(BF16), mat, preferred_element_type=F32)
                + jnp.dot(mid.astype(BF16), mat, preferred_element_type=F32)
                + jnp.dot(lo.astype(BF16), mat, preferred_element_type=F32))

    carry = jnp.zeros((rows, LANES), F32)
    for b in range(length // LANES):
        s = times(p_ref[:, b * LANES:(b + 1) * LANES], upper) + carry
        op_ref[:, b * LANES:(b + 1) * LANES] = s
        carry = jnp.where(lane_odd, s[:, LANES - 1:LANES], s[:, LANES - 2:LANES - 1])
    on_ref[...] = times(times(n_ref[...], upper) + carry, unzip)


def _cumsum_pairs(p, n, *, t_new):
    return pl.pallas_call(
        functools.partial(_cumsum_pairs_kernel, t_new=t_new),
        out_shape=[jax.ShapeDtypeStruct(p.shape, F32), jax.ShapeDtypeStruct(n.shape, F32)],
        compiler_params=pltpu.CompilerParams(vmem_limit_bytes=VMEM_LIMIT),
        name="cumsum_pairs",
    )(p, n)


def _split3(x):
    hi = x.astype(BF16).astype(F32)
    r = x - hi
    mid = r.astype(BF16).astype(F32)
    lo = (r - mid).astype(BF16).astype(F32)
    return hi, mid, lo


def _fox_prompt_kernel(q_ref, k_ref, v_ref, cq_ref, ck_ref, *refs, tq, tk, n_cast):
    assert tq == 2 * tk
    cast_in, (o_ref, *cast_out), scratch = refs[:n_cast], refs[n_cast:2 * n_cast + 1], refs[2 * n_cast + 1:]
    kaug_ref, vt_ref, qaug0_ref, qaug1_ref, acc0_ref, acc1_ref, s0_ref, s1_ref = scratch
    for src_ref, dst_ref in zip(cast_in, cast_out):
        dst_ref[...] = src_ref[...].astype(BF16)
    qaug_refs, acc_refs = (qaug0_ref, qaug1_ref), (acc0_ref, acc1_ref)
    h = pl.program_id(1)
    seq = kaug_ref.shape[0]
    nk = seq // tk

    lane8 = lax.broadcasted_iota(jnp.int32, (tk, N_FOX_HEADS), 1)
    lane = lax.broadcasted_iota(jnp.int32, (tk, HEAD_DIM), 1)
    ones_row = (lax.broadcasted_iota(jnp.int32, (V_PAD_ROWS, tk), 0) == 0).astype(BF16)

    def prep(c, carry):
        rows = pl.ds(pl.multiple_of(c * tk, tk), tk)
        kaug_ref[rows, 0:HEAD_DIM] = k_ref[rows, :]
        col = jnp.sum(jnp.where(lane8 == h, ck_ref[0, rows, :], 0.0), axis=1, keepdims=True) * LOG2_E
        hi, mid, lo = _split3(col)
        aug = jnp.where(lane == 0, hi, jnp.where(lane == 1, mid, jnp.where(lane == 2, lo,
                        jnp.where(lane < 6, 1.0, 0.0))))
        kaug_ref[rows, HEAD_DIM:2 * HEAD_DIM] = aug.astype(BF16)
        vt_ref[c, 0:HEAD_DIM, :] = v_ref[rows, :].astype(F32).T.astype(BF16)
        vt_ref[c, HEAD_DIM:, :] = ones_row
        return carry

    lax.fori_loop(0, nk, prep, 0)

    sub = lax.broadcasted_iota(jnp.int32, (HEAD_DIM, tq), 0)

    def scores(qaug_ref, ki, dst_ref, lane0=0):
        dst_ref[:, lane0:] = jnp.dot(kaug_ref[ki * tk:(ki + 1) * tk, :], qaug_ref[:, lane0:],
                                     preferred_element_type=F32)

    def softmax_pv(acc_ref, ki, m_all, s_ref, lane0=0, mask_offset=None):
        m_prev = m_all[:, lane0:]
        s = s_ref[:, lane0:]
        if mask_offset is not None:
            r = lax.broadcasted_iota(jnp.int32, s.shape, 0) + mask_offset
            c = lax.broadcasted_iota(jnp.int32, s.shape, 1) + lane0
            s = jnp.where(r <= c, s, -jnp.inf)
        m_new = jnp.maximum(m_prev, jnp.max(s, axis=0, keepdims=True))
        alpha = jnp.exp2(m_prev - m_new)
        p = jnp.exp2(s - m_new).astype(BF16)
        pv = jnp.dot(vt_ref[ki], p, preferred_element_type=F32)
        acc_ref[:, lane0:] = alpha * acc_ref[:, lane0:] + pv
        return jnp.concatenate([m_all[:, :lane0], m_new], axis=1) if lane0 else m_new

    s_refs = (s0_ref, s1_ref)
    for qi in range(seq // tq):
        qaug_ref, acc_ref = qaug_refs[qi % 2], acc_refs[qi % 2]
        qrows = slice(qi * tq, (qi + 1) * tq)
        qaug_ref[0:HEAD_DIM, :] = q_ref[qrows, :].astype(F32).T.astype(BF16)
        hi, mid, lo = _split3(cq_ref[0, pl.ds(h, 1), qrows] * LOG2_E)
        qaug_ref[HEAD_DIM:2 * HEAD_DIM, :] = jnp.where(
            sub < 3, -1.0, jnp.where(sub == 3, hi, jnp.where(sub == 4, mid, jnp.where(sub == 5, lo, 0.0)))
        ).astype(BF16)
        acc_ref[...] = jnp.zeros(acc_ref.shape, F32)
        n_full = 2 * qi
        m = jnp.full((1, tq), -jnp.inf, F32)
        scores(qaug_ref, 0, s_refs[0])
        for ki in range(n_full):
            scores(qaug_ref, ki + 1, s_refs[(ki + 1) % 2])
            m = softmax_pv(acc_ref, ki, m, s_refs[ki % 2])
        scores(qaug_ref, n_full + 1, s_refs[1], lane0=tk)
        m = softmax_pv(acc_ref, n_full, m, s_refs[0], mask_offset=0)
        softmax_pv(acc_ref, n_full + 1, m, s_refs[1], lane0=tk, mask_offset=tk)
        o_ref[qrows, :] = (acc_ref[0:HEAD_DIM, :] / acc_ref[HEAD_DIM:HEAD_DIM + 1, :]).T


def _cast_specs(cast, steps, step_of):
    specs = []
    for w, axis in cast:
        block = tuple(n // steps if a == axis else n for a, n in enumerate(w.shape))
        specs.append(pl.BlockSpec(block, (lambda *g: (step_of(*g), 0)) if axis == 0
                                  else (lambda *g: (0, step_of(*g)))))
    return specs


def _fox_prompt(qb, kb, vb, cq, ck, *, n_batch, seq, tq, tk, cast=()):
    m = qb.shape[0]
    head = lambda n, h: (n, h)
    cast_specs = _cast_specs(cast, n_batch * N_FOX_HEADS, lambda n, h: n * N_FOX_HEADS + h)
    return pl.pallas_call(
        functools.partial(_fox_prompt_kernel, tq=tq, tk=tk, n_cast=len(cast)),
        grid=(n_batch, N_FOX_HEADS),
        in_specs=[
            pl.BlockSpec((seq, HEAD_DIM), head),
            pl.BlockSpec((seq, HEAD_DIM), head),
            pl.BlockSpec((seq, HEAD_DIM), head),
            pl.BlockSpec((1, N_FOX_HEADS, seq), lambda n, h: (n, 0, 0)),
            pl.BlockSpec((1, seq, N_FOX_HEADS), lambda n, h: (n, 0, 0)),
        ] + cast_specs,
        out_specs=[pl.BlockSpec((seq, HEAD_DIM), head)] + cast_specs,
        out_shape=[jax.ShapeDtypeStruct((m, N_FOX_HEADS * HEAD_DIM), F32)]
        + [jax.ShapeDtypeStruct(w.shape, BF16) for w, _ in cast],
        scratch_shapes=[
            pltpu.VMEM((seq, 2 * HEAD_DIM), BF16),
            pltpu.VMEM((seq // tk, HEAD_DIM + V_PAD_ROWS, tk), BF16),
            pltpu.VMEM((2 * HEAD_DIM, tq), BF16),
            pltpu.VMEM((2 * HEAD_DIM, tq), BF16),
            pltpu.VMEM((HEAD_DIM + V_PAD_ROWS, tq), F32),
            pltpu.VMEM((HEAD_DIM + V_PAD_ROWS, tq), F32),
            pltpu.VMEM((tk, tq), F32),
            pltpu.VMEM((tk, tq), F32),
        ],
        compiler_params=pltpu.CompilerParams(dimension_semantics=("arbitrary", "arbitrary"),
                                             vmem_limit_bytes=BIG_VMEM_LIMIT),
        name="fox_prompt",
    )(qb, kb, vb, cq, ck, *[w for w, _ in cast])


def _fox_sample_kernel(q_ref, kn_ref, vn_ref, kc_ref, vc_ref, cq_ref, ckp_ref, ckn_ref, o_ref, *, past):
    scale = HEAD_DIM ** -0.5
    t = q_ref.shape[0]
    half = N_FOX_HEADS // 2
    r = lax.broadcasted_iota(jnp.int32, (2 * t, 2 * past), 0)
    c = lax.broadcasted_iota(jnp.int32, (2 * t, 2 * past), 1)
    own_past = (r >= t) == ((c & 1) == 1)
    rn = lax.broadcasted_iota(jnp.int32, (2 * t, 2 * t), 0)
    cn = lax.broadcasted_iota(jnp.int32, (2 * t, 2 * t), 1)
    own_new = ((rn >= t) == (cn >= t)) & ((cn & (t - 1)) <= (rn & (t - 1)))
    for j in range(half):
        sl0 = slice(j * HEAD_DIM, (j + 1) * HEAD_DIM)
        sl1 = slice((j + half) * HEAD_DIM, (j + half + 1) * HEAD_DIM)
        both = lambda ref: jnp.concatenate([ref[:, sl0], ref[:, sl1]], axis=0)
        q = both(q_ref)
        kp = kc_ref[0, pl.ds(j, 2 * past, stride=half), :].astype(BF16)
        vp = vc_ref[0, pl.ds(j, 2 * past, stride=half), :].astype(BF16)
        cq = jnp.concatenate([cq_ref[0, :, j:j + 1], cq_ref[0, :, j + half:j + half + 1]], axis=0)
        s1 = lax.dot_general(q, kp, _NT, preferred_element_type=F32) * scale
        s1 = jnp.where(own_past, s1 + (cq - ckp_ref[0, j:j + 1, :]), -jnp.inf)
        s2 = lax.dot_general(q, both(kn_ref), _NT, preferred_element_type=F32) * scale
        s2 = jnp.where(own_new, s2 + (cq - ckn_ref[0, j:j + 1, :]), -jnp.inf)
        m = jnp.maximum(jnp.max(s1, axis=1, keepdims=True), jnp.max(s2, axis=1, keepdims=True))
        p1 = jnp.exp(s1 - m)
        p2 = jnp.exp(s2 - m)
        l = jnp.sum(p1, axis=1, keepdims=True) + jnp.sum(p2, axis=1, keepdims=True)
        o = (jnp.dot(p1.astype(BF16), vp, preferred_element_type=F32)
             + jnp.dot(p2.astype(BF16), both(vn_ref), preferred_element_type=F32)) / l
        o_ref[:, sl0] = o[:t]
        o_ref[:, sl1] = o[t:]


def _fox_sample(qb, knb, vnb, cache_k, cache_v, cq, ck_past, ck_new):
    n_batch, rows, _ = cache_k.shape
    past = rows // N_FOX_HEADS
    t = qb.shape[0] // n_batch
    width = qb.shape[1]
    tok = lambda b: (b, 0)
    lead = lambda b: (b, 0, 0)
    return pl.pallas_call(
        functools.partial(_fox_sample_kernel, past=past),
        grid=(n_batch,),
        in_specs=[
            pl.BlockSpec((t, width), tok),
            pl.BlockSpec((t, width), tok),
            pl.BlockSpec((t, width), tok),
            pl.BlockSpec((1, rows, HEAD_DIM), lead),
            pl.BlockSpec((1, rows, HEAD_DIM), lead),
            pl.BlockSpec((1, t, N_FOX_HEADS), lead),
            pl.BlockSpec((1, N_FOX_HEADS // 2, 2 * past), lead),
            pl.BlockSpec((1, N_FOX_HEADS // 2, 2 * t), lead),
        ],
        out_specs=pl.BlockSpec((t, width), tok),
        out_shape=jax.ShapeDtypeStruct(qb.shape, F32),
        compiler_params=_params(("arbitrary",)),
        name="fox_sample",
    )(qb, knb, vnb, cache_k, cache_v, cq, ck_past, ck_new)


def _mix_kernel(*refs, seg, nseg, tiles_per_seq, has_prev, n_cast):
    if has_prev:
        (x_ref, gb_ref, gc_ref, hin_ref, pgc_ref, phin_ref, *refs) = refs
    else:
        (x_ref, gb_ref, gc_ref, hin_ref, *refs) = refs
    (buf_ref, ya_ref, cw_ref, gco_ref, gao_ref, wout_ref, gx_ref, wxq_ref, mk_ref, mv_ref, wxo_ref, gm_ref,
     *refs) = refs
    cast_in, (x2_ref, h_ref, cnew_ref, *cast_out, o_scr) = refs[:n_cast], refs[n_cast:]
    for src_ref, dst_ref in zip(cast_in, cast_out):
        dst_ref[...] = src_ref[...].astype(BF16)
    i = pl.program_id(0)
    u = gc_ref[...] * hin_ref[...]
    gb = gb_ref[...]
    w0 = cw_ref[0:1, :]
    w1 = cw_ref[1:2, :]
    w2 = cw_ref[2:3, :]
    rid = lax.broadcasted_iota(jnp.int32, (seg, u.shape[1]), 0)
    pieces = []
    for s in range(nseg):
        u_s = u[s * seg:(s + 1) * seg]
        b0 = buf_ref[s, 0:1, :]
        b1 = buf_ref[s, 1:2, :]
        if has_prev:
            pu = pgc_ref[...] * phin_ref[...]
            first = (i % tiles_per_seq) == 0
            b0 = jnp.where(first, b0, pu[6:7])
            b1 = jnp.where(first, b1, pu[7:8])
        r1 = jnp.where(rid == 0, b1, pltpu.roll(u_s, 1, 0))
        r2 = jnp.where(rid == 0, b0, jnp.where(rid == 1, b1, pltpu.roll(u_s, 2, 0)))
        y = w0 * r2 + w1 * r1 + w2 * u_s
        pieces.append(gb[s * seg:(s + 1) * seg] * y)
        cnew_ref[s] = u_s[seg - 2:seg]
    yc = pieces[0] if nseg == 1 else jnp.concatenate(pieces, axis=0)
    cat = jnp.concatenate([_rms(yc, gco_ref[...]).astype(BF16),
                           _rms(ya_ref[...], gao_ref[...]).astype(BF16)], axis=-1)
    x1 = x_ref[...] + jnp.dot(cat, wout_ref[...], preferred_element_type=F32)
    xn = _rms(x1, gx_ref[...]).astype(BF16)
    qx = jnp.dot(xn, wxq_ref[...], preferred_element_type=F32).astype(BF16)

    scale = HEAD_DIM ** -0.5
    n_mem = mk_ref.shape[1] // N_X_HEADS
    for s in range(nseg):
        rows = slice(s * seg, (s + 1) * seg)
        for hh in range(N_X_HEADS):
            sl = slice(hh * HEAD_DIM, (hh + 1) * HEAD_DIM)
            kh = _load_head_rows(mk_ref, s, hh, n_mem, N_X_HEADS).astype(BF16)
            vh = _load_head_rows(mv_ref, s, hh, n_mem, N_X_HEADS).astype(BF16)
            sc = lax.dot_general(qx[rows, sl], kh, _NT, preferred_element_type=F32) * scale
            m = jnp.max(sc, axis=1, keepdims=True)
            p = jnp.exp(sc - m)
            l = jnp.sum(p, axis=1, keepdims=True)
            o = jnp.dot(p.astype(BF16), vh, preferred_element_type=F32) / l
            o_scr[rows, sl] = o.astype(BF16)
    x2 = x1 + jnp.dot(o_scr[...], wxo_ref[...], preferred_element_type=F32)
    x2_ref[...] = x2
    h_ref[...] = _rms(x2, gm_ref[...]).astype(BF16)


def _mix(x, zc, conv_buf, ya, conv_w, g_conv_out, g_attn_out, w_out, g_xattn, w_xq, mk, mv, w_xo, g_mlp, *, tm, seq,
         cast=()):
    m, d = x.shape
    dc = zc.shape[1] // 3
    dx = w_xq.shape[1]
    mem_rows = mk.shape[1]
    has_prev = seq > tm
    seg = tm if has_prev else seq
    nseg = tm // seg
    tiles_per_seq = max(seq // tm, 1)
    row = lambda i: (i, 0)
    const = lambda i: (0, 0)
    resident = functools.partial(pl.BlockSpec, index_map=const, pipeline_mode=pl.Buffered(1))
    in_specs = [
        pl.BlockSpec((tm, d), row),
        pl.BlockSpec((tm, dc), lambda i: (i, 0)),
        pl.BlockSpec((tm, dc), lambda i: (i, 1)),
        pl.BlockSpec((tm, dc), lambda i: (i, 2)),
    ]
    args = [x, zc, zc, zc]
    if has_prev:
        prev = lambda col: (lambda i: (jnp.maximum(i * (tm // 8) - 1, 0), col))
        in_specs += [pl.BlockSpec((8, dc), prev(1)), pl.BlockSpec((8, dc), prev(2))]
        args += [zc, zc]
        buf_map = lambda i: (i // tiles_per_seq, 0, 0)
    else:
        buf_map = lambda i: (i, 0, 0)
    in_specs += [
        pl.BlockSpec((nseg, CONV_WIDTH - 1, dc), buf_map),
        pl.BlockSpec((tm, ya.shape[1]), row),
        pl.BlockSpec((CONV_WIDTH, dc), const),
        pl.BlockSpec((1, dc), const),
        pl.BlockSpec((1, ya.shape[1]), const),
        resident(w_out.shape),
        pl.BlockSpec((1, d), const),
        resident(w_xq.shape),
        pl.BlockSpec((nseg, mem_rows, HEAD_DIM), buf_map),
        pl.BlockSpec((nseg, mem_rows, HEAD_DIM), buf_map),
        resident(w_xo.shape),
        pl.BlockSpec((1, d), const),
    ]
    args += [conv_buf, ya, conv_w, g_conv_out, g_attn_out, w_out, g_xattn, w_xq, mk, mv, w_xo, g_mlp]
    steps = m // tm
    cast_specs = _cast_specs(cast, steps, lambda i: i)
    args += [w for w, _ in cast]
    return pl.pallas_call(
        functools.partial(_mix_kernel, seg=seg, nseg=nseg, tiles_per_seq=tiles_per_seq, has_prev=has_prev,
                          n_cast=len(cast)),
        grid=(steps,),
        in_specs=in_specs + cast_specs,
        out_specs=[
            pl.BlockSpec((tm, d), row),
            pl.BlockSpec((tm, d), row),
            pl.BlockSpec((nseg, CONV_WIDTH - 1, dc), buf_map),
        ] + cast_specs,
        out_shape=[
            jax.ShapeDtypeStruct((m, d), F32),
            jax.ShapeDtypeStruct((m, d), BF16),
            jax.ShapeDtypeStruct(conv_buf.shape, F32),
        ] + [jax.ShapeDtypeStruct(w.shape, BF16) for w, _ in cast],
        scratch_shapes=[pltpu.VMEM((tm, dx), BF16)],
        compiler_params=pltpu.CompilerParams(dimension_semantics=("arbitrary",), vmem_limit_bytes=BIG_VMEM_LIMIT),
        name="mix",
    )(*args)


def _memkv_kernel(x_ref, g_ref, w_ref, o_ref):
    xn = _rms(x_ref[...], g_ref[...]).astype(BF16)
    _store_head_rows(o_ref.at[0], jnp.dot(xn, w_ref[...], preferred_element_type=F32), N_X_HEADS)


def _memkv(mem, g, w):
    m, d = mem.shape
    half = w.shape[1] // 2
    return pl.pallas_call(
        _memkv_kernel,
        grid=(2,),
        in_specs=[
            pl.BlockSpec((m, d), lambda j: (0, 0)),
            pl.BlockSpec((1, d), lambda j: (0, 0)),
            pl.BlockSpec((d, half), lambda j: (0, j)),
        ],
        out_specs=pl.BlockSpec((1, m * N_X_HEADS, HEAD_DIM), lambda j: (j, 0, 0)),
        out_shape=jax.ShapeDtypeStruct((2, m * N_X_HEADS, HEAD_DIM), F32),
        compiler_params=_params(("arbitrary",)),
        name="memkv",
    )(mem, g, w)


def _mlp_kernel(h_ref, x2_hbm, wu_ref, wd_ref, g_ref, y_ref, sem):
    i = pl.program_id(0)
    j = pl.program_id(1)
    tm = y_ref.shape[0]
    x2_copy = pltpu.make_async_copy(x2_hbm.at[pl.ds(i * tm, tm), :], y_ref, sem)

    def hidden():
        a = jnp.dot(h_ref[...], wu_ref[...], preferred_element_type=F32)
        return jnp.square(jnp.maximum(a, 0.0)).astype(BF16)

    @pl.when(j == 0)
    def _():
        x2_copy.start()
        a = hidden()
        x2_copy.wait()
        y_ref[...] += jnp.dot(a, wd_ref[...], preferred_element_type=F32)

    @pl.when(j > 0)
    def _():
        y_ref[...] += jnp.dot(hidden(), wd_ref[...], preferred_element_type=F32)

    @pl.when(j == pl.num_programs(1) - 1)
    def _():
        y_ref[...] = _rms(y_ref[...], g_ref[...])


def _mlp(h, x2, w_up, w_down, g_final, *, tm, tf):
    m, d = h.shape
    ff = w_up.shape[1]
    row = lambda i, j: (i, 0)
    return pl.pallas_call(
        _mlp_kernel,
        grid=(m // tm, ff // tf),
        in_specs=[
            pl.BlockSpec((tm, d), row),
            pl.BlockSpec(memory_space=pl.ANY),
            pl.BlockSpec((d, tf), lambda i, j: (0, j)),
            pl.BlockSpec((tf, d), lambda i, j: (j, 0)),
            pl.BlockSpec((1, d), lambda i, j: (0, 0)),
        ],
        out_specs=pl.BlockSpec((tm, d), row),
        out_shape=jax.ShapeDtypeStruct((m, d), F32),
        scratch_shapes=[pltpu.SemaphoreType.DMA(())],
        compiler_params=pltpu.CompilerParams(dimension_semantics=("arbitrary", "arbitrary"),
                                             vmem_limit_bytes=BIG_VMEM_LIMIT),
        name="mlp",
    )(h, x2, w_up, w_down, g_final)


def kernel(x_prompt, x_sample, cache_k, cache_v, cache_logf, cache_conv, cache_mem_k, cache_mem_v, mem_prompt, g_mix, w_in, b_f, conv_w, g_conv_out, g_attn_out, w_out, g_xattn, g_mem, w_xq, w_xkv, w_xo, g_mlp, w_up, w_down, g_final):
    depth = w_in.shape[0]
    assert depth == 1, "single-layer trunk"
    nb, seq, d = x_prompt.shape
    db, dseq, _ = x_sample.shape
    past = cache_k.shape[2]
    heads, hd = cache_k.shape[3], cache_k.shape[4]
    assert (heads, hd) == (N_FOX_HEADS, HEAD_DIM)
    d_attn = heads * hd
    d_conv = cache_conv.shape[-1]
    n_mem, xh, xhd = cache_mem_k.shape[2:]
    assert (xh, xhd) == (N_X_HEADS, HEAD_DIM)
    d_x = xh * xhd
    n_main = 3 * d_conv + 3 * d_attn
    assert d_conv == d_attn and w_in.shape[2] == n_main + heads

    w_in_t = jnp.swapaxes(w_in[0], 0, 1)
    bf = b_f[0].reshape(1, heads)
    row = lambda g: g.reshape(1, -1)

    def project(x, q_scale=None):
        xn, lf = _xnorm(x, row(g_mix[0]), w_in_t, bf, tm=512, gate_row=n_main)
        proj = functools.partial(_proj, xn, w_in_t, tn=d_attn, tm=1024)
        zc, = proj(j0=0, nj=3, mode="f32")
        qb, = proj(j0=3, nj=1, mode="bf16", out_scale=q_scale)
        k, kb = proj(j0=4, nj=1, mode="heads")
        v, vb = proj(j0=5, nj=1, mode="heads")
        return zc, qb, k, kb, v, vb, lf

    def mix_mlp(x, zc, conv_buf, ya, seq_len, mk, mv, mix_w, mlp_w, tm):
        w_out_b, w_xq_b, w_xo_b = mix_w
        x2, h, conv_new = _mix(x, zc, conv_buf, ya, conv_w[0], row(g_conv_out[0]), row(g_attn_out[0]),
                               w_out_b, row(g_xattn[0]), w_xq_b, mk, mv, w_xo_b, row(g_mlp[0]), tm=tm, seq=seq_len)
        return _mlp(h, x2, *mlp_w, row(g_final), tm=1024, tf=1024), conv_new

    xp = x_prompt.reshape(nb * seq, d)
    zc, qb, k_p, kb, v_p, vb, lf_p = project(xp, q_scale=hd ** -0.5 * LOG2_E)
    lft = lf_p.reshape(nb, seq, heads).transpose(0, 2, 1).reshape(nb * heads, seq)
    ct = _cumsum_lanes(lft).reshape(nb, heads, seq)
    ya, *mix_w, w_xkv_b, w_up_b, w_down_b = _fox_prompt(
        qb, kb, vb, ct, ct.transpose(0, 2, 1), n_batch=nb, seq=seq, tq=1024, tk=512,
        cast=((w_out[0], 0), (w_xq[0], 0), (w_xo[0], 0), (w_xkv[0], 0), (w_up[0], 1), (w_down[0], 0)))
    mlp_w = (w_up_b, w_down_b)
    kv = _memkv(mem_prompt.reshape(nb * n_mem, d), row(g_mem[0]), w_xkv_b)
    mk_p = kv[0].reshape(nb, n_mem * xh, xhd)
    mv_p = kv[1].reshape(nb, n_mem * xh, xhd)
    conv0 = jnp.zeros((nb, CONV_WIDTH - 1, d_conv), F32)
    y_p, conv_p = mix_mlp(xp, zc, conv0, ya, seq, mk_p, mv_p, mix_w, mlp_w, tm=512)

    hp = heads // 2

    def sample_attention(qb, kb, vb, lf):
        lf_past = cache_logf[0].reshape(db, 2 * past, hp).transpose(0, 2, 1).reshape(db * hp, 2 * past)
        lf_new = lf.reshape(db, 2 * dseq, hp).transpose(0, 2, 1).reshape(db * hp, 2 * dseq)
        lf_new = jnp.pad(lf_new, ((0, 0), (0, LANES - 2 * dseq)))
        ck_past, ck_new = _cumsum_pairs(lf_past, lf_new, t_new=dseq)
        ck_new = ck_new[:, :2 * dseq].reshape(db, hp, 2 * dseq)
        cq = ck_new.reshape(db, hp, 2, dseq).transpose(0, 3, 2, 1).reshape(db, dseq, heads)
        return _fox_sample(qb, kb, vb, cache_k.reshape(db, past * heads, hd),
                           cache_v.reshape(db, past * heads, hd), cq, ck_past.reshape(db, hp, 2 * past), ck_new)

    xs = x_sample.reshape(db * dseq, d)
    zc, qb, k_s, kb, v_s, vb, lf_s = project(xs)
    ya = sample_attention(qb, kb, vb, lf_s)
    y_s, conv_s = mix_mlp(xs, zc, cache_conv[0], ya, dseq, cache_mem_k.reshape(db, n_mem * xh, xhd),
                          cache_mem_v.reshape(db, n_mem * xh, xhd), mix_w, mlp_w, tm=256)

    return (y_p.reshape(nb, seq, d),
            y_s.reshape(db, dseq, d),
            k_p.reshape(1, nb, seq, heads, hd),
            v_p.reshape(1, nb, seq, heads, hd),
            lf_p.reshape(1, nb, seq, heads),
            conv_p[None],
            mk_p.reshape(1, nb, n_mem, xh, xhd),
            mv_p.reshape(1, nb, n_mem, xh, xhd),
            k_s.reshape(1, db, dseq, heads, hd),
            v_s.reshape(1, db, dseq, heads, hd),
            lf_s.reshape(1, db, dseq, heads),
            conv_s[None])
```

```python
import functools

import jax
import jax.numpy as jnp
from jax import lax
from jax.experimental import pallas as pl
from jax.experimental.pallas import tpu as pltpu

F32 = jnp.float32
BF16 = jnp.bfloat16

RMS_EPS = 1e-6
HEAD_DIM = 128
N_FOX_HEADS = 8
N_X_HEADS = 4
CONV_WIDTH = 3
LANES = 128
V_PAD_ROWS = 16
LOG2_E = 1.4426950408889634
VMEM_LIMIT = 52 * 1024 * 1024
BIG_VMEM_LIMIT = 58 * 1024 * 1024

_NT = (((1,), (1,)), ((), ()))


def _params(semantics):
    return pltpu.CompilerParams(dimension_semantics=semantics, vmem_limit_bytes=VMEM_LIMIT)


def _rms(x, g):
    ms = jnp.mean(x * x, axis=-1, keepdims=True)
    return x * lax.rsqrt(ms + RMS_EPS) * g


def _log_sigmoid(x):
    return jnp.minimum(x, 0.0) - jnp.log1p(jnp.exp(-jnp.abs(x)))


def _store_head_rows(ref, val, heads):
    n = val.shape[0]
    for h in range(heads):
        ref[pl.ds(h, n, stride=heads), :] = val[:, h * HEAD_DIM:(h + 1) * HEAD_DIM]


def _load_head_rows(ref, lead, h, n, heads):
    return ref[lead, pl.ds(h, n, stride=heads), :]


def _xnorm_kernel(x_ref, g_ref, wf_ref, bf_ref, xn_ref, lf_ref):
    xn = _rms(x_ref[...], g_ref[...]).astype(BF16)
    xn_ref[...] = xn
    wf = wf_ref[...].astype(BF16)
    wf = jnp.concatenate([wf, jnp.zeros((LANES - wf.shape[0], wf.shape[1]), BF16)], axis=0)
    fz = lax.dot_general(xn, wf, _NT, preferred_element_type=F32)
    lf_ref[...] = _log_sigmoid(fz[:, :N_FOX_HEADS] + bf_ref[...])


def _xnorm(x, g, w_t, b_f, *, tm, gate_row):
    m, d = x.shape
    row = lambda i: (i, 0)
    return pl.pallas_call(
        _xnorm_kernel,
        grid=(m // tm,),
        in_specs=[
            pl.BlockSpec((tm, d), row),
            pl.BlockSpec((1, d), lambda i: (0, 0)),
            pl.BlockSpec((N_FOX_HEADS, d), lambda i: (gate_row // N_FOX_HEADS, 0)),
            pl.BlockSpec((1, N_FOX_HEADS), lambda i: (0, 0)),
        ],
        out_specs=[pl.BlockSpec((tm, d), row), pl.BlockSpec((tm, N_FOX_HEADS), row)],
        out_shape=[jax.ShapeDtypeStruct((m, d), BF16), jax.ShapeDtypeStruct((m, N_FOX_HEADS), F32)],
        compiler_params=_params(("arbitrary",)),
        name="xnorm",
    )(x, g, w_t, b_f)


def _proj_kernel(xn_ref, wt_ref, *refs, mode, out_scale):
    out_refs, wb_ref = refs[:-1], refs[-1]

    @pl.when(pl.program_id(1) == 0)
    def _():
        wb_ref[...] = wt_ref[...].astype(BF16)

    res = lax.dot_general(xn_ref[...], wb_ref[...], _NT, preferred_element_type=F32)
    if mode == "f32":
        out_refs[0][...] = res
    elif mode == "bf16":
        out_refs[0][...] = (res if out_scale is None else res * out_scale).astype(BF16)
    else:
        _store_head_rows(out_refs[0], res, N_FOX_HEADS)
        out_refs[1][...] = res.astype(BF16)


def _proj(xn, w_t, *, j0, nj, tn, tm, mode, out_scale=None):
    m, d = xn.shape
    if mode == "heads":
        assert nj == 1
        out_specs = [pl.BlockSpec((tm * N_FOX_HEADS, HEAD_DIM), lambda j, i: (i, 0)),
                     pl.BlockSpec((tm, tn), lambda j, i: (i, 0))]
        out_shape = [jax.ShapeDtypeStruct((m * N_FOX_HEADS, HEAD_DIM), F32), jax.ShapeDtypeStruct((m, tn), BF16)]
    else:
        out_specs = [pl.BlockSpec((tm, tn), lambda j, i: (i, j))]
        out_shape = [jax.ShapeDtypeStruct((m, nj * tn), F32 if mode == "f32" else BF16)]
    return pl.pallas_call(
        functools.partial(_proj_kernel, mode=mode, out_scale=out_scale),
        grid=(nj, m // tm),
        in_specs=[
            pl.BlockSpec((tm, d), lambda j, i: (i, 0)),
            pl.BlockSpec((tn, d), lambda j, i: (j0 + j, 0)),
        ],
        out_specs=out_specs,
        out_shape=out_shape,
        scratch_shapes=[pltpu.VMEM((tn, d), BF16)],
        compiler_params=_params(("arbitrary", "arbitrary")),
        name="proj_" + mode,
    )(xn, w_t)


def _cumsum_kernel(a_ref, o_ref):
    rows, length = a_ref.shape
    r = lax.broadcasted_iota(jnp.int32, (LANES, LANES), 0)
    c = lax.broadcasted_iota(jnp.int32, (LANES, LANES), 1)
    upper = (r <= c).astype(BF16)
    carry = jnp.zeros((rows, 1), F32)
    for b in range(length // LANES):
        a = a_ref[:, b * LANES:(b + 1) * LANES]
        hi = a.astype(BF16)
        r1 = a - hi.astype(F32)
        mid = r1.astype(BF16)
        lo = (r1 - mid.astype(F32)).astype(BF16)
        s = (jnp.dot(hi, upper, preferred_element_type=F32)
             + jnp.dot(mid, upper, preferred_element_type=F32)
             + jnp.dot(lo, upper, preferred_element_type=F32)) + carry
        o_ref[:, b * LANES:(b + 1) * LANES] = s
        carry = s[:, LANES - 1:LANES]


def _cumsum_lanes(a):
    return pl.pallas_call(
        _cumsum_kernel,
        out_shape=jax.ShapeDtypeStruct(a.shape, F32),
        compiler_params=pltpu.CompilerParams(vmem_limit_bytes=VMEM_LIMIT),
        name="cumsum",
    )(a)


def _cumsum_pairs_kernel(p_ref, n_ref, op_ref, on_ref, *, t_new):
    rows, length = p_ref.shape
    r = lax.broadcasted_iota(jnp.int32, (LANES, LANES), 0)
    c = lax.broadcasted_iota(jnp.int32, (LANES, LANES), 1)
    upper = ((r <= c) & (((r ^ c) & 1) == 0)).astype(BF16)
    unzip = ((r < 2 * t_new) & (c == (r & 1) * t_new + (r >> 1))).astype(BF16)
    lane_odd = (lax.broadcasted_iota(jnp.int32, (rows, LANES), 1) & 1) == 1

    def times(a, mat):
        hi, mid, lo = _split3(a)
        return (jnp.dot(hi.astype(BF16), mat, preferred_element_type=F32)
                + jnp.dot(mid.astype(BF16), mat, preferred_element_type=F32)
                + jnp.dot(lo.astype(BF16), mat, preferred_element_type=F32))

    carry = jnp.zeros((rows, LANES), F32)
    for b in range(length // LANES):
        s = times(p_ref[:, b * LANES:(b + 1) * LANES], upper) + carry
        op_ref[:, b * LANES:(b + 1) * LANES] = s
        carry = jnp.where(lane_odd, s[:, LANES - 1:LANES], s[:, LANES - 2:LANES - 1])
    on_ref[...] = times(times(n_ref[...], upper) + carry, unzip)


def _cumsum_pairs(p, n, *, t_new):
    return pl.pallas_call(
        functools.partial(_cumsum_pairs_kernel, t_new=t_new),
        out_shape=[jax.ShapeDtypeStruct(p.shape, F32), jax.ShapeDtypeStruct(n.shape, F32)],
        compiler_params=pltpu.CompilerParams(vmem_limit_bytes=VMEM_LIMIT),
        name="cumsum_pairs",
    )(p, n)


def _split3(x):
    hi = x.astype(BF16).astype(F32)
    r = x - hi
    mid = r.astype(BF16).astype(F32)
    lo = (r - mid).astype(BF16).astype(F32)
    return hi, mid, lo


def _fox_prompt_kernel(q_ref, k_ref, v_ref, cq_ref, ck_ref, *refs, tq, tk, n_cast):
    assert tq == 2 * tk
    cast_in, (o_ref, *cast_out), scratch = refs[:n_cast], refs[n_cast:2 * n_cast + 1], refs[2 * n_cast + 1:]
    kaug_ref, vt_ref, qaug0_ref, qaug1_ref, acc0_ref, acc1_ref, s0_ref, s1_ref = scratch
    for src_ref, dst_ref in zip(cast_in, cast_out):
        dst_ref[...] = src_ref[...].astype(BF16)
    qaug_refs, acc_refs = (qaug0_ref, qaug1_ref), (acc0_ref, acc1_ref)
    h = pl.program_id(1)
    seq = kaug_ref.shape[0]
    nk = seq // tk

    lane8 = lax.broadcasted_iota(jnp.int32, (tk, N_FOX_HEADS), 1)
    lane = lax.broadcasted_iota(jnp.int32, (tk, HEAD_DIM), 1)
    ones_row = (lax.broadcasted_iota(jnp.int32, (V_PAD_ROWS, tk), 0) == 0).astype(BF16)

    def prep(c):
        rows = slice(c * tk, (c + 1) * tk)
        kaug_ref[rows, 0:HEAD_DIM] = k_ref[rows, :]
        col = jnp.sum(jnp.where(lane8 == h, ck_ref[0, rows, :], 0.0), axis=1, keepdims=True) * LOG2_E
        hi, mid, lo = _split3(col)
        aug = jnp.where(lane == 0, hi, jnp.where(lane == 1, mid, jnp.where(lane == 2, lo,
                        jnp.where(lane < 6, 1.0, 0.0))))
        kaug_ref[rows, HEAD_DIM:2 * HEAD_DIM] = aug.astype(BF16)
        vt_ref[c, 0:HEAD_DIM, :] = v_ref[rows, :].astype(F32).T.astype(BF16)
        vt_ref[c, HEAD_DIM:, :] = ones_row

    sub = lax.broadcasted_iota(jnp.int32, (HEAD_DIM, tq), 0)

    def scores(qaug_ref, ki, dst_ref, lane0=0):
        dst_ref[:, lane0:] = jnp.dot(kaug_ref[ki * tk:(ki + 1) * tk, :], qaug_ref[:, lane0:],
                                     preferred_element_type=F32)

    def softmax_pv(acc_ref, ki, m_all, s_ref, lane0=0, mask_offset=None):
        m_prev = m_all[:, lane0:]
        s = s_ref[:, lane0:]
        if mask_offset is not None:
            r = lax.broadcasted_iota(jnp.int32, s.shape, 0) + mask_offset
            c = lax.broadcasted_iota(jnp.int32, s.shape, 1) + lane0
            s = jnp.where(r <= c, s, -jnp.inf)
        m_new = jnp.maximum(m_prev, jnp.max(s, axis=0, keepdims=True))
        alpha = jnp.exp2(m_prev - m_new)
        p = jnp.exp2(s - m_new).astype(BF16)
        pv = jnp.dot(vt_ref[ki], p, preferred_element_type=F32)
        acc_ref[:, lane0:] = alpha * acc_ref[:, lane0:] + pv
        return jnp.concatenate([m_all[:, :lane0], m_new], axis=1) if lane0 else m_new

    s_refs = (s0_ref, s1_ref)
    for qi in range(seq // tq):
        qaug_ref, acc_ref = qaug_refs[qi % 2], acc_refs[qi % 2]
        qrows = slice(qi * tq, (qi + 1) * tq)
        qaug_ref[0:HEAD_DIM, :] = q_ref[qrows, :].astype(F32).T.astype(BF16)
        hi, mid, lo = _split3(cq_ref[0, pl.ds(h, 1), qrows] * LOG2_E)
        qaug_ref[HEAD_DIM:2 * HEAD_DIM, :] = jnp.where(
            sub < 3, -1.0, jnp.where(sub == 3, hi, jnp.where(sub == 4, mid, jnp.where(sub == 5, lo, 0.0)))
        ).astype(BF16)
        acc_ref[...] = jnp.zeros(acc_ref.shape, F32)
        n_full = 2 * qi
        prep(n_full)
        prep(n_full + 1)
        m = jnp.full((1, tq), -jnp.inf, F32)
        scores(qaug_ref, 0, s_refs[0])
        for ki in range(n_full):
            scores(qaug_ref, ki + 1, s_refs[(ki + 1) % 2])
            m = softmax_pv(acc_ref, ki, m, s_refs[ki % 2])
        scores(qaug_ref, n_full + 1, s_refs[1], lane0=tk)
        m = softmax_pv(acc_ref, n_full, m, s_refs[0], mask_offset=0)
        softmax_pv(acc_ref, n_full + 1, m, s_refs[1], lane0=tk, mask_offset=tk)
        o_ref[qrows, :] = (acc_ref[0:HEAD_DIM, :] / acc_ref[HEAD_DIM:HEAD_DIM + 1, :]).T


def _cast_specs(cast, steps, step_of):
    specs = []
    for w, axis in cast:
        block = tuple(n // steps if a == axis else n for a, n in enumerate(w.shape))
        specs.append(pl.BlockSpec(block, (lambda *g: (step_of(*g), 0)) if axis == 0
                                  else (lambda *g: (0, step_of(*g)))))
    return specs


def _fox_prompt(qb, kb, vb, cq, ck, *, n_batch, seq, tq, tk, cast=()):
    m = qb.shape[0]
    head = lambda n, h: (n, h)
    cast_specs = _cast_specs(cast, n_batch * N_FOX_HEADS, lambda n, h: n * N_FOX_HEADS + h)
    return pl.pallas_call(
        functools.partial(_fox_prompt_kernel, tq=tq, tk=tk, n_cast=len(cast)),
        grid=(n_batch, N_FOX_HEADS),
        in_specs=[
            pl.BlockSpec((seq, HEAD_DIM), head),
            pl.BlockSpec((seq, HEAD_DIM), head),
            pl.BlockSpec((seq, HEAD_DIM), head),
            pl.BlockSpec((1, N_FOX_HEADS, seq), lambda n, h: (n, 0, 0)),
            pl.BlockSpec((1, seq, N_FOX_HEADS), lambda n, h: (n, 0, 0)),
        ] + cast_specs,
        out_specs=[pl.BlockSpec((seq, HEAD_DIM), head)] + cast_specs,
        out_shape=[jax.ShapeDtypeStruct((m, N_FOX_HEADS * HEAD_DIM), F32)]
        + [jax.ShapeDtypeStruct(w.shape, BF16) for w, _ in cast],
        scratch_shapes=[
            pltpu.VMEM((seq, 2 * HEAD_DIM), BF16),
            pltpu.VMEM((seq // tk, HEAD_DIM + V_PAD_ROWS, tk), BF16),
            pltpu.VMEM((2 * HEAD_DIM, tq), BF16),
            pltpu.VMEM((2 * HEAD_DIM, tq), BF16),
            pltpu.VMEM((HEAD_DIM + V_PAD_ROWS, tq), F32),
            pltpu.VMEM((HEAD_DIM + V_PAD_ROWS, tq), F32),
            pltpu.VMEM((tk, tq), F32),
            pltpu.VMEM((tk, tq), F32),
        ],
        compiler_params=pltpu.CompilerParams(dimension_semantics=("arbitrary", "arbitrary"),
                                             vmem_limit_bytes=BIG_VMEM_LIMIT),
        name="fox_prompt",
    )(qb, kb, vb, cq, ck, *[w for w, _ in cast])


def _fox_sample_kernel(q_ref, kn_ref, vn_ref, kc_ref, vc_ref, cq_ref, ckp_ref, ckn_ref, o_ref, *, past):
    scale = HEAD_DIM ** -0.5
    t = q_ref.shape[0]
    half = N_FOX_HEADS // 2
    r = lax.broadcasted_iota(jnp.int32, (2 * t, 2 * past), 0)
    c = lax.broadcasted_iota(jnp.int32, (2 * t, 2 * past), 1)
    own_past = (r >= t) == ((c & 1) == 1)
    rn = lax.broadcasted_iota(jnp.int32, (2 * t, 2 * t), 0)
    cn = lax.broadcasted_iota(jnp.int32, (2 * t, 2 * t), 1)
    own_new = ((rn >= t) == (cn >= t)) & ((cn & (t - 1)) <= (rn & (t - 1)))
    for j in range(half):
        sl0 = slice(j * HEAD_DIM, (j + 1) * HEAD_DIM)
        sl1 = slice((j + half) * HEAD_DIM, (j + half + 1) * HEAD_DIM)
        both = lambda ref: jnp.concatenate([ref[:, sl0], ref[:, sl1]], axis=0)
        q = both(q_ref)
        kp = kc_ref[0, pl.ds(j, 2 * past, stride=half), :].astype(BF16)
        vp = vc_ref[0, pl.ds(j, 2 * past, stride=half), :].astype(BF16)
        cq = jnp.concatenate([cq_ref[0, :, j:j + 1], cq_ref[0, :, j + half:j + half + 1]], axis=0)
        s1 = lax.dot_general(q, kp, _NT, preferred_element_type=F32) * scale
        s1 = jnp.where(own_past, s1 + (cq - ckp_ref[0, j:j + 1, :]), -jnp.inf)
        s2 = lax.dot_general(q, both(kn_ref), _NT, preferred_element_type=F32) * scale
        s2 = jnp.where(own_new, s2 + (cq - ckn_ref[0, j:j + 1, :]), -jnp.inf)
        m = jnp.maximum(jnp.max(s1, axis=1, keepdims=True), jnp.max(s2, axis=1, keepdims=True))
        p1 = jnp.exp(s1 - m)
        p2 = jnp.exp(s2 - m)
        l = jnp.sum(p1, axis=1, keepdims=True) + jnp.sum(p2, axis=1, keepdims=True)
        o = (jnp.dot(p1.astype(BF16), vp, preferred_element_type=F32)
             + jnp.dot(p2.astype(BF16), both(vn_ref), preferred_element_type=F32)) / l
        o_ref[:, sl0] = o[:t]
        o_ref[:, sl1] = o[t:]


def _fox_sample(qb, knb, vnb, cache_k, cache_v, cq, ck_past, ck_new):
    n_batch, rows, _ = cache_k.shape
    past = rows // N_FOX_HEADS
    t = qb.shape[0] // n_batch
    width = qb.shape[1]
    tok = lambda b: (b, 0)
    lead = lambda b: (b, 0, 0)
    return pl.pallas_call(
        functools.partial(_fox_sample_kernel, past=past),
        grid=(n_batch,),
        in_specs=[
            pl.BlockSpec((t, width), tok),
            pl.BlockSpec((t, width), tok),
            pl.BlockSpec((t, width), tok),
            pl.BlockSpec((1, rows, HEAD_DIM), lead),
            pl.BlockSpec((1, rows, HEAD_DIM), lead),
            pl.BlockSpec((1, t, N_FOX_HEADS), lead),
            pl.BlockSpec((1, N_FOX_HEADS // 2, 2 * past), lead),
            pl.BlockSpec((1, N_FOX_HEADS // 2, 2 * t), lead),
        ],
        out_specs=pl.BlockSpec((t, width), tok),
        out_shape=jax.ShapeDtypeStruct(qb.shape, F32),
        compiler_params=_params(("arbitrary",)),
        name="fox_sample",
    )(qb, knb, vnb, cache_k, cache_v, cq, ck_past, ck_new)


def _mix_kernel(*refs, seg, nseg, tiles_per_seq, has_prev, n_cast):
    if has_prev:
        (x_ref, gb_ref, gc_ref, hin_ref, pgc_ref, phin_ref, *refs) = refs
    else:
        (x_ref, gb_ref, gc_ref, hin_ref, *refs) = refs
    (buf_ref, ya_ref, cw_ref, gco_ref, gao_ref, wout_ref, gx_ref, wxq_ref, mk_ref, mv_ref, wxo_ref, gm_ref,
     *refs) = refs
    cast_in, (x2_ref, h_ref, cnew_ref, *cast_out, o_scr) = refs[:n_cast], refs[n_cast:]
    for src_ref, dst_ref in zip(cast_in, cast_out):
        dst_ref[...] = src_ref[...].astype(BF16)
    i = pl.program_id(0)
    u = gc_ref[...] * hin_ref[...]
    gb = gb_ref[...]
    w0 = cw_ref[0:1, :]
    w1 = cw_ref[1:2, :]
    w2 = cw_ref[2:3, :]
    rid = lax.broadcasted_iota(jnp.int32, (seg, u.shape[1]), 0)
    pieces = []
    for s in range(nseg):
        u_s = u[s * seg:(s + 1) * seg]
        b0 = buf_ref[s, 0:1, :]
        b1 = buf_ref[s, 1:2, :]
        if has_prev:
            pu = pgc_ref[...] * phin_ref[...]
            first = (i % tiles_per_seq) == 0
            b0 = jnp.where(first, b0, pu[6:7])
            b1 = jnp.where(first, b1, pu[7:8])
        r1 = jnp.where(rid == 0, b1, pltpu.roll(u_s, 1, 0))
        r2 = jnp.where(rid == 0, b0, jnp.where(rid == 1, b1, pltpu.roll(u_s, 2, 0)))
        y = w0 * r2 + w1 * r1 + w2 * u_s
        pieces.append(gb[s * seg:(s + 1) * seg] * y)
        cnew_ref[s] = u_s[seg - 2:seg]
    yc = pieces[0] if nseg == 1 else jnp.concatenate(pieces, axis=0)
    cat = jnp.concatenate([_rms(yc, gco_ref[...]).astype(BF16),
                           _rms(ya_ref[...], gao_ref[...]).astype(BF16)], axis=-1)
    x1 = x_ref[...] + jnp.dot(cat, wout_ref[...], preferred_element_type=F32)
    xn = _rms(x1, gx_ref[...]).astype(BF16)
    qx = jnp.dot(xn, wxq_ref[...], preferred_element_type=F32).astype(BF16)

    scale = HEAD_DIM ** -0.5
    n_mem = mk_ref.shape[1] // N_X_HEADS
    if nseg > 1:
        assert seg & (seg - 1) == 0 and n_mem & (n_mem - 1) == 0
        r = lax.broadcasted_iota(jnp.int32, (nseg * seg, nseg * n_mem), 0) >> (seg.bit_length() - 1)
        c = lax.broadcasted_iota(jnp.int32, (nseg * seg, nseg * n_mem), 1) >> (n_mem.bit_length() - 1)
        own = r == c
    for hh in range(N_X_HEADS):
        sl = slice(hh * HEAD_DIM, (hh + 1) * HEAD_DIM)
        head_rows = lambda ref: jnp.concatenate(
            [_load_head_rows(ref, s, hh, n_mem, N_X_HEADS).astype(BF16) for s in range(nseg)], axis=0)
        sc = lax.dot_general(qx[:, sl], head_rows(mk_ref), _NT, preferred_element_type=F32) * scale
        if nseg > 1:
            sc = jnp.where(own, sc, -jnp.inf)
        m = jnp.max(sc, axis=1, keepdims=True)
        p = jnp.exp(sc - m)
        l = jnp.sum(p, axis=1, keepdims=True)
        o = jnp.dot(p.astype(BF16), head_rows(mv_ref), preferred_element_type=F32) / l
        o_scr[:, sl] = o.astype(BF16)
    x2 = x1 + jnp.dot(o_scr[...], wxo_ref[...], preferred_element_type=F32)
    x2_ref[...] = x2
    h_ref[...] = _rms(x2, gm_ref[...]).astype(BF16)


def _mix(x, zc, conv_buf, ya, conv_w, g_conv_out, g_attn_out, w_out, g_xattn, w_xq, mk, mv, w_xo, g_mlp, *, tm, seq,
         cast=()):
    m, d = x.shape
    dc = zc.shape[1] // 3
    dx = w_xq.shape[1]
    mem_rows = mk.shape[1]
    has_prev = seq > tm
    seg = tm if has_prev else seq
    nseg = tm // seg
    tiles_per_seq = max(seq // tm, 1)
    row = lambda i: (i, 0)
    const = lambda i: (0, 0)
    resident = functools.partial(pl.BlockSpec, index_map=const, pipeline_mode=pl.Buffered(1))
    in_specs = [
        pl.BlockSpec((tm, d), row),
        pl.BlockSpec((tm, dc), lambda i: (i, 0)),
        pl.BlockSpec((tm, dc), lambda i: (i, 1)),
        pl.BlockSpec((tm, dc), lambda i: (i, 2)),
    ]
    args = [x, zc, zc, zc]
    if has_prev:
        prev = lambda col: (lambda i: (jnp.maximum(i * (tm // 8) - 1, 0), col))
        in_specs += [pl.BlockSpec((8, dc), prev(1)), pl.BlockSpec((8, dc), prev(2))]
        args += [zc, zc]
        buf_map = lambda i: (i // tiles_per_seq, 0, 0)
    else:
        buf_map = lambda i: (i, 0, 0)
    in_specs += [
        pl.BlockSpec((nseg, CONV_WIDTH - 1, dc), buf_map),
        pl.BlockSpec((tm, ya.shape[1]), row),
        pl.BlockSpec((CONV_WIDTH, dc), const),
        pl.BlockSpec((1, dc), const),
        pl.BlockSpec((1, ya.shape[1]), const),
        resident(w_out.shape),
        pl.BlockSpec((1, d), const),
        resident(w_xq.shape),
        pl.BlockSpec((nseg, mem_rows, HEAD_DIM), buf_map),
        pl.BlockSpec((nseg, mem_rows, HEAD_DIM), buf_map),
        resident(w_xo.shape),
        pl.BlockSpec((1, d), const),
    ]
    args += [conv_buf, ya, conv_w, g_conv_out, g_attn_out, w_out, g_xattn, w_xq, mk, mv, w_xo, g_mlp]
    steps = m // tm
    cast_specs = _cast_specs(cast, steps, lambda i: i)
    args += [w for w, _ in cast]
    return pl.pallas_call(
        functools.partial(_mix_kernel, seg=seg, nseg=nseg, tiles_per_seq=tiles_per_seq, has_prev=has_prev,
                          n_cast=len(cast)),
        grid=(steps,),
        in_specs=in_specs + cast_specs,
        out_specs=[
            pl.BlockSpec((tm, d), row),
            pl.BlockSpec((tm, d), row),
            pl.BlockSpec((nseg, CONV_WIDTH - 1, dc), buf_map),
        ] + cast_specs,
        out_shape=[
            jax.ShapeDtypeStruct((m, d), F32),
            jax.ShapeDtypeStruct((m, d), BF16),
            jax.ShapeDtypeStruct(conv_buf.shape, F32),
        ] + [jax.ShapeDtypeStruct(w.shape, BF16) for w, _ in cast],
        scratch_shapes=[pltpu.VMEM((tm, dx), BF16)],
        compiler_params=pltpu.CompilerParams(dimension_semantics=("arbitrary",), vmem_limit_bytes=BIG_VMEM_LIMIT),
        name="mix",
    )(*args)


def _memkv_kernel(x_ref, g_ref, w_ref, o_ref):
    xn = _rms(x_ref[...], g_ref[...]).astype(BF16)
    _store_head_rows(o_ref.at[0], jnp.dot(xn, w_ref[...], preferred_element_type=F32), N_X_HEADS)


def _memkv(mem, g, w):
    m, d = mem.shape
    half = w.shape[1] // 2
    return pl.pallas_call(
        _memkv_kernel,
        grid=(2,),
        in_specs=[
            pl.BlockSpec((m, d), lambda j: (0, 0)),
            pl.BlockSpec((1, d), lambda j: (0, 0)),
            pl.BlockSpec((d, half), lambda j: (0, j)),
        ],
        out_specs=pl.BlockSpec((1, m * N_X_HEADS, HEAD_DIM), lambda j: (j, 0, 0)),
        out_shape=jax.ShapeDtypeStruct((2, m * N_X_HEADS, HEAD_DIM), F32),
        compiler_params=_params(("arbitrary",)),
        name="memkv",
    )(mem, g, w)


def _mlp_kernel(h_ref, x2_hbm, wu_ref, wd_ref, g_ref, y_ref, sem):
    i = pl.program_id(0)
    j = pl.program_id(1)
    tm = y_ref.shape[0]
    x2_copy = pltpu.make_async_copy(x2_hbm.at[pl.ds(i * tm, tm), :], y_ref, sem)

    def hidden():
        a = jnp.dot(h_ref[...], wu_ref[...], preferred_element_type=F32)
        return jnp.square(jnp.maximum(a, 0.0)).astype(BF16)

    @pl.when(j == 0)
    def _():
        x2_copy.start()
        a = hidden()
        x2_copy.wait()
        y_ref[...] += jnp.dot(a, wd_ref[...], preferred_element_type=F32)

    @pl.when(j > 0)
    def _():
        y_ref[...] += jnp.dot(hidden(), wd_ref[...], preferred_element_type=F32)

    @pl.when(j == pl.num_programs(1) - 1)
    def _():
        y_ref[...] = _rms(y_ref[...], g_ref[...])


def _mlp(h, x2, w_up, w_down, g_final, *, tm, tf):
    m, d = h.shape
    ff = w_up.shape[1]
    row = lambda i, j: (i, 0)
    return pl.pallas_call(
        _mlp_kernel,
        grid=(m // tm, ff // tf),
        in_specs=[
            pl.BlockSpec((tm, d), row),
            pl.BlockSpec(memory_space=pl.ANY),
            pl.BlockSpec((d, tf), lambda i, j: (0, j)),
            pl.BlockSpec((tf, d), lambda i, j: (j, 0)),
            pl.BlockSpec((1, d), lambda i, j: (0, 0)),
        ],
        out_specs=pl.BlockSpec((tm, d), row),
        out_shape=jax.ShapeDtypeStruct((m, d), F32),
        scratch_shapes=[pltpu.SemaphoreType.DMA(())],
        compiler_params=pltpu.CompilerParams(dimension_semantics=("arbitrary", "arbitrary"),
                                             vmem_limit_bytes=BIG_VMEM_LIMIT),
        name="mlp",
    )(h, x2, w_up, w_down, g_final)


def kernel(x_prompt, x_sample, cache_k, cache_v, cache_logf, cache_conv, cache_mem_k, cache_mem_v, mem_prompt, g_mix, w_in, b_f, conv_w, g_conv_out, g_attn_out, w_out, g_xattn, g_mem, w_xq, w_xkv, w_xo, g_mlp, w_up, w_down, g_final):
    depth = w_in.shape[0]
    assert depth == 1, "single-layer trunk"
    nb, seq, d = x_prompt.shape
    db, dseq, _ = x_sample.shape
    past = cache_k.shape[2]
    heads, hd = cache_k.shape[3], cache_k.shape[4]
    assert (heads, hd) == (N_FOX_HEADS, HEAD_DIM)
    d_attn = heads * hd
    d_conv = cache_conv.shape[-1]
    n_mem, xh, xhd = cache_mem_k.shape[2:]
    assert (xh, xhd) == (N_X_HEADS, HEAD_DIM)
    d_x = xh * xhd
    n_main = 3 * d_conv + 3 * d_attn
    assert d_conv == d_attn and w_in.shape[2] == n_main + heads

    w_in_t = jnp.swapaxes(w_in[0], 0, 1)
    bf = b_f[0].reshape(1, heads)
    row = lambda g: g.reshape(1, -1)

    def project(x, q_scale=None):
        xn, lf = _xnorm(x, row(g_mix[0]), w_in_t, bf, tm=512, gate_row=n_main)
        proj = functools.partial(_proj, xn, w_in_t, tn=d_attn, tm=1024)
        zc, = proj(j0=0, nj=3, mode="f32")
        qb, = proj(j0=3, nj=1, mode="bf16", out_scale=q_scale)
        k, kb = proj(j0=4, nj=1, mode="heads")
        v, vb = proj(j0=5, nj=1, mode="heads")
        return zc, qb, k, kb, v, vb, lf

    def mix_mlp(x, zc, conv_buf, ya, seq_len, mk, mv, mix_w, mlp_w, tm):
        w_out_b, w_xq_b, w_xo_b = mix_w
        x2, h, conv_new = _mix(x, zc, conv_buf, ya, conv_w[0], row(g_conv_out[0]), row(g_attn_out[0]),
                               w_out_b, row(g_xattn[0]), w_xq_b, mk, mv, w_xo_b, row(g_mlp[0]), tm=tm, seq=seq_len)
        return _mlp(h, x2, *mlp_w, row(g_final), tm=1024, tf=1024), conv_new

    xp = x_prompt.reshape(nb * seq, d)
    zc, qb, k_p, kb, v_p, vb, lf_p = project(xp, q_scale=hd ** -0.5 * LOG2_E)
    lft = lf_p.reshape(nb, seq, heads).transpose(0, 2, 1).reshape(nb * heads, seq)
    ct = _cumsum_lanes(lft).reshape(nb, heads, seq)
    ya, *mix_w, w_xkv_b, w_up_b, w_down_b = _fox_prompt(
        qb, kb, vb, ct, ct.transpose(0, 2, 1), n_batch=nb, seq=seq, tq=1024, tk=512,
        cast=((w_out[0], 0), (w_xq[0], 0), (w_xo[0], 0), (w_xkv[0], 0), (w_up[0], 1), (w_down[0], 0)))
    mlp_w = (w_up_b, w_down_b)
    kv = _memkv(mem_prompt.reshape(nb * n_mem, d), row(g_mem[0]), w_xkv_b)
    mk_p = kv[0].reshape(nb, n_mem * xh, xhd)
    mv_p = kv[1].reshape(nb, n_mem * xh, xhd)
    conv0 = jnp.zeros((nb, CONV_WIDTH - 1, d_conv), F32)
    y_p, conv_p = mix_mlp(xp, zc, conv0, ya, seq, mk_p, mv_p, mix_w, mlp_w, tm=512)

    hp = heads // 2

    def sample_attention(qb, kb, vb, lf):
        lf_past = cache_logf[0].reshape(db, 2 * past, hp).transpose(0, 2, 1).reshape(db * hp, 2 * past)
        lf_new = lf.reshape(db, 2 * dseq, hp).transpose(0, 2, 1).reshape(db * hp, 2 * dseq)
        lf_new = jnp.pad(lf_new, ((0, 0), (0, LANES - 2 * dseq)))
        ck_past, ck_new = _cumsum_pairs(lf_past, lf_new, t_new=dseq)
        ck_new = ck_new[:, :2 * dseq].reshape(db, hp, 2 * dseq)
        cq = ck_new.reshape(db, hp, 2, dseq).transpose(0, 3, 2, 1).reshape(db, dseq, heads)
        return _fox_sample(qb, kb, vb, cache_k.reshape(db, past * heads, hd),
                           cache_v.reshape(db, past * heads, hd), cq, ck_past.reshape(db, hp, 2 * past), ck_new)

    xs = x_sample.reshape(db * dseq, d)
    zc, qb, k_s, kb, v_s, vb, lf_s = project(xs)
    ya = sample_attention(qb, kb, vb, lf_s)
    y_s, conv_s = mix_mlp(xs, zc, cache_conv[0], ya, dseq, cache_mem_k.reshape(db, n_mem * xh, xhd),
                          cache_mem_v.reshape(db, n_mem * xh, xhd), mix_w, mlp_w, tm=256)

    return (y_p.reshape(nb, seq, d),
            y_s.reshape(db, dseq, d),
            k_p.reshape(1, nb, seq, heads, hd),
            v_p.reshape(1, nb, seq, heads, hd),
            lf_p.reshape(1, nb, seq, heads),
            conv_p[None],
            mk_p.reshape(1, nb, n_mem, xh, xhd),
            mv_p.reshape(1, nb, n_mem, xh, xhd),
            k_s.reshape(1, db, dseq, heads, hd),
            v_s.reshape(1, db, dseq, heads, hd),
            lf_s.reshape(1, db, dseq, heads),
            conv_s[None])
```

```python
import functools

import jax
import jax.numpy as jnp
from jax import lax
from jax.experimental import pallas as pl
from jax.experimental.pallas import tpu as pltpu

F32 = jnp.float32
BF16 = jnp.bfloat16

RMS_EPS = 1e-6
HEAD_DIM = 128
N_FOX_HEADS = 8
N_X_HEADS = 4
CONV_WIDTH = 3
LANES = 128
V_PAD_ROWS = 16
LOG2_E = 1.4426950408889634
VMEM_LIMIT = 52 * 1024 * 1024
BIG_VMEM_LIMIT = 58 * 1024 * 1024

_NT = (((1,), (1,)), ((), ()))


def _params(semantics):
    return pltpu.CompilerParams(dimension_semantics=semantics, vmem_limit_bytes=VMEM_LIMIT)


def _rms(x, g):
    ms = jnp.mean(x * x, axis=-1, keepdims=True)
    return x * lax.rsqrt(ms + RMS_EPS) * g


def _log_sigmoid(x):
    return jnp.minimum(x, 0.0) - jnp.log1p(jnp.exp(-jnp.abs(x)))


def _store_head_rows(ref, val, heads):
    n = val.shape[0]
    for h in range(heads):
        ref[pl.ds(h, n, stride=heads), :] = val[:, h * HEAD_DIM:(h + 1) * HEAD_DIM]


def _load_head_rows(ref, lead, h, n, heads):
    return ref[lead, pl.ds(h, n, stride=heads), :]


def _xnorm_kernel(xa_ref, xb_ref, g_ref, wf_ref, bf_ref, xn_ref, lf_ref, *, tiles_a):
    x = jnp.where(pl.program_id(0) < tiles_a, xa_ref[...], xb_ref[...])
    xn = _rms(x, g_ref[...]).astype(BF16)
    xn_ref[...] = xn
    wf = wf_ref[...].astype(BF16)
    wf = jnp.concatenate([wf, jnp.zeros((LANES - wf.shape[0], wf.shape[1]), BF16)], axis=0)
    fz = lax.dot_general(xn, wf, _NT, preferred_element_type=F32)
    lf_ref[...] = _log_sigmoid(fz[:, :N_FOX_HEADS] + bf_ref[...])


def _xnorm(xa, xb, g, w_t, b_f, *, tm, gate_row):
    d = xa.shape[1]
    tiles_a, tiles_b = xa.shape[0] // tm, xb.shape[0] // tm
    m = xa.shape[0] + xb.shape[0]
    row = lambda i: (i, 0)
    return pl.pallas_call(
        functools.partial(_xnorm_kernel, tiles_a=tiles_a),
        grid=(tiles_a + tiles_b,),
        in_specs=[
            pl.BlockSpec((tm, d), lambda i: (jnp.minimum(i, tiles_a - 1), 0)),
            pl.BlockSpec((tm, d), lambda i: (jnp.maximum(i - tiles_a, 0), 0)),
            pl.BlockSpec((1, d), lambda i: (0, 0)),
            pl.BlockSpec((N_FOX_HEADS, d), lambda i: (gate_row // N_FOX_HEADS, 0)),
            pl.BlockSpec((1, N_FOX_HEADS), lambda i: (0, 0)),
        ],
        out_specs=[pl.BlockSpec((tm, d), row), pl.BlockSpec((tm, N_FOX_HEADS), row)],
        out_shape=[jax.ShapeDtypeStruct((m, d), BF16), jax.ShapeDtypeStruct((m, N_FOX_HEADS), F32)],
        compiler_params=_params(("arbitrary",)),
        name="xnorm",
    )(xa, xb, g, w_t, b_f)


def _proj_kernel(xn_ref, wt_ref, *refs, mode, out_scale, tiles_a):
    out_refs, wb_ref = refs[:-1], refs[-1]
    i = pl.program_id(1)

    @pl.when(i == 0)
    def _():
        wb_ref[...] = wt_ref[...].astype(BF16)

    def matmul():
        return lax.dot_general(xn_ref[...], wb_ref[...], _NT, preferred_element_type=F32)

    if mode == "f32":
        out_refs[0][...] = matmul()
    elif mode == "bf16":
        scale = None if out_scale is None else jnp.where(i < tiles_a, out_scale, 1.0)
        res = matmul()
        out_refs[0][...] = (res if scale is None else res * scale).astype(BF16)
    else:
        def to(head_ref):
            res = matmul()
            _store_head_rows(head_ref, res, N_FOX_HEADS)
            out_refs[2][...] = res.astype(BF16)

        pl.when(i < tiles_a)(lambda: to(out_refs[0]))
        pl.when(i >= tiles_a)(lambda: to(out_refs[1]))


def _proj(xn, w_t, *, j0, nj, tn, tm, mode, tiles_a, out_scale=None):
    m, d = xn.shape
    tiles_b = m // tm - tiles_a
    if mode == "heads":
        assert nj == 1
        head_block = (tm * N_FOX_HEADS, HEAD_DIM)
        out_specs = [pl.BlockSpec(head_block, lambda j, i: (jnp.minimum(i, tiles_a - 1), 0)),
                     pl.BlockSpec(head_block, lambda j, i: (jnp.maximum(i - tiles_a, 0), 0)),
                     pl.BlockSpec((tm, tn), lambda j, i: (i, 0))]
        out_shape = [jax.ShapeDtypeStruct((tiles_a * tm * N_FOX_HEADS, HEAD_DIM), F32),
                     jax.ShapeDtypeStruct((tiles_b * tm * N_FOX_HEADS, HEAD_DIM), F32),
                     jax.ShapeDtypeStruct((m, tn), BF16)]
    else:
        out_specs = [pl.BlockSpec((tm, tn), lambda j, i: (i, j))]
        out_shape = [jax.ShapeDtypeStruct((m, nj * tn), F32 if mode == "f32" else BF16)]
    return pl.pallas_call(
        functools.partial(_proj_kernel, mode=mode, out_scale=out_scale, tiles_a=tiles_a),
        grid=(nj, m // tm),
        in_specs=[
            pl.BlockSpec((tm, d), lambda j, i: (i, 0)),
            pl.BlockSpec((tn, d), lambda j, i: (j0 + j, 0)),
        ],
        out_specs=out_specs,
        out_shape=out_shape,
        scratch_shapes=[pltpu.VMEM((tn, d), BF16)],
        compiler_params=_params(("arbitrary", "arbitrary")),
        name="proj_" + mode,
    )(xn, w_t)


def _cumsum_kernel(a_ref, o_ref):
    rows, length = a_ref.shape
    r = lax.broadcasted_iota(jnp.int32, (LANES, LANES), 0)
    c = lax.broadcasted_iota(jnp.int32, (LANES, LANES), 1)
    upper = (r <= c).astype(BF16)
    carry = jnp.zeros((rows, 1), F32)
    for b in range(length // LANES):
        a = a_ref[:, b * LANES:(b + 1) * LANES]
        hi = a.astype(BF16)
        r1 = a - hi.astype(F32)
        mid = r1.astype(BF16)
        lo = (r1 - mid.astype(F32)).astype(BF16)
        s = (jnp.dot(hi, upper, preferred_element_type=F32)
             + jnp.dot(mid, upper, preferred_element_type=F32)
             + jnp.dot(lo, upper, preferred_element_type=F32)) + carry
        o_ref[:, b * LANES:(b + 1) * LANES] = s
        carry = s[:, LANES - 1:LANES]


def _cumsum_lanes(a):
    return pl.pallas_call(
        _cumsum_kernel,
        out_shape=jax.ShapeDtypeStruct(a.shape, F32),
        compiler_params=pltpu.CompilerParams(vmem_limit_bytes=VMEM_LIMIT),
        name="cumsum",
    )(a)


def _cumsum_pairs_kernel(p_ref, n_ref, op_ref, on_ref, *, t_new):
    rows, length = p_ref.shape
    r = lax.broadcasted_iota(jnp.int32, (LANES, LANES), 0)
    c = lax.broadcasted_iota(jnp.int32, (LANES, LANES), 1)
    upper = ((r <= c) & (((r ^ c) & 1) == 0)).astype(BF16)
    unzip = ((r < 2 * t_new) & (c == (r & 1) * t_new + (r >> 1))).astype(BF16)
    lane_odd = (lax.broadcasted_iota(jnp.int32, (rows, LANES), 1) & 1) == 1

    def times(a, mat):
        hi, mid, lo = _split3(a)
        return (jnp.dot(hi.astype(BF16), mat, preferred_element_type=F32)
                + jnp.dot(mid.astype(BF16), mat, preferred_element_type=F32)
                + jnp.dot(lo.astype(BF16), mat, preferred_element_type=F32))

    carry = jnp.zeros((rows, LANES), F32)
    for b in range(length // LANES):
        s = times(p_ref[:, b * LANES:(b + 1) * LANES], upper) + carry
        op_ref[:, b * LANES:(b + 1) * LANES] = s
        carry = jnp.where(lane_odd, s[:, LANES - 1:LANES], s[:, LANES - 2:LANES - 1])
    on_ref[...] = times(times(n_ref[...], upper) + carry, unzip)


def _cumsum_pairs(p, n, *, t_new):
    return pl.pallas_call(
        functools.partial(_cumsum_pairs_kernel, t_new=t_new),
        out_shape=[jax.ShapeDtypeStruct(p.shape, F32), jax.ShapeDtypeStruct(n.shape, F32)],
        compiler_params=pltpu.CompilerParams(vmem_limit_bytes=VMEM_LIMIT),
        name="cumsum_pairs",
    )(p, n)


def _split3(x):
    hi = x.astype(BF16).astype(F32)
    r = x - hi
    mid = r.astype(BF16).astype(F32)
    lo = (r - mid).astype(BF16).astype(F32)
    return hi, mid, lo


def _fox_prompt_kernel(q_ref, k_ref, v_ref, cq_ref, ck_ref, *refs, tq, tk, n_cast):
    assert tq == 2 * tk
    cast_in, (o_ref, *cast_out), scratch = refs[:n_cast], refs[n_cast:2 * n_cast + 1], refs[2 * n_cast + 1:]
    kaug_ref, vt_ref, qaug0_ref, qaug1_ref, acc0_ref, acc1_ref, s0_ref, s1_ref = scratch
    for src_ref, dst_ref in zip(cast_in, cast_out):
        dst_ref[...] = src_ref[...].astype(BF16)
    qaug_refs, acc_refs = (qaug0_ref, qaug1_ref), (acc0_ref, acc1_ref)
    h = pl.program_id(1)
    seq = kaug_ref.shape[0]
    nk = seq // tk

    lane8 = lax.broadcasted_iota(jnp.int32, (tk, N_FOX_HEADS), 1)
    lane = lax.broadcasted_iota(jnp.int32, (tk, HEAD_DIM), 1)
    ones_row = (lax.broadcasted_iota(jnp.int32, (V_PAD_ROWS, tk), 0) == 0).astype(BF16)

    def prep(c):
        rows = slice(c * tk, (c + 1) * tk)
        kaug_ref[rows, 0:HEAD_DIM] = k_ref[rows, :]
        col = jnp.sum(jnp.where(lane8 == h, ck_ref[0, rows, :], 0.0), axis=1, keepdims=True) * LOG2_E
        hi, mid, lo = _split3(col)
        aug = jnp.where(lane == 0, hi, jnp.where(lane == 1, mid, jnp.where(lane == 2, lo,
                        jnp.where(lane < 6, 1.0, 0.0))))
        kaug_ref[rows, HEAD_DIM:2 * HEAD_DIM] = aug.astype(BF16)
        vt_ref[c, 0:HEAD_DIM, :] = v_ref[rows, :].astype(F32).T.astype(BF16)
        vt_ref[c, HEAD_DIM:, :] = ones_row

    sub = lax.broadcasted_iota(jnp.int32, (HEAD_DIM, tq), 0)

    def scores(qaug_ref, ki, dst_ref, lane0=0):
        dst_ref[:, lane0:] = jnp.dot(kaug_ref[ki * tk:(ki + 1) * tk, :], qaug_ref[:, lane0:],
                                     preferred_element_type=F32)

    def softmax_pv(acc_ref, ki, m_all, s_ref, lane0=0, mask_offset=None):
        m_prev = m_all[:, lane0:]
        s = s_ref[:, lane0:]
        if mask_offset is not None:
            r = lax.broadcasted_iota(jnp.int32, s.shape, 0) + mask_offset
            c = lax.broadcasted_iota(jnp.int32, s.shape, 1) + lane0
            s = jnp.where(r <= c, s, -jnp.inf)
        m_new = jnp.maximum(m_prev, jnp.max(s, axis=0, keepdims=True))
        alpha = jnp.exp2(m_prev - m_new)
        p = jnp.exp2(s - m_new).astype(BF16)
        pv = jnp.dot(vt_ref[ki], p, preferred_element_type=F32)
        acc_ref[:, lane0:] = alpha * acc_ref[:, lane0:] + pv
        return jnp.concatenate([m_all[:, :lane0], m_new], axis=1) if lane0 else m_new

    s_refs = (s0_ref, s1_ref)
    for qi in range(seq // tq):
        qaug_ref, acc_ref = qaug_refs[qi % 2], acc_refs[qi % 2]
        qrows = slice(qi * tq, (qi + 1) * tq)
        qaug_ref[0:HEAD_DIM, :] = q_ref[qrows, :].astype(F32).T.astype(BF16)
        hi, mid, lo = _split3(cq_ref[0, pl.ds(h, 1), qrows] * LOG2_E)
        qaug_ref[HEAD_DIM:2 * HEAD_DIM, :] = jnp.where(
            sub < 3, -1.0, jnp.where(sub == 3, hi, jnp.where(sub == 4, mid, jnp.where(sub == 5, lo, 0.0)))
        ).astype(BF16)
        acc_ref[...] = jnp.zeros(acc_ref.shape, F32)
        n_full = 2 * qi
        prep(n_full)
        prep(n_full + 1)
        m = jnp.full((1, tq), -jnp.inf, F32)
        scores(qaug_ref, 0, s_refs[0])
        for ki in range(n_full):
            scores(qaug_ref, ki + 1, s_refs[(ki + 1) % 2])
            m = softmax_pv(acc_ref, ki, m, s_refs[ki % 2])
        scores(qaug_ref, n_full + 1, s_refs[1], lane0=tk)
        m = softmax_pv(acc_ref, n_full, m, s_refs[0], mask_offset=0)
        softmax_pv(acc_ref, n_full + 1, m, s_refs[1], lane0=tk, mask_offset=tk)
        o_ref[qrows, :] = (acc_ref[0:HEAD_DIM, :] / acc_ref[HEAD_DIM:HEAD_DIM + 1, :]).T


def _cast_specs(cast, steps, step_of):
    specs = []
    for w, axis in cast:
        block = tuple(n // steps if a == axis else n for a, n in enumerate(w.shape))
        specs.append(pl.BlockSpec(block, (lambda *g: (step_of(*g), 0)) if axis == 0
                                  else (lambda *g: (0, step_of(*g)))))
    return specs


def _fox_prompt(qb, kb, vb, cq, ck, *, n_batch, seq, tq, tk, cast=()):
    m = n_batch * seq
    head = lambda n, h: (n, h)
    cast_specs = _cast_specs(cast, n_batch * N_FOX_HEADS, lambda n, h: n * N_FOX_HEADS + h)
    return pl.pallas_call(
        functools.partial(_fox_prompt_kernel, tq=tq, tk=tk, n_cast=len(cast)),
        grid=(n_batch, N_FOX_HEADS),
        in_specs=[
            pl.BlockSpec((seq, HEAD_DIM), head),
            pl.BlockSpec((seq, HEAD_DIM), head),
            pl.BlockSpec((seq, HEAD_DIM), head),
            pl.BlockSpec((1, N_FOX_HEADS, seq), lambda n, h: (n, 0, 0)),
            pl.BlockSpec((1, seq, N_FOX_HEADS), lambda n, h: (n, 0, 0)),
        ] + cast_specs,
        out_specs=[pl.BlockSpec((seq, HEAD_DIM), head)] + cast_specs,
        out_shape=[jax.ShapeDtypeStruct((m, N_FOX_HEADS * HEAD_DIM), F32)]
        + [jax.ShapeDtypeStruct(w.shape, BF16) for w, _ in cast],
        scratch_shapes=[
            pltpu.VMEM((seq, 2 * HEAD_DIM), BF16),
            pltpu.VMEM((seq // tk, HEAD_DIM + V_PAD_ROWS, tk), BF16),
            pltpu.VMEM((2 * HEAD_DIM, tq), BF16),
            pltpu.VMEM((2 * HEAD_DIM, tq), BF16),
            pltpu.VMEM((HEAD_DIM + V_PAD_ROWS, tq), F32),
            pltpu.VMEM((HEAD_DIM + V_PAD_ROWS, tq), F32),
            pltpu.VMEM((tk, tq), F32),
            pltpu.VMEM((tk, tq), F32),
        ],
        compiler_params=pltpu.CompilerParams(dimension_semantics=("arbitrary", "arbitrary"),
                                             vmem_limit_bytes=BIG_VMEM_LIMIT),
        name="fox_prompt",
    )(qb, kb, vb, cq, ck, *[w for w, _ in cast])


def _fox_sample_kernel(q_ref, kn_ref, vn_ref, kc_ref, vc_ref, cq_ref, ckp_ref, ckn_ref, o_ref, *, past):
    scale = HEAD_DIM ** -0.5
    t = q_ref.shape[0]
    half = N_FOX_HEADS // 2
    r = lax.broadcasted_iota(jnp.int32, (2 * t, 2 * past), 0)
    c = lax.broadcasted_iota(jnp.int32, (2 * t, 2 * past), 1)
    own_past = (r >= t) == ((c & 1) == 1)
    rn = lax.broadcasted_iota(jnp.int32, (2 * t, 2 * t), 0)
    cn = lax.broadcasted_iota(jnp.int32, (2 * t, 2 * t), 1)
    own_new = ((rn >= t) == (cn >= t)) & ((cn & (t - 1)) <= (rn & (t - 1)))
    for j in range(half):
        sl0 = slice(j * HEAD_DIM, (j + 1) * HEAD_DIM)
        sl1 = slice((j + half) * HEAD_DIM, (j + half + 1) * HEAD_DIM)
        both = lambda ref: jnp.concatenate([ref[:, sl0], ref[:, sl1]], axis=0)
        q = both(q_ref)
        kp = kc_ref[0, pl.ds(j, 2 * past, stride=half), :].astype(BF16)
        vp = vc_ref[0, pl.ds(j, 2 * past, stride=half), :].astype(BF16)
        cq = jnp.concatenate([cq_ref[0, :, j:j + 1], cq_ref[0, :, j + half:j + half + 1]], axis=0)
        s1 = lax.dot_general(q, kp, _NT, preferred_element_type=F32) * scale
        s1 = jnp.where(own_past, s1 + (cq - ckp_ref[0, j:j + 1, :]), -jnp.inf)
        s2 = lax.dot_general(q, both(kn_ref), _NT, preferred_element_type=F32) * scale
        s2 = jnp.where(own_new, s2 + (cq - ckn_ref[0, j:j + 1, :]), -jnp.inf)
        m = jnp.maximum(jnp.max(s1, axis=1, keepdims=True), jnp.max(s2, axis=1, keepdims=True))
        p1 = jnp.exp(s1 - m)
        p2 = jnp.exp(s2 - m)
        l = jnp.sum(p1, axis=1, keepdims=True) + jnp.sum(p2, axis=1, keepdims=True)
        o = (jnp.dot(p1.astype(BF16), vp, preferred_element_type=F32)
             + jnp.dot(p2.astype(BF16), both(vn_ref), preferred_element_type=F32)) / l
        o_ref[:, sl0] = o[:t]
        o_ref[:, sl1] = o[t:]


def _fox_sample(qb, knb, vnb, cache_k, cache_v, cq, ck_past, ck_new, *, t, row0):
    n_batch, rows, _ = cache_k.shape
    past = rows // N_FOX_HEADS
    width = qb.shape[1]
    tok = lambda b: (b, 0)
    tok_in = lambda b: (b + row0 // t, 0)
    lead = lambda b: (b, 0, 0)
    return pl.pallas_call(
        functools.partial(_fox_sample_kernel, past=past),
        grid=(n_batch,),
        in_specs=[
            pl.BlockSpec((t, width), tok_in),
            pl.BlockSpec((t, width), tok_in),
            pl.BlockSpec((t, width), tok_in),
            pl.BlockSpec((1, rows, HEAD_DIM), lead),
            pl.BlockSpec((1, rows, HEAD_DIM), lead),
            pl.BlockSpec((1, t, N_FOX_HEADS), lead),
            pl.BlockSpec((1, N_FOX_HEADS // 2, 2 * past), lead),
            pl.BlockSpec((1, N_FOX_HEADS // 2, 2 * t), lead),
        ],
        out_specs=pl.BlockSpec((t, width), tok),
        out_shape=jax.ShapeDtypeStruct((n_batch * t, width), F32),
        compiler_params=_params(("arbitrary",)),
        name="fox_sample",
    )(qb, knb, vnb, cache_k, cache_v, cq, ck_past, ck_new)


def _mix_kernel(*refs, seg, nseg, tiles_per_seq, has_prev, n_cast):
    if has_prev:
        (x_ref, gb_ref, gc_ref, hin_ref, pgc_ref, phin_ref, *refs) = refs
    else:
        (x_ref, gb_ref, gc_ref, hin_ref, *refs) = refs
    (buf_ref, ya_ref, cw_ref, gco_ref, gao_ref, wout_ref, gx_ref, wxq_ref, mk_ref, mv_ref, wxo_ref, gm_ref,
     *refs) = refs
    cast_in, (x2_ref, h_ref, cnew_ref, *cast_out, o_scr) = refs[:n_cast], refs[n_cast:]
    for src_ref, dst_ref in zip(cast_in, cast_out):
        dst_ref[...] = src_ref[...].astype(BF16)
    i = pl.program_id(0)
    u = gc_ref[...] * hin_ref[...]
    gb = gb_ref[...]
    w0 = cw_ref[0:1, :]
    w1 = cw_ref[1:2, :]
    w2 = cw_ref[2:3, :]
    rid = lax.broadcasted_iota(jnp.int32, (seg, u.shape[1]), 0)
    pieces = []
    for s in range(nseg):
        u_s = u[s * seg:(s + 1) * seg]
        b0 = buf_ref[s, 0:1, :]
        b1 = buf_ref[s, 1:2, :]
        if has_prev:
            pu = pgc_ref[...] * phin_ref[...]
            first = (i % tiles_per_seq) == 0
            b0 = jnp.where(first, b0, pu[6:7])
            b1 = jnp.where(first, b1, pu[7:8])
        r1 = jnp.where(rid == 0, b1, pltpu.roll(u_s, 1, 0))
        r2 = jnp.where(rid == 0, b0, jnp.where(rid == 1, b1, pltpu.roll(u_s, 2, 0)))
        y = w0 * r2 + w1 * r1 + w2 * u_s
        pieces.append(gb[s * seg:(s + 1) * seg] * y)
        cnew_ref[s] = u_s[seg - 2:seg]
    yc = pieces[0] if nseg == 1 else jnp.concatenate(pieces, axis=0)
    cat = jnp.concatenate([_rms(yc, gco_ref[...]).astype(BF16),
                           _rms(ya_ref[...], gao_ref[...]).astype(BF16)], axis=-1)
    x1 = x_ref[...] + jnp.dot(cat, wout_ref[...], preferred_element_type=F32)
    xn = _rms(x1, gx_ref[...]).astype(BF16)
    qx = jnp.dot(xn, wxq_ref[...], preferred_element_type=F32).astype(BF16)

    scale = HEAD_DIM ** -0.5
    n_mem = mk_ref.shape[1] // N_X_HEADS
    if nseg > 1:
        assert seg & (seg - 1) == 0 and n_mem & (n_mem - 1) == 0
        r = lax.broadcasted_iota(jnp.int32, (nseg * seg, nseg * n_mem), 0) >> (seg.bit_length() - 1)
        c = lax.broadcasted_iota(jnp.int32, (nseg * seg, nseg * n_mem), 1) >> (n_mem.bit_length() - 1)
        own = r == c
    for hh in range(N_X_HEADS):
        sl = slice(hh * HEAD_DIM, (hh + 1) * HEAD_DIM)
        head_rows = lambda ref: jnp.concatenate(
            [_load_head_rows(ref, s, hh, n_mem, N_X_HEADS).astype(BF16) for s in range(nseg)], axis=0)
        sc = lax.dot_general(qx[:, sl], head_rows(mk_ref), _NT, preferred_element_type=F32) * scale
        if nseg > 1:
            sc = jnp.where(own, sc, -jnp.inf)
        m = jnp.max(sc, axis=1, keepdims=True)
        p = jnp.exp(sc - m)
        l = jnp.sum(p, axis=1, keepdims=True)
        o = jnp.dot(p.astype(BF16), head_rows(mv_ref), preferred_element_type=F32) / l
        o_scr[:, sl] = o.astype(BF16)
    x2 = x1 + jnp.dot(o_scr[...], wxo_ref[...], preferred_element_type=F32)
    x2_ref[...] = x2
    h_ref[...] = _rms(x2, gm_ref[...]).astype(BF16)


def _mix(x, zc, conv_buf, ya, conv_w, g_conv_out, g_attn_out, w_out, g_xattn, w_xq, mk, mv, w_xo, g_mlp, *, tm, seq,
         zc_row0=0, cast=()):
    m, d = x.shape
    dc = zc.shape[1] // 3
    dx = w_xq.shape[1]
    mem_rows = mk.shape[1]
    has_prev = seq > tm
    seg = tm if has_prev else seq
    nseg = tm // seg
    tiles_per_seq = max(seq // tm, 1)
    t0 = zc_row0 // tm
    row = lambda i: (i, 0)
    const = lambda i: (0, 0)
    resident = functools.partial(pl.BlockSpec, index_map=const, pipeline_mode=pl.Buffered(1))
    in_specs = [
        pl.BlockSpec((tm, d), row),
        pl.BlockSpec((tm, dc), lambda i: (t0 + i, 0)),
        pl.BlockSpec((tm, dc), lambda i: (t0 + i, 1)),
        pl.BlockSpec((tm, dc), lambda i: (t0 + i, 2)),
    ]
    args = [x, zc, zc, zc]
    if has_prev:
        assert zc_row0 == 0
        prev = lambda col: (lambda i: (jnp.maximum(i * (tm // 8) - 1, 0), col))
        in_specs += [pl.BlockSpec((8, dc), prev(1)), pl.BlockSpec((8, dc), prev(2))]
        args += [zc, zc]
        buf_map = lambda i: (i // tiles_per_seq, 0, 0)
    else:
        buf_map = lambda i: (i, 0, 0)
    in_specs += [
        pl.BlockSpec((nseg, CONV_WIDTH - 1, dc), buf_map),
        pl.BlockSpec((tm, ya.shape[1]), row),
        pl.BlockSpec((CONV_WIDTH, dc), const),
        pl.BlockSpec((1, dc), const),
        pl.BlockSpec((1, ya.shape[1]), const),
        resident(w_out.shape),
        pl.BlockSpec((1, d), const),
        resident(w_xq.shape),
        pl.BlockSpec((nseg, mem_rows, HEAD_DIM), buf_map),
        pl.BlockSpec((nseg, mem_rows, HEAD_DIM), buf_map),
        resident(w_xo.shape),
        pl.BlockSpec((1, d), const),
    ]
    args += [conv_buf, ya, conv_w, g_conv_out, g_attn_out, w_out, g_xattn, w_xq, mk, mv, w_xo, g_mlp]
    steps = m // tm
    cast_specs = _cast_specs(cast, steps, lambda i: i)
    args += [w for w, _ in cast]
    return pl.pallas_call(
        functools.partial(_mix_kernel, seg=seg, nseg=nseg, tiles_per_seq=tiles_per_seq, has_prev=has_prev,
                          n_cast=len(cast)),
        grid=(steps,),
        in_specs=in_specs + cast_specs,
        out_specs=[
            pl.BlockSpec((tm, d), row),
            pl.BlockSpec((tm, d), row),
            pl.BlockSpec((nseg, CONV_WIDTH - 1, dc), buf_map),
        ] + cast_specs,
        out_shape=[
            jax.ShapeDtypeStruct((m, d), F32),
            jax.ShapeDtypeStruct((m, d), BF16),
            jax.ShapeDtypeStruct(conv_buf.shape, F32),
        ] + [jax.ShapeDtypeStruct(w.shape, BF16) for w, _ in cast],
        scratch_shapes=[pltpu.VMEM((tm, dx), BF16)],
        compiler_params=pltpu.CompilerParams(dimension_semantics=("arbitrary",), vmem_limit_bytes=BIG_VMEM_LIMIT),
        name="mix",
    )(*args)


def _memkv_kernel(x_ref, g_ref, w_ref, o_ref):
    xn = _rms(x_ref[...], g_ref[...]).astype(BF16)
    _store_head_rows(o_ref.at[0], jnp.dot(xn, w_ref[...], preferred_element_type=F32), N_X_HEADS)


def _memkv(mem, g, w):
    m, d = mem.shape
    half = w.shape[1] // 2
    return pl.pallas_call(
        _memkv_kernel,
        grid=(2,),
        in_specs=[
            pl.BlockSpec((m, d), lambda j: (0, 0)),
            pl.BlockSpec((1, d), lambda j: (0, 0)),
            pl.BlockSpec((d, half), lambda j: (0, j)),
        ],
        out_specs=pl.BlockSpec((1, m * N_X_HEADS, HEAD_DIM), lambda j: (j, 0, 0)),
        out_shape=jax.ShapeDtypeStruct((2, m * N_X_HEADS, HEAD_DIM), F32),
        compiler_params=_params(("arbitrary",)),
        name="memkv",
    )(mem, g, w)


def _mlp_kernel(h_ref, x2_hbm, wu_ref, wd_ref, g_ref, y_ref, sem):
    i = pl.program_id(0)
    j = pl.program_id(1)
    tm = y_ref.shape[0]
    x2_copy = pltpu.make_async_copy(x2_hbm.at[pl.ds(i * tm, tm), :], y_ref, sem)

    def hidden():
        a = jnp.dot(h_ref[...], wu_ref[...], preferred_element_type=F32)
        return jnp.square(jnp.maximum(a, 0.0)).astype(BF16)

    @pl.when(j == 0)
    def _():
        x2_copy.start()
        a = hidden()
        x2_copy.wait()
        y_ref[...] += jnp.dot(a, wd_ref[...], preferred_element_type=F32)

    @pl.when(j > 0)
    def _():
        y_ref[...] += jnp.dot(hidden(), wd_ref[...], preferred_element_type=F32)

    @pl.when(j == pl.num_programs(1) - 1)
    def _():
        y_ref[...] = _rms(y_ref[...], g_ref[...])


def _mlp(h, x2, w_up, w_down, g_final, *, tm, tf):
    m, d = h.shape
    ff = w_up.shape[1]
    row = lambda i, j: (i, 0)
    return pl.pallas_call(
        _mlp_kernel,
        grid=(m // tm, ff // tf),
        in_specs=[
            pl.BlockSpec((tm, d), row),
            pl.BlockSpec(memory_space=pl.ANY),
            pl.BlockSpec((d, tf), lambda i, j: (0, j)),
            pl.BlockSpec((tf, d), lambda i, j: (j, 0)),
            pl.BlockSpec((1, d), lambda i, j: (0, 0)),
        ],
        out_specs=pl.BlockSpec((tm, d), row),
        out_shape=jax.ShapeDtypeStruct((m, d), F32),
        scratch_shapes=[pltpu.SemaphoreType.DMA(())],
        compiler_params=pltpu.CompilerParams(dimension_semantics=("arbitrary", "arbitrary"),
                                             vmem_limit_bytes=BIG_VMEM_LIMIT),
        name="mlp",
    )(h, x2, w_up, w_down, g_final)


def kernel(x_prompt, x_sample, cache_k, cache_v, cache_logf, cache_conv, cache_mem_k, cache_mem_v, mem_prompt, g_mix, w_in, b_f, conv_w, g_conv_out, g_attn_out, w_out, g_xattn, g_mem, w_xq, w_xkv, w_xo, g_mlp, w_up, w_down, g_final):
    depth = w_in.shape[0]
    assert depth == 1, "single-layer trunk"
    nb, seq, d = x_prompt.shape
    db, dseq, _ = x_sample.shape
    past = cache_k.shape[2]
    heads, hd = cache_k.shape[3], cache_k.shape[4]
    assert (heads, hd) == (N_FOX_HEADS, HEAD_DIM)
    d_attn = heads * hd
    d_conv = cache_conv.shape[-1]
    n_mem, xh, xhd = cache_mem_k.shape[2:]
    assert (xh, xhd) == (N_X_HEADS, HEAD_DIM)
    d_x = xh * xhd
    n_main = 3 * d_conv + 3 * d_attn
    assert d_conv == d_attn and w_in.shape[2] == n_main + heads

    w_in_t = jnp.swapaxes(w_in[0], 0, 1)
    bf = b_f[0].reshape(1, heads)
    row = lambda g: g.reshape(1, -1)

    def mix_mlp(x, zc_row0, conv_buf, ya, seq_len, mk, mv, mix_w, mlp_w, tm):
        w_out_b, w_xq_b, w_xo_b = mix_w
        x2, h, conv_new = _mix(x, zc, conv_buf, ya, conv_w[0], row(g_conv_out[0]), row(g_attn_out[0]),
                               w_out_b, row(g_xattn[0]), w_xq_b, mk, mv, w_xo_b, row(g_mlp[0]), tm=tm, seq=seq_len,
                               zc_row0=zc_row0)
        return _mlp(h, x2, *mlp_w, row(g_final), tm=1024, tf=1024), conv_new

    xp = x_prompt.reshape(nb * seq, d)
    xs = x_sample.reshape(db * dseq, d)
    n_p = nb * seq
    proj_tm = 1024
    xn, lf = _xnorm(xp, xs, row(g_mix[0]), w_in_t, bf, tm=512, gate_row=n_main)
    proj = functools.partial(_proj, xn, w_in_t, tn=d_attn, tm=proj_tm, tiles_a=n_p // proj_tm)
    zc, = proj(j0=0, nj=3, mode="f32")
    qb, = proj(j0=3, nj=1, mode="bf16", out_scale=hd ** -0.5 * LOG2_E)
    k_p, k_s, kb = proj(j0=4, nj=1, mode="heads")
    v_p, v_s, vb = proj(j0=5, nj=1, mode="heads")
    lf_p, lf_s = lf[:n_p], lf[n_p:]

    lft = lf_p.reshape(nb, seq, heads).transpose(0, 2, 1).reshape(nb * heads, seq)
    ct = _cumsum_lanes(lft).reshape(nb, heads, seq)
    ya, *mix_w, w_xkv_b, w_up_b, w_down_b = _fox_prompt(
        qb, kb, vb, ct, ct.transpose(0, 2, 1), n_batch=nb, seq=seq, tq=1024, tk=512,
        cast=((w_out[0], 0), (w_xq[0], 0), (w_xo[0], 0), (w_xkv[0], 0), (w_up[0], 1), (w_down[0], 0)))
    mlp_w = (w_up_b, w_down_b)
    kv = _memkv(mem_prompt.reshape(nb * n_mem, d), row(g_mem[0]), w_xkv_b)
    mk_p = kv[0].reshape(nb, n_mem * xh, xhd)
    mv_p = kv[1].reshape(nb, n_mem * xh, xhd)
    conv0 = jnp.zeros((nb, CONV_WIDTH - 1, d_conv), F32)
    y_p, conv_p = mix_mlp(xp, 0, conv0, ya, seq, mk_p, mv_p, mix_w, mlp_w, tm=512)

    hp = heads // 2

    def sample_attention(qb, kb, vb, lf):
        lf_past = cache_logf[0].reshape(db, 2 * past, hp).transpose(0, 2, 1).reshape(db * hp, 2 * past)
        lf_new = lf.reshape(db, 2 * dseq, hp).transpose(0, 2, 1).reshape(db * hp, 2 * dseq)
        lf_new = jnp.pad(lf_new, ((0, 0), (0, LANES - 2 * dseq)))
        ck_past, ck_new = _cumsum_pairs(lf_past, lf_new, t_new=dseq)
        ck_new = ck_new[:, :2 * dseq].reshape(db, hp, 2 * dseq)
        cq = ck_new.reshape(db, hp, 2, dseq).transpose(0, 3, 2, 1).reshape(db, dseq, heads)
        return _fox_sample(qb, kb, vb, cache_k.reshape(db, past * heads, hd),
                           cache_v.reshape(db, past * heads, hd), cq, ck_past.reshape(db, hp, 2 * past), ck_new,
                           t=dseq, row0=n_p)

    ya = sample_attention(qb, kb, vb, lf_s)
    y_s, conv_s = mix_mlp(xs, n_p, cache_conv[0], ya, dseq, cache_mem_k.reshape(db, n_mem * xh, xhd),
                          cache_mem_v.reshape(db, n_mem * xh, xhd), mix_w, mlp_w, tm=256)

    return (y_p.reshape(nb, seq, d),
            y_s.reshape(db, dseq, d),
            k_p.reshape(1, nb, seq, heads, hd),
            v_p.reshape(1, nb, seq, heads, hd),
            lf_p.reshape(1, nb, seq, heads),
            conv_p[None],
            mk_p.reshape(1, nb, n_mem, xh, xhd),
            mv_p.reshape(1, nb, n_mem, xh, xhd),
            k_s.reshape(1, db, dseq, heads, hd),
            v_s.reshape(1, db, dseq, heads, hd),
            lf_s.reshape(1, db, dseq, heads),
            conv_s[None])
```

```python
import functools

import jax
import jax.numpy as jnp
from jax import lax
from jax.experimental import pallas as pl
from jax.experimental.pallas import tpu as pltpu

F32 = jnp.float32
BF16 = jnp.bfloat16

RMS_EPS = 1e-6
HEAD_DIM = 128
N_FOX_HEADS = 8
N_X_HEADS = 4
CONV_WIDTH = 3
LANES = 128
V_PAD_ROWS = 16
LOG2_E = 1.4426950408889634
XNORM_TM = 1024
PROJ_TM = 1024
FOX_TQ, FOX_TK = 1024, 512
MIX_TM_PROMPT = 512
MIX_TM_SAMPLE = 256
MLP_TM, MLP_TF = 1024, 1024
VMEM_LIMIT = 52 * 1024 * 1024
BIG_VMEM_LIMIT = 58 * 1024 * 1024

_NT = (((1,), (1,)), ((), ()))


def _params(semantics):
    return pltpu.CompilerParams(dimension_semantics=semantics, vmem_limit_bytes=VMEM_LIMIT)


def _rms(x, g):
    ms = jnp.mean(x * x, axis=-1, keepdims=True)
    return x * lax.rsqrt(ms + RMS_EPS) * g


def _log_sigmoid(x):
    return jnp.minimum(x, 0.0) - jnp.log1p(jnp.exp(-jnp.abs(x)))


def _store_head_rows(ref, val, heads):
    n = val.shape[0]
    for h in range(heads):
        ref[pl.ds(h, n, stride=heads), :] = val[:, h * HEAD_DIM:(h + 1) * HEAD_DIM]


def _load_head_rows(ref, lead, h, n, heads):
    return ref[lead, pl.ds(h, n, stride=heads), :]


def _xnorm_kernel(xa_ref, xb_ref, g_ref, wf_ref, bf_ref, xn_ref, lfa_ref, lfb_ref, *, tiles_a):
    in_a = pl.program_id(0) < tiles_a
    x = jnp.where(in_a, xa_ref[...], xb_ref[...])
    xn = _rms(x, g_ref[...]).astype(BF16)
    xn_ref[...] = xn
    wf = wf_ref[...].astype(BF16)
    wf = jnp.concatenate([wf, jnp.zeros((LANES - wf.shape[0], wf.shape[1]), BF16)], axis=0)
    fz = lax.dot_general(xn, wf, _NT, preferred_element_type=F32)
    lf = _log_sigmoid(fz[:, :N_FOX_HEADS] + bf_ref[...])

    @pl.when(in_a)
    def _():
        lfa_ref[...] = lf

    @pl.when(jnp.logical_not(in_a))
    def _():
        lfb_ref[...] = lf


def _xnorm(xa, xb, g, w_t, b_f, *, tm, gate_row):
    d = xa.shape[1]
    tiles_a, tiles_b = xa.shape[0] // tm, xb.shape[0] // tm
    m = xa.shape[0] + xb.shape[0]
    row = lambda i: (i, 0)
    row_a = lambda i: (jnp.minimum(i, tiles_a - 1), 0)
    row_b = lambda i: (jnp.maximum(i - tiles_a, 0), 0)
    return pl.pallas_call(
        functools.partial(_xnorm_kernel, tiles_a=tiles_a),
        grid=(tiles_a + tiles_b,),
        in_specs=[
            pl.BlockSpec((tm, d), row_a),
            pl.BlockSpec((tm, d), row_b),
            pl.BlockSpec((1, d), lambda i: (0, 0)),
            pl.BlockSpec((N_FOX_HEADS, d), lambda i: (gate_row // N_FOX_HEADS, 0)),
            pl.BlockSpec((1, N_FOX_HEADS), lambda i: (0, 0)),
        ],
        out_specs=[pl.BlockSpec((tm, d), row), pl.BlockSpec((tm, N_FOX_HEADS), row_a),
                   pl.BlockSpec((tm, N_FOX_HEADS), row_b)],
        out_shape=[jax.ShapeDtypeStruct((m, d), BF16), jax.ShapeDtypeStruct((xa.shape[0], N_FOX_HEADS), F32),
                   jax.ShapeDtypeStruct((xb.shape[0], N_FOX_HEADS), F32)],
        compiler_params=_params(("arbitrary",)),
        name="xnorm",
    )(xa, xb, g, w_t, b_f)


def _proj_kernel(xn_ref, wt_ref, *refs, mode, out_scale, tiles_a):
    out_refs, wb_ref = refs[:-1], refs[-1]
    i = pl.program_id(1)

    @pl.when(i == 0)
    def _():
        wb_ref[...] = wt_ref[...].astype(BF16)

    def matmul():
        return lax.dot_general(xn_ref[...], wb_ref[...], _NT, preferred_element_type=F32)

    if mode == "f32":
        out_refs[0][...] = matmul()
    elif mode == "bf16":
        scale = None if out_scale is None else jnp.where(i < tiles_a, out_scale, 1.0)
        res = matmul()
        out_refs[0][...] = (res if scale is None else res * scale).astype(BF16)
    else:
        def to(head_ref):
            res = matmul()
            _store_head_rows(head_ref, res, N_FOX_HEADS)
            out_refs[2][...] = res.astype(BF16)

        pl.when(i < tiles_a)(lambda: to(out_refs[0]))
        pl.when(i >= tiles_a)(lambda: to(out_refs[1]))


def _proj(xn, w_t, *, j0, nj, tn, tm, mode, tiles_a, out_scale=None):
    m, d = xn.shape
    tiles_b = m // tm - tiles_a
    if mode == "heads":
        assert nj == 1
        head_block = (tm * N_FOX_HEADS, HEAD_DIM)
        out_specs = [pl.BlockSpec(head_block, lambda j, i: (jnp.minimum(i, tiles_a - 1), 0)),
                     pl.BlockSpec(head_block, lambda j, i: (jnp.maximum(i - tiles_a, 0), 0)),
                     pl.BlockSpec((tm, tn), lambda j, i: (i, 0))]
        out_shape = [jax.ShapeDtypeStruct((tiles_a * tm * N_FOX_HEADS, HEAD_DIM), F32),
                     jax.ShapeDtypeStruct((tiles_b * tm * N_FOX_HEADS, HEAD_DIM), F32),
                     jax.ShapeDtypeStruct((m, tn), BF16)]
    else:
        out_specs = [pl.BlockSpec((tm, tn), lambda j, i: (i, j))]
        out_shape = [jax.ShapeDtypeStruct((m, nj * tn), F32 if mode == "f32" else BF16)]
    return pl.pallas_call(
        functools.partial(_proj_kernel, mode=mode, out_scale=out_scale, tiles_a=tiles_a),
        grid=(nj, m // tm),
        in_specs=[
            pl.BlockSpec((tm, d), lambda j, i: (i, 0)),
            pl.BlockSpec((tn, d), lambda j, i: (j0 + j, 0)),
        ],
        out_specs=out_specs,
        out_shape=out_shape,
        scratch_shapes=[pltpu.VMEM((tn, d), BF16)],
        compiler_params=_params(("arbitrary", "arbitrary")),
        name="proj_" + mode,
    )(xn, w_t)


def _cumsum_kernel(a_ref, o_ref):
    rows, length = a_ref.shape
    r = lax.broadcasted_iota(jnp.int32, (LANES, LANES), 0)
    c = lax.broadcasted_iota(jnp.int32, (LANES, LANES), 1)
    upper = (r <= c).astype(BF16)
    carry = jnp.zeros((rows, 1), F32)
    for b in range(length // LANES):
        a = a_ref[:, b * LANES:(b + 1) * LANES]
        hi = a.astype(BF16)
        r1 = a - hi.astype(F32)
        mid = r1.astype(BF16)
        lo = (r1 - mid.astype(F32)).astype(BF16)
        s = (jnp.dot(hi, upper, preferred_element_type=F32)
             + jnp.dot(mid, upper, preferred_element_type=F32)
             + jnp.dot(lo, upper, preferred_element_type=F32)) + carry
        o_ref[:, b * LANES:(b + 1) * LANES] = s
        carry = s[:, LANES - 1:LANES]


def _cumsum_lanes(a):
    return pl.pallas_call(
        _cumsum_kernel,
        out_shape=jax.ShapeDtypeStruct(a.shape, F32),
        compiler_params=pltpu.CompilerParams(vmem_limit_bytes=VMEM_LIMIT),
        name="cumsum",
    )(a)


def _cumsum_pairs_kernel(p_ref, n_ref, op_ref, on_ref, *, t_new):
    rows, length = p_ref.shape
    r = lax.broadcasted_iota(jnp.int32, (LANES, LANES), 0)
    c = lax.broadcasted_iota(jnp.int32, (LANES, LANES), 1)
    upper = ((r <= c) & (((r ^ c) & 1) == 0)).astype(BF16)
    unzip = ((r < 2 * t_new) & (c == (r & 1) * t_new + (r >> 1))).astype(BF16)
    lane_odd = (lax.broadcasted_iota(jnp.int32, (rows, LANES), 1) & 1) == 1

    def times(a, mat):
        hi, mid, lo = _split3(a)
        return (jnp.dot(hi.astype(BF16), mat, preferred_element_type=F32)
                + jnp.dot(mid.astype(BF16), mat, preferred_element_type=F32)
                + jnp.dot(lo.astype(BF16), mat, preferred_element_type=F32))

    carry = jnp.zeros((rows, LANES), F32)
    for b in range(length // LANES):
        s = times(p_ref[:, b * LANES:(b + 1) * LANES], upper) + carry
        op_ref[:, b * LANES:(b + 1) * LANES] = s
        carry = jnp.where(lane_odd, s[:, LANES - 1:LANES], s[:, LANES - 2:LANES - 1])
    on_ref[...] = times(times(n_ref[...], upper) + carry, unzip)


def _cumsum_pairs(p, n, *, t_new):
    return pl.pallas_call(
        functools.partial(_cumsum_pairs_kernel, t_new=t_new),
        out_shape=[jax.ShapeDtypeStruct(p.shape, F32), jax.ShapeDtypeStruct(n.shape, F32)],
        compiler_params=pltpu.CompilerParams(vmem_limit_bytes=VMEM_LIMIT),
        name="cumsum_pairs",
    )(p, n)


def _split3(x):
    hi = x.astype(BF16).astype(F32)
    r = x - hi
    mid = r.astype(BF16).astype(F32)
    lo = (r - mid).astype(BF16).astype(F32)
    return hi, mid, lo


def _fox_prompt_kernel(q_ref, k_ref, v_ref, cq_ref, ck_ref, *refs, tq, tk, n_cast):
    assert tq == 2 * tk
    cast_in, (o_ref, *cast_out), scratch = refs[:n_cast], refs[n_cast:2 * n_cast + 1], refs[2 * n_cast + 1:]
    kaug_ref, vt_ref, qaug0_ref, qaug1_ref, acc0_ref, acc1_ref, s0_ref, s1_ref = scratch
    for src_ref, dst_ref in zip(cast_in, cast_out):
        dst_ref[...] = src_ref[...].astype(BF16)
    qaug_refs, acc_refs = (qaug0_ref, qaug1_ref), (acc0_ref, acc1_ref)
    h = pl.program_id(1)
    seq = kaug_ref.shape[0]

    lane8 = lax.broadcasted_iota(jnp.int32, (tk, N_FOX_HEADS), 1)
    lane = lax.broadcasted_iota(jnp.int32, (tk, HEAD_DIM), 1)
    ones_row = (lax.broadcasted_iota(jnp.int32, (V_PAD_ROWS, tk), 0) == 0).astype(BF16)

    def prep(c):
        rows = slice(c * tk, (c + 1) * tk)
        kaug_ref[rows, 0:HEAD_DIM] = k_ref[rows, :]
        col = jnp.sum(jnp.where(lane8 == h, ck_ref[0, rows, :], 0.0), axis=1, keepdims=True) * LOG2_E
        hi, mid, lo = _split3(col)
        aug = jnp.where(lane == 0, hi, jnp.where(lane == 1, mid, jnp.where(lane == 2, lo,
                        jnp.where(lane < 6, 1.0, 0.0))))
        kaug_ref[rows, HEAD_DIM:2 * HEAD_DIM] = aug.astype(BF16)
        vt_ref[c, 0:HEAD_DIM, :] = v_ref[rows, :].astype(F32).T.astype(BF16)
        vt_ref[c, HEAD_DIM:, :] = ones_row

    sub = lax.broadcasted_iota(jnp.int32, (HEAD_DIM, tq), 0)

    def scores(qaug_ref, ki, dst_ref, lane0=0):
        dst_ref[:, lane0:] = jnp.dot(kaug_ref[ki * tk:(ki + 1) * tk, :], qaug_ref[:, lane0:],
                                     preferred_element_type=F32)

    def softmax_pv(acc_ref, ki, m_all, s_ref, lane0=0, mask_offset=None):
        m_prev = m_all[:, lane0:]
        s = s_ref[:, lane0:]
        if mask_offset is not None:
            r = lax.broadcasted_iota(jnp.int32, s.shape, 0) + mask_offset
            c = lax.broadcasted_iota(jnp.int32, s.shape, 1) + lane0
            s = jnp.where(r <= c, s, -jnp.inf)
        m_new = jnp.maximum(m_prev, jnp.max(s, axis=0, keepdims=True))
        alpha = jnp.exp2(m_prev - m_new)
        p = jnp.exp2(s - m_new).astype(BF16)
        pv = jnp.dot(vt_ref[ki], p, preferred_element_type=F32)
        acc_ref[:, lane0:] = alpha * acc_ref[:, lane0:] + pv
        return jnp.concatenate([m_all[:, :lane0], m_new], axis=1) if lane0 else m_new

    s_refs = (s0_ref, s1_ref)
    for qi in range(seq // tq):
        qaug_ref, acc_ref = qaug_refs[qi % 2], acc_refs[qi % 2]
        qrows = slice(qi * tq, (qi + 1) * tq)
        qaug_ref[0:HEAD_DIM, :] = q_ref[qrows, :].astype(F32).T.astype(BF16)
        hi, mid, lo = _split3(cq_ref[0, pl.ds(h, 1), qrows] * LOG2_E)
        qaug_ref[HEAD_DIM:2 * HEAD_DIM, :] = jnp.where(
            sub < 3, -1.0, jnp.where(sub == 3, hi, jnp.where(sub == 4, mid, jnp.where(sub == 5, lo, 0.0)))
        ).astype(BF16)
        acc_ref[...] = jnp.zeros(acc_ref.shape, F32)
        n_full = 2 * qi
        prep(n_full)
        prep(n_full + 1)
        m = jnp.full((1, tq), -jnp.inf, F32)
        scores(qaug_ref, 0, s_refs[0])
        for ki in range(n_full):
            scores(qaug_ref, ki + 1, s_refs[(ki + 1) % 2])
            m = softmax_pv(acc_ref, ki, m, s_refs[ki % 2])
        scores(qaug_ref, n_full + 1, s_refs[1], lane0=tk)
        m = softmax_pv(acc_ref, n_full, m, s_refs[0], mask_offset=0)
        softmax_pv(acc_ref, n_full + 1, m, s_refs[1], lane0=tk, mask_offset=tk)
        o_ref[qrows, :] = (acc_ref[0:HEAD_DIM, :] / acc_ref[HEAD_DIM:HEAD_DIM + 1, :]).T


def _cast_specs(cast, steps, step_of):
    specs = []
    for w, axis in cast:
        block = tuple(n // steps if a == axis else n for a, n in enumerate(w.shape))
        specs.append(pl.BlockSpec(block, (lambda *g: (step_of(*g), 0)) if axis == 0
                                  else (lambda *g: (0, step_of(*g)))))
    return specs


def _fox_prompt(qb, kb, vb, cq, ck, *, n_batch, seq, tq, tk, cast=()):
    m = n_batch * seq
    head = lambda n, h: (n, h)
    cast_specs = _cast_specs(cast, n_batch * N_FOX_HEADS, lambda n, h: n * N_FOX_HEADS + h)
    return pl.pallas_call(
        functools.partial(_fox_prompt_kernel, tq=tq, tk=tk, n_cast=len(cast)),
        grid=(n_batch, N_FOX_HEADS),
        in_specs=[
            pl.BlockSpec((seq, HEAD_DIM), head),
            pl.BlockSpec((seq, HEAD_DIM), head),
            pl.BlockSpec((seq, HEAD_DIM), head),
            pl.BlockSpec((1, N_FOX_HEADS, seq), lambda n, h: (n, 0, 0)),
            pl.BlockSpec((1, seq, N_FOX_HEADS), lambda n, h: (n, 0, 0)),
        ] + cast_specs,
        out_specs=[pl.BlockSpec((seq, HEAD_DIM), head)] + cast_specs,
        out_shape=[jax.ShapeDtypeStruct((m, N_FOX_HEADS * HEAD_DIM), F32)]
        + [jax.ShapeDtypeStruct(w.shape, BF16) for w, _ in cast],
        scratch_shapes=[
            pltpu.VMEM((seq, 2 * HEAD_DIM), BF16),
            pltpu.VMEM((seq // tk, HEAD_DIM + V_PAD_ROWS, tk), BF16),
            pltpu.VMEM((2 * HEAD_DIM, tq), BF16),
            pltpu.VMEM((2 * HEAD_DIM, tq), BF16),
            pltpu.VMEM((HEAD_DIM + V_PAD_ROWS, tq), F32),
            pltpu.VMEM((HEAD_DIM + V_PAD_ROWS, tq), F32),
            pltpu.VMEM((tk, tq), F32),
            pltpu.VMEM((tk, tq), F32),
        ],
        compiler_params=pltpu.CompilerParams(dimension_semantics=("arbitrary", "arbitrary"),
                                             vmem_limit_bytes=BIG_VMEM_LIMIT),
        name="fox_prompt",
    )(qb, kb, vb, cq, ck, *[w for w, _ in cast])


def _fox_sample_kernel(q_ref, kn_ref, vn_ref, kc_ref, vc_ref, cq_ref, ckp_ref, ckn_ref, o_ref, *, past):
    scale = HEAD_DIM ** -0.5
    t = q_ref.shape[0]
    half = N_FOX_HEADS // 2
    r = lax.broadcasted_iota(jnp.int32, (2 * t, 2 * past), 0)
    c = lax.broadcasted_iota(jnp.int32, (2 * t, 2 * past), 1)
    own_past = (r >= t) == ((c & 1) == 1)
    rn = lax.broadcasted_iota(jnp.int32, (2 * t, 2 * t), 0)
    cn = lax.broadcasted_iota(jnp.int32, (2 * t, 2 * t), 1)
    own_new = ((rn >= t) == (cn >= t)) & ((cn & (t - 1)) <= (rn & (t - 1)))
    for j in range(half):
        sl0 = slice(j * HEAD_DIM, (j + 1) * HEAD_DIM)
        sl1 = slice((j + half) * HEAD_DIM, (j + half + 1) * HEAD_DIM)
        both = lambda ref: jnp.concatenate([ref[:, sl0], ref[:, sl1]], axis=0)
        q = both(q_ref)
        kp = kc_ref[0, pl.ds(j, 2 * past, stride=half), :].astype(BF16)
        vp = vc_ref[0, pl.ds(j, 2 * past, stride=half), :].astype(BF16)
        cq = jnp.concatenate([cq_ref[0, :, j:j + 1], cq_ref[0, :, j + half:j + half + 1]], axis=0)
        s1 = lax.dot_general(q, kp, _NT, preferred_element_type=F32) * scale
        s1 = jnp.where(own_past, s1 + (cq - ckp_ref[0, j:j + 1, :]), -jnp.inf)
        s2 = lax.dot_general(q, both(kn_ref), _NT, preferred_element_type=F32) * scale
        s2 = jnp.where(own_new, s2 + (cq - ckn_ref[0, j:j + 1, :]), -jnp.inf)
        m = jnp.maximum(jnp.max(s1, axis=1, keepdims=True), jnp.max(s2, axis=1, keepdims=True))
        p1 = jnp.exp(s1 - m)
        p2 = jnp.exp(s2 - m)
        l = jnp.sum(p1, axis=1, keepdims=True) + jnp.sum(p2, axis=1, keepdims=True)
        o = (jnp.dot(p1.astype(BF16), vp, preferred_element_type=F32)
             + jnp.dot(p2.astype(BF16), both(vn_ref), preferred_element_type=F32)) / l
        o_ref[:, sl0] = o[:t]
        o_ref[:, sl1] = o[t:]


def _fox_sample(qb, knb, vnb, cache_k, cache_v, cq, ck_past, ck_new, *, t, row0):
    n_batch, rows, _ = cache_k.shape
    past = rows // N_FOX_HEADS
    width = qb.shape[1]
    tok = lambda b: (b, 0)
    tok_in = lambda b: (b + row0 // t, 0)
    lead = lambda b: (b, 0, 0)
    return pl.pallas_call(
        functools.partial(_fox_sample_kernel, past=past),
        grid=(n_batch,),
        in_specs=[
            pl.BlockSpec((t, width), tok_in),
            pl.BlockSpec((t, width), tok_in),
            pl.BlockSpec((t, width), tok_in),
            pl.BlockSpec((1, rows, HEAD_DIM), lead),
            pl.BlockSpec((1, rows, HEAD_DIM), lead),
            pl.BlockSpec((1, t, N_FOX_HEADS), lead),
            pl.BlockSpec((1, N_FOX_HEADS // 2, 2 * past), lead),
            pl.BlockSpec((1, N_FOX_HEADS // 2, 2 * t), lead),
        ],
        out_specs=pl.BlockSpec((t, width), tok),
        out_shape=jax.ShapeDtypeStruct((n_batch * t, width), F32),
        compiler_params=_params(("arbitrary",)),
        name="fox_sample",
    )(qb, knb, vnb, cache_k, cache_v, cq, ck_past, ck_new)


def _mix_kernel(*refs, seg, nseg, tiles_per_seq, has_prev, n_cast):
    if has_prev:
        (x_ref, gb_ref, gc_ref, hin_ref, pgc_ref, phin_ref, *refs) = refs
    else:
        (x_ref, gb_ref, gc_ref, hin_ref, *refs) = refs
    (buf_ref, ya_ref, cw_ref, gco_ref, gao_ref, wout_ref, gx_ref, wxq_ref, mk_ref, mv_ref, wxo_ref, gm_ref,
     *refs) = refs
    cast_in, (x2_ref, h_ref, cnew_ref, *cast_out, o_scr) = refs[:n_cast], refs[n_cast:]
    for src_ref, dst_ref in zip(cast_in, cast_out):
        dst_ref[...] = src_ref[...].astype(BF16)
    i = pl.program_id(0)
    u = gc_ref[...] * hin_ref[...]
    gb = gb_ref[...]
    w0 = cw_ref[0:1, :]
    w1 = cw_ref[1:2, :]
    w2 = cw_ref[2:3, :]
    rid = lax.broadcasted_iota(jnp.int32, (seg, u.shape[1]), 0)
    pieces = []
    for s in range(nseg):
        u_s = u[s * seg:(s + 1) * seg]
        b0 = buf_ref[s, 0:1, :]
        b1 = buf_ref[s, 1:2, :]
        if has_prev:
            pu = pgc_ref[...] * phin_ref[...]
            first = (i % tiles_per_seq) == 0
            b0 = jnp.where(first, b0, pu[6:7])
            b1 = jnp.where(first, b1, pu[7:8])
        r1 = jnp.where(rid == 0, b1, pltpu.roll(u_s, 1, 0))
        r2 = jnp.where(rid == 0, b0, jnp.where(rid == 1, b1, pltpu.roll(u_s, 2, 0)))
        y = w0 * r2 + w1 * r1 + w2 * u_s
        pieces.append(gb[s * seg:(s + 1) * seg] * y)
        cnew_ref[s] = u_s[seg - 2:seg]
    yc = pieces[0] if nseg == 1 else jnp.concatenate(pieces, axis=0)

    scale = HEAD_DIM ** -0.5
    n_mem = mk_ref.shape[1] // N_X_HEADS
    if nseg > 1:
        assert seg & (seg - 1) == 0 and n_mem & (n_mem - 1) == 0
        r = lax.broadcasted_iota(jnp.int32, (nseg * seg, nseg * n_mem), 0) >> (seg.bit_length() - 1)
        c = lax.broadcasted_iota(jnp.int32, (nseg * seg, nseg * n_mem), 1) >> (n_mem.bit_length() - 1)
        own = r == c
    cat = jnp.concatenate([_rms(yc, gco_ref[...]).astype(BF16),
                           _rms(ya_ref[...], gao_ref[...]).astype(BF16)], axis=-1)
    x1 = x_ref[...] + jnp.dot(cat, wout_ref[...], preferred_element_type=F32)
    xn = _rms(x1, gx_ref[...]).astype(BF16)
    qx = jnp.dot(xn, wxq_ref[...], preferred_element_type=F32).astype(BF16)
    for hh in range(N_X_HEADS):
        sl = slice(hh * HEAD_DIM, (hh + 1) * HEAD_DIM)
        head_rows = lambda ref: jnp.concatenate(
            [_load_head_rows(ref, s, hh, n_mem, N_X_HEADS).astype(BF16) for s in range(nseg)], axis=0)
        sc = lax.dot_general(qx[:, sl], head_rows(mk_ref), _NT, preferred_element_type=F32) * scale
        if nseg > 1:
            sc = jnp.where(own, sc, -jnp.inf)
        m = jnp.max(sc, axis=1, keepdims=True)
        p = jnp.exp(sc - m)
        l = jnp.sum(p, axis=1, keepdims=True)
        o = jnp.dot(p.astype(BF16), head_rows(mv_ref), preferred_element_type=F32) / l
        o_scr[:, sl] = o.astype(BF16)
    x2 = x1 + jnp.dot(o_scr[...], wxo_ref[...], preferred_element_type=F32)
    x2_ref[...] = x2
    h_ref[...] = _rms(x2, gm_ref[...]).astype(BF16)


def _mix(x, zc, conv_buf, ya, conv_w, g_conv_out, g_attn_out, w_out, g_xattn, w_xq, mk, mv, w_xo, g_mlp, *, tm, seq,
         zc_row0=0, cast=()):
    m, d = x.shape
    dc = zc.shape[1] // 3
    dx = w_xq.shape[1]
    mem_rows = mk.shape[1]
    has_prev = seq > tm
    seg = tm if has_prev else seq
    nseg = tm // seg
    tiles_per_seq = max(seq // tm, 1)
    t0 = zc_row0 // tm
    row = lambda i: (i, 0)
    const = lambda i: (0, 0)
    resident = functools.partial(pl.BlockSpec, index_map=const, pipeline_mode=pl.Buffered(1))
    in_specs = [
        pl.BlockSpec((tm, d), row),
        pl.BlockSpec((tm, dc), lambda i: (t0 + i, 0)),
        pl.BlockSpec((tm, dc), lambda i: (t0 + i, 1)),
        pl.BlockSpec((tm, dc), lambda i: (t0 + i, 2)),
    ]
    args = [x, zc, zc, zc]
    if has_prev:
        assert zc_row0 == 0
        prev = lambda col: (lambda i: (jnp.maximum(i * (tm // 8) - 1, 0), col))
        in_specs += [pl.BlockSpec((8, dc), prev(1)), pl.BlockSpec((8, dc), prev(2))]
        args += [zc, zc]
        buf_map = lambda i: (i // tiles_per_seq, 0, 0)
    else:
        buf_map = lambda i: (i, 0, 0)
    in_specs += [
        pl.BlockSpec((nseg, CONV_WIDTH - 1, dc), buf_map),
        pl.BlockSpec((tm, ya.shape[1]), row),
        pl.BlockSpec((CONV_WIDTH, dc), const),
        pl.BlockSpec((1, dc), const),
        pl.BlockSpec((1, ya.shape[1]), const),
        resident(w_out.shape),
        pl.BlockSpec((1, d), const),
        resident(w_xq.shape),
        pl.BlockSpec((nseg, mem_rows, HEAD_DIM), buf_map),
        pl.BlockSpec((nseg, mem_rows, HEAD_DIM), buf_map),
        resident(w_xo.shape),
        pl.BlockSpec((1, d), const),
    ]
    args += [conv_buf, ya, conv_w, g_conv_out, g_attn_out, w_out, g_xattn, w_xq, mk, mv, w_xo, g_mlp]
    steps = m // tm
    cast_specs = _cast_specs(cast, steps, lambda i: i)
    args += [w for w, _ in cast]
    return pl.pallas_call(
        functools.partial(_mix_kernel, seg=seg, nseg=nseg, tiles_per_seq=tiles_per_seq, has_prev=has_prev,
                          n_cast=len(cast)),
        grid=(steps,),
        in_specs=in_specs + cast_specs,
        out_specs=[
            pl.BlockSpec((tm, d), row),
            pl.BlockSpec((tm, d), row),
            pl.BlockSpec((nseg, CONV_WIDTH - 1, dc), buf_map),
        ] + cast_specs,
        out_shape=[
            jax.ShapeDtypeStruct((m, d), F32),
            jax.ShapeDtypeStruct((m, d), BF16),
            jax.ShapeDtypeStruct(conv_buf.shape, F32),
        ] + [jax.ShapeDtypeStruct(w.shape, BF16) for w, _ in cast],
        scratch_shapes=[pltpu.VMEM((tm, dx), BF16)],
        compiler_params=pltpu.CompilerParams(dimension_semantics=("arbitrary",), vmem_limit_bytes=BIG_VMEM_LIMIT),
        name="mix",
    )(*args)


def _memkv_kernel(x_ref, g_ref, w_ref, k_ref, v_ref):
    xn = _rms(x_ref[...], g_ref[...]).astype(BF16)
    res = jnp.dot(xn, w_ref[...], preferred_element_type=F32)
    pl.when(pl.program_id(0) == 0)(lambda: _store_head_rows(k_ref, res, N_X_HEADS))
    pl.when(pl.program_id(0) == 1)(lambda: _store_head_rows(v_ref, res, N_X_HEADS))


def _memkv(mem, g, w):
    m, d = mem.shape
    half = w.shape[1] // 2
    whole = pl.BlockSpec((m * N_X_HEADS, HEAD_DIM), lambda j: (0, 0))
    return pl.pallas_call(
        _memkv_kernel,
        grid=(2,),
        in_specs=[
            pl.BlockSpec((m, d), lambda j: (0, 0)),
            pl.BlockSpec((1, d), lambda j: (0, 0)),
            pl.BlockSpec((d, half), lambda j: (0, j)),
        ],
        out_specs=[whole, whole],
        out_shape=[jax.ShapeDtypeStruct((m * N_X_HEADS, HEAD_DIM), F32)] * 2,
        compiler_params=_params(("arbitrary",)),
        name="memkv",
    )(mem, g, w)


def _mlp_kernel(h_ref, x2_hbm, wu_ref, wd_ref, g_ref, y_ref, sem):
    i = pl.program_id(0)
    j = pl.program_id(1)
    tm = y_ref.shape[0]
    x2_copy = pltpu.make_async_copy(x2_hbm.at[pl.ds(i * tm, tm), :], y_ref, sem)

    def hidden():
        a = jnp.dot(h_ref[...], wu_ref[...], preferred_element_type=F32)
        return jnp.square(jnp.maximum(a, 0.0)).astype(BF16)

    @pl.when(j == 0)
    def _():
        x2_copy.start()
        a = hidden()
        x2_copy.wait()
        y_ref[...] += jnp.dot(a, wd_ref[...], preferred_element_type=F32)

    @pl.when(j > 0)
    def _():
        y_ref[...] += jnp.dot(hidden(), wd_ref[...], preferred_element_type=F32)

    @pl.when(j == pl.num_programs(1) - 1)
    def _():
        y_ref[...] = _rms(y_ref[...], g_ref[...])


def _mlp(h, x2, w_up, w_down, g_final, *, tm, tf):
    m, d = h.shape
    ff = w_up.shape[1]
    row = lambda i, j: (i, 0)
    return pl.pallas_call(
        _mlp_kernel,
        grid=(m // tm, ff // tf),
        in_specs=[
            pl.BlockSpec((tm, d), row),
            pl.BlockSpec(memory_space=pl.ANY),
            pl.BlockSpec((d, tf), lambda i, j: (0, j)),
            pl.BlockSpec((tf, d), lambda i, j: (j, 0)),
            pl.BlockSpec((1, d), lambda i, j: (0, 0)),
        ],
        out_specs=pl.BlockSpec((tm, d), row),
        out_shape=jax.ShapeDtypeStruct((m, d), F32),
        scratch_shapes=[pltpu.SemaphoreType.DMA(())],
        compiler_params=pltpu.CompilerParams(dimension_semantics=("arbitrary", "arbitrary"),
                                             vmem_limit_bytes=BIG_VMEM_LIMIT),
        name="mlp",
    )(h, x2, w_up, w_down, g_final)


def kernel(x_prompt, x_sample, cache_k, cache_v, cache_logf, cache_conv, cache_mem_k, cache_mem_v, mem_prompt, g_mix, w_in, b_f, conv_w, g_conv_out, g_attn_out, w_out, g_xattn, g_mem, w_xq, w_xkv, w_xo, g_mlp, w_up, w_down, g_final):
    depth = w_in.shape[0]
    assert depth == 1, "single-layer trunk"
    nb, seq, d = x_prompt.shape
    db, dseq, _ = x_sample.shape
    past = cache_k.shape[2]
    heads, hd = cache_k.shape[3], cache_k.shape[4]
    assert (heads, hd) == (N_FOX_HEADS, HEAD_DIM)
    d_attn = heads * hd
    d_conv = cache_conv.shape[-1]
    n_mem, xh, xhd = cache_mem_k.shape[2:]
    assert (xh, xhd) == (N_X_HEADS, HEAD_DIM)
    n_main = 3 * d_conv + 3 * d_attn
    assert d_conv == d_attn and w_in.shape[2] == n_main + heads

    w_in_t = jnp.swapaxes(w_in[0], 0, 1)
    bf = b_f[0].reshape(1, heads)
    row = lambda g: g.reshape(1, -1)

    def mix_mlp(x, zc_row0, conv_buf, ya, seq_len, mk, mv, mix_w, mlp_w, tm):
        w_out_b, w_xq_b, w_xo_b = mix_w
        x2, h, conv_new = _mix(x, zc, conv_buf, ya, conv_w[0], row(g_conv_out[0]), row(g_attn_out[0]),
                               w_out_b, row(g_xattn[0]), w_xq_b, mk, mv, w_xo_b, row(g_mlp[0]), tm=tm, seq=seq_len,
                               zc_row0=zc_row0)
        return _mlp(h, x2, *mlp_w, row(g_final), tm=MLP_TM, tf=MLP_TF), conv_new

    xp = x_prompt.reshape(nb * seq, d)
    xs = x_sample.reshape(db * dseq, d)
    n_p = nb * seq
    xn, lf_p, lf_s = _xnorm(xp, xs, row(g_mix[0]), w_in_t, bf, tm=XNORM_TM, gate_row=n_main)
    proj = functools.partial(_proj, xn, w_in_t, tn=d_attn, tm=PROJ_TM, tiles_a=n_p // PROJ_TM)
    zc, = proj(j0=0, nj=3, mode="f32")
    qb, = proj(j0=3, nj=1, mode="bf16", out_scale=hd ** -0.5 * LOG2_E)
    k_p, k_s, kb = proj(j0=4, nj=1, mode="heads")
    v_p, v_s, vb = proj(j0=5, nj=1, mode="heads")

    lft = lf_p.reshape(nb, seq, heads).transpose(0, 2, 1).reshape(nb * heads, seq)
    ct = _cumsum_lanes(lft).reshape(nb, heads, seq)
    ya, *mix_w, w_xkv_b, w_up_b, w_down_b = _fox_prompt(
        qb, kb, vb, ct, ct.transpose(0, 2, 1), n_batch=nb, seq=seq, tq=FOX_TQ, tk=FOX_TK,
        cast=((w_out[0], 0), (w_xq[0], 0), (w_xo[0], 0), (w_xkv[0], 0), (w_up[0], 1), (w_down[0], 0)))
    mlp_w = (w_up_b, w_down_b)
    mk_p, mv_p = _memkv(mem_prompt.reshape(nb * n_mem, d), row(g_mem[0]), w_xkv_b)
    mk_p = mk_p.reshape(nb, n_mem * xh, xhd)
    mv_p = mv_p.reshape(nb, n_mem * xh, xhd)
    conv0 = jnp.zeros((nb, CONV_WIDTH - 1, d_conv), F32)
    y_p, conv_p = mix_mlp(xp, 0, conv0, ya, seq, mk_p, mv_p, mix_w, mlp_w, tm=MIX_TM_PROMPT)

    hp = heads // 2

    def sample_attention(qb, kb, vb, lf):
        lf_past = cache_logf[0].reshape(db, 2 * past, hp).transpose(0, 2, 1).reshape(db * hp, 2 * past)
        lf_new = lf.reshape(db, 2 * dseq, hp).transpose(0, 2, 1).reshape(db * hp, 2 * dseq)
        lf_new = jnp.pad(lf_new, ((0, 0), (0, LANES - 2 * dseq)))
        ck_past, ck_new = _cumsum_pairs(lf_past, lf_new, t_new=dseq)
        ck_new = ck_new[:, :2 * dseq].reshape(db, hp, 2 * dseq)
        cq = ck_new.reshape(db, hp, 2, dseq).transpose(0, 3, 2, 1).reshape(db, dseq, heads)
        return _fox_sample(qb, kb, vb, cache_k.reshape(db, past * heads, hd),
                           cache_v.reshape(db, past * heads, hd), cq, ck_past.reshape(db, hp, 2 * past), ck_new,
                           t=dseq, row0=n_p)

    ya = sample_attention(qb, kb, vb, lf_s)
    y_s, conv_s = mix_mlp(xs, n_p, cache_conv[0], ya, dseq, cache_mem_k.reshape(db, n_mem * xh, xhd),
                          cache_mem_v.reshape(db, n_mem * xh, xhd), mix_w, mlp_w, tm=MIX_TM_SAMPLE)

    return (y_p.reshape(nb, seq, d),
            y_s.reshape(db, dseq, d),
            k_p.reshape(1, nb, seq, heads, hd),
            v_p.reshape(1, nb, seq, heads, hd),
            lf_p.reshape(1, nb, seq, heads),
            conv_p[None],
            mk_p.reshape(1, nb, n_mem, xh, xhd),
            mv_p.reshape(1, nb, n_mem, xh, xhd),
            k_s.reshape(1, db, dseq, heads, hd),
            v_s.reshape(1, db, dseq, heads, hd),
            lf_s.reshape(1, db, dseq, heads),
            conv_s[None])
```

```python
import functools

import jax
import jax.numpy as jnp
from jax import lax
from jax.experimental import pallas as pl
from jax.experimental.pallas import tpu as pltpu

F32 = jnp.float32
BF16 = jnp.bfloat16

RMS_EPS = 1e-6
HEAD_DIM = 128
N_FOX_HEADS = 8
N_X_HEADS = 4
CONV_WIDTH = 3
LANES = 128
V_PAD_ROWS = 16
LOG2_E = 1.4426950408889634
XNORM_TM = 1024
PROJ_TM = 1024
FOX_TQ, FOX_TK = 1024, 512
MIX_TM_PROMPT = 512
MIX_TM_SAMPLE = 256
MLP_TM, MLP_TF = 1024, 1024
VMEM_LIMIT = 52 * 1024 * 1024
BIG_VMEM_LIMIT = 58 * 1024 * 1024

_NT = (((1,), (1,)), ((), ()))


def _params(semantics):
    return pltpu.CompilerParams(dimension_semantics=semantics, vmem_limit_bytes=VMEM_LIMIT)


def _rms(x, g):
    ms = jnp.mean(x * x, axis=-1, keepdims=True)
    return x * lax.rsqrt(ms + RMS_EPS) * g


def _log_sigmoid(x):
    return jnp.minimum(x, 0.0) - jnp.log1p(jnp.exp(-jnp.abs(x)))


def _store_head_rows(ref, val, heads):
    n = val.shape[0]
    for h in range(heads):
        ref[pl.ds(h, n, stride=heads), :] = val[:, h * HEAD_DIM:(h + 1) * HEAD_DIM]


def _load_head_rows(ref, lead, h, n, heads):
    return ref[lead, pl.ds(h, n, stride=heads), :]


def _xnorm_kernel(xa_ref, xb_ref, g_ref, wf_ref, bf_ref, xn_ref, lfa_ref, lfb_ref, *, tiles_a):
    in_a = pl.program_id(0) < tiles_a
    x = jnp.where(in_a, xa_ref[...], xb_ref[...])
    xn = _rms(x, g_ref[...]).astype(BF16)
    xn_ref[...] = xn
    wf = wf_ref[...].astype(BF16)
    wf = jnp.concatenate([wf, jnp.zeros((LANES - wf.shape[0], wf.shape[1]), BF16)], axis=0)
    fz = lax.dot_general(xn, wf, _NT, preferred_element_type=F32)
    lf = _log_sigmoid(fz[:, :N_FOX_HEADS] + bf_ref[...])

    @pl.when(in_a)
    def _():
        lfa_ref[...] = lf

    @pl.when(jnp.logical_not(in_a))
    def _():
        lfb_ref[...] = lf


def _xnorm(xa, xb, g, w_t, b_f, *, tm, gate_row):
    d = xa.shape[1]
    tiles_a, tiles_b = xa.shape[0] // tm, xb.shape[0] // tm
    m = xa.shape[0] + xb.shape[0]
    row = lambda i: (i, 0)
    row_a = lambda i: (jnp.minimum(i, tiles_a - 1), 0)
    row_b = lambda i: (jnp.maximum(i - tiles_a, 0), 0)
    return pl.pallas_call(
        functools.partial(_xnorm_kernel, tiles_a=tiles_a),
        grid=(tiles_a + tiles_b,),
        in_specs=[
            pl.BlockSpec((tm, d), row_a),
            pl.BlockSpec((tm, d), row_b),
            pl.BlockSpec((1, d), lambda i: (0, 0)),
            pl.BlockSpec((N_FOX_HEADS, d), lambda i: (gate_row // N_FOX_HEADS, 0)),
            pl.BlockSpec((1, N_FOX_HEADS), lambda i: (0, 0)),
        ],
        out_specs=[pl.BlockSpec((tm, d), row), pl.BlockSpec((tm, N_FOX_HEADS), row_a),
                   pl.BlockSpec((tm, N_FOX_HEADS), row_b)],
        out_shape=[jax.ShapeDtypeStruct((m, d), BF16), jax.ShapeDtypeStruct((xa.shape[0], N_FOX_HEADS), F32),
                   jax.ShapeDtypeStruct((xb.shape[0], N_FOX_HEADS), F32)],
        compiler_params=_params(("arbitrary",)),
        name="xnorm",
    )(xa, xb, g, w_t, b_f)


def _proj_kernel(xn_ref, wt_ref, *refs, mode, n_f32, out_scale, tiles_a):
    out_refs, wb_ref = refs[:-1], refs[-1]
    j = pl.program_id(0)
    i = pl.program_id(1)

    @pl.when(i == 0)
    def _():
        wb_ref[...] = wt_ref[...].astype(BF16)

    def matmul():
        return lax.dot_general(xn_ref[...], wb_ref[...], _NT, preferred_element_type=F32)

    if mode == "f32_bf16":
        @pl.when(j < n_f32)
        def _():
            out_refs[0][...] = matmul()

        @pl.when(j >= n_f32)
        def _():
            out_refs[1][...] = (matmul() * jnp.where(i < tiles_a, out_scale, 1.0)).astype(BF16)
    else:
        def to(head_ref):
            res = matmul()
            _store_head_rows(head_ref, res, N_FOX_HEADS)
            out_refs[2][...] = res.astype(BF16)

        pl.when(i < tiles_a)(lambda: to(out_refs[0]))
        pl.when(i >= tiles_a)(lambda: to(out_refs[1]))


def _proj(xn, w_t, *, j0, nj, tn, tm, mode, tiles_a, out_scale=1.0):
    m, d = xn.shape
    n_i = m // tm
    tiles_b = n_i - tiles_a
    n_f32 = nj - 1
    if mode == "heads":
        assert nj == 1
        head_block = (tm * N_FOX_HEADS, HEAD_DIM)
        out_specs = [pl.BlockSpec(head_block, lambda j, i: (jnp.minimum(i, tiles_a - 1), 0)),
                     pl.BlockSpec(head_block, lambda j, i: (jnp.maximum(i - tiles_a, 0), 0)),
                     pl.BlockSpec((tm, tn), lambda j, i: (i, 0))]
        out_shape = [jax.ShapeDtypeStruct((tiles_a * tm * N_FOX_HEADS, HEAD_DIM), F32),
                     jax.ShapeDtypeStruct((tiles_b * tm * N_FOX_HEADS, HEAD_DIM), F32),
                     jax.ShapeDtypeStruct((m, tn), BF16)]
    else:
        out_specs = [pl.BlockSpec((tm, tn), lambda j, i: (jnp.where(j < n_f32, i, n_i - 1),
                                                          jnp.minimum(j, n_f32 - 1))),
                     pl.BlockSpec((tm, tn), lambda j, i: (jnp.where(j < n_f32, 0, i), 0))]
        out_shape = [jax.ShapeDtypeStruct((m, n_f32 * tn), F32), jax.ShapeDtypeStruct((m, tn), BF16)]
    return pl.pallas_call(
        functools.partial(_proj_kernel, mode=mode, n_f32=n_f32, out_scale=out_scale, tiles_a=tiles_a),
        grid=(nj, m // tm),
        in_specs=[
            pl.BlockSpec((tm, d), lambda j, i: (i, 0)),
            pl.BlockSpec((tn, d), lambda j, i: (j0 + j, 0)),
        ],
        out_specs=out_specs,
        out_shape=out_shape,
        scratch_shapes=[pltpu.VMEM((tn, d), BF16)],
        compiler_params=_params(("arbitrary", "arbitrary")),
        name="proj_" + mode,
    )(xn, w_t)


def _cumsum_kernel(a_ref, o_ref):
    rows, length = a_ref.shape
    r = lax.broadcasted_iota(jnp.int32, (LANES, LANES), 0)
    c = lax.broadcasted_iota(jnp.int32, (LANES, LANES), 1)
    upper = (r <= c).astype(BF16)
    carry = jnp.zeros((rows, 1), F32)
    for b in range(length // LANES):
        a = a_ref[:, b * LANES:(b + 1) * LANES]
        hi = a.astype(BF16)
        r1 = a - hi.astype(F32)
        mid = r1.astype(BF16)
        lo = (r1 - mid.astype(F32)).astype(BF16)
        s = (jnp.dot(hi, upper, preferred_element_type=F32)
             + jnp.dot(mid, upper, preferred_element_type=F32)
             + jnp.dot(lo, upper, preferred_element_type=F32)) + carry
        o_ref[:, b * LANES:(b + 1) * LANES] = s
        carry = s[:, LANES - 1:LANES]


def _cumsum_lanes(a):
    return pl.pallas_call(
        _cumsum_kernel,
        out_shape=jax.ShapeDtypeStruct(a.shape, F32),
        compiler_params=pltpu.CompilerParams(vmem_limit_bytes=VMEM_LIMIT),
        name="cumsum",
    )(a)


def _cumsum_pairs_kernel(p_ref, n_ref, op_ref, on_ref, *, t_new):
    rows, length = p_ref.shape
    r = lax.broadcasted_iota(jnp.int32, (LANES, LANES), 0)
    c = lax.broadcasted_iota(jnp.int32, (LANES, LANES), 1)
    upper = ((r <= c) & (((r ^ c) & 1) == 0)).astype(BF16)
    unzip = ((r < 2 * t_new) & (c == (r & 1) * t_new + (r >> 1))).astype(BF16)
    lane_odd = (lax.broadcasted_iota(jnp.int32, (rows, LANES), 1) & 1) == 1

    def times(a, mat):
        hi, mid, lo = _split3(a)
        return (jnp.dot(hi.astype(BF16), mat, preferred_element_type=F32)
                + jnp.dot(mid.astype(BF16), mat, preferred_element_type=F32)
                + jnp.dot(lo.astype(BF16), mat, preferred_element_type=F32))

    carry = jnp.zeros((rows, LANES), F32)
    for b in range(length // LANES):
        s = times(p_ref[:, b * LANES:(b + 1) * LANES], upper) + carry
        op_ref[:, b * LANES:(b + 1) * LANES] = s
        carry = jnp.where(lane_odd, s[:, LANES - 1:LANES], s[:, LANES - 2:LANES - 1])
    on_ref[...] = times(times(n_ref[...], upper) + carry, unzip)


def _cumsum_pairs(p, n, *, t_new):
    return pl.pallas_call(
        functools.partial(_cumsum_pairs_kernel, t_new=t_new),
        out_shape=[jax.ShapeDtypeStruct(p.shape, F32), jax.ShapeDtypeStruct(n.shape, F32)],
        compiler_params=pltpu.CompilerParams(vmem_limit_bytes=VMEM_LIMIT),
        name="cumsum_pairs",
    )(p, n)


def _split3(x):
    hi = x.astype(BF16).astype(F32)
    r = x - hi
    mid = r.astype(BF16).astype(F32)
    lo = (r - mid).astype(BF16).astype(F32)
    return hi, mid, lo


def _fox_prompt_kernel(q_ref, k_ref, v_ref, cq_ref, ck_ref, *refs, tq, tk, n_cast):
    assert tq == 2 * tk
    cast_in, (o_ref, *cast_out), scratch = refs[:n_cast], refs[n_cast:2 * n_cast + 1], refs[2 * n_cast + 1:]
    kaug_ref, vt_ref, qaug0_ref, qaug1_ref, acc0_ref, acc1_ref, s0_ref, s1_ref = scratch
    for src_ref, dst_ref in zip(cast_in, cast_out):
        dst_ref[...] = src_ref[...].astype(BF16)
    qaug_refs, acc_refs = (qaug0_ref, qaug1_ref), (acc0_ref, acc1_ref)
    h = pl.program_id(1)
    seq = kaug_ref.shape[0]

    lane8 = lax.broadcasted_iota(jnp.int32, (tk, N_FOX_HEADS), 1)
    lane = lax.broadcasted_iota(jnp.int32, (tk, HEAD_DIM), 1)
    ones_row = (lax.broadcasted_iota(jnp.int32, (V_PAD_ROWS, tk), 0) == 0).astype(BF16)

    def prep(c):
        rows = slice(c * tk, (c + 1) * tk)
        kaug_ref[rows, 0:HEAD_DIM] = k_ref[rows, :]
        col = jnp.sum(jnp.where(lane8 == h, ck_ref[0, rows, :], 0.0), axis=1, keepdims=True) * LOG2_E
        hi, mid, lo = _split3(col)
        aug = jnp.where(lane == 0, hi, jnp.where(lane == 1, mid, jnp.where(lane == 2, lo,
                        jnp.where(lane < 6, 1.0, 0.0))))
        kaug_ref[rows, HEAD_DIM:2 * HEAD_DIM] = aug.astype(BF16)
        vt_ref[c, 0:HEAD_DIM, :] = v_ref[rows, :].astype(F32).T.astype(BF16)
        vt_ref[c, HEAD_DIM:, :] = ones_row

    sub = lax.broadcasted_iota(jnp.int32, (HEAD_DIM, tq), 0)

    def scores(qaug_ref, ki, dst_ref, lane0=0):
        dst_ref[:, lane0:] = jnp.dot(kaug_ref[ki * tk:(ki + 1) * tk, :], qaug_ref[:, lane0:],
                                     preferred_element_type=F32)

    def softmax_pv(acc_ref, ki, m_all, s_ref, lane0=0, mask_offset=None):
        m_prev = m_all[:, lane0:]
        s = s_ref[:, lane0:]
        if mask_offset is not None:
            r = lax.broadcasted_iota(jnp.int32, s.shape, 0) + mask_offset
            c = lax.broadcasted_iota(jnp.int32, s.shape, 1) + lane0
            s = jnp.where(r <= c, s, -jnp.inf)
        m_new = jnp.maximum(m_prev, jnp.max(s, axis=0, keepdims=True))
        alpha = jnp.exp2(m_prev - m_new)
        p = jnp.exp2(s - m_new).astype(BF16)
        pv = jnp.dot(vt_ref[ki], p, preferred_element_type=F32)
        acc_ref[:, lane0:] = alpha * acc_ref[:, lane0:] + pv
        return jnp.concatenate([m_all[:, :lane0], m_new], axis=1) if lane0 else m_new

    s_refs = (s0_ref, s1_ref)
    for qi in range(seq // tq):
        qaug_ref, acc_ref = qaug_refs[qi % 2], acc_refs[qi % 2]
        qrows = slice(qi * tq, (qi + 1) * tq)
        qaug_ref[0:HEAD_DIM, :] = q_ref[qrows, :].astype(F32).T.astype(BF16)
        hi, mid, lo = _split3(cq_ref[0, pl.ds(h, 1), qrows] * LOG2_E)
        qaug_ref[HEAD_DIM:2 * HEAD_DIM, :] = jnp.where(
            sub < 3, -1.0, jnp.where(sub == 3, hi, jnp.where(sub == 4, mid, jnp.where(sub == 5, lo, 0.0)))
        ).astype(BF16)
        acc_ref[...] = jnp.zeros(acc_ref.shape, F32)
        n_full = 2 * qi
        prep(n_full)
        prep(n_full + 1)
        m = jnp.full((1, tq), -jnp.inf, F32)
        scores(qaug_ref, 0, s_refs[0])
        for ki in range(n_full):
            scores(qaug_ref, ki + 1, s_refs[(ki + 1) % 2])
            m = softmax_pv(acc_ref, ki, m, s_refs[ki % 2])
        scores(qaug_ref, n_full + 1, s_refs[1], lane0=tk)
        m = softmax_pv(acc_ref, n_full, m, s_refs[0], mask_offset=0)
        softmax_pv(acc_ref, n_full + 1, m, s_refs[1], lane0=tk, mask_offset=tk)
        o_ref[qrows, :] = (acc_ref[0:HEAD_DIM, :] / acc_ref[HEAD_DIM:HEAD_DIM + 1, :]).T


def _cast_specs(cast, steps, step_of):
    specs = []
    for w, axis in cast:
        block = tuple(n // steps if a == axis else n for a, n in enumerate(w.shape))
        specs.append(pl.BlockSpec(block, (lambda *g: (step_of(*g), 0)) if axis == 0
                                  else (lambda *g: (0, step_of(*g)))))
    return specs


def _fox_prompt(qb, kb, vb, cq, ck, *, n_batch, seq, tq, tk, cast=()):
    m = n_batch * seq
    head = lambda n, h: (n, h)
    cast_specs = _cast_specs(cast, n_batch * N_FOX_HEADS, lambda n, h: n * N_FOX_HEADS + h)
    return pl.pallas_call(
        functools.partial(_fox_prompt_kernel, tq=tq, tk=tk, n_cast=len(cast)),
        grid=(n_batch, N_FOX_HEADS),
        in_specs=[
            pl.BlockSpec((seq, HEAD_DIM), head),
            pl.BlockSpec((seq, HEAD_DIM), head),
            pl.BlockSpec((seq, HEAD_DIM), head),
            pl.BlockSpec((1, N_FOX_HEADS, seq), lambda n, h: (n, 0, 0)),
            pl.BlockSpec((1, seq, N_FOX_HEADS), lambda n, h: (n, 0, 0)),
        ] + cast_specs,
        out_specs=[pl.BlockSpec((seq, HEAD_DIM), head)] + cast_specs,
        out_shape=[jax.ShapeDtypeStruct((m, N_FOX_HEADS * HEAD_DIM), F32)]
        + [jax.ShapeDtypeStruct(w.shape, BF16) for w, _ in cast],
        scratch_shapes=[
            pltpu.VMEM((seq, 2 * HEAD_DIM), BF16),
            pltpu.VMEM((seq // tk, HEAD_DIM + V_PAD_ROWS, tk), BF16),
            pltpu.VMEM((2 * HEAD_DIM, tq), BF16),
            pltpu.VMEM((2 * HEAD_DIM, tq), BF16),
            pltpu.VMEM((HEAD_DIM + V_PAD_ROWS, tq), F32),
            pltpu.VMEM((HEAD_DIM + V_PAD_ROWS, tq), F32),
            pltpu.VMEM((tk, tq), F32),
            pltpu.VMEM((tk, tq), F32),
        ],
        compiler_params=pltpu.CompilerParams(dimension_semantics=("arbitrary", "arbitrary"),
                                             vmem_limit_bytes=BIG_VMEM_LIMIT),
        name="fox_prompt",
    )(qb, kb, vb, cq, ck, *[w for w, _ in cast])


def _fox_sample_kernel(q_ref, kn_ref, vn_ref, kc_ref, vc_ref, cq_ref, ckp_ref, ckn_ref, o_ref, *, past):
    scale = HEAD_DIM ** -0.5
    t = q_ref.shape[0]
    half = N_FOX_HEADS // 2
    r = lax.broadcasted_iota(jnp.int32, (2 * t, 2 * past), 0)
    c = lax.broadcasted_iota(jnp.int32, (2 * t, 2 * past), 1)
    own_past = (r >= t) == ((c & 1) == 1)
    rn = lax.broadcasted_iota(jnp.int32, (2 * t, 2 * t), 0)
    cn = lax.broadcasted_iota(jnp.int32, (2 * t, 2 * t), 1)
    own_new = ((rn >= t) == (cn >= t)) & ((cn & (t - 1)) <= (rn & (t - 1)))
    for j in range(half):
        sl0 = slice(j * HEAD_DIM, (j + 1) * HEAD_DIM)
        sl1 = slice((j + half) * HEAD_DIM, (j + half + 1) * HEAD_DIM)
        both = lambda ref: jnp.concatenate([ref[:, sl0], ref[:, sl1]], axis=0)
        q = both(q_ref)
        kp = kc_ref[0, pl.ds(j, 2 * past, stride=half), :].astype(BF16)
        vp = vc_ref[0, pl.ds(j, 2 * past, stride=half), :].astype(BF16)
        cq = jnp.concatenate([cq_ref[0, :, j:j + 1], cq_ref[0, :, j + half:j + half + 1]], axis=0)
        s1 = lax.dot_general(q, kp, _NT, preferred_element_type=F32) * scale
        s1 = jnp.where(own_past, s1 + (cq - ckp_ref[0, j:j + 1, :]), -jnp.inf)
        s2 = lax.dot_general(q, both(kn_ref), _NT, preferred_element_type=F32) * scale
        s2 = jnp.where(own_new, s2 + (cq - ckn_ref[0, j:j + 1, :]), -jnp.inf)
        m = jnp.maximum(jnp.max(s1, axis=1, keepdims=True), jnp.max(s2, axis=1, keepdims=True))
        p1 = jnp.exp(s1 - m)
        p2 = jnp.exp(s2 - m)
        l = jnp.sum(p1, axis=1, keepdims=True) + jnp.sum(p2, axis=1, keepdims=True)
        o = (jnp.dot(p1.astype(BF16), vp, preferred_element_type=F32)
             + jnp.dot(p2.astype(BF16), both(vn_ref), preferred_element_type=F32)) / l
        o_ref[:, sl0] = o[:t]
        o_ref[:, sl1] = o[t:]


def _fox_sample(qb, knb, vnb, cache_k, cache_v, cq, ck_past, ck_new, *, t, row0):
    n_batch, rows, _ = cache_k.shape
    past = rows // N_FOX_HEADS
    width = qb.shape[1]
    tok = lambda b: (b, 0)
    tok_in = lambda b: (b + row0 // t, 0)
    lead = lambda b: (b, 0, 0)
    return pl.pallas_call(
        functools.partial(_fox_sample_kernel, past=past),
        grid=(n_batch,),
        in_specs=[
            pl.BlockSpec((t, width), tok_in),
            pl.BlockSpec((t, width), tok_in),
            pl.BlockSpec((t, width), tok_in),
            pl.BlockSpec((1, rows, HEAD_DIM), lead),
            pl.BlockSpec((1, rows, HEAD_DIM), lead),
            pl.BlockSpec((1, t, N_FOX_HEADS), lead),
            pl.BlockSpec((1, N_FOX_HEADS // 2, 2 * past), lead),
            pl.BlockSpec((1, N_FOX_HEADS // 2, 2 * t), lead),
        ],
        out_specs=pl.BlockSpec((t, width), tok),
        out_shape=jax.ShapeDtypeStruct((n_batch * t, width), F32),
        compiler_params=_params(("arbitrary",)),
        name="fox_sample",
    )(qb, knb, vnb, cache_k, cache_v, cq, ck_past, ck_new)


def _mix_kernel(*refs, seg, nseg, tiles_per_seq, has_prev, n_cast):
    if has_prev:
        (x_ref, gb_ref, gc_ref, hin_ref, pgc_ref, phin_ref, *refs) = refs
    else:
        (x_ref, gb_ref, gc_ref, hin_ref, *refs) = refs
    (buf_ref, ya_ref, cw_ref, gco_ref, gao_ref, wout_ref, gx_ref, wxq_ref, mk_ref, mv_ref, wxo_ref, gm_ref,
     *refs) = refs
    cast_in, (x2_ref, h_ref, cnew_ref, *cast_out, o_scr) = refs[:n_cast], refs[n_cast:]
    for src_ref, dst_ref in zip(cast_in, cast_out):
        dst_ref[...] = src_ref[...].astype(BF16)
    i = pl.program_id(0)
    u = gc_ref[...] * hin_ref[...]
    gb = gb_ref[...]
    w0 = cw_ref[0:1, :]
    w1 = cw_ref[1:2, :]
    w2 = cw_ref[2:3, :]
    rid = lax.broadcasted_iota(jnp.int32, (seg, u.shape[1]), 0)
    pieces = []
    for s in range(nseg):
        u_s = u[s * seg:(s + 1) * seg]
        b0 = buf_ref[s, 0:1, :]
        b1 = buf_ref[s, 1:2, :]
        if has_prev:
            pu = pgc_ref[...] * phin_ref[...]
            first = (i % tiles_per_seq) == 0
            b0 = jnp.where(first, b0, pu[6:7])
            b1 = jnp.where(first, b1, pu[7:8])
        r1 = jnp.where(rid == 0, b1, pltpu.roll(u_s, 1, 0))
        r2 = jnp.where(rid == 0, b0, jnp.where(rid == 1, b1, pltpu.roll(u_s, 2, 0)))
        y = w0 * r2 + w1 * r1 + w2 * u_s
        pieces.append(gb[s * seg:(s + 1) * seg] * y)
        cnew_ref[s] = u_s[seg - 2:seg]
    yc = pieces[0] if nseg == 1 else jnp.concatenate(pieces, axis=0)

    scale = HEAD_DIM ** -0.5
    n_mem = mk_ref.shape[1] // N_X_HEADS
    if nseg > 1:
        assert seg & (seg - 1) == 0 and n_mem & (n_mem - 1) == 0
        r = lax.broadcasted_iota(jnp.int32, (nseg * seg, nseg * n_mem), 0) >> (seg.bit_length() - 1)
        c = lax.broadcasted_iota(jnp.int32, (nseg * seg, nseg * n_mem), 1) >> (n_mem.bit_length() - 1)
        own = r == c
    cat = jnp.concatenate([_rms(yc, gco_ref[...]).astype(BF16),
                           _rms(ya_ref[...], gao_ref[...]).astype(BF16)], axis=-1)
    x1 = x_ref[...] + jnp.dot(cat, wout_ref[...], preferred_element_type=F32)
    xn = _rms(x1, gx_ref[...]).astype(BF16)
    qx = jnp.dot(xn, wxq_ref[...], preferred_element_type=F32).astype(BF16)
    for hh in range(N_X_HEADS):
        sl = slice(hh * HEAD_DIM, (hh + 1) * HEAD_DIM)
        head_rows = lambda ref: jnp.concatenate(
            [_load_head_rows(ref, s, hh, n_mem, N_X_HEADS).astype(BF16) for s in range(nseg)], axis=0)
        sc = lax.dot_general(qx[:, sl], head_rows(mk_ref), _NT, preferred_element_type=F32) * scale
        if nseg > 1:
            sc = jnp.where(own, sc, -jnp.inf)
        m = jnp.max(sc, axis=1, keepdims=True)
        p = jnp.exp(sc - m)
        l = jnp.sum(p, axis=1, keepdims=True)
        o = jnp.dot(p.astype(BF16), head_rows(mv_ref), preferred_element_type=F32) / l
        o_scr[:, sl] = o.astype(BF16)
    x2 = x1 + jnp.dot(o_scr[...], wxo_ref[...], preferred_element_type=F32)
    x2_ref[...] = x2
    h_ref[...] = _rms(x2, gm_ref[...]).astype(BF16)


def _mix(x, zc, conv_buf, ya, conv_w, g_conv_out, g_attn_out, w_out, g_xattn, w_xq, mk, mv, w_xo, g_mlp, *, tm, seq,
         zc_row0=0, cast=()):
    m, d = x.shape
    dc = zc.shape[1] // 3
    dx = w_xq.shape[1]
    mem_rows = mk.shape[1]
    has_prev = seq > tm
    seg = tm if has_prev else seq
    nseg = tm // seg
    tiles_per_seq = max(seq // tm, 1)
    t0 = zc_row0 // tm
    row = lambda i: (i, 0)
    const = lambda i: (0, 0)
    resident = functools.partial(pl.BlockSpec, index_map=const, pipeline_mode=pl.Buffered(1))
    in_specs = [
        pl.BlockSpec((tm, d), row),
        pl.BlockSpec((tm, dc), lambda i: (t0 + i, 0)),
        pl.BlockSpec((tm, dc), lambda i: (t0 + i, 1)),
        pl.BlockSpec((tm, dc), lambda i: (t0 + i, 2)),
    ]
    args = [x, zc, zc, zc]
    if has_prev:
        assert zc_row0 == 0
        prev = lambda col: (lambda i: (jnp.maximum(i * (tm // 8) - 1, 0), col))
        in_specs += [pl.BlockSpec((8, dc), prev(1)), pl.BlockSpec((8, dc), prev(2))]
        args += [zc, zc]
        buf_map = lambda i: (i // tiles_per_seq, 0, 0)
    else:
        buf_map = lambda i: (i, 0, 0)
    in_specs += [
        pl.BlockSpec((nseg, CONV_WIDTH - 1, dc), buf_map),
        pl.BlockSpec((tm, ya.shape[1]), row),
        pl.BlockSpec((CONV_WIDTH, dc), const),
        pl.BlockSpec((1, dc), const),
        pl.BlockSpec((1, ya.shape[1]), const),
        resident(w_out.shape),
        pl.BlockSpec((1, d), const),
        resident(w_xq.shape),
        pl.BlockSpec((nseg, mem_rows, HEAD_DIM), buf_map),
        pl.BlockSpec((nseg, mem_rows, HEAD_DIM), buf_map),
        resident(w_xo.shape),
        pl.BlockSpec((1, d), const),
    ]
    args += [conv_buf, ya, conv_w, g_conv_out, g_attn_out, w_out, g_xattn, w_xq, mk, mv, w_xo, g_mlp]
    steps = m // tm
    cast_specs = _cast_specs(cast, steps, lambda i: i)
    args += [w for w, _ in cast]
    return pl.pallas_call(
        functools.partial(_mix_kernel, seg=seg, nseg=nseg, tiles_per_seq=tiles_per_seq, has_prev=has_prev,
                          n_cast=len(cast)),
        grid=(steps,),
        in_specs=in_specs + cast_specs,
        out_specs=[
            pl.BlockSpec((tm, d), row),
            pl.BlockSpec((tm, d), row),
            pl.BlockSpec((nseg, CONV_WIDTH - 1, dc), buf_map),
        ] + cast_specs,
        out_shape=[
            jax.ShapeDtypeStruct((m, d), F32),
            jax.ShapeDtypeStruct((m, d), BF16),
            jax.ShapeDtypeStruct(conv_buf.shape, F32),
        ] + [jax.ShapeDtypeStruct(w.shape, BF16) for w, _ in cast],
        scratch_shapes=[pltpu.VMEM((tm, dx), BF16)],
        compiler_params=pltpu.CompilerParams(dimension_semantics=("arbitrary",), vmem_limit_bytes=BIG_VMEM_LIMIT),
        name="mix",
    )(*args)


def _memkv_kernel(x_ref, g_ref, w_ref, k_ref, v_ref):
    xn = _rms(x_ref[...], g_ref[...]).astype(BF16)
    res = jnp.dot(xn, w_ref[...], preferred_element_type=F32)
    pl.when(pl.program_id(0) == 0)(lambda: _store_head_rows(k_ref, res, N_X_HEADS))
    pl.when(pl.program_id(0) == 1)(lambda: _store_head_rows(v_ref, res, N_X_HEADS))


def _memkv(mem, g, w):
    m, d = mem.shape
    half = w.shape[1] // 2
    whole = pl.BlockSpec((m * N_X_HEADS, HEAD_DIM), lambda j: (0, 0))
    return pl.pallas_call(
        _memkv_kernel,
        grid=(2,),
        in_specs=[
            pl.BlockSpec((m, d), lambda j: (0, 0)),
            pl.BlockSpec((1, d), lambda j: (0, 0)),
            pl.BlockSpec((d, half), lambda j: (0, j)),
        ],
        out_specs=[whole, whole],
        out_shape=[jax.ShapeDtypeStruct((m * N_X_HEADS, HEAD_DIM), F32)] * 2,
        compiler_params=_params(("arbitrary",)),
        name="memkv",
    )(mem, g, w)


def _mlp_kernel(h_ref, x2_hbm, wu_ref, wd_ref, g_ref, y_ref, sem):
    i = pl.program_id(0)
    j = pl.program_id(1)
    tm = y_ref.shape[0]
    x2_copy = pltpu.make_async_copy(x2_hbm.at[pl.ds(i * tm, tm), :], y_ref, sem)

    def hidden():
        a = jnp.dot(h_ref[...], wu_ref[...], preferred_element_type=F32)
        return jnp.square(jnp.maximum(a, 0.0)).astype(BF16)

    @pl.when(j == 0)
    def _():
        x2_copy.start()
        a = hidden()
        x2_copy.wait()
        y_ref[...] += jnp.dot(a, wd_ref[...], preferred_element_type=F32)

    @pl.when(j > 0)
    def _():
        y_ref[...] += jnp.dot(hidden(), wd_ref[...], preferred_element_type=F32)

    @pl.when(j == pl.num_programs(1) - 1)
    def _():
        y_ref[...] = _rms(y_ref[...], g_ref[...])


def _mlp(h, x2, w_up, w_down, g_final, *, tm, tf):
    m, d = h.shape
    ff = w_up.shape[1]
    row = lambda i, j: (i, 0)
    return pl.pallas_call(
        _mlp_kernel,
        grid=(m // tm, ff // tf),
        in_specs=[
            pl.BlockSpec((tm, d), row),
            pl.BlockSpec(memory_space=pl.ANY),
            pl.BlockSpec((d, tf), lambda i, j: (0, j)),
            pl.BlockSpec((tf, d), lambda i, j: (j, 0)),
            pl.BlockSpec((1, d), lambda i, j: (0, 0)),
        ],
        out_specs=pl.BlockSpec((tm, d), row),
        out_shape=jax.ShapeDtypeStruct((m, d), F32),
        scratch_shapes=[pltpu.SemaphoreType.DMA(())],
        compiler_params=pltpu.CompilerParams(dimension_semantics=("arbitrary", "arbitrary"),
                                             vmem_limit_bytes=BIG_VMEM_LIMIT),
        name="mlp",
    )(h, x2, w_up, w_down, g_final)


def kernel(x_prompt, x_sample, cache_k, cache_v, cache_logf, cache_conv, cache_mem_k, cache_mem_v, mem_prompt, g_mix, w_in, b_f, conv_w, g_conv_out, g_attn_out, w_out, g_xattn, g_mem, w_xq, w_xkv, w_xo, g_mlp, w_up, w_down, g_final):
    depth = w_in.shape[0]
    assert depth == 1, "single-layer trunk"
    nb, seq, d = x_prompt.shape
    db, dseq, _ = x_sample.shape
    past = cache_k.shape[2]
    heads, hd = cache_k.shape[3], cache_k.shape[4]
    assert (heads, hd) == (N_FOX_HEADS, HEAD_DIM)
    d_attn = heads * hd
    d_conv = cache_conv.shape[-1]
    n_mem, xh, xhd = cache_mem_k.shape[2:]
    assert (xh, xhd) == (N_X_HEADS, HEAD_DIM)
    n_main = 3 * d_conv + 3 * d_attn
    assert d_conv == d_attn and w_in.shape[2] == n_main + heads

    w_in_t = jnp.swapaxes(w_in[0], 0, 1)
    bf = b_f[0].reshape(1, heads)
    row = lambda g: g.reshape(1, -1)

    def mix_mlp(x, zc_row0, conv_buf, ya, seq_len, mk, mv, mix_w, mlp_w, tm):
        w_out_b, w_xq_b, w_xo_b = mix_w
        x2, h, conv_new = _mix(x, zc, conv_buf, ya, conv_w[0], row(g_conv_out[0]), row(g_attn_out[0]),
                               w_out_b, row(g_xattn[0]), w_xq_b, mk, mv, w_xo_b, row(g_mlp[0]), tm=tm, seq=seq_len,
                               zc_row0=zc_row0)
        return _mlp(h, x2, *mlp_w, row(g_final), tm=MLP_TM, tf=MLP_TF), conv_new

    xp = x_prompt.reshape(nb * seq, d)
    xs = x_sample.reshape(db * dseq, d)
    n_p, n_s = nb * seq, db * dseq
    assert n_p % max(XNORM_TM, PROJ_TM, MLP_TM, MIX_TM_PROMPT) == 0 and seq % FOX_TQ == 0 and seq % MIX_TM_PROMPT == 0
    assert n_s % max(XNORM_TM, PROJ_TM, MLP_TM, MIX_TM_SAMPLE) == 0 and MIX_TM_SAMPLE % dseq == 0
    xn, lf_p, lf_s = _xnorm(xp, xs, row(g_mix[0]), w_in_t, bf, tm=XNORM_TM, gate_row=n_main)
    proj = functools.partial(_proj, xn, w_in_t, tn=d_attn, tm=PROJ_TM, tiles_a=n_p // PROJ_TM)
    zc, qb = proj(j0=0, nj=4, mode="f32_bf16", out_scale=hd ** -0.5 * LOG2_E)
    k_p, k_s, kb = proj(j0=4, nj=1, mode="heads")
    v_p, v_s, vb = proj(j0=5, nj=1, mode="heads")

    lft = lf_p.reshape(nb, seq, heads).transpose(0, 2, 1).reshape(nb * heads, seq)
    ct = _cumsum_lanes(lft).reshape(nb, heads, seq)
    ya, *mix_w, w_xkv_b, w_up_b, w_down_b = _fox_prompt(
        qb, kb, vb, ct, ct.transpose(0, 2, 1), n_batch=nb, seq=seq, tq=FOX_TQ, tk=FOX_TK,
        cast=((w_out[0], 0), (w_xq[0], 0), (w_xo[0], 0), (w_xkv[0], 0), (w_up[0], 1), (w_down[0], 0)))
    mlp_w = (w_up_b, w_down_b)
    mk_p, mv_p = _memkv(mem_prompt.reshape(nb * n_mem, d), row(g_mem[0]), w_xkv_b)
    mk_p = mk_p.reshape(nb, n_mem * xh, xhd)
    mv_p = mv_p.reshape(nb, n_mem * xh, xhd)
    conv0 = jnp.zeros((nb, CONV_WIDTH - 1, d_conv), F32)
    y_p, conv_p = mix_mlp(xp, 0, conv0, ya, seq, mk_p, mv_p, mix_w, mlp_w, tm=MIX_TM_PROMPT)

    hp = heads // 2

    def sample_attention(qb, kb, vb, lf):
        lf_past = cache_logf[0].reshape(db, 2 * past, hp).transpose(0, 2, 1).reshape(db * hp, 2 * past)
        lf_new = lf.reshape(db, 2 * dseq, hp).transpose(0, 2, 1).reshape(db * hp, 2 * dseq)
        lf_new = jnp.pad(lf_new, ((0, 0), (0, LANES - 2 * dseq)))
        ck_past, ck_new = _cumsum_pairs(lf_past, lf_new, t_new=dseq)
        ck_new = ck_new[:, :2 * dseq].reshape(db, hp, 2 * dseq)
        cq = ck_new.reshape(db, hp, 2, dseq).transpose(0, 3, 2, 1).reshape(db, dseq, heads)
        return _fox_sample(qb, kb, vb, cache_k.reshape(db, past * heads, hd),
                           cache_v.reshape(db, past * heads, hd), cq, ck_past.reshape(db, hp, 2 * past), ck_new,
                           t=dseq, row0=n_p)

    ya = sample_attention(qb, kb, vb, lf_s)
    y_s, conv_s = mix_mlp(xs, n_p, cache_conv[0], ya, dseq, cache_mem_k.reshape(db, n_mem * xh, xhd),
                          cache_mem_v.reshape(db, n_mem * xh, xhd), mix_w, mlp_w, tm=MIX_TM_SAMPLE)

    return (y_p.reshape(nb, seq, d),
            y_s.reshape(db, dseq, d),
            k_p.reshape(1, nb, seq, heads, hd),
            v_p.reshape(1, nb, seq, heads, hd),
            lf_p.reshape(1, nb, seq, heads),
            conv_p[None],
            mk_p.reshape(1, nb, n_mem, xh, xhd),
            mv_p.reshape(1, nb, n_mem, xh, xhd),
            k_s.reshape(1, db, dseq, heads, hd),
            v_s.reshape(1, db, dseq, heads, hd),
            lf_s.reshape(1, db, dseq, heads),
            conv_s[None])
```

```python
import functools

import jax
import jax.numpy as jnp
from jax import lax
from jax.experimental import pallas as pl
from jax.experimental.pallas import tpu as pltpu

F32 = jnp.float32
BF16 = jnp.bfloat16

RMS_EPS = 1e-6
HEAD_DIM = 128
N_FOX_HEADS = 8
N_X_HEADS = 4
CONV_WIDTH = 3
LANES = 128
V_PAD_ROWS = 16
LOG2_E = 1.4426950408889634
XNORM_TM = 1024
PROJ_TM = 1024
FOX_TQ, FOX_TK = 1024, 512
MIX_TM_PROMPT = 512
MIX_TM_SAMPLE = 256
MLP_TM, MLP_TF = 1024, 1024
VMEM_LIMIT = 52 * 1024 * 1024
BIG_VMEM_LIMIT = 58 * 1024 * 1024

_NT = (((1,), (1,)), ((), ()))


def _params(semantics):
    return pltpu.CompilerParams(dimension_semantics=semantics, vmem_limit_bytes=VMEM_LIMIT)


def _rms(x, g):
    ms = jnp.mean(x * x, axis=-1, keepdims=True)
    return x * lax.rsqrt(ms + RMS_EPS) * g


def _log_sigmoid(x):
    return jnp.minimum(x, 0.0) - jnp.log1p(jnp.exp(-jnp.abs(x)))


def _store_head_rows(ref, val, heads):
    n = val.shape[0]
    for h in range(heads):
        ref[pl.ds(h, n, stride=heads), :] = val[:, h * HEAD_DIM:(h + 1) * HEAD_DIM]


def _load_head_rows(ref, lead, h, n, heads):
    return ref[lead, pl.ds(h, n, stride=heads), :]


def _xnorm_kernel(xa_ref, xb_ref, g_ref, wf_ref, bf_ref, xn_ref, lfa_ref, lfb_ref, *, tiles_a):
    in_a = pl.program_id(0) < tiles_a
    x = jnp.where(in_a, xa_ref[...], xb_ref[...])
    xn = _rms(x, g_ref[...]).astype(BF16)
    xn_ref[...] = xn
    wf = wf_ref[...].astype(BF16)
    wf = jnp.concatenate([wf, jnp.zeros((LANES - wf.shape[0], wf.shape[1]), BF16)], axis=0)
    fz = lax.dot_general(xn, wf, _NT, preferred_element_type=F32)
    lf = _log_sigmoid(fz[:, :N_FOX_HEADS] + bf_ref[...])

    @pl.when(in_a)
    def _():
        lfa_ref[...] = lf

    @pl.when(jnp.logical_not(in_a))
    def _():
        lfb_ref[...] = lf


def _xnorm(xa, xb, g, w_t, b_f, *, tm, gate_row):
    d = xa.shape[1]
    tiles_a, tiles_b = xa.shape[0] // tm, xb.shape[0] // tm
    m = xa.shape[0] + xb.shape[0]
    row = lambda i: (i, 0)
    row_a = lambda i: (jnp.minimum(i, tiles_a - 1), 0)
    row_b = lambda i: (jnp.maximum(i - tiles_a, 0), 0)
    return pl.pallas_call(
        functools.partial(_xnorm_kernel, tiles_a=tiles_a),
        grid=(tiles_a + tiles_b,),
        in_specs=[
            pl.BlockSpec((tm, d), row_a),
            pl.BlockSpec((tm, d), row_b),
            pl.BlockSpec((1, d), lambda i: (0, 0)),
            pl.BlockSpec((N_FOX_HEADS, d), lambda i: (gate_row // N_FOX_HEADS, 0)),
            pl.BlockSpec((1, N_FOX_HEADS), lambda i: (0, 0)),
        ],
        out_specs=[pl.BlockSpec((tm, d), row), pl.BlockSpec((tm, N_FOX_HEADS), row_a),
                   pl.BlockSpec((tm, N_FOX_HEADS), row_b)],
        out_shape=[jax.ShapeDtypeStruct((m, d), BF16), jax.ShapeDtypeStruct((xa.shape[0], N_FOX_HEADS), F32),
                   jax.ShapeDtypeStruct((xb.shape[0], N_FOX_HEADS), F32)],
        compiler_params=_params(("arbitrary",)),
        name="xnorm",
    )(xa, xb, g, w_t, b_f)


def _proj_kernel(xn_ref, wt_ref, *refs, mode, n_f32, out_scale, tiles_a):
    out_refs, wb_ref = refs[:-1], refs[-1]
    j = pl.program_id(0)
    i = pl.program_id(1)

    @pl.when(i == 0)
    def _():
        wb_ref[...] = wt_ref[...].astype(BF16)

    def matmul():
        return lax.dot_general(xn_ref[...], wb_ref[...], _NT, preferred_element_type=F32)

    if mode == "f32_bf16":
        @pl.when(j < n_f32)
        def _():
            out_refs[0][...] = matmul()

        @pl.when(j >= n_f32)
        def _():
            out_refs[1][...] = (matmul() * jnp.where(i < tiles_a, out_scale, 1.0)).astype(BF16)
    else:
        def to(head_ref):
            res = matmul()
            _store_head_rows(head_ref, res, N_FOX_HEADS)
            out_refs[2][...] = res.astype(BF16)

        pl.when(i < tiles_a)(lambda: to(out_refs[0]))
        pl.when(i >= tiles_a)(lambda: to(out_refs[1]))


def _proj(xn, w_t, *, j0, nj, tn, tm, mode, tiles_a, out_scale=1.0):
    m, d = xn.shape
    n_i = m // tm
    tiles_b = n_i - tiles_a
    n_f32 = nj - 1
    if mode == "heads":
        assert nj == 1
        head_block = (tm * N_FOX_HEADS, HEAD_DIM)
        out_specs = [pl.BlockSpec(head_block, lambda j, i: (jnp.minimum(i, tiles_a - 1), 0)),
                     pl.BlockSpec(head_block, lambda j, i: (jnp.maximum(i - tiles_a, 0), 0)),
                     pl.BlockSpec((tm, tn), lambda j, i: (i, 0))]
        out_shape = [jax.ShapeDtypeStruct((tiles_a * tm * N_FOX_HEADS, HEAD_DIM), F32),
                     jax.ShapeDtypeStruct((tiles_b * tm * N_FOX_HEADS, HEAD_DIM), F32),
                     jax.ShapeDtypeStruct((m, tn), BF16)]
    else:
        out_specs = [pl.BlockSpec((tm, tn), lambda j, i: (jnp.where(j < n_f32, i, n_i - 1),
                                                          jnp.minimum(j, n_f32 - 1))),
                     pl.BlockSpec((tm, tn), lambda j, i: (jnp.where(j < n_f32, 0, i), 0))]
        out_shape = [jax.ShapeDtypeStruct((m, n_f32 * tn), F32), jax.ShapeDtypeStruct((m, tn), BF16)]
    return pl.pallas_call(
        functools.partial(_proj_kernel, mode=mode, n_f32=n_f32, out_scale=out_scale, tiles_a=tiles_a),
        grid=(nj, m // tm),
        in_specs=[
            pl.BlockSpec((tm, d), lambda j, i: (i, 0)),
            pl.BlockSpec((tn, d), lambda j, i: (j0 + j, 0)),
        ],
        out_specs=out_specs,
        out_shape=out_shape,
        scratch_shapes=[pltpu.VMEM((tn, d), BF16)],
        compiler_params=_params(("arbitrary", "arbitrary")),
        name="proj_" + mode,
    )(xn, w_t)


def _cumsum_kernel(a_ref, o_ref):
    rows, length = a_ref.shape
    r = lax.broadcasted_iota(jnp.int32, (LANES, LANES), 0)
    c = lax.broadcasted_iota(jnp.int32, (LANES, LANES), 1)
    upper = (r <= c).astype(BF16)
    carry = jnp.zeros((rows, 1), F32)
    for b in range(length // LANES):
        a = a_ref[:, b * LANES:(b + 1) * LANES]
        hi = a.astype(BF16)
        r1 = a - hi.astype(F32)
        mid = r1.astype(BF16)
        lo = (r1 - mid.astype(F32)).astype(BF16)
        s = (jnp.dot(hi, upper, preferred_element_type=F32)
             + jnp.dot(mid, upper, preferred_element_type=F32)
             + jnp.dot(lo, upper, preferred_element_type=F32)) + carry
        o_ref[:, b * LANES:(b + 1) * LANES] = s
        carry = s[:, LANES - 1:LANES]


def _cumsum_lanes(a):
    return pl.pallas_call(
        _cumsum_kernel,
        out_shape=jax.ShapeDtypeStruct(a.shape, F32),
        compiler_params=pltpu.CompilerParams(vmem_limit_bytes=VMEM_LIMIT),
        name="cumsum",
    )(a)


def _cumsum_pairs_kernel(p_ref, n_ref, op_ref, on_ref, *, t_new):
    rows, length = p_ref.shape
    r = lax.broadcasted_iota(jnp.int32, (LANES, LANES), 0)
    c = lax.broadcasted_iota(jnp.int32, (LANES, LANES), 1)
    upper = ((r <= c) & (((r ^ c) & 1) == 0)).astype(BF16)
    unzip = ((r < 2 * t_new) & (c == (r & 1) * t_new + (r >> 1))).astype(BF16)
    lane_odd = (lax.broadcasted_iota(jnp.int32, (rows, LANES), 1) & 1) == 1

    def times(a, mat):
        hi, mid, lo = _split3(a)
        return (jnp.dot(hi.astype(BF16), mat, preferred_element_type=F32)
                + jnp.dot(mid.astype(BF16), mat, preferred_element_type=F32)
                + jnp.dot(lo.astype(BF16), mat, preferred_element_type=F32))

    carry = jnp.zeros((rows, LANES), F32)
    for b in range(length // LANES):
        s = times(p_ref[:, b * LANES:(b + 1) * LANES], upper) + carry
        op_ref[:, b * LANES:(b + 1) * LANES] = s
        carry = jnp.where(lane_odd, s[:, LANES - 1:LANES], s[:, LANES - 2:LANES - 1])
    on_ref[...] = times(times(n_ref[...], upper) + carry, unzip)


def _cumsum_pairs(p, n, *, t_new):
    return pl.pallas_call(
        functools.partial(_cumsum_pairs_kernel, t_new=t_new),
        out_shape=[jax.ShapeDtypeStruct(p.shape, F32), jax.ShapeDtypeStruct(n.shape, F32)],
        compiler_params=pltpu.CompilerParams(vmem_limit_bytes=VMEM_LIMIT),
        name="cumsum_pairs",
    )(p, n)


def _split3(x):
    hi = x.astype(BF16).astype(F32)
    r = x - hi
    mid = r.astype(BF16).astype(F32)
    lo = (r - mid).astype(BF16).astype(F32)
    return hi, mid, lo


def _fox_prompt_kernel(q_ref, k_ref, v_ref, cq_ref, ck_ref, *refs, tq, tk, n_cast):
    assert tq == 2 * tk
    cast_in, (o_ref, *cast_out), scratch = refs[:n_cast], refs[n_cast:2 * n_cast + 1], refs[2 * n_cast + 1:]
    kaug_ref, vt_ref, qaug0_ref, qaug1_ref, acc0_ref, acc1_ref, s0_ref, s1_ref = scratch
    for src_ref, dst_ref in zip(cast_in, cast_out):
        dst_ref[...] = src_ref[...].astype(BF16)
    qaug_refs, acc_refs = (qaug0_ref, qaug1_ref), (acc0_ref, acc1_ref)
    h = pl.program_id(1)
    seq = kaug_ref.shape[0]

    lane8 = lax.broadcasted_iota(jnp.int32, (tk, N_FOX_HEADS), 1)
    lane = lax.broadcasted_iota(jnp.int32, (tk, HEAD_DIM), 1)
    ones_row = (lax.broadcasted_iota(jnp.int32, (V_PAD_ROWS, tk), 0) == 0).astype(BF16)

    def prep(c):
        rows = slice(c * tk, (c + 1) * tk)
        kaug_ref[rows, 0:HEAD_DIM] = k_ref[rows, :]
        col = jnp.sum(jnp.where(lane8 == h, ck_ref[0, rows, :], 0.0), axis=1, keepdims=True) * LOG2_E
        hi, mid, lo = _split3(col)
        aug = jnp.where(lane == 0, hi, jnp.where(lane == 1, mid, jnp.where(lane == 2, lo,
                        jnp.where(lane < 6, 1.0, 0.0))))
        kaug_ref[rows, HEAD_DIM:2 * HEAD_DIM] = aug.astype(BF16)
        vt_ref[c, 0:HEAD_DIM, :] = v_ref[rows, :].astype(F32).T.astype(BF16)
        vt_ref[c, HEAD_DIM:, :] = ones_row

    sub = lax.broadcasted_iota(jnp.int32, (HEAD_DIM, tq), 0)

    def scores(qaug_ref, ki, dst_ref, lane0=0):
        dst_ref[:, lane0:] = jnp.dot(kaug_ref[ki * tk:(ki + 1) * tk, :], qaug_ref[:, lane0:],
                                     preferred_element_type=F32)

    def softmax_pv(acc_ref, ki, m_all, s_ref, lane0=0, mask_offset=None):
        m_prev = m_all[:, lane0:]
        s = s_ref[:, lane0:]
        if mask_offset is not None:
            r = lax.broadcasted_iota(jnp.int32, s.shape, 0) + mask_offset
            c = lax.broadcasted_iota(jnp.int32, s.shape, 1) + lane0
            s = jnp.where(r <= c, s, -jnp.inf)
        m_new = jnp.maximum(m_prev, jnp.max(s, axis=0, keepdims=True))
        alpha = jnp.exp2(m_prev - m_new)
        p = jnp.exp2(s - m_new).astype(BF16)
        pv = jnp.dot(vt_ref[ki], p, preferred_element_type=F32)
        acc_ref[:, lane0:] = alpha * acc_ref[:, lane0:] + pv
        return jnp.concatenate([m_all[:, :lane0], m_new], axis=1) if lane0 else m_new

    s_refs = (s0_ref, s1_ref)
    for qi in range(seq // tq):
        qaug_ref, acc_ref = qaug_refs[qi % 2], acc_refs[qi % 2]
        qrows = slice(qi * tq, (qi + 1) * tq)
        qaug_ref[0:HEAD_DIM, :] = q_ref[qrows, :].astype(F32).T.astype(BF16)
        hi, mid, lo = _split3(cq_ref[0, pl.ds(h, 1), qrows] * LOG2_E)
        qaug_ref[HEAD_DIM:2 * HEAD_DIM, :] = jnp.where(
            sub < 3, -1.0, jnp.where(sub == 3, hi, jnp.where(sub == 4, mid, jnp.where(sub == 5, lo, 0.0)))
        ).astype(BF16)
        acc_ref[...] = jnp.zeros(acc_ref.shape, F32)
        n_full = 2 * qi
        prep(n_full)
        prep(n_full + 1)
        m = jnp.full((1, tq), -jnp.inf, F32)
        scores(qaug_ref, 0, s_refs[0])
        for ki in range(n_full):
            scores(qaug_ref, ki + 1, s_refs[(ki + 1) % 2])
            m = softmax_pv(acc_ref, ki, m, s_refs[ki % 2])
        scores(qaug_ref, n_full + 1, s_refs[1], lane0=tk)
        m = softmax_pv(acc_ref, n_full, m, s_refs[0], mask_offset=0)
        softmax_pv(acc_ref, n_full + 1, m, s_refs[1], lane0=tk, mask_offset=tk)
        o_ref[qrows, :] = (acc_ref[0:HEAD_DIM, :] / acc_ref[HEAD_DIM:HEAD_DIM + 1, :]).T


def _cast_specs(cast, steps, step_of):
    specs = []
    for w, axis in cast:
        block = tuple(n // steps if a == axis else n for a, n in enumerate(w.shape))
        specs.append(pl.BlockSpec(block, (lambda *g: (step_of(*g), 0)) if axis == 0
                                  else (lambda *g: (0, step_of(*g)))))
    return specs


def _fox_prompt(qb, kb, vb, cq, ck, *, n_batch, seq, tq, tk, cast=()):
    m = n_batch * seq
    head = lambda n, h: (n, h)
    cast_specs = _cast_specs(cast, n_batch * N_FOX_HEADS, lambda n, h: n * N_FOX_HEADS + h)
    return pl.pallas_call(
        functools.partial(_fox_prompt_kernel, tq=tq, tk=tk, n_cast=len(cast)),
        grid=(n_batch, N_FOX_HEADS),
        in_specs=[
            pl.BlockSpec((seq, HEAD_DIM), head),
            pl.BlockSpec((seq, HEAD_DIM), head),
            pl.BlockSpec((seq, HEAD_DIM), head),
            pl.BlockSpec((1, N_FOX_HEADS, seq), lambda n, h: (n, 0, 0)),
            pl.BlockSpec((1, seq, N_FOX_HEADS), lambda n, h: (n, 0, 0)),
        ] + cast_specs,
        out_specs=[pl.BlockSpec((seq, HEAD_DIM), head)] + cast_specs,
        out_shape=[jax.ShapeDtypeStruct((m, N_FOX_HEADS * HEAD_DIM), F32)]
        + [jax.ShapeDtypeStruct(w.shape, BF16) for w, _ in cast],
        scratch_shapes=[
            pltpu.VMEM((seq, 2 * HEAD_DIM), BF16),
            pltpu.VMEM((seq // tk, HEAD_DIM + V_PAD_ROWS, tk), BF16),
            pltpu.VMEM((2 * HEAD_DIM, tq), BF16),
            pltpu.VMEM((2 * HEAD_DIM, tq), BF16),
            pltpu.VMEM((HEAD_DIM + V_PAD_ROWS, tq), F32),
            pltpu.VMEM((HEAD_DIM + V_PAD_ROWS, tq), F32),
            pltpu.VMEM((tk, tq), F32),
            pltpu.VMEM((tk, tq), F32),
        ],
        compiler_params=pltpu.CompilerParams(dimension_semantics=("arbitrary", "arbitrary"),
                                             vmem_limit_bytes=BIG_VMEM_LIMIT),
        name="fox_prompt",
    )(qb, kb, vb, cq, ck, *[w for w, _ in cast])


def _fox_sample_kernel(q_ref, kn_ref, vn_ref, kc0_ref, kc1_ref, vc0_ref, vc1_ref, cq_ref, ckp_ref, ckn_ref, o_ref,
                       *, past):
    def pair_rows(lo_ref, hi_ref, j):
        return jnp.concatenate([lo_ref[0, pl.ds(j, past, stride=N_FOX_HEADS // 2), :],
                                hi_ref[0, pl.ds(j, past, stride=N_FOX_HEADS // 2), :]], axis=0).astype(BF16)

    scale = HEAD_DIM ** -0.5
    t = q_ref.shape[0]
    half = N_FOX_HEADS // 2
    r = lax.broadcasted_iota(jnp.int32, (2 * t, 2 * past), 0)
    c = lax.broadcasted_iota(jnp.int32, (2 * t, 2 * past), 1)
    own_past = (r >= t) == ((c & 1) == 1)
    rn = lax.broadcasted_iota(jnp.int32, (2 * t, 2 * t), 0)
    cn = lax.broadcasted_iota(jnp.int32, (2 * t, 2 * t), 1)
    own_new = ((rn >= t) == (cn >= t)) & ((cn & (t - 1)) <= (rn & (t - 1)))
    for j in range(half):
        sl0 = slice(j * HEAD_DIM, (j + 1) * HEAD_DIM)
        sl1 = slice((j + half) * HEAD_DIM, (j + half + 1) * HEAD_DIM)
        both = lambda ref: jnp.concatenate([ref[:, sl0], ref[:, sl1]], axis=0)
        q = both(q_ref)
        kp = pair_rows(kc0_ref, kc1_ref, j)
        vp = pair_rows(vc0_ref, vc1_ref, j)
        cq = jnp.concatenate([cq_ref[0, :, j:j + 1], cq_ref[0, :, j + half:j + half + 1]], axis=0)
        s1 = lax.dot_general(q, kp, _NT, preferred_element_type=F32) * scale
        s1 = jnp.where(own_past, s1 + (cq - ckp_ref[0, j:j + 1, :]), -jnp.inf)
        s2 = lax.dot_general(q, both(kn_ref), _NT, preferred_element_type=F32) * scale
        s2 = jnp.where(own_new, s2 + (cq - ckn_ref[0, j:j + 1, :]), -jnp.inf)
        m = jnp.maximum(jnp.max(s1, axis=1, keepdims=True), jnp.max(s2, axis=1, keepdims=True))
        p1 = jnp.exp(s1 - m)
        p2 = jnp.exp(s2 - m)
        l = jnp.sum(p1, axis=1, keepdims=True) + jnp.sum(p2, axis=1, keepdims=True)
        o = (jnp.dot(p1.astype(BF16), vp, preferred_element_type=F32)
             + jnp.dot(p2.astype(BF16), both(vn_ref), preferred_element_type=F32)) / l
        o_ref[:, sl0] = o[:t]
        o_ref[:, sl1] = o[t:]


def _fox_sample(qb, knb, vnb, cache_k, cache_v, cq, ck_past, ck_new, *, t, row0):
    n_batch, rows, _ = cache_k.shape
    past = rows // N_FOX_HEADS
    width = qb.shape[1]
    tok = lambda b: (b, 0)
    tok_in = lambda b: (b + row0 // t, 0)
    lead = lambda b: (b, 0, 0)
    return pl.pallas_call(
        functools.partial(_fox_sample_kernel, past=past),
        grid=(n_batch,),
        in_specs=[
            pl.BlockSpec((t, width), tok_in),
            pl.BlockSpec((t, width), tok_in),
            pl.BlockSpec((t, width), tok_in),
            pl.BlockSpec((1, rows // 2, HEAD_DIM), lead),
            pl.BlockSpec((1, rows // 2, HEAD_DIM), lambda b: (b, 1, 0)),
            pl.BlockSpec((1, rows // 2, HEAD_DIM), lead),
            pl.BlockSpec((1, rows // 2, HEAD_DIM), lambda b: (b, 1, 0)),
            pl.BlockSpec((1, t, N_FOX_HEADS), lead),
            pl.BlockSpec((1, N_FOX_HEADS // 2, 2 * past), lead),
            pl.BlockSpec((1, N_FOX_HEADS // 2, 2 * t), lead),
        ],
        out_specs=pl.BlockSpec((t, width), tok),
        out_shape=jax.ShapeDtypeStruct((n_batch * t, width), F32),
        compiler_params=_params(("arbitrary",)),
        name="fox_sample",
    )(qb, knb, vnb, cache_k, cache_k, cache_v, cache_v, cq, ck_past, ck_new)


def _mix_kernel(*refs, seg, nseg, tiles_per_seq, has_prev, n_cast):
    if has_prev:
        (x_ref, gb_ref, gc_ref, hin_ref, pgc_ref, phin_ref, *refs) = refs
    else:
        (x_ref, gb_ref, gc_ref, hin_ref, *refs) = refs
    (buf_ref, ya_ref, cw_ref, gco_ref, gao_ref, wout_ref, gx_ref, wxq_ref, mk_ref, mv_ref, wxo_ref, gm_ref,
     *refs) = refs
    cast_in, (x2_ref, h_ref, cnew_ref, *cast_out, o_scr) = refs[:n_cast], refs[n_cast:]
    for src_ref, dst_ref in zip(cast_in, cast_out):
        dst_ref[...] = src_ref[...].astype(BF16)
    i = pl.program_id(0)
    u = gc_ref[...] * hin_ref[...]
    gb = gb_ref[...]
    w0 = cw_ref[0:1, :]
    w1 = cw_ref[1:2, :]
    w2 = cw_ref[2:3, :]
    rid = lax.broadcasted_iota(jnp.int32, (seg, u.shape[1]), 0)
    pieces = []
    for s in range(nseg):
        u_s = u[s * seg:(s + 1) * seg]
        b0 = buf_ref[s, 0:1, :]
        b1 = buf_ref[s, 1:2, :]
        if has_prev:
            pu = pgc_ref[...] * phin_ref[...]
            first = (i % tiles_per_seq) == 0
            b0 = jnp.where(first, b0, pu[6:7])
            b1 = jnp.where(first, b1, pu[7:8])
        r1 = jnp.where(rid == 0, b1, pltpu.roll(u_s, 1, 0))
        r2 = jnp.where(rid == 0, b0, jnp.where(rid == 1, b1, pltpu.roll(u_s, 2, 0)))
        y = w0 * r2 + w1 * r1 + w2 * u_s
        pieces.append(gb[s * seg:(s + 1) * seg] * y)
        cnew_ref[s] = u_s[seg - 2:seg]
    yc = pieces[0] if nseg == 1 else jnp.concatenate(pieces, axis=0)

    scale = HEAD_DIM ** -0.5
    n_mem = mk_ref.shape[1] // N_X_HEADS
    if nseg > 1:
        assert seg & (seg - 1) == 0 and n_mem & (n_mem - 1) == 0
        r = lax.broadcasted_iota(jnp.int32, (nseg * seg, nseg * n_mem), 0) >> (seg.bit_length() - 1)
        c = lax.broadcasted_iota(jnp.int32, (nseg * seg, nseg * n_mem), 1) >> (n_mem.bit_length() - 1)
        own = r == c
    cat = jnp.concatenate([_rms(yc, gco_ref[...]).astype(BF16),
                           _rms(ya_ref[...], gao_ref[...]).astype(BF16)], axis=-1)
    x1 = x_ref[...] + jnp.dot(cat, wout_ref[...], preferred_element_type=F32)
    xn = _rms(x1, gx_ref[...]).astype(BF16)
    qx = jnp.dot(xn, wxq_ref[...], preferred_element_type=F32).astype(BF16)
    for hh in range(N_X_HEADS):
        sl = slice(hh * HEAD_DIM, (hh + 1) * HEAD_DIM)
        head_rows = lambda ref: jnp.concatenate(
            [_load_head_rows(ref, s, hh, n_mem, N_X_HEADS).astype(BF16) for s in range(nseg)], axis=0)
        sc = lax.dot_general(qx[:, sl], head_rows(mk_ref), _NT, preferred_element_type=F32) * scale
        if nseg > 1:
            sc = jnp.where(own, sc, -jnp.inf)
        m = jnp.max(sc, axis=1, keepdims=True)
        p = jnp.exp(sc - m)
        l = jnp.sum(p, axis=1, keepdims=True)
        o = jnp.dot(p.astype(BF16), head_rows(mv_ref), preferred_element_type=F32) / l
        o_scr[:, sl] = o.astype(BF16)
    x2 = x1 + jnp.dot(o_scr[...], wxo_ref[...], preferred_element_type=F32)
    x2_ref[...] = x2
    h_ref[...] = _rms(x2, gm_ref[...]).astype(BF16)


def _mix(x, zc, conv_buf, ya, conv_w, g_conv_out, g_attn_out, w_out, g_xattn, w_xq, mk, mv, w_xo, g_mlp, *, tm, seq,
         zc_row0=0, cast=()):
    m, d = x.shape
    dc = zc.shape[1] // 3
    dx = w_xq.shape[1]
    mem_rows = mk.shape[1]
    has_prev = seq > tm
    seg = tm if has_prev else seq
    nseg = tm // seg
    tiles_per_seq = max(seq // tm, 1)
    t0 = zc_row0 // tm
    row = lambda i: (i, 0)
    const = lambda i: (0, 0)
    resident = functools.partial(pl.BlockSpec, index_map=const, pipeline_mode=pl.Buffered(1))
    in_specs = [
        pl.BlockSpec((tm, d), row),
        pl.BlockSpec((tm, dc), lambda i: (t0 + i, 0)),
        pl.BlockSpec((tm, dc), lambda i: (t0 + i, 1)),
        pl.BlockSpec((tm, dc), lambda i: (t0 + i, 2)),
    ]
    args = [x, zc, zc, zc]
    if has_prev:
        assert zc_row0 == 0
        prev = lambda col: (lambda i: (jnp.maximum(i * (tm // 8) - 1, 0), col))
        in_specs += [pl.BlockSpec((8, dc), prev(1)), pl.BlockSpec((8, dc), prev(2))]
        args += [zc, zc]
        buf_map = lambda i: (i // tiles_per_seq, 0, 0)
    else:
        buf_map = lambda i: (i, 0, 0)
    in_specs += [
        pl.BlockSpec((nseg, CONV_WIDTH - 1, dc), buf_map),
        pl.BlockSpec((tm, ya.shape[1]), row),
        pl.BlockSpec((CONV_WIDTH, dc), const),
        pl.BlockSpec((1, dc), const),
        pl.BlockSpec((1, ya.shape[1]), const),
        resident(w_out.shape),
        pl.BlockSpec((1, d), const),
        resident(w_xq.shape),
        pl.BlockSpec((nseg, mem_rows, HEAD_DIM), buf_map),
        pl.BlockSpec((nseg, mem_rows, HEAD_DIM), buf_map),
        resident(w_xo.shape),
        pl.BlockSpec((1, d), const),
    ]
    args += [conv_buf, ya, conv_w, g_conv_out, g_attn_out, w_out, g_xattn, w_xq, mk, mv, w_xo, g_mlp]
    steps = m // tm
    cast_specs = _cast_specs(cast, steps, lambda i: i)
    args += [w for w, _ in cast]
    return pl.pallas_call(
        functools.partial(_mix_kernel, seg=seg, nseg=nseg, tiles_per_seq=tiles_per_seq, has_prev=has_prev,
                          n_cast=len(cast)),
        grid=(steps,),
        in_specs=in_specs + cast_specs,
        out_specs=[
            pl.BlockSpec((tm, d), row),
            pl.BlockSpec((tm, d), row),
            pl.BlockSpec((nseg, CONV_WIDTH - 1, dc), buf_map),
        ] + cast_specs,
        out_shape=[
            jax.ShapeDtypeStruct((m, d), F32),
            jax.ShapeDtypeStruct((m, d), BF16),
            jax.ShapeDtypeStruct(conv_buf.shape, F32),
        ] + [jax.ShapeDtypeStruct(w.shape, BF16) for w, _ in cast],
        scratch_shapes=[pltpu.VMEM((tm, dx), BF16)],
        compiler_params=pltpu.CompilerParams(dimension_semantics=("arbitrary",), vmem_limit_bytes=BIG_VMEM_LIMIT),
        name="mix",
    )(*args)


def _memkv_kernel(x_ref, g_ref, w_ref, k_ref, v_ref):
    xn = _rms(x_ref[...], g_ref[...]).astype(BF16)
    res = jnp.dot(xn, w_ref[...], preferred_element_type=F32)
    pl.when(pl.program_id(0) == 0)(lambda: _store_head_rows(k_ref, res, N_X_HEADS))
    pl.when(pl.program_id(0) == 1)(lambda: _store_head_rows(v_ref, res, N_X_HEADS))


def _memkv(mem, g, w):
    m, d = mem.shape
    half = w.shape[1] // 2
    whole = pl.BlockSpec((m * N_X_HEADS, HEAD_DIM), lambda j: (0, 0))
    return pl.pallas_call(
        _memkv_kernel,
        grid=(2,),
        in_specs=[
            pl.BlockSpec((m, d), lambda j: (0, 0)),
            pl.BlockSpec((1, d), lambda j: (0, 0)),
            pl.BlockSpec((d, half), lambda j: (0, j)),
        ],
        out_specs=[whole, whole],
        out_shape=[jax.ShapeDtypeStruct((m * N_X_HEADS, HEAD_DIM), F32)] * 2,
        compiler_params=_params(("arbitrary",)),
        name="memkv",
    )(mem, g, w)


def _mlp_kernel(h_ref, x2_hbm, wu_ref, wd_ref, g_ref, y_ref, sem):
    i = pl.program_id(0)
    j = pl.program_id(1)
    tm = y_ref.shape[0]
    x2_copy = pltpu.make_async_copy(x2_hbm.at[pl.ds(i * tm, tm), :], y_ref, sem)

    def hidden():
        a = jnp.dot(h_ref[...], wu_ref[...], preferred_element_type=F32)
        return jnp.square(jnp.maximum(a, 0.0)).astype(BF16)

    @pl.when(j == 0)
    def _():
        x2_copy.start()
        a = hidden()
        x2_copy.wait()
        y_ref[...] += jnp.dot(a, wd_ref[...], preferred_element_type=F32)

    @pl.when(j > 0)
    def _():
        y_ref[...] += jnp.dot(hidden(), wd_ref[...], preferred_element_type=F32)

    @pl.when(j == pl.num_programs(1) - 1)
    def _():
        y_ref[...] = _rms(y_ref[...], g_ref[...])


def _mlp(h, x2, w_up, w_down, g_final, *, tm, tf):
    m, d = h.shape
    ff = w_up.shape[1]
    row = lambda i, j: (i, 0)
    return pl.pallas_call(
        _mlp_kernel,
        grid=(m // tm, ff // tf),
        in_specs=[
            pl.BlockSpec((tm, d), row),
            pl.BlockSpec(memory_space=pl.ANY),
            pl.BlockSpec((d, tf), lambda i, j: (0, j)),
            pl.BlockSpec((tf, d), lambda i, j: (j, 0)),
            pl.BlockSpec((1, d), lambda i, j: (0, 0)),
        ],
        out_specs=pl.BlockSpec((tm, d), row),
        out_shape=jax.ShapeDtypeStruct((m, d), F32),
        scratch_shapes=[pltpu.SemaphoreType.DMA(())],
        compiler_params=pltpu.CompilerParams(dimension_semantics=("arbitrary", "arbitrary"),
                                             vmem_limit_bytes=BIG_VMEM_LIMIT),
        name="mlp",
    )(h, x2, w_up, w_down, g_final)


def kernel(x_prompt, x_sample, cache_k, cache_v, cache_logf, cache_conv, cache_mem_k, cache_mem_v, mem_prompt, g_mix, w_in, b_f, conv_w, g_conv_out, g_attn_out, w_out, g_xattn, g_mem, w_xq, w_xkv, w_xo, g_mlp, w_up, w_down, g_final):
    depth = w_in.shape[0]
    assert depth == 1, "single-layer trunk"
    nb, seq, d = x_prompt.shape
    db, dseq, _ = x_sample.shape
    past = cache_k.shape[2]
    heads, hd = cache_k.shape[3], cache_k.shape[4]
    assert (heads, hd) == (N_FOX_HEADS, HEAD_DIM)
    d_attn = heads * hd
    d_conv = cache_conv.shape[-1]
    n_mem, xh, xhd = cache_mem_k.shape[2:]
    assert (xh, xhd) == (N_X_HEADS, HEAD_DIM)
    n_main = 3 * d_conv + 3 * d_attn
    assert d_conv == d_attn and w_in.shape[2] == n_main + heads

    w_in_t = jnp.swapaxes(w_in[0], 0, 1)
    bf = b_f[0].reshape(1, heads)
    row = lambda g: g.reshape(1, -1)

    def mix_mlp(x, zc_row0, conv_buf, ya, seq_len, mk, mv, mix_w, mlp_w, tm):
        w_out_b, w_xq_b, w_xo_b = mix_w
        x2, h, conv_new = _mix(x, zc, conv_buf, ya, conv_w[0], row(g_conv_out[0]), row(g_attn_out[0]),
                               w_out_b, row(g_xattn[0]), w_xq_b, mk, mv, w_xo_b, row(g_mlp[0]), tm=tm, seq=seq_len,
                               zc_row0=zc_row0)
        return _mlp(h, x2, *mlp_w, row(g_final), tm=MLP_TM, tf=MLP_TF), conv_new

    xp = x_prompt.reshape(nb * seq, d)
    xs = x_sample.reshape(db * dseq, d)
    n_p, n_s = nb * seq, db * dseq
    assert n_p % max(XNORM_TM, PROJ_TM, MLP_TM, MIX_TM_PROMPT) == 0 and seq % FOX_TQ == 0 and seq % MIX_TM_PROMPT == 0
    assert n_s % max(XNORM_TM, PROJ_TM, MLP_TM, MIX_TM_SAMPLE) == 0 and MIX_TM_SAMPLE % dseq == 0
    xn, lf_p, lf_s = _xnorm(xp, xs, row(g_mix[0]), w_in_t, bf, tm=XNORM_TM, gate_row=n_main)
    proj = functools.partial(_proj, xn, w_in_t, tn=d_attn, tm=PROJ_TM, tiles_a=n_p // PROJ_TM)
    zc, qb = proj(j0=0, nj=4, mode="f32_bf16", out_scale=hd ** -0.5 * LOG2_E)
    k_p, k_s, kb = proj(j0=4, nj=1, mode="heads")
    v_p, v_s, vb = proj(j0=5, nj=1, mode="heads")

    lft = lf_p.reshape(nb, seq, heads).transpose(0, 2, 1).reshape(nb * heads, seq)
    ct = _cumsum_lanes(lft).reshape(nb, heads, seq)
    ya, *mix_w, w_xkv_b, w_up_b, w_down_b = _fox_prompt(
        qb, kb, vb, ct, ct.transpose(0, 2, 1), n_batch=nb, seq=seq, tq=FOX_TQ, tk=FOX_TK,
        cast=((w_out[0], 0), (w_xq[0], 0), (w_xo[0], 0), (w_xkv[0], 0), (w_up[0], 1), (w_down[0], 0)))
    mlp_w = (w_up_b, w_down_b)
    mk_p, mv_p = _memkv(mem_prompt.reshape(nb * n_mem, d), row(g_mem[0]), w_xkv_b)
    mk_p = mk_p.reshape(nb, n_mem * xh, xhd)
    mv_p = mv_p.reshape(nb, n_mem * xh, xhd)
    conv0 = jnp.zeros((nb, CONV_WIDTH - 1, d_conv), F32)
    y_p, conv_p = mix_mlp(xp, 0, conv0, ya, seq, mk_p, mv_p, mix_w, mlp_w, tm=MIX_TM_PROMPT)

    hp = heads // 2

    def sample_attention(qb, kb, vb, lf):
        lf_past = cache_logf[0].reshape(db, 2 * past, hp).transpose(0, 2, 1).reshape(db * hp, 2 * past)
        lf_new = lf.reshape(db, 2 * dseq, hp).transpose(0, 2, 1).reshape(db * hp, 2 * dseq)
        lf_new = jnp.pad(lf_new, ((0, 0), (0, LANES - 2 * dseq)))
        ck_past, ck_new = _cumsum_pairs(lf_past, lf_new, t_new=dseq)
        ck_new = ck_new[:, :2 * dseq].reshape(db, hp, 2 * dseq)
        cq = ck_new.reshape(db, hp, 2, dseq).transpose(0, 3, 2, 1).reshape(db, dseq, heads)
        return _fox_sample(qb, kb, vb, cache_k.reshape(db, past * heads, hd),
                           cache_v.reshape(db, past * heads, hd), cq, ck_past.reshape(db, hp, 2 * past), ck_new,
                           t=dseq, row0=n_p)

    ya = sample_attention(qb, kb, vb, lf_s)
    y_s, conv_s = mix_mlp(xs, n_p, cache_conv[0], ya, dseq, cache_mem_k.reshape(db, n_mem * xh, xhd),
                          cache_mem_v.reshape(db, n_mem * xh, xhd), mix_w, mlp_w, tm=MIX_TM_SAMPLE)

    return (y_p.reshape(nb, seq, d),
            y_s.reshape(db, dseq, d),
            k_p.reshape(1, nb, seq, heads, hd),
            v_p.reshape(1, nb, seq, heads, hd),
            lf_p.reshape(1, nb, seq, heads),
            conv_p[None],
            mk_p.reshape(1, nb, n_mem, xh, xhd),
            mv_p.reshape(1, nb, n_mem, xh, xhd),
            k_s.reshape(1, db, dseq, heads, hd),
            v_s.reshape(1, db, dseq, heads, hd),
            lf_s.reshape(1, db, dseq, heads),
            conv_s[None])
```

```python
import functools

import jax
import jax.numpy as jnp
from jax import lax
from jax.experimental import pallas as pl
from jax.experimental.pallas import tpu as pltpu

F32 = jnp.float32
BF16 = jnp.bfloat16

RMS_EPS = 1e-6
HEAD_DIM = 128
N_FOX_HEADS = 8
N_X_HEADS = 4
CONV_WIDTH = 3
LANES = 128
V_PAD_ROWS = 16
LOG2_E = 1.4426950408889634
XNORM_TM = 1024
PROJ_TM = 1024
FOX_TQ, FOX_TK = 1024, 512
MIX_TM_PROMPT = 512
MIX_TM_SAMPLE = 256
MLP_TM, MLP_TF = 1024, 1024
VMEM_LIMIT = 52 * 1024 * 1024
BIG_VMEM_LIMIT = 58 * 1024 * 1024

_NT = (((1,), (1,)), ((), ()))


def _params(semantics):
    return pltpu.CompilerParams(dimension_semantics=semantics, vmem_limit_bytes=VMEM_LIMIT)


def _rms(x, g):
    ms = jnp.mean(x * x, axis=-1, keepdims=True)
    return x * lax.rsqrt(ms + RMS_EPS) * g


def _log_sigmoid(x):
    return jnp.minimum(x, 0.0) - jnp.log1p(jnp.exp(-jnp.abs(x)))


def _store_head_rows(ref, val, heads):
    n = val.shape[0]
    for h in range(heads):
        ref[pl.ds(h, n, stride=heads), :] = val[:, h * HEAD_DIM:(h + 1) * HEAD_DIM]


def _load_head_rows(ref, lead, h, n, heads):
    return ref[lead, pl.ds(h, n, stride=heads), :]


def _xnorm_kernel(xa_ref, xb_ref, g_ref, wf_ref, bf_ref, xn_ref, lfa_ref, lfb_ref, *, tiles_a):
    in_a = pl.program_id(0) < tiles_a
    x = jnp.where(in_a, xa_ref[...], xb_ref[...])
    xn = _rms(x, g_ref[...]).astype(BF16)
    xn_ref[...] = xn
    wf = wf_ref[...].astype(BF16)
    wf = jnp.concatenate([wf, jnp.zeros((LANES - wf.shape[0], wf.shape[1]), BF16)], axis=0)
    fz = lax.dot_general(xn, wf, _NT, preferred_element_type=F32)
    lf = _log_sigmoid(fz[:, :N_FOX_HEADS] + bf_ref[...])

    @pl.when(in_a)
    def _():
        lfa_ref[...] = lf

    @pl.when(jnp.logical_not(in_a))
    def _():
        lfb_ref[...] = lf


def _xnorm(xa, xb, g, w_t, b_f, *, tm, gate_row):
    d = xa.shape[1]
    tiles_a, tiles_b = xa.shape[0] // tm, xb.shape[0] // tm
    m = xa.shape[0] + xb.shape[0]
    row = lambda i: (i, 0)
    row_a = lambda i: (jnp.minimum(i, tiles_a - 1), 0)
    row_b = lambda i: (jnp.maximum(i - tiles_a, 0), 0)
    return pl.pallas_call(
        functools.partial(_xnorm_kernel, tiles_a=tiles_a),
        grid=(tiles_a + tiles_b,),
        in_specs=[
            pl.BlockSpec((tm, d), row_a),
            pl.BlockSpec((tm, d), row_b),
            pl.BlockSpec((1, d), lambda i: (0, 0)),
            pl.BlockSpec((N_FOX_HEADS, d), lambda i: (gate_row // N_FOX_HEADS, 0)),
            pl.BlockSpec((1, N_FOX_HEADS), lambda i: (0, 0)),
        ],
        out_specs=[pl.BlockSpec((tm, d), row), pl.BlockSpec((tm, N_FOX_HEADS), row_a),
                   pl.BlockSpec((tm, N_FOX_HEADS), row_b)],
        out_shape=[jax.ShapeDtypeStruct((m, d), BF16), jax.ShapeDtypeStruct((xa.shape[0], N_FOX_HEADS), F32),
                   jax.ShapeDtypeStruct((xb.shape[0], N_FOX_HEADS), F32)],
        compiler_params=_params(("arbitrary",)),
        name="xnorm",
    )(xa, xb, g, w_t, b_f)


def _proj_kernel(xn_ref, wt_ref, *refs, mode, n_f32, out_scale, tiles_a):
    out_refs, wb_ref = refs[:-1], refs[-1]
    j = pl.program_id(0)
    i = pl.program_id(1)

    @pl.when(i == 0)
    def _():
        wb_ref[...] = wt_ref[...].astype(BF16)

    def matmul():
        return lax.dot_general(xn_ref[...], wb_ref[...], _NT, preferred_element_type=F32)

    if mode == "f32_bf16":
        @pl.when(j < n_f32)
        def _():
            out_refs[0][...] = matmul()

        @pl.when(j >= n_f32)
        def _():
            out_refs[1][...] = (matmul() * jnp.where(i < tiles_a, out_scale, 1.0)).astype(BF16)
    else:
        def to(head_ref):
            res = matmul()
            _store_head_rows(head_ref, res, N_FOX_HEADS)
            out_refs[2][...] = res.astype(BF16)

        pl.when(i < tiles_a)(lambda: to(out_refs[0]))
        pl.when(i >= tiles_a)(lambda: to(out_refs[1]))


def _proj(xn, w_t, *, j0, nj, tn, tm, mode, tiles_a, out_scale=1.0):
    m, d = xn.shape
    n_i = m // tm
    tiles_b = n_i - tiles_a
    n_f32 = nj - 1
    if mode == "heads":
        assert nj == 1
        head_block = (tm * N_FOX_HEADS, HEAD_DIM)
        out_specs = [pl.BlockSpec(head_block, lambda j, i: (jnp.minimum(i, tiles_a - 1), 0)),
                     pl.BlockSpec(head_block, lambda j, i: (jnp.maximum(i - tiles_a, 0), 0)),
                     pl.BlockSpec((tm, tn), lambda j, i: (i, 0))]
        out_shape = [jax.ShapeDtypeStruct((tiles_a * tm * N_FOX_HEADS, HEAD_DIM), F32),
                     jax.ShapeDtypeStruct((tiles_b * tm * N_FOX_HEADS, HEAD_DIM), F32),
                     jax.ShapeDtypeStruct((m, tn), BF16)]
    else:
        out_specs = [pl.BlockSpec((tm, tn), lambda j, i: (jnp.where(j < n_f32, i, n_i - 1),
                                                          jnp.minimum(j, n_f32 - 1))),
                     pl.BlockSpec((tm, tn), lambda j, i: (jnp.where(j < n_f32, 0, i), 0))]
        out_shape = [jax.ShapeDtypeStruct((m, n_f32 * tn), F32), jax.ShapeDtypeStruct((m, tn), BF16)]
    return pl.pallas_call(
        functools.partial(_proj_kernel, mode=mode, n_f32=n_f32, out_scale=out_scale, tiles_a=tiles_a),
        grid=(nj, m // tm),
        in_specs=[
            pl.BlockSpec((tm, d), lambda j, i: (i, 0)),
            pl.BlockSpec((tn, d), lambda j, i: (j0 + j, 0)),
        ],
        out_specs=out_specs,
        out_shape=out_shape,
        scratch_shapes=[pltpu.VMEM((tn, d), BF16)],
        compiler_params=_params(("arbitrary", "arbitrary")),
        name="proj_" + mode,
    )(xn, w_t)


def _cumsum_kernel(a_ref, o_ref):
    rows, length = a_ref.shape
    r = lax.broadcasted_iota(jnp.int32, (LANES, LANES), 0)
    c = lax.broadcasted_iota(jnp.int32, (LANES, LANES), 1)
    upper = (r <= c).astype(BF16)
    carry = jnp.zeros((rows, 1), F32)
    for b in range(length // LANES):
        a = a_ref[:, b * LANES:(b + 1) * LANES]
        hi = a.astype(BF16)
        r1 = a - hi.astype(F32)
        mid = r1.astype(BF16)
        lo = (r1 - mid.astype(F32)).astype(BF16)
        s = (jnp.dot(hi, upper, preferred_element_type=F32)
             + jnp.dot(mid, upper, preferred_element_type=F32)
             + jnp.dot(lo, upper, preferred_element_type=F32)) + carry
        o_ref[:, b * LANES:(b + 1) * LANES] = s
        carry = s[:, LANES - 1:LANES]


def _cumsum_lanes(a):
    return pl.pallas_call(
        _cumsum_kernel,
        out_shape=jax.ShapeDtypeStruct(a.shape, F32),
        compiler_params=pltpu.CompilerParams(vmem_limit_bytes=VMEM_LIMIT),
        name="cumsum",
    )(a)


def _cumsum_pairs_kernel(p_ref, n_ref, op_ref, on_ref, *, t_new):
    rows, length = p_ref.shape
    r = lax.broadcasted_iota(jnp.int32, (LANES, LANES), 0)
    c = lax.broadcasted_iota(jnp.int32, (LANES, LANES), 1)
    upper = ((r <= c) & (((r ^ c) & 1) == 0)).astype(BF16)
    unzip = ((r < 2 * t_new) & (c == (r & 1) * t_new + (r >> 1))).astype(BF16)
    lane_odd = (lax.broadcasted_iota(jnp.int32, (rows, LANES), 1) & 1) == 1

    def times(a, mat):
        hi, mid, lo = _split3(a)
        return (jnp.dot(hi.astype(BF16), mat, preferred_element_type=F32)
                + jnp.dot(mid.astype(BF16), mat, preferred_element_type=F32)
                + jnp.dot(lo.astype(BF16), mat, preferred_element_type=F32))

    carry = jnp.zeros((rows, LANES), F32)
    for b in range(length // LANES):
        s = times(p_ref[:, b * LANES:(b + 1) * LANES], upper) + carry
        op_ref[:, b * LANES:(b + 1) * LANES] = s
        carry = jnp.where(lane_odd, s[:, LANES - 1:LANES], s[:, LANES - 2:LANES - 1])
    on_ref[...] = times(times(n_ref[...], upper) + carry, unzip)


def _cumsum_pairs(p, n, *, t_new):
    return pl.pallas_call(
        functools.partial(_cumsum_pairs_kernel, t_new=t_new),
        out_shape=[jax.ShapeDtypeStruct(p.shape, F32), jax.ShapeDtypeStruct(n.shape, F32)],
        compiler_params=pltpu.CompilerParams(vmem_limit_bytes=VMEM_LIMIT),
        name="cumsum_pairs",
    )(p, n)


def _split3(x):
    hi = x.astype(BF16).astype(F32)
    r = x - hi
    mid = r.astype(BF16).astype(F32)
    lo = (r - mid).astype(BF16).astype(F32)
    return hi, mid, lo


def _fox_prompt_kernel(q_ref, k_ref, v_ref, cq_ref, ck_ref, *refs, tq, tk, n_cast):
    assert tq == 2 * tk
    cast_in, (o_ref, *cast_out), scratch = refs[:n_cast], refs[n_cast:2 * n_cast + 1], refs[2 * n_cast + 1:]
    kaug_ref, vt_ref, qaug0_ref, qaug1_ref, acc0_ref, acc1_ref, s0_ref, s1_ref = scratch
    for src_ref, dst_ref in zip(cast_in, cast_out):
        dst_ref[...] = src_ref[...].astype(BF16)
    qaug_refs, acc_refs = (qaug0_ref, qaug1_ref), (acc0_ref, acc1_ref)
    h = pl.program_id(1)
    seq = kaug_ref.shape[0]

    lane8 = lax.broadcasted_iota(jnp.int32, (tk, N_FOX_HEADS), 1)
    lane = lax.broadcasted_iota(jnp.int32, (tk, HEAD_DIM), 1)
    ones_row = (lax.broadcasted_iota(jnp.int32, (V_PAD_ROWS, tk), 0) == 0).astype(BF16)

    def prep(c):
        rows = slice(c * tk, (c + 1) * tk)
        kaug_ref[rows, 0:HEAD_DIM] = k_ref[rows, :]
        col = jnp.sum(jnp.where(lane8 == h, ck_ref[0, rows, :], 0.0), axis=1, keepdims=True) * LOG2_E
        hi, mid, lo = _split3(col)
        aug = jnp.where(lane == 0, hi, jnp.where(lane == 1, mid, jnp.where(lane == 2, lo,
                        jnp.where(lane < 6, 1.0, 0.0))))
        kaug_ref[rows, HEAD_DIM:2 * HEAD_DIM] = aug.astype(BF16)
        vt_ref[c, 0:HEAD_DIM, :] = v_ref[rows, :].astype(F32).T.astype(BF16)
        vt_ref[c, HEAD_DIM:, :] = ones_row

    sub = lax.broadcasted_iota(jnp.int32, (HEAD_DIM, tq), 0)

    def scores(qaug_ref, ki, dst_ref, lane0=0):
        dst_ref[:, lane0:] = jnp.dot(kaug_ref[ki * tk:(ki + 1) * tk, :], qaug_ref[:, lane0:],
                                     preferred_element_type=F32)

    def softmax_pv(acc_ref, ki, m_all, s_ref, lane0=0, mask_offset=None):
        m_prev = m_all[:, lane0:]
        s = s_ref[:, lane0:]
        if mask_offset is not None:
            r = lax.broadcasted_iota(jnp.int32, s.shape, 0) + mask_offset
            c = lax.broadcasted_iota(jnp.int32, s.shape, 1) + lane0
            s = jnp.where(r <= c, s, -jnp.inf)
        m_new = jnp.maximum(m_prev, jnp.max(s, axis=0, keepdims=True))
        alpha = jnp.exp2(m_prev - m_new)
        p = jnp.exp2(s - m_new).astype(BF16)
        pv = jnp.dot(vt_ref[ki], p, preferred_element_type=F32)
        acc_ref[:, lane0:] = alpha * acc_ref[:, lane0:] + pv
        return jnp.concatenate([m_all[:, :lane0], m_new], axis=1) if lane0 else m_new

    s_refs = (s0_ref, s1_ref)
    for qi in range(seq // tq):
        qaug_ref, acc_ref = qaug_refs[qi % 2], acc_refs[qi % 2]
        qrows = slice(qi * tq, (qi + 1) * tq)
        qaug_ref[0:HEAD_DIM, :] = q_ref[qrows, :].astype(F32).T.astype(BF16)
        hi, mid, lo = _split3(cq_ref[0, pl.ds(h, 1), qrows] * LOG2_E)
        qaug_ref[HEAD_DIM:2 * HEAD_DIM, :] = jnp.where(
            sub < 3, -1.0, jnp.where(sub == 3, hi, jnp.where(sub == 4, mid, jnp.where(sub == 5, lo, 0.0)))
        ).astype(BF16)
        acc_ref[...] = jnp.zeros(acc_ref.shape, F32)
        n_full = 2 * qi
        prep(n_full)
        prep(n_full + 1)
        m = jnp.full((1, tq), -jnp.inf, F32)
        scores(qaug_ref, 0, s_refs[0])
        for ki in range(n_full):
            scores(qaug_ref, ki + 1, s_refs[(ki + 1) % 2])
            m = softmax_pv(acc_ref, ki, m, s_refs[ki % 2])
        scores(qaug_ref, n_full + 1, s_refs[1], lane0=tk)
        m = softmax_pv(acc_ref, n_full, m, s_refs[0], mask_offset=0)
        softmax_pv(acc_ref, n_full + 1, m, s_refs[1], lane0=tk, mask_offset=tk)
        o_ref[qrows, :] = (acc_ref[0:HEAD_DIM, :] / acc_ref[HEAD_DIM:HEAD_DIM + 1, :]).T


def _cast_specs(cast, steps, step_of):
    specs = []
    for w, axis in cast:
        block = tuple(n // steps if a == axis else n for a, n in enumerate(w.shape))
        specs.append(pl.BlockSpec(block, (lambda *g: (step_of(*g), 0)) if axis == 0
                                  else (lambda *g: (0, step_of(*g)))))
    return specs


def _fox_prompt(qb, kb, vb, cq, ck, *, n_batch, seq, tq, tk, cast=()):
    m = n_batch * seq
    head = lambda n, h: (n, h)
    cast_specs = _cast_specs(cast, n_batch * N_FOX_HEADS, lambda n, h: n * N_FOX_HEADS + h)
    return pl.pallas_call(
        functools.partial(_fox_prompt_kernel, tq=tq, tk=tk, n_cast=len(cast)),
        grid=(n_batch, N_FOX_HEADS),
        in_specs=[
            pl.BlockSpec((seq, HEAD_DIM), head),
            pl.BlockSpec((seq, HEAD_DIM), head),
            pl.BlockSpec((seq, HEAD_DIM), head),
            pl.BlockSpec((1, N_FOX_HEADS, seq), lambda n, h: (n, 0, 0)),
            pl.BlockSpec((1, seq, N_FOX_HEADS), lambda n, h: (n, 0, 0)),
        ] + cast_specs,
        out_specs=[pl.BlockSpec((seq, HEAD_DIM), head)] + cast_specs,
        out_shape=[jax.ShapeDtypeStruct((m, N_FOX_HEADS * HEAD_DIM), F32)]
        + [jax.ShapeDtypeStruct(w.shape, BF16) for w, _ in cast],
        scratch_shapes=[
            pltpu.VMEM((seq, 2 * HEAD_DIM), BF16),
            pltpu.VMEM((seq // tk, HEAD_DIM + V_PAD_ROWS, tk), BF16),
            pltpu.VMEM((2 * HEAD_DIM, tq), BF16),
            pltpu.VMEM((2 * HEAD_DIM, tq), BF16),
            pltpu.VMEM((HEAD_DIM + V_PAD_ROWS, tq), F32),
            pltpu.VMEM((HEAD_DIM + V_PAD_ROWS, tq), F32),
            pltpu.VMEM((tk, tq), F32),
            pltpu.VMEM((tk, tq), F32),
        ],
        compiler_params=pltpu.CompilerParams(dimension_semantics=("arbitrary", "arbitrary"),
                                             vmem_limit_bytes=BIG_VMEM_LIMIT),
        name="fox_prompt",
    )(qb, kb, vb, cq, ck, *[w for w, _ in cast])


def _fox_sample_kernel(q_ref, kn_ref, vn_ref, kc_ref, vc_ref, cq_ref, ckp_ref, ckn_ref, o_ref, *, past):
    scale = HEAD_DIM ** -0.5
    t = q_ref.shape[0]
    half = N_FOX_HEADS // 2
    r = lax.broadcasted_iota(jnp.int32, (2 * t, 2 * past), 0)
    c = lax.broadcasted_iota(jnp.int32, (2 * t, 2 * past), 1)
    own_past = (r >= t) == ((c & 1) == 1)
    rn = lax.broadcasted_iota(jnp.int32, (2 * t, 2 * t), 0)
    cn = lax.broadcasted_iota(jnp.int32, (2 * t, 2 * t), 1)
    own_new = ((rn >= t) == (cn >= t)) & ((cn & (t - 1)) <= (rn & (t - 1)))
    for j in range(half):
        sl0 = slice(j * HEAD_DIM, (j + 1) * HEAD_DIM)
        sl1 = slice((j + half) * HEAD_DIM, (j + half + 1) * HEAD_DIM)
        both = lambda ref: jnp.concatenate([ref[:, sl0], ref[:, sl1]], axis=0)
        q = both(q_ref)
        kp = kc_ref[0, pl.ds(j, 2 * past, stride=half), :].astype(BF16)
        vp = vc_ref[0, pl.ds(j, 2 * past, stride=half), :].astype(BF16)
        cq = jnp.concatenate([cq_ref[0, :, j:j + 1], cq_ref[0, :, j + half:j + half + 1]], axis=0)
        s1 = lax.dot_general(q, kp, _NT, preferred_element_type=F32) * scale
        s1 = jnp.where(own_past, s1 + (cq - ckp_ref[0, j:j + 1, :]), -jnp.inf)
        s2 = lax.dot_general(q, both(kn_ref), _NT, preferred_element_type=F32) * scale
        s2 = jnp.where(own_new, s2 + (cq - ckn_ref[0, j:j + 1, :]), -jnp.inf)
        m = jnp.maximum(jnp.max(s1, axis=1, keepdims=True), jnp.max(s2, axis=1, keepdims=True))
        p1 = jnp.exp(s1 - m)
        p2 = jnp.exp(s2 - m)
        l = jnp.sum(p1, axis=1, keepdims=True) + jnp.sum(p2, axis=1, keepdims=True)
        o = (jnp.dot(p1.astype(BF16), vp, preferred_element_type=F32)
             + jnp.dot(p2.astype(BF16), both(vn_ref), preferred_element_type=F32)) / l
        o_ref[:, sl0] = o[:t]
        o_ref[:, sl1] = o[t:]


def _fox_sample(qb, knb, vnb, cache_k, cache_v, cq, ck_past, ck_new, *, t, row0):
    n_batch, rows, _ = cache_k.shape
    past = rows // N_FOX_HEADS
    width = qb.shape[1]
    tok = lambda b: (b, 0)
    tok_in = lambda b: (b + row0 // t, 0)
    lead = lambda b: (b, 0, 0)
    return pl.pallas_call(
        functools.partial(_fox_sample_kernel, past=past),
        grid=(n_batch,),
        in_specs=[
            pl.BlockSpec((t, width), tok_in),
            pl.BlockSpec((t, width), tok_in),
            pl.BlockSpec((t, width), tok_in),
            pl.BlockSpec((1, rows, HEAD_DIM), lead),
            pl.BlockSpec((1, rows, HEAD_DIM), lead),
            pl.BlockSpec((1, t, N_FOX_HEADS), lead),
            pl.BlockSpec((1, N_FOX_HEADS // 2, 2 * past), lead),
            pl.BlockSpec((1, N_FOX_HEADS // 2, 2 * t), lead),
        ],
        out_specs=pl.BlockSpec((t, width), tok),
        out_shape=jax.ShapeDtypeStruct((n_batch * t, width), F32),
        compiler_params=_params(("arbitrary",)),
        name="fox_sample",
    )(qb, knb, vnb, cache_k, cache_v, cq, ck_past, ck_new)


def _mix_kernel(*refs, seg, nseg, tiles_per_seq, has_prev, n_cast):
    if has_prev:
        (x_ref, gb_ref, gc_ref, hin_ref, pgc_ref, phin_ref, *refs) = refs
    else:
        (x_ref, gb_ref, gc_ref, hin_ref, *refs) = refs
    (buf_ref, ya_ref, cw_ref, gco_ref, gao_ref, wout_ref, gx_ref, wxq_ref, mk_ref, mv_ref, wxo_ref, gm_ref,
     *refs) = refs
    cast_in, (x2_ref, h_ref, cnew_ref, *cast_out, o_scr) = refs[:n_cast], refs[n_cast:]
    for src_ref, dst_ref in zip(cast_in, cast_out):
        dst_ref[...] = src_ref[...].astype(BF16)
    i = pl.program_id(0)
    u = gc_ref[...] * hin_ref[...]
    gb = gb_ref[...]
    w0 = cw_ref[0:1, :]
    w1 = cw_ref[1:2, :]
    w2 = cw_ref[2:3, :]
    rid = lax.broadcasted_iota(jnp.int32, (seg, u.shape[1]), 0)
    pieces = []
    for s in range(nseg):
        u_s = u[s * seg:(s + 1) * seg]
        b0 = buf_ref[s, 0:1, :]
        b1 = buf_ref[s, 1:2, :]
        if has_prev:
            pu = pgc_ref[...] * phin_ref[...]
            first = (i % tiles_per_seq) == 0
            b0 = jnp.where(first, b0, pu[6:7])
            b1 = jnp.where(first, b1, pu[7:8])
        r1 = jnp.where(rid == 0, b1, pltpu.roll(u_s, 1, 0))
        r2 = jnp.where(rid == 0, b0, jnp.where(rid == 1, b1, pltpu.roll(u_s, 2, 0)))
        y = w0 * r2 + w1 * r1 + w2 * u_s
        pieces.append(gb[s * seg:(s + 1) * seg] * y)
        cnew_ref[s] = u_s[seg - 2:seg]
    yc = pieces[0] if nseg == 1 else jnp.concatenate(pieces, axis=0)

    scale = HEAD_DIM ** -0.5
    n_mem = mk_ref.shape[1] // N_X_HEADS
    if nseg > 1:
        assert seg & (seg - 1) == 0 and n_mem & (n_mem - 1) == 0
        r = lax.broadcasted_iota(jnp.int32, (nseg * seg, nseg * n_mem), 0) >> (seg.bit_length() - 1)
        c = lax.broadcasted_iota(jnp.int32, (nseg * seg, nseg * n_mem), 1) >> (n_mem.bit_length() - 1)
        own = r == c
    cat = jnp.concatenate([_rms(yc, gco_ref[...]).astype(BF16),
                           _rms(ya_ref[...], gao_ref[...]).astype(BF16)], axis=-1)
    x1 = x_ref[...] + jnp.dot(cat, wout_ref[...], preferred_element_type=F32)
    xn = _rms(x1, gx_ref[...]).astype(BF16)
    qx = jnp.dot(xn, wxq_ref[...], preferred_element_type=F32).astype(BF16)
    for hh in range(N_X_HEADS):
        sl = slice(hh * HEAD_DIM, (hh + 1) * HEAD_DIM)
        head_rows = lambda ref: jnp.concatenate(
            [_load_head_rows(ref, s, hh, n_mem, N_X_HEADS).astype(BF16) for s in range(nseg)], axis=0)
        sc = lax.dot_general(qx[:, sl], head_rows(mk_ref), _NT, preferred_element_type=F32) * scale
        if nseg > 1:
            sc = jnp.where(own, sc, -jnp.inf)
        m = jnp.max(sc, axis=1, keepdims=True)
        p = jnp.exp(sc - m)
        l = jnp.sum(p, axis=1, keepdims=True)
        o = jnp.dot(p.astype(BF16), head_rows(mv_ref), preferred_element_type=F32) / l
        o_scr[:, sl] = o.astype(BF16)
    x2 = x1 + jnp.dot(o_scr[...], wxo_ref[...], preferred_element_type=F32)
    x2_ref[...] = x2
    h_ref[...] = _rms(x2, gm_ref[...]).astype(BF16)


def _mix(x, zc, conv_buf, ya, conv_w, g_conv_out, g_attn_out, w_out, g_xattn, w_xq, mk, mv, w_xo, g_mlp, *, tm, seq,
         zc_row0=0, cast=()):
    m, d = x.shape
    dc = zc.shape[1] // 3
    dx = w_xq.shape[1]
    mem_rows = mk.shape[1]
    has_prev = seq > tm
    seg = tm if has_prev else seq
    nseg = tm // seg
    tiles_per_seq = max(seq // tm, 1)
    t0 = zc_row0 // tm
    row = lambda i: (i, 0)
    const = lambda i: (0, 0)
    resident = functools.partial(pl.BlockSpec, index_map=const, pipeline_mode=pl.Buffered(1))
    in_specs = [
        pl.BlockSpec((tm, d), row),
        pl.BlockSpec((tm, dc), lambda i: (t0 + i, 0)),
        pl.BlockSpec((tm, dc), lambda i: (t0 + i, 1)),
        pl.BlockSpec((tm, dc), lambda i: (t0 + i, 2)),
    ]
    args = [x, zc, zc, zc]
    if has_prev:
        assert zc_row0 == 0
        prev = lambda col: (lambda i: (jnp.maximum(i * (tm // 8) - 1, 0), col))
        in_specs += [pl.BlockSpec((8, dc), prev(1)), pl.BlockSpec((8, dc), prev(2))]
        args += [zc, zc]
        buf_map = lambda i: (i // tiles_per_seq, 0, 0)
    else:
        buf_map = lambda i: (i, 0, 0)
    in_specs += [
        pl.BlockSpec((nseg, CONV_WIDTH - 1, dc), buf_map),
        pl.BlockSpec((tm, ya.shape[1]), row),
        pl.BlockSpec((CONV_WIDTH, dc), const),
        pl.BlockSpec((1, dc), const),
        pl.BlockSpec((1, ya.shape[1]), const),
        resident(w_out.shape),
        pl.BlockSpec((1, d), const),
        resident(w_xq.shape),
        pl.BlockSpec((nseg, mem_rows, HEAD_DIM), buf_map),
        pl.BlockSpec((nseg, mem_rows, HEAD_DIM), buf_map),
        resident(w_xo.shape),
        pl.BlockSpec((1, d), const),
    ]
    args += [conv_buf, ya, conv_w, g_conv_out, g_attn_out, w_out, g_xattn, w_xq, mk, mv, w_xo, g_mlp]
    steps = m // tm
    cast_specs = _cast_specs(cast, steps, lambda i: i)
    args += [w for w, _ in cast]
    return pl.pallas_call(
        functools.partial(_mix_kernel, seg=seg, nseg=nseg, tiles_per_seq=tiles_per_seq, has_prev=has_prev,
                          n_cast=len(cast)),
        grid=(steps,),
        in_specs=in_specs + cast_specs,
        out_specs=[
            pl.BlockSpec((tm, d), row),
            pl.BlockSpec((tm, d), row),
            pl.BlockSpec((nseg, CONV_WIDTH - 1, dc), buf_map),
        ] + cast_specs,
        out_shape=[
            jax.ShapeDtypeStruct((m, d), F32),
            jax.ShapeDtypeStruct((m, d), BF16),
            jax.ShapeDtypeStruct(conv_buf.shape, F32),
        ] + [jax.ShapeDtypeStruct(w.shape, BF16) for w, _ in cast],
        scratch_shapes=[pltpu.VMEM((tm, dx), BF16)],
        compiler_params=pltpu.CompilerParams(dimension_semantics=("arbitrary",), vmem_limit_bytes=BIG_VMEM_LIMIT),
        name="mix",
    )(*args)


def _memkv_kernel(x_ref, g_ref, w_ref, k_ref, v_ref):
    xn = _rms(x_ref[...], g_ref[...]).astype(BF16)
    res = jnp.dot(xn, w_ref[...], preferred_element_type=F32)
    pl.when(pl.program_id(0) == 0)(lambda: _store_head_rows(k_ref, res, N_X_HEADS))
    pl.when(pl.program_id(0) == 1)(lambda: _store_head_rows(v_ref, res, N_X_HEADS))


def _memkv(mem, g, w):
    m, d = mem.shape
    half = w.shape[1] // 2
    whole = pl.BlockSpec((m * N_X_HEADS, HEAD_DIM), lambda j: (0, 0))
    return pl.pallas_call(
        _memkv_kernel,
        grid=(2,),
        in_specs=[
            pl.BlockSpec((m, d), lambda j: (0, 0)),
            pl.BlockSpec((1, d), lambda j: (0, 0)),
            pl.BlockSpec((d, half), lambda j: (0, j)),
        ],
        out_specs=[whole, whole],
        out_shape=[jax.ShapeDtypeStruct((m * N_X_HEADS, HEAD_DIM), F32)] * 2,
        compiler_params=_params(("arbitrary",)),
        name="memkv",
    )(mem, g, w)


def _mlp_kernel(h_ref, x2_hbm, wu_ref, wd_ref, g_ref, y_ref, sem):
    i = pl.program_id(0)
    j = pl.program_id(1)
    tm = y_ref.shape[0]
    x2_copy = pltpu.make_async_copy(x2_hbm.at[pl.ds(i * tm, tm), :], y_ref, sem)

    def hidden():
        a = jnp.dot(h_ref[...], wu_ref[...], preferred_element_type=F32)
        return jnp.square(jnp.maximum(a, 0.0)).astype(BF16)

    @pl.when(j == 0)
    def _():
        x2_copy.start()
        a = hidden()
        x2_copy.wait()
        y_ref[...] += jnp.dot(a, wd_ref[...], preferred_element_type=F32)

    last = pl.num_programs(1) - 1

    @pl.when(jnp.logical_and(j > 0, j < last))
    def _():
        y_ref[...] += jnp.dot(hidden(), wd_ref[...], preferred_element_type=F32)

    @pl.when(j == last)
    def _():
        y_ref[...] = _rms(y_ref[...] + jnp.dot(hidden(), wd_ref[...], preferred_element_type=F32), g_ref[...])


def _mlp(h, x2, w_up, w_down, g_final, *, tm, tf):
    m, d = h.shape
    ff = w_up.shape[1]
    assert ff // tf >= 2
    row = lambda i, j: (i, 0)
    return pl.pallas_call(
        _mlp_kernel,
        grid=(m // tm, ff // tf),
        in_specs=[
            pl.BlockSpec((tm, d), row),
            pl.BlockSpec(memory_space=pl.ANY),
            pl.BlockSpec((d, tf), lambda i, j: (0, j)),
            pl.BlockSpec((tf, d), lambda i, j: (j, 0)),
            pl.BlockSpec((1, d), lambda i, j: (0, 0)),
        ],
        out_specs=pl.BlockSpec((tm, d), row),
        out_shape=jax.ShapeDtypeStruct((m, d), F32),
        scratch_shapes=[pltpu.SemaphoreType.DMA(())],
        compiler_params=pltpu.CompilerParams(dimension_semantics=("arbitrary", "arbitrary"),
                                             vmem_limit_bytes=BIG_VMEM_LIMIT),
        name="mlp",
    )(h, x2, w_up, w_down, g_final)


def kernel(x_prompt, x_sample, cache_k, cache_v, cache_logf, cache_conv, cache_mem_k, cache_mem_v, mem_prompt, g_mix, w_in, b_f, conv_w, g_conv_out, g_attn_out, w_out, g_xattn, g_mem, w_xq, w_xkv, w_xo, g_mlp, w_up, w_down, g_final):
    depth = w_in.shape[0]
    assert depth == 1, "single-layer trunk"
    nb, seq, d = x_prompt.shape
    db, dseq, _ = x_sample.shape
    past = cache_k.shape[2]
    heads, hd = cache_k.shape[3], cache_k.shape[4]
    assert (heads, hd) == (N_FOX_HEADS, HEAD_DIM)
    d_attn = heads * hd
    d_conv = cache_conv.shape[-1]
    n_mem, xh, xhd = cache_mem_k.shape[2:]
    assert (xh, xhd) == (N_X_HEADS, HEAD_DIM)
    n_main = 3 * d_conv + 3 * d_attn
    assert d_conv == d_attn and w_in.shape[2] == n_main + heads

    w_in_t = jnp.swapaxes(w_in[0], 0, 1)
    bf = b_f[0].reshape(1, heads)
    row = lambda g: g.reshape(1, -1)

    def mix_mlp(x, zc_row0, conv_buf, ya, seq_len, mk, mv, mix_w, mlp_w, tm):
        w_out_b, w_xq_b, w_xo_b = mix_w
        x2, h, conv_new = _mix(x, zc, conv_buf, ya, conv_w[0], row(g_conv_out[0]), row(g_attn_out[0]),
                               w_out_b, row(g_xattn[0]), w_xq_b, mk, mv, w_xo_b, row(g_mlp[0]), tm=tm, seq=seq_len,
                               zc_row0=zc_row0)
        return _mlp(h, x2, *mlp_w, row(g_final), tm=MLP_TM, tf=MLP_TF), conv_new

    xp = x_prompt.reshape(nb * seq, d)
    xs = x_sample.reshape(db * dseq, d)
    n_p, n_s = nb * seq, db * dseq
    assert n_p % max(XNORM_TM, PROJ_TM, MLP_TM, MIX_TM_PROMPT) == 0 and seq % FOX_TQ == 0 and seq % MIX_TM_PROMPT == 0
    assert n_s % max(XNORM_TM, PROJ_TM, MLP_TM, MIX_TM_SAMPLE) == 0 and MIX_TM_SAMPLE % dseq == 0
    xn, lf_p, lf_s = _xnorm(xp, xs, row(g_mix[0]), w_in_t, bf, tm=XNORM_TM, gate_row=n_main)
    proj = functools.partial(_proj, xn, w_in_t, tn=d_attn, tm=PROJ_TM, tiles_a=n_p // PROJ_TM)
    zc, qb = proj(j0=0, nj=4, mode="f32_bf16", out_scale=hd ** -0.5 * LOG2_E)
    k_p, k_s, kb = proj(j0=4, nj=1, mode="heads")
    v_p, v_s, vb = proj(j0=5, nj=1, mode="heads")

    lft = lf_p.reshape(nb, seq, heads).transpose(0, 2, 1).reshape(nb * heads, seq)
    ct = _cumsum_lanes(lft).reshape(nb, heads, seq)
    ya, *mix_w, w_xkv_b, w_up_b, w_down_b = _fox_prompt(
        qb, kb, vb, ct, ct.transpose(0, 2, 1), n_batch=nb, seq=seq, tq=FOX_TQ, tk=FOX_TK,
        cast=((w_out[0], 0), (w_xq[0], 0), (w_xo[0], 0), (w_xkv[0], 0), (w_up[0], 1), (w_down[0], 0)))
    mlp_w = (w_up_b, w_down_b)
    mk_p, mv_p = _memkv(mem_prompt.reshape(nb * n_mem, d), row(g_mem[0]), w_xkv_b)
    mk_p = mk_p.reshape(nb, n_mem * xh, xhd)
    mv_p = mv_p.reshape(nb, n_mem * xh, xhd)
    conv0 = jnp.zeros((nb, CONV_WIDTH - 1, d_conv), F32)
    y_p, conv_p = mix_mlp(xp, 0, conv0, ya, seq, mk_p, mv_p, mix_w, mlp_w, tm=MIX_TM_PROMPT)

    hp = heads // 2

    def sample_attention(qb, kb, vb, lf):
        lf_past = cache_logf[0].reshape(db, 2 * past, hp).transpose(0, 2, 1).reshape(db * hp, 2 * past)
        lf_new = lf.reshape(db, 2 * dseq, hp).transpose(0, 2, 1).reshape(db * hp, 2 * dseq)
        lf_new = jnp.pad(lf_new, ((0, 0), (0, LANES - 2 * dseq)))
        ck_past, ck_new = _cumsum_pairs(lf_past, lf_new, t_new=dseq)
        ck_new = ck_new[:, :2 * dseq].reshape(db, hp, 2 * dseq)
        cq = ck_new.reshape(db, hp, 2, dseq).transpose(0, 3, 2, 1).reshape(db, dseq, heads)
        return _fox_sample(qb, kb, vb, cache_k.reshape(db, past * heads, hd),
                           cache_v.reshape(db, past * heads, hd), cq, ck_past.reshape(db, hp, 2 * past), ck_new,
                           t=dseq, row0=n_p)

    ya = sample_attention(qb, kb, vb, lf_s)
    y_s, conv_s = mix_mlp(xs, n_p, cache_conv[0], ya, dseq, cache_mem_k.reshape(db, n_mem * xh, xhd),
                          cache_mem_v.reshape(db, n_mem * xh, xhd), mix_w, mlp_w, tm=MIX_TM_SAMPLE)

    return (y_p.reshape(nb, seq, d),
            y_s.reshape(db, dseq, d),
            k_p.reshape(1, nb, seq, heads, hd),
            v_p.reshape(1, nb, seq, heads, hd),
            lf_p.reshape(1, nb, seq, heads),
            conv_p[None],
            mk_p.reshape(1, nb, n_mem, xh, xhd),
            mv_p.reshape(1, nb, n_mem, xh, xhd),
            k_s.reshape(1, db, dseq, heads, hd),
            v_s.reshape(1, db, dseq, heads, hd),
            lf_s.reshape(1, db, dseq, heads),
            conv_s[None])
```

```python
import functools

import jax
import jax.numpy as jnp
from jax import lax
from jax.experimental import pallas as pl
from jax.experimental.pallas import tpu as pltpu

F32 = jnp.float32
BF16 = jnp.bfloat16

RMS_EPS = 1e-6
HEAD_DIM = 128
N_FOX_HEADS = 8
N_X_HEADS = 4
CONV_WIDTH = 3
LANES = 128
V_PAD_ROWS = 16
LOG2_E = 1.4426950408889634
RING_SLOTS = 4
XNORM_TM = 1024
PROJ_TM = 1024
FOX_TQ, FOX_TK = 1024, 512
MIX_TM_PROMPT = 512
MIX_TM_SAMPLE = 256
MLP_TM, MLP_TF = 1024, 1024
VMEM_LIMIT = 52 * 1024 * 1024
BIG_VMEM_LIMIT = 58 * 1024 * 1024

_NT = (((1,), (1,)), ((), ()))


def _params(semantics):
    return pltpu.CompilerParams(dimension_semantics=semantics, vmem_limit_bytes=VMEM_LIMIT)


def _rms(x, g):
    ms = jnp.mean(x * x, axis=-1, keepdims=True)
    return x * lax.rsqrt(ms + RMS_EPS) * g


def _log_sigmoid(x):
    return jnp.minimum(x, 0.0) - jnp.log1p(jnp.exp(-jnp.abs(x)))


def _store_head_rows(ref, val, heads):
    n = val.shape[0]
    for h in range(heads):
        ref[pl.ds(h, n, stride=heads), :] = val[:, h * HEAD_DIM:(h + 1) * HEAD_DIM]


def _load_head_rows(ref, lead, h, n, heads):
    return ref[lead, pl.ds(h, n, stride=heads), :]


def _xnorm_kernel(xa_hbm, xb_hbm, g_ref, wf_ref, bf_ref, xn_ref, lfa_ref, lfb_ref, ring_ref, sems, *, tiles_a):
    i = pl.program_id(0)
    n = pl.num_programs(0)
    tm = ring_ref.shape[1]
    in_a = i < tiles_a

    def tile_copy(idx, act):
        slot = idx % 3

        @pl.when(idx < tiles_a)
        def _():
            act(pltpu.make_async_copy(xa_hbm.at[pl.ds(idx * tm, tm), :], ring_ref.at[slot], sems.at[slot]))

        @pl.when(idx >= tiles_a)
        def _():
            act(pltpu.make_async_copy(xb_hbm.at[pl.ds((idx - tiles_a) * tm, tm), :], ring_ref.at[slot],
                                      sems.at[slot]))

    @pl.when(i == 0)
    def _():
        tile_copy(i, lambda c: c.start())
        pl.when(n > 1)(lambda: tile_copy(i + 1, lambda c: c.start()))

    tile_copy(i, lambda c: c.wait())
    pl.when(i + 2 < n)(lambda: tile_copy(i + 2, lambda c: c.start()))
    x = ring_ref[i % 3]
    xn = _rms(x, g_ref[...]).astype(BF16)
    xn_ref[...] = xn
    wf = wf_ref[...].astype(BF16)
    wf = jnp.concatenate([wf, jnp.zeros((LANES - wf.shape[0], wf.shape[1]), BF16)], axis=0)
    fz = lax.dot_general(xn, wf, _NT, preferred_element_type=F32)
    lf = _log_sigmoid(fz[:, :N_FOX_HEADS] + bf_ref[...])

    @pl.when(in_a)
    def _():
        lfa_ref[...] = lf

    @pl.when(jnp.logical_not(in_a))
    def _():
        lfb_ref[...] = lf


def _xnorm(xa, xb, g, w_t, b_f, *, tm, gate_row):
    d = xa.shape[1]
    tiles_a, tiles_b = xa.shape[0] // tm, xb.shape[0] // tm
    m = xa.shape[0] + xb.shape[0]
    row = lambda i: (i, 0)
    row_a = lambda i: (jnp.minimum(i, tiles_a - 1), 0)
    row_b = lambda i: (jnp.maximum(i - tiles_a, 0), 0)
    return pl.pallas_call(
        functools.partial(_xnorm_kernel, tiles_a=tiles_a),
        grid=(tiles_a + tiles_b,),
        in_specs=[
            pl.BlockSpec(memory_space=pl.ANY),
            pl.BlockSpec(memory_space=pl.ANY),
            pl.BlockSpec((1, d), lambda i: (0, 0)),
            pl.BlockSpec((N_FOX_HEADS, d), lambda i: (gate_row // N_FOX_HEADS, 0)),
            pl.BlockSpec((1, N_FOX_HEADS), lambda i: (0, 0)),
        ],
        out_specs=[pl.BlockSpec((tm, d), row), pl.BlockSpec((tm, N_FOX_HEADS), row_a),
                   pl.BlockSpec((tm, N_FOX_HEADS), row_b)],
        out_shape=[jax.ShapeDtypeStruct((m, d), BF16), jax.ShapeDtypeStruct((xa.shape[0], N_FOX_HEADS), F32),
                   jax.ShapeDtypeStruct((xb.shape[0], N_FOX_HEADS), F32)],
        scratch_shapes=[pltpu.VMEM((3, tm, d), F32), pltpu.SemaphoreType.DMA((3,))],
        compiler_params=_params(("arbitrary",)),
        name="xnorm",
    )(xa, xb, g, w_t, b_f)


def _proj_kernel(xn_ref, wt_ref, *refs, mode, n_f32, out_scale, tiles_a):
    out_refs, wb_ref = refs[:-1], refs[-1]
    j = pl.program_id(0)
    i = pl.program_id(1)

    @pl.when(i == 0)
    def _():
        wb_ref[...] = wt_ref[...].astype(BF16)

    def matmul():
        return lax.dot_general(xn_ref[...], wb_ref[...], _NT, preferred_element_type=F32)

    if mode == "f32_bf16":
        @pl.when(j < n_f32)
        def _():
            out_refs[0][...] = matmul()

        @pl.when(j >= n_f32)
        def _():
            out_refs[1][...] = (matmul() * jnp.where(i < tiles_a, out_scale, 1.0)).astype(BF16)
    else:
        def to(head_ref):
            res = matmul()
            _store_head_rows(head_ref, res, N_FOX_HEADS)
            out_refs[2][...] = res.astype(BF16)

        pl.when(i < tiles_a)(lambda: to(out_refs[0]))
        pl.when(i >= tiles_a)(lambda: to(out_refs[1]))


def _proj(xn, w_t, *, j0, nj, tn, tm, mode, tiles_a, out_scale=1.0):
    m, d = xn.shape
    n_i = m // tm
    tiles_b = n_i - tiles_a
    n_f32 = nj - 1
    if mode == "heads":
        assert nj == 1
        head_block = (tm * N_FOX_HEADS, HEAD_DIM)
        out_specs = [pl.BlockSpec(head_block, lambda j, i: (jnp.minimum(i, tiles_a - 1), 0)),
                     pl.BlockSpec(head_block, lambda j, i: (jnp.maximum(i - tiles_a, 0), 0)),
                     pl.BlockSpec((tm, tn), lambda j, i: (i, 0))]
        out_shape = [jax.ShapeDtypeStruct((tiles_a * tm * N_FOX_HEADS, HEAD_DIM), F32),
                     jax.ShapeDtypeStruct((tiles_b * tm * N_FOX_HEADS, HEAD_DIM), F32),
                     jax.ShapeDtypeStruct((m, tn), BF16)]
    else:
        out_specs = [pl.BlockSpec((tm, tn), lambda j, i: (jnp.where(j < n_f32, i, n_i - 1),
                                                          jnp.minimum(j, n_f32 - 1))),
                     pl.BlockSpec((tm, tn), lambda j, i: (jnp.where(j < n_f32, 0, i), 0))]
        out_shape = [jax.ShapeDtypeStruct((m, n_f32 * tn), F32), jax.ShapeDtypeStruct((m, tn), BF16)]
    return pl.pallas_call(
        functools.partial(_proj_kernel, mode=mode, n_f32=n_f32, out_scale=out_scale, tiles_a=tiles_a),
        grid=(nj, m // tm),
        in_specs=[
            pl.BlockSpec((tm, d), lambda j, i: (i, 0)),
            pl.BlockSpec((tn, d), lambda j, i: (j0 + j, 0)),
        ],
        out_specs=out_specs,
        out_shape=out_shape,
        scratch_shapes=[pltpu.VMEM((tn, d), BF16)],
        compiler_params=_params(("arbitrary", "arbitrary")),
        name="proj_" + mode,
    )(xn, w_t)


def _cumsum_kernel(a_ref, o_ref):
    rows, length = a_ref.shape
    r = lax.broadcasted_iota(jnp.int32, (LANES, LANES), 0)
    c = lax.broadcasted_iota(jnp.int32, (LANES, LANES), 1)
    upper = (r <= c).astype(BF16)
    carry = jnp.zeros((rows, 1), F32)
    for b in range(length // LANES):
        a = a_ref[:, b * LANES:(b + 1) * LANES]
        hi = a.astype(BF16)
        r1 = a - hi.astype(F32)
        mid = r1.astype(BF16)
        lo = (r1 - mid.astype(F32)).astype(BF16)
        s = (jnp.dot(hi, upper, preferred_element_type=F32)
             + jnp.dot(mid, upper, preferred_element_type=F32)
             + jnp.dot(lo, upper, preferred_element_type=F32)) + carry
        o_ref[:, b * LANES:(b + 1) * LANES] = s
        carry = s[:, LANES - 1:LANES]


def _cumsum_lanes(a):
    return pl.pallas_call(
        _cumsum_kernel,
        out_shape=jax.ShapeDtypeStruct(a.shape, F32),
        compiler_params=pltpu.CompilerParams(vmem_limit_bytes=VMEM_LIMIT),
        name="cumsum",
    )(a)


def _cumsum_pairs_kernel(p_ref, n_ref, op_ref, on_ref, *, t_new):
    rows, length = p_ref.shape
    r = lax.broadcasted_iota(jnp.int32, (LANES, LANES), 0)
    c = lax.broadcasted_iota(jnp.int32, (LANES, LANES), 1)
    upper = ((r <= c) & (((r ^ c) & 1) == 0)).astype(BF16)
    unzip = ((r < 2 * t_new) & (c == (r & 1) * t_new + (r >> 1))).astype(BF16)
    lane_odd = (lax.broadcasted_iota(jnp.int32, (rows, LANES), 1) & 1) == 1

    def times(a, mat):
        hi, mid, lo = _split3(a)
        return (jnp.dot(hi.astype(BF16), mat, preferred_element_type=F32)
                + jnp.dot(mid.astype(BF16), mat, preferred_element_type=F32)
                + jnp.dot(lo.astype(BF16), mat, preferred_element_type=F32))

    carry = jnp.zeros((rows, LANES), F32)
    for b in range(length // LANES):
        s = times(p_ref[:, b * LANES:(b + 1) * LANES], upper) + carry
        op_ref[:, b * LANES:(b + 1) * LANES] = s
        carry = jnp.where(lane_odd, s[:, LANES - 1:LANES], s[:, LANES - 2:LANES - 1])
    on_ref[...] = times(times(n_ref[...], upper) + carry, unzip)


def _cumsum_pairs(p, n, *, t_new):
    return pl.pallas_call(
        functools.partial(_cumsum_pairs_kernel, t_new=t_new),
        out_shape=[jax.ShapeDtypeStruct(p.shape, F32), jax.ShapeDtypeStruct(n.shape, F32)],
        compiler_params=pltpu.CompilerParams(vmem_limit_bytes=VMEM_LIMIT),
        name="cumsum_pairs",
    )(p, n)


def _split3(x):
    hi = x.astype(BF16).astype(F32)
    r = x - hi
    mid = r.astype(BF16).astype(F32)
    lo = (r - mid).astype(BF16).astype(F32)
    return hi, mid, lo


def _fox_prompt_kernel(q_ref, k_ref, v_ref, cq_ref, ck_ref, *refs, tq, tk, n_cast):
    assert tq == 2 * tk
    cast_in, (o_ref, *cast_out), scratch = refs[:n_cast], refs[n_cast:2 * n_cast + 1], refs[2 * n_cast + 1:]
    kaug_ref, vt_ref, qaug0_ref, qaug1_ref, acc0_ref, acc1_ref, s0_ref, s1_ref = scratch
    for src_ref, dst_ref in zip(cast_in, cast_out):
        dst_ref[...] = src_ref[...].astype(BF16)
    qaug_refs, acc_refs = (qaug0_ref, qaug1_ref), (acc0_ref, acc1_ref)
    h = pl.program_id(1)
    seq = kaug_ref.shape[0]

    lane8 = lax.broadcasted_iota(jnp.int32, (tk, N_FOX_HEADS), 1)
    lane = lax.broadcasted_iota(jnp.int32, (tk, HEAD_DIM), 1)
    ones_row = (lax.broadcasted_iota(jnp.int32, (V_PAD_ROWS, tk), 0) == 0).astype(BF16)

    def prep(c):
        rows = slice(c * tk, (c + 1) * tk)
        kaug_ref[rows, 0:HEAD_DIM] = k_ref[rows, :]
        col = jnp.sum(jnp.where(lane8 == h, ck_ref[0, rows, :], 0.0), axis=1, keepdims=True) * LOG2_E
        hi, mid, lo = _split3(col)
        aug = jnp.where(lane == 0, hi, jnp.where(lane == 1, mid, jnp.where(lane == 2, lo,
                        jnp.where(lane < 6, 1.0, 0.0))))
        kaug_ref[rows, HEAD_DIM:2 * HEAD_DIM] = aug.astype(BF16)
        vt_ref[c, 0:HEAD_DIM, :] = v_ref[rows, :].astype(F32).T.astype(BF16)
        vt_ref[c, HEAD_DIM:, :] = ones_row

    sub = lax.broadcasted_iota(jnp.int32, (HEAD_DIM, tq), 0)

    def scores(qaug_ref, ki, dst_ref, lane0=0):
        dst_ref[:, lane0:] = jnp.dot(kaug_ref[ki * tk:(ki + 1) * tk, :], qaug_ref[:, lane0:],
                                     preferred_element_type=F32)

    def softmax_pv(acc_ref, ki, m_all, s_ref, lane0=0, mask_offset=None):
        m_prev = m_all[:, lane0:]
        s = s_ref[:, lane0:]
        if mask_offset is not None:
            r = lax.broadcasted_iota(jnp.int32, s.shape, 0) + mask_offset
            c = lax.broadcasted_iota(jnp.int32, s.shape, 1) + lane0
            s = jnp.where(r <= c, s, -jnp.inf)
        m_new = jnp.maximum(m_prev, jnp.max(s, axis=0, keepdims=True))
        alpha = jnp.exp2(m_prev - m_new)
        p = jnp.exp2(s - m_new).astype(BF16)
        pv = jnp.dot(vt_ref[ki], p, preferred_element_type=F32)
        acc_ref[:, lane0:] = alpha * acc_ref[:, lane0:] + pv
        return jnp.concatenate([m_all[:, :lane0], m_new], axis=1) if lane0 else m_new

    s_refs = (s0_ref, s1_ref)
    for qi in range(seq // tq):
        qaug_ref, acc_ref = qaug_refs[qi % 2], acc_refs[qi % 2]
        qrows = slice(qi * tq, (qi + 1) * tq)
        qaug_ref[0:HEAD_DIM, :] = q_ref[qrows, :].astype(F32).T.astype(BF16)
        hi, mid, lo = _split3(cq_ref[0, pl.ds(h, 1), qrows] * LOG2_E)
        qaug_ref[HEAD_DIM:2 * HEAD_DIM, :] = jnp.where(
            sub < 3, -1.0, jnp.where(sub == 3, hi, jnp.where(sub == 4, mid, jnp.where(sub == 5, lo, 0.0)))
        ).astype(BF16)
        acc_ref[...] = jnp.zeros(acc_ref.shape, F32)
        n_full = 2 * qi
        prep(n_full)
        prep(n_full + 1)
        m = jnp.full((1, tq), -jnp.inf, F32)
        scores(qaug_ref, 0, s_refs[0])
        for ki in range(n_full):
            scores(qaug_ref, ki + 1, s_refs[(ki + 1) % 2])
            m = softmax_pv(acc_ref, ki, m, s_refs[ki % 2])
        scores(qaug_ref, n_full + 1, s_refs[1], lane0=tk)
        m = softmax_pv(acc_ref, n_full, m, s_refs[0], mask_offset=0)
        softmax_pv(acc_ref, n_full + 1, m, s_refs[1], lane0=tk, mask_offset=tk)
        o_ref[qrows, :] = (acc_ref[0:HEAD_DIM, :] / acc_ref[HEAD_DIM:HEAD_DIM + 1, :]).T


def _cast_specs(cast, steps, step_of):
    specs = []
    for w, axis in cast:
        block = tuple(n // steps if a == axis else n for a, n in enumerate(w.shape))
        specs.append(pl.BlockSpec(block, (lambda *g: (step_of(*g), 0)) if axis == 0
                                  else (lambda *g: (0, step_of(*g)))))
    return specs


def _fox_prompt(qb, kb, vb, cq, ck, *, n_batch, seq, tq, tk, cast=()):
    m = n_batch * seq
    head = lambda n, h: (n, h)
    cast_specs = _cast_specs(cast, n_batch * N_FOX_HEADS, lambda n, h: n * N_FOX_HEADS + h)
    return pl.pallas_call(
        functools.partial(_fox_prompt_kernel, tq=tq, tk=tk, n_cast=len(cast)),
        grid=(n_batch, N_FOX_HEADS),
        in_specs=[
            pl.BlockSpec((seq, HEAD_DIM), head),
            pl.BlockSpec((seq, HEAD_DIM), head),
            pl.BlockSpec((seq, HEAD_DIM), head),
            pl.BlockSpec((1, N_FOX_HEADS, seq), lambda n, h: (n, 0, 0)),
            pl.BlockSpec((1, seq, N_FOX_HEADS), lambda n, h: (n, 0, 0)),
        ] + cast_specs,
        out_specs=[pl.BlockSpec((seq, HEAD_DIM), head)] + cast_specs,
        out_shape=[jax.ShapeDtypeStruct((m, N_FOX_HEADS * HEAD_DIM), F32)]
        + [jax.ShapeDtypeStruct(w.shape, BF16) for w, _ in cast],
        scratch_shapes=[
            pltpu.VMEM((seq, 2 * HEAD_DIM), BF16),
            pltpu.VMEM((seq // tk, HEAD_DIM + V_PAD_ROWS, tk), BF16),
            pltpu.VMEM((2 * HEAD_DIM, tq), BF16),
            pltpu.VMEM((2 * HEAD_DIM, tq), BF16),
            pltpu.VMEM((HEAD_DIM + V_PAD_ROWS, tq), F32),
            pltpu.VMEM((HEAD_DIM + V_PAD_ROWS, tq), F32),
            pltpu.VMEM((tk, tq), F32),
            pltpu.VMEM((tk, tq), F32),
        ],
        compiler_params=pltpu.CompilerParams(dimension_semantics=("arbitrary", "arbitrary"),
                                             vmem_limit_bytes=BIG_VMEM_LIMIT),
        name="fox_prompt",
    )(qb, kb, vb, cq, ck, *[w for w, _ in cast])


def _fox_sample_kernel(q_ref, kn_ref, vn_ref, kc_hbm, vc_hbm, cq_ref, ckp_ref, ckn_ref, o_ref, ring_ref, p_ref, sems,
                       *, past):
    b = pl.program_id(0)
    n_batch = pl.num_programs(0)
    k_slot = (2 * b) % RING_SLOTS
    v_slot = (2 * b + 1) % RING_SLOTS
    ahead = RING_SLOTS // 2

    def chunk(src_hbm, batch, slot):
        return pltpu.make_async_copy(src_hbm.at[batch], ring_ref.at[slot], sems.at[slot])

    @pl.when(b == 0)
    def _():
        for a in range(ahead):
            @pl.when(a < n_batch)
            def _():
                chunk(kc_hbm, a, (2 * a) % RING_SLOTS).start()
                chunk(vc_hbm, a, (2 * a + 1) % RING_SLOTS).start()

    chunk(kc_hbm, b, k_slot).wait()

    scale = HEAD_DIM ** -0.5
    t = q_ref.shape[0]
    half = N_FOX_HEADS // 2
    r = lax.broadcasted_iota(jnp.int32, (2 * t, 2 * past), 0)
    c = lax.broadcasted_iota(jnp.int32, (2 * t, 2 * past), 1)
    own_past = (r >= t) == ((c & 1) == 1)
    rn = lax.broadcasted_iota(jnp.int32, (2 * t, 2 * t), 0)
    cn = lax.broadcasted_iota(jnp.int32, (2 * t, 2 * t), 1)
    own_new = ((rn >= t) == (cn >= t)) & ((cn & (t - 1)) <= (rn & (t - 1)))
    def both(ref, j):
        return jnp.concatenate([ref[:, j * HEAD_DIM:(j + 1) * HEAD_DIM],
                                ref[:, (j + half) * HEAD_DIM:(j + half + 1) * HEAD_DIM]], axis=0)

    new_part = []
    for j in range(half):
        q = both(q_ref, j)
        kp = ring_ref[k_slot, pl.ds(j, 2 * past, stride=half), :].astype(BF16)
        cq = jnp.concatenate([cq_ref[0, :, j:j + 1], cq_ref[0, :, j + half:j + half + 1]], axis=0)
        s1 = lax.dot_general(q, kp, _NT, preferred_element_type=F32) * scale
        s1 = jnp.where(own_past, s1 + (cq - ckp_ref[0, j:j + 1, :]), -jnp.inf)
        s2 = lax.dot_general(q, both(kn_ref, j), _NT, preferred_element_type=F32) * scale
        s2 = jnp.where(own_new, s2 + (cq - ckn_ref[0, j:j + 1, :]), -jnp.inf)
        m = jnp.maximum(jnp.max(s1, axis=1, keepdims=True), jnp.max(s2, axis=1, keepdims=True))
        p1 = jnp.exp(s1 - m)
        p2 = jnp.exp(s2 - m)
        l = jnp.sum(p1, axis=1, keepdims=True) + jnp.sum(p2, axis=1, keepdims=True)
        p_ref[j] = p1.astype(BF16)
        new_part.append((p2.astype(BF16), l))

    pl.when(b + ahead < n_batch)(lambda: chunk(kc_hbm, b + ahead, k_slot).start())
    chunk(vc_hbm, b, v_slot).wait()

    for j in range(half):
        p2, l = new_part[j]
        vp = ring_ref[v_slot, pl.ds(j, 2 * past, stride=half), :].astype(BF16)
        o = (jnp.dot(p_ref[j], vp, preferred_element_type=F32)
             + jnp.dot(p2, both(vn_ref, j), preferred_element_type=F32)) / l
        o_ref[:, j * HEAD_DIM:(j + 1) * HEAD_DIM] = o[:t]
        o_ref[:, (j + half) * HEAD_DIM:(j + half + 1) * HEAD_DIM] = o[t:]

    pl.when(b + ahead < n_batch)(lambda: chunk(vc_hbm, b + ahead, v_slot).start())


def _fox_sample(qb, knb, vnb, cache_k, cache_v, cq, ck_past, ck_new, *, t, row0):
    n_batch, rows, _ = cache_k.shape
    past = rows // N_FOX_HEADS
    width = qb.shape[1]
    tok = lambda b: (b, 0)
    tok_in = lambda b: (b + row0 // t, 0)
    lead = lambda b: (b, 0, 0)
    return pl.pallas_call(
        functools.partial(_fox_sample_kernel, past=past),
        grid=(n_batch,),
        in_specs=[
            pl.BlockSpec((t, width), tok_in),
            pl.BlockSpec((t, width), tok_in),
            pl.BlockSpec((t, width), tok_in),
            pl.BlockSpec(memory_space=pl.ANY),
            pl.BlockSpec(memory_space=pl.ANY),
            pl.BlockSpec((1, t, N_FOX_HEADS), lead),
            pl.BlockSpec((1, N_FOX_HEADS // 2, 2 * past), lead),
            pl.BlockSpec((1, N_FOX_HEADS // 2, 2 * t), lead),
        ],
        out_specs=pl.BlockSpec((t, width), tok),
        out_shape=jax.ShapeDtypeStruct((n_batch * t, width), F32),
        scratch_shapes=[
            pltpu.VMEM((RING_SLOTS, rows, HEAD_DIM), F32),
            pltpu.VMEM((N_FOX_HEADS // 2, 2 * t, 2 * past), BF16),
            pltpu.SemaphoreType.DMA((RING_SLOTS,)),
        ],
        compiler_params=_params(("arbitrary",)),
        name="fox_sample",
    )(qb, knb, vnb, cache_k, cache_v, cq, ck_past, ck_new)


def _mix_kernel(*refs, seg, nseg, tiles_per_seq, has_prev, n_cast):
    if has_prev:
        (x_ref, gb_ref, gc_ref, hin_ref, pgc_ref, phin_ref, *refs) = refs
    else:
        (x_ref, gb_ref, gc_ref, hin_ref, *refs) = refs
    (buf_ref, ya_ref, cw_ref, gco_ref, gao_ref, wout_ref, gx_ref, wxq_ref, mk_ref, mv_ref, wxo_ref, gm_ref,
     *refs) = refs
    cast_in, (x2_ref, h_ref, cnew_ref, *cast_out, o_scr) = refs[:n_cast], refs[n_cast:]
    for src_ref, dst_ref in zip(cast_in, cast_out):
        dst_ref[...] = src_ref[...].astype(BF16)
    i = pl.program_id(0)
    u = gc_ref[...] * hin_ref[...]
    gb = gb_ref[...]
    w0 = cw_ref[0:1, :]
    w1 = cw_ref[1:2, :]
    w2 = cw_ref[2:3, :]
    rid = lax.broadcasted_iota(jnp.int32, (seg, u.shape[1]), 0)
    pieces = []
    for s in range(nseg):
        u_s = u[s * seg:(s + 1) * seg]
        b0 = buf_ref[s, 0:1, :]
        b1 = buf_ref[s, 1:2, :]
        if has_prev:
            pu = pgc_ref[...] * phin_ref[...]
            first = (i % tiles_per_seq) == 0
            b0 = jnp.where(first, b0, pu[6:7])
            b1 = jnp.where(first, b1, pu[7:8])
        r1 = jnp.where(rid == 0, b1, pltpu.roll(u_s, 1, 0))
        r2 = jnp.where(rid == 0, b0, jnp.where(rid == 1, b1, pltpu.roll(u_s, 2, 0)))
        y = w0 * r2 + w1 * r1 + w2 * u_s
        pieces.append(gb[s * seg:(s + 1) * seg] * y)
        cnew_ref[s] = u_s[seg - 2:seg]
    yc = pieces[0] if nseg == 1 else jnp.concatenate(pieces, axis=0)

    scale = HEAD_DIM ** -0.5
    n_mem = mk_ref.shape[1] // N_X_HEADS
    if nseg > 1:
        assert seg & (seg - 1) == 0 and n_mem & (n_mem - 1) == 0
        r = lax.broadcasted_iota(jnp.int32, (nseg * seg, nseg * n_mem), 0) >> (seg.bit_length() - 1)
        c = lax.broadcasted_iota(jnp.int32, (nseg * seg, nseg * n_mem), 1) >> (n_mem.bit_length() - 1)
        own = r == c
    cat = jnp.concatenate([_rms(yc, gco_ref[...]).astype(BF16),
                           _rms(ya_ref[...], gao_ref[...]).astype(BF16)], axis=-1)
    x1 = x_ref[...] + jnp.dot(cat, wout_ref[...], preferred_element_type=F32)
    xn = _rms(x1, gx_ref[...]).astype(BF16)
    qx = jnp.dot(xn, wxq_ref[...], preferred_element_type=F32).astype(BF16)
    for hh in range(N_X_HEADS):
        sl = slice(hh * HEAD_DIM, (hh + 1) * HEAD_DIM)
        head_rows = lambda ref: jnp.concatenate(
            [_load_head_rows(ref, s, hh, n_mem, N_X_HEADS).astype(BF16) for s in range(nseg)], axis=0)
        sc = lax.dot_general(qx[:, sl], head_rows(mk_ref), _NT, preferred_element_type=F32) * scale
        if nseg > 1:
            sc = jnp.where(own, sc, -jnp.inf)
        m = jnp.max(sc, axis=1, keepdims=True)
        p = jnp.exp(sc - m)
        l = jnp.sum(p, axis=1, keepdims=True)
        o = jnp.dot(p.astype(BF16), head_rows(mv_ref), preferred_element_type=F32) / l
        o_scr[:, sl] = o.astype(BF16)
    x2 = x1 + jnp.dot(o_scr[...], wxo_ref[...], preferred_element_type=F32)
    x2_ref[...] = x2
    h_ref[...] = _rms(x2, gm_ref[...]).astype(BF16)


def _mix(x, zc, conv_buf, ya, conv_w, g_conv_out, g_attn_out, w_out, g_xattn, w_xq, mk, mv, w_xo, g_mlp, *, tm, seq,
         zc_row0=0, cast=()):
    m, d = x.shape
    dc = zc.shape[1] // 3
    dx = w_xq.shape[1]
    mem_rows = mk.shape[1]
    has_prev = seq > tm
    seg = tm if has_prev else seq
    nseg = tm // seg
    tiles_per_seq = max(seq // tm, 1)
    t0 = zc_row0 // tm
    row = lambda i: (i, 0)
    const = lambda i: (0, 0)
    resident = functools.partial(pl.BlockSpec, index_map=const, pipeline_mode=pl.Buffered(1))
    in_specs = [
        pl.BlockSpec((tm, d), row),
        pl.BlockSpec((tm, dc), lambda i: (t0 + i, 0)),
        pl.BlockSpec((tm, dc), lambda i: (t0 + i, 1)),
        pl.BlockSpec((tm, dc), lambda i: (t0 + i, 2)),
    ]
    args = [x, zc, zc, zc]
    if has_prev:
        assert zc_row0 == 0
        prev = lambda col: (lambda i: (jnp.maximum(i * (tm // 8) - 1, 0), col))
        in_specs += [pl.BlockSpec((8, dc), prev(1)), pl.BlockSpec((8, dc), prev(2))]
        args += [zc, zc]
        buf_map = lambda i: (i // tiles_per_seq, 0, 0)
    else:
        buf_map = lambda i: (i, 0, 0)
    in_specs += [
        pl.BlockSpec((nseg, CONV_WIDTH - 1, dc), buf_map),
        pl.BlockSpec((tm, ya.shape[1]), row),
        pl.BlockSpec((CONV_WIDTH, dc), const),
        pl.BlockSpec((1, dc), const),
        pl.BlockSpec((1, ya.shape[1]), const),
        resident(w_out.shape),
        pl.BlockSpec((1, d), const),
        resident(w_xq.shape),
        pl.BlockSpec((nseg, mem_rows, HEAD_DIM), buf_map),
        pl.BlockSpec((nseg, mem_rows, HEAD_DIM), buf_map),
        resident(w_xo.shape),
        pl.BlockSpec((1, d), const),
    ]
    args += [conv_buf, ya, conv_w, g_conv_out, g_attn_out, w_out, g_xattn, w_xq, mk, mv, w_xo, g_mlp]
    steps = m // tm
    cast_specs = _cast_specs(cast, steps, lambda i: i)
    args += [w for w, _ in cast]
    return pl.pallas_call(
        functools.partial(_mix_kernel, seg=seg, nseg=nseg, tiles_per_seq=tiles_per_seq, has_prev=has_prev,
                          n_cast=len(cast)),
        grid=(steps,),
        in_specs=in_specs + cast_specs,
        out_specs=[
            pl.BlockSpec((tm, d), row),
            pl.BlockSpec((tm, d), row),
            pl.BlockSpec((nseg, CONV_WIDTH - 1, dc), buf_map),
        ] + cast_specs,
        out_shape=[
            jax.ShapeDtypeStruct((m, d), F32),
            jax.ShapeDtypeStruct((m, d), BF16),
            jax.ShapeDtypeStruct(conv_buf.shape, F32),
        ] + [jax.ShapeDtypeStruct(w.shape, BF16) for w, _ in cast],
        scratch_shapes=[pltpu.VMEM((tm, dx), BF16)],
        compiler_params=pltpu.CompilerParams(dimension_semantics=("arbitrary",), vmem_limit_bytes=BIG_VMEM_LIMIT),
        name="mix",
    )(*args)


def _memkv_kernel(x_ref, g_ref, w_ref, k_ref, v_ref):
    xn = _rms(x_ref[...], g_ref[...]).astype(BF16)
    res = jnp.dot(xn, w_ref[...], preferred_element_type=F32)
    pl.when(pl.program_id(0) == 0)(lambda: _store_head_rows(k_ref, res, N_X_HEADS))
    pl.when(pl.program_id(0) == 1)(lambda: _store_head_rows(v_ref, res, N_X_HEADS))


def _memkv(mem, g, w):
    m, d = mem.shape
    half = w.shape[1] // 2
    whole = pl.BlockSpec((m * N_X_HEADS, HEAD_DIM), lambda j: (0, 0))
    return pl.pallas_call(
        _memkv_kernel,
        grid=(2,),
        in_specs=[
            pl.BlockSpec((m, d), lambda j: (0, 0)),
            pl.BlockSpec((1, d), lambda j: (0, 0)),
            pl.BlockSpec((d, half), lambda j: (0, j)),
        ],
        out_specs=[whole, whole],
        out_shape=[jax.ShapeDtypeStruct((m * N_X_HEADS, HEAD_DIM), F32)] * 2,
        compiler_params=_params(("arbitrary",)),
        name="memkv",
    )(mem, g, w)


def _mlp_kernel(h_ref, x2_hbm, wu_ref, wd_ref, g_ref, y_ref, sem):
    i = pl.program_id(0)
    j = pl.program_id(1)
    tm = y_ref.shape[0]
    x2_copy = pltpu.make_async_copy(x2_hbm.at[pl.ds(i * tm, tm), :], y_ref, sem)

    def hidden():
        a = jnp.dot(h_ref[...], wu_ref[...], preferred_element_type=F32)
        return jnp.square(jnp.maximum(a, 0.0)).astype(BF16)

    @pl.when(j == 0)
    def _():
        x2_copy.start()
        a = hidden()
        x2_copy.wait()
        y_ref[...] += jnp.dot(a, wd_ref[...], preferred_element_type=F32)

    @pl.when(j > 0)
    def _():
        y_ref[...] += jnp.dot(hidden(), wd_ref[...], preferred_element_type=F32)

    @pl.when(j == pl.num_programs(1) - 1)
    def _():
        y_ref[...] = _rms(y_ref[...], g_ref[...])


def _mlp(h, x2, w_up, w_down, g_final, *, tm, tf):
    m, d = h.shape
    ff = w_up.shape[1]
    row = lambda i, j: (i, 0)
    return pl.pallas_call(
        _mlp_kernel,
        grid=(m // tm, ff // tf),
        in_specs=[
            pl.BlockSpec((tm, d), row),
            pl.BlockSpec(memory_space=pl.ANY),
            pl.BlockSpec((d, tf), lambda i, j: (0, j)),
            pl.BlockSpec((tf, d), lambda i, j: (j, 0)),
            pl.BlockSpec((1, d), lambda i, j: (0, 0)),
        ],
        out_specs=pl.BlockSpec((tm, d), row),
        out_shape=jax.ShapeDtypeStruct((m, d), F32),
        scratch_shapes=[pltpu.SemaphoreType.DMA(())],
        compiler_params=pltpu.CompilerParams(dimension_semantics=("arbitrary", "arbitrary"),
                                             vmem_limit_bytes=BIG_VMEM_LIMIT),
        name="mlp",
    )(h, x2, w_up, w_down, g_final)


def kernel(x_prompt, x_sample, cache_k, cache_v, cache_logf, cache_conv, cache_mem_k, cache_mem_v, mem_prompt, g_mix, w_in, b_f, conv_w, g_conv_out, g_attn_out, w_out, g_xattn, g_mem, w_xq, w_xkv, w_xo, g_mlp, w_up, w_down, g_final):
    depth = w_in.shape[0]
    assert depth == 1, "single-layer trunk"
    nb, seq, d = x_prompt.shape
    db, dseq, _ = x_sample.shape
    past = cache_k.shape[2]
    heads, hd = cache_k.shape[3], cache_k.shape[4]
    assert (heads, hd) == (N_FOX_HEADS, HEAD_DIM)
    d_attn = heads * hd
    d_conv = cache_conv.shape[-1]
    n_mem, xh, xhd = cache_mem_k.shape[2:]
    assert (xh, xhd) == (N_X_HEADS, HEAD_DIM)
    n_main = 3 * d_conv + 3 * d_attn
    assert d_conv == d_attn and w_in.shape[2] == n_main + heads

    w_in_t = jnp.swapaxes(w_in[0], 0, 1)
    bf = b_f[0].reshape(1, heads)
    row = lambda g: g.reshape(1, -1)

    def mix_mlp(x, zc_row0, conv_buf, ya, seq_len, mk, mv, mix_w, mlp_w, tm):
        w_out_b, w_xq_b, w_xo_b = mix_w
        x2, h, conv_new = _mix(x, zc, conv_buf, ya, conv_w[0], row(g_conv_out[0]), row(g_attn_out[0]),
                               w_out_b, row(g_xattn[0]), w_xq_b, mk, mv, w_xo_b, row(g_mlp[0]), tm=tm, seq=seq_len,
                               zc_row0=zc_row0)
        return _mlp(h, x2, *mlp_w, row(g_final), tm=MLP_TM, tf=MLP_TF), conv_new

    xp = x_prompt.reshape(nb * seq, d)
    xs = x_sample.reshape(db * dseq, d)
    n_p, n_s = nb * seq, db * dseq
    assert n_p % max(XNORM_TM, PROJ_TM, MLP_TM, MIX_TM_PROMPT) == 0 and seq % FOX_TQ == 0 and seq % MIX_TM_PROMPT == 0
    assert n_s % max(XNORM_TM, PROJ_TM, MLP_TM, MIX_TM_SAMPLE) == 0 and MIX_TM_SAMPLE % dseq == 0
    xn, lf_p, lf_s = _xnorm(xp, xs, row(g_mix[0]), w_in_t, bf, tm=XNORM_TM, gate_row=n_main)
    proj = functools.partial(_proj, xn, w_in_t, tn=d_attn, tm=PROJ_TM, tiles_a=n_p // PROJ_TM)
    zc, qb = proj(j0=0, nj=4, mode="f32_bf16", out_scale=hd ** -0.5 * LOG2_E)
    k_p, k_s, kb = proj(j0=4, nj=1, mode="heads")
    v_p, v_s, vb = proj(j0=5, nj=1, mode="heads")

    lft = lf_p.reshape(nb, seq, heads).transpose(0, 2, 1).reshape(nb * heads, seq)
    ct = _cumsum_lanes(lft).reshape(nb, heads, seq)
    ya, *mix_w, w_xkv_b, w_up_b, w_down_b = _fox_prompt(
        qb, kb, vb, ct, ct.transpose(0, 2, 1), n_batch=nb, seq=seq, tq=FOX_TQ, tk=FOX_TK,
        cast=((w_out[0], 0), (w_xq[0], 0), (w_xo[0], 0), (w_xkv[0], 0), (w_up[0], 1), (w_down[0], 0)))
    mlp_w = (w_up_b, w_down_b)
    mk_p, mv_p = _memkv(mem_prompt.reshape(nb * n_mem, d), row(g_mem[0]), w_xkv_b)
    mk_p = mk_p.reshape(nb, n_mem * xh, xhd)
    mv_p = mv_p.reshape(nb, n_mem * xh, xhd)
    conv0 = jnp.zeros((nb, CONV_WIDTH - 1, d_conv), F32)
    y_p, conv_p = mix_mlp(xp, 0, conv0, ya, seq, mk_p, mv_p, mix_w, mlp_w, tm=MIX_TM_PROMPT)

    hp = heads // 2

    def sample_attention(qb, kb, vb, lf):
        lf_past = cache_logf[0].reshape(db, 2 * past, hp).transpose(0, 2, 1).reshape(db * hp, 2 * past)
        lf_new = lf.reshape(db, 2 * dseq, hp).transpose(0, 2, 1).reshape(db * hp, 2 * dseq)
        lf_new = jnp.pad(lf_new, ((0, 0), (0, LANES - 2 * dseq)))
        ck_past, ck_new = _cumsum_pairs(lf_past, lf_new, t_new=dseq)
        ck_new = ck_new[:, :2 * dseq].reshape(db, hp, 2 * dseq)
        cq = ck_new.reshape(db, hp, 2, dseq).transpose(0, 3, 2, 1).reshape(db, dseq, heads)
        return _fox_sample(qb, kb, vb, cache_k.reshape(db, past * heads, hd),
                           cache_v.reshape(db, past * heads, hd), cq, ck_past.reshape(db, hp, 2 * past), ck_new,
                           t=dseq, row0=n_p)

    ya = sample_attention(qb, kb, vb, lf_s)
    y_s, conv_s = mix_mlp(xs, n_p, cache_conv[0], ya, dseq, cache_mem_k.reshape(db, n_mem * xh, xhd),
                          cache_mem_v.reshape(db, n_mem * xh, xhd), mix_w, mlp_w, tm=MIX_TM_SAMPLE)

    return (y_p.reshape(nb, seq, d),
            y_s.reshape(db, dseq, d),
            k_p.reshape(1, nb, seq, heads, hd),
            v_p.reshape(1, nb, seq, heads, hd),
            lf_p.reshape(1, nb, seq, heads),
            conv_p[None],
            mk_p.reshape(1, nb, n_mem, xh, xhd),
            mv_p.reshape(1, nb, n_mem, xh, xhd),
            k_s.reshape(1, db, dseq, heads, hd),
            v_s.reshape(1, db, dseq, heads, hd),
            lf_s.reshape(1, db, dseq, heads),
            conv_s[None])
```
